```python
import math
import jax, jax.numpy as jnp
from jax import lax
import numpy as np

D_MODEL = 1024
BATCH = 1
SEQ = 16384
DEPTH = 2

D_MIX = D_MODEL
N_GROUPS = 4
GROUP_W = D_MIX // N_GROUPS

A_HEADS = 4
A_QK_DIM = GROUP_W // A_HEADS // 2
A_V_DIM = 2 * A_QK_DIM

B_HEADS = 4
B_HEAD_DIM = GROUP_W // B_HEADS
MOBA_BLOCK = 256
MOBA_TOPK = 3

C_HEADS = 4
C_V_DIM = GROUP_W // C_HEADS
C_K_DIM = C_V_DIM // 2
GLA_RANK = 16
GLA_TAU = 16.0
GLA_CHUNK = 64

D_WIDTH = GROUP_W
D_BLOCKS = 4
D_BLOCK_DIM = D_WIDTH // D_BLOCKS
CONV_W = 4
LRU_C = 8.0

N_EXPERTS = 32
TOP_K = 4
D_FF = D_MODEL
SWIGLU_ALPHA = 1.702
SWIGLU_LIMIT = 7.0
MOE_BLOCK = 128

Q_BLOCK = 128
EPS = 1e-6

A_Q = A_HEADS * 2 * A_QK_DIM
A_K = A_HEADS * 2 * A_QK_DIM
A_V = A_HEADS * A_V_DIM
B_Q = GROUP_W
B_K = GROUP_W
B_V = GROUP_W
C_Q = C_HEADS * C_K_DIM
C_K = C_HEADS * C_K_DIM
C_V = C_HEADS * C_V_DIM
C_G = GLA_RANK
C_R = C_HEADS * C_V_DIM
D_X = D_WIDTH
D_G = D_WIDTH
D_IN = A_Q + A_K + A_V + B_Q + B_K + B_V + C_Q + C_K + C_V + C_G + C_R + D_X + D_G

kernel_name = "hybrid_parallel_heads_moe_block"


def rms_norm(x, gain):
    xf = x.astype(jnp.float32)
    y = xf * lax.rsqrt(jnp.mean(xf * xf, axis=-1, keepdims=True) + EPS)
    return (y * gain.astype(jnp.float32)).astype(x.dtype)


def to_heads(t, n):
    b, s, f = t.shape
    return t.reshape(b, s, n, f // n).transpose(0, 2, 1, 3)


def merge_heads(t):
    b, h, s, d = t.shape
    return t.transpose(0, 2, 1, 3).reshape(b, s, h * d)


def split_columns(proj):
    sizes = (A_Q, A_K, A_V, B_Q, B_K, B_V, C_Q, C_K, C_V, C_G, C_R, D_X, D_G)
    out, off = [], 0
    for sz in sizes:
        out.append(proj[..., off:off + sz])
        off += sz
    return out


def diff_attention(q, k, v, q_gain, k_gain, lam_q1, lam_k1, lam_q2, lam_k2, out_gain, layer_idx):
    bsz, s, _ = q.shape
    q = rms_norm(q.reshape(bsz, s, A_HEADS, 2, A_QK_DIM), q_gain).transpose(0, 2, 3, 1, 4)
    k = rms_norm(k.reshape(bsz, s, A_HEADS, 2, A_QK_DIM), k_gain).transpose(0, 2, 3, 1, 4)
    v = to_heads(v, A_HEADS)
    lam_init = 0.8 - 0.6 * math.exp(-0.3 * layer_idx)
    lam = (jnp.exp(jnp.sum(lam_q1.astype(jnp.float32) * lam_k1.astype(jnp.float32)))
           - jnp.exp(jnp.sum(lam_q2.astype(jnp.float32) * lam_k2.astype(jnp.float32))) + lam_init)
    scale = A_QK_DIM ** -0.5
    nq = s // Q_BLOCK
    qb = q.reshape(bsz, A_HEADS, 2, nq, Q_BLOCK, A_QK_DIM).transpose(3, 0, 1, 2, 4, 5)
    kpos = jnp.arange(s)

    def block(args):
        j, qj = args
        qpos = j * Q_BLOCK + jnp.arange(Q_BLOCK)
        sc = jnp.einsum('bhiqd,bhikd->bhiqk', qj, k).astype(jnp.float32) * scale
        sc = jnp.where(kpos[None, :] <= qpos[:, None], sc, -jnp.inf)
        p = jax.nn.softmax(sc, axis=-1)
        w = p[:, :, 0] - lam * p[:, :, 1]
        return jnp.einsum('bhqk,bhkd->bhqd', w.astype(v.dtype), v)

    o = lax.map(block, (jnp.arange(nq), qb))
    o = o.transpose(1, 2, 0, 3, 4).reshape(bsz, A_HEADS, s, A_V_DIM)
    o = rms_norm(o, out_gain) * (1.0 - lam_init)
    return merge_heads(o)


def moba_attention(q, k, v, q_gain, k_gain):
    bsz, s, _ = q.shape
    q = rms_norm(to_heads(q, B_HEADS), q_gain)
    k = rms_norm(to_heads(k, B_HEADS), k_gain)
    v = to_heads(v, B_HEADS)
    nb = -(-s // MOBA_BLOCK)
    pad = nb * MOBA_BLOCK - s
    k_p = jnp.pad(k, ((0, 0), (0, 0), (0, pad), (0, 0)))
    v_p = jnp.pad(v, ((0, 0), (0, 0), (0, pad), (0, 0)))
    kb = k_p.reshape(bsz, B_HEADS, nb, MOBA_BLOCK, B_HEAD_DIM)
    vb = v_p.reshape(bsz, B_HEADS, nb, MOBA_BLOCK, B_HEAD_DIM)
    k_mean = jnp.mean(kb.astype(jnp.float32), axis=3)
    topk = min(MOBA_TOPK, nb)
    scale = B_HEAD_DIM ** -0.5
    nq = s // Q_BLOCK
    qb = q.reshape(bsz, B_HEADS, nq, Q_BLOCK, B_HEAD_DIM).transpose(2, 0, 1, 3, 4)
    b_idx = jnp.arange(bsz)[:, None, None, None]
    h_idx = jnp.arange(B_HEADS)[None, :, None, None]
    blk_ids = jnp.arange(nb)

    def block(args):
        j, qj = args
        qpos = j * Q_BLOCK + jnp.arange(Q_BLOCK)
        own = (j * Q_BLOCK) // MOBA_BLOCK
        gate = jnp.einsum('bhqd,bhnd->bhqn', qj.astype(jnp.float32), k_mean)
        gate = jnp.where(blk_ids < own, gate, -jnp.inf)
        _, sel = lax.top_k(gate, topk)
        valid = sel < own
        k_sel = kb[b_idx, h_idx, sel]
        v_sel = vb[b_idx, h_idx, sel]
        s_sel = jnp.einsum('bhqd,bhqnkd->bhqnk', qj, k_sel).astype(jnp.float32) * scale
        s_sel = jnp.where(valid[..., None], s_sel, -jnp.inf).reshape(bsz, B_HEADS, Q_BLOCK, topk * MOBA_BLOCK)
        k_own = lax.dynamic_slice_in_dim(k_p, own * MOBA_BLOCK, MOBA_BLOCK, axis=2)
        v_own = lax.dynamic_slice_in_dim(v_p, own * MOBA_BLOCK, MOBA_BLOCK, axis=2)
        kpos = own * MOBA_BLOCK + jnp.arange(MOBA_BLOCK)
        s_own = jnp.einsum('bhqd,bhkd->bhqk', qj, k_own).astype(jnp.float32) * scale
        s_own = jnp.where(kpos[None, :] <= qpos[:, None], s_own, -jnp.inf)
        p = jax.nn.softmax(jnp.concatenate([s_sel, s_own], axis=-1), axis=-1).astype(v.dtype)
        p_sel = p[..., :topk * MOBA_BLOCK].reshape(bsz, B_HEADS, Q_BLOCK, topk, MOBA_BLOCK)
        p_own = p[..., topk * MOBA_BLOCK:]
        return (jnp.einsum('bhqnk,bhqnkd->bhqd', p_sel, v_sel)
                + jnp.einsum('bhqk,bhkd->bhqd', p_own, v_own))

    o = lax.map(block, (jnp.arange(nq), qb))
    o = o.transpose(1, 2, 0, 3, 4).reshape(bsz, B_HEADS, s, B_HEAD_DIM)
    return merge_heads(o)


def gla(q, k, v, g_low, r, w_g2, b_g, out_gain):
    bsz, s, _ = q.shape
    dt = q.dtype
    f32 = jnp.float32
    q = to_heads(q, C_HEADS).astype(f32) * (C_K_DIM ** -0.5)
    k = to_heads(k, C_HEADS).astype(f32)
    v = to_heads(v, C_HEADS).astype(f32)
    g = jax.nn.log_sigmoid((g_low @ w_g2 + b_g).astype(f32)) / GLA_TAU
    g = to_heads(g, C_HEADS)
    nc = s // GLA_CHUNK

    def chunks(t):
        return t.reshape(bsz, C_HEADS, nc, GLA_CHUNK, t.shape[-1]).transpose(2, 0, 1, 3, 4)

    causal = jnp.tril(jnp.ones((GLA_CHUNK, GLA_CHUNK), bool))

    def step(state, inp):
        qc, kc, vc, gc = inp
        bcum = jnp.cumsum(gc, axis=2)
        o_inter = jnp.einsum('bhcd,bhde->bhce', qc * jnp.exp(bcum), state)
        diff = bcum[:, :, :, None, :] - bcum[:, :, None, :, :]
        decay = jnp.exp(jnp.where(causal[:, :, None], diff, -jnp.inf))
        att = jnp.einsum('bhid,bhjd,bhijd->bhij', qc, kc, decay)
        o_intra = jnp.einsum('bhij,bhje->bhie', att, vc)
        b_last = bcum[:, :, -1:, :]
        state = (jnp.exp(b_last[:, :, 0, :])[..., None] * state
                 + jnp.einsum('bhcd,bhce->bhde', kc * jnp.exp(b_last - bcum), vc))
        return state, o_inter + o_intra

    state0 = jnp.zeros((bsz, C_HEADS, C_K_DIM, C_V_DIM), f32)
    _, o = lax.scan(step, state0, (chunks(q), chunks(k), chunks(v), chunks(g)))
    o = o.transpose(1, 2, 0, 3, 4).reshape(bsz, C_HEADS, s, C_V_DIM)
    o = merge_heads(rms_norm(o, out_gain)) * jax.nn.silu(r.astype(f32))
    return o.astype(dt)


def rg_lru(xb, gate_in, conv_w, conv_b, w_a, b_a, w_x, b_x, lam):
    bsz, s, cw = xb.shape
    f32 = jnp.float32
    xc = lax.conv_general_dilated(xb, conv_w[:, None, :], window_strides=(1,),
                                  padding=[(CONV_W - 1, 0)],
                                  dimension_numbers=('NWC', 'WIO', 'NWC'),
                                  feature_group_count=cw) + conv_b
    xs = xc.reshape(bsz, s, D_BLOCKS, D_BLOCK_DIM)
    r_gate = jax.nn.sigmoid((jnp.einsum('bsnd,nde->bsne', xs, w_a).reshape(bsz, s, cw) + b_a).astype(f32))
    i_gate = jax.nn.sigmoid((jnp.einsum('bsnd,nde->bsne', xs, w_x).reshape(bsz, s, cw) + b_x).astype(f32))
    log_a = LRU_C * r_gate * jax.nn.log_sigmoid(lam.astype(f32))
    a = jnp.exp(log_a)
    bt = jnp.sqrt(-jnp.expm1(2.0 * log_a)) * (i_gate * xc.astype(f32))

    def combine(c1, c2):
        a1, b1 = c1
        a2, b2 = c2
        return a1 * a2, a2 * b1 + b2

    _, h = lax.associative_scan(combine, (a, bt), axis=1)
    return (h * jax.nn.gelu(gate_in.astype(f32))).astype(xb.dtype)


def hybrid_mixer(h, w_in, w_out, a_q_gain, a_k_gain, a_lam_q1, a_lam_k1, a_lam_q2, a_lam_k2,
                 a_out_gain, b_q_gain, b_k_gain, c_w_g2, c_b_g, c_out_gain, d_conv_w, d_conv_b,
                 d_w_a, d_b_a, d_w_x, d_b_x, d_lambda, layer_idx):
    proj = h @ w_in
    aq, ak, av, bq, bk, bv, cq, ck, cv, cg, cr, dx, dg = split_columns(proj)
    o_a = diff_attention(aq, ak, av, a_q_gain, a_k_gain, a_lam_q1, a_lam_k1, a_lam_q2, a_lam_k2,
                         a_out_gain, layer_idx)
    o_b = moba_attention(bq, bk, bv, b_q_gain, b_k_gain)
    o_c = gla(cq, ck, cv, cg, cr, c_w_g2, c_b_g, c_out_gain)
    o_d = rg_lru(dx, dg, d_conv_w, d_conv_b, d_w_a, d_b_a, d_w_x, d_b_x, d_lambda)
    y = jnp.concatenate([o_a.astype(h.dtype), o_b.astype(h.dtype), o_c, o_d], axis=-1)
    return y @ w_out


def moe(h, router_w, router_b, w_up, b_up, w_down, b_down):
    bsz, s, d = h.shape
    t = bsz * s
    xt = h.reshape(t, d)
    logits = (xt @ router_w + router_b).astype(jnp.float32)
    top_vals, top_idx = lax.top_k(logits, TOP_K)
    gates = jax.nn.softmax(top_vals, axis=-1)
    n = t * TOP_K
    e_flat = top_idx.reshape(n)
    tok_flat = jnp.arange(n, dtype=jnp.int32) // TOP_K
    g_flat = gates.reshape(n)
    order = jnp.argsort(e_flat)
    e_sorted = e_flat[order]
    tok_sorted = tok_flat[order]
    g_sorted = g_flat[order]
    counts = jax.ops.segment_sum(jnp.ones((n,), jnp.int32), e_flat, num_segments=N_EXPERTS)
    starts = jnp.cumsum(counts) - counts
    padded = (counts + MOE_BLOCK - 1) // MOE_BLOCK * MOE_BLOCK
    pad_ends = jnp.cumsum(padded)
    pad_starts = pad_ends - padded
    dest = pad_starts[e_sorted] + (jnp.arange(n, dtype=jnp.int32) - starts[e_sorted])
    n_rows = -(-n // MOE_BLOCK) * MOE_BLOCK + N_EXPERTS * MOE_BLOCK
    n_blocks = n_rows // MOE_BLOCK
    row_tok = jnp.full((n_rows,), t, jnp.int32).at[dest].set(tok_sorted)
    row_gate = jnp.zeros((n_rows,), jnp.float32).at[dest].set(g_sorted)
    blk_expert = jnp.minimum(jnp.searchsorted(pad_ends, jnp.arange(n_blocks) * MOE_BLOCK, side='right'),
                             N_EXPERTS - 1)
    x_pad = jnp.concatenate([xt, jnp.zeros((1, d), xt.dtype)], axis=0)
    x_rows = x_pad[row_tok].reshape(n_blocks, MOE_BLOCK, d)

    def expert_block(args):
        e, xb = args
        hu = xb @ w_up[e] + b_up[e]
        g = jnp.minimum(hu[:, :D_FF], SWIGLU_LIMIT)
        lin = jnp.clip(hu[:, D_FF:], -SWIGLU_LIMIT, SWIGLU_LIMIT)
        glu = g * jax.nn.sigmoid(SWIGLU_ALPHA * g)
        return ((lin + 1.0) * glu) @ w_down[e] + b_down[e]

    y_rows = lax.map(expert_block, (blk_expert, x_rows)).reshape(n_rows, d)
    y = jax.ops.segment_sum(y_rows * row_gate[:, None].astype(y_rows.dtype), row_tok, num_segments=t + 1)[:t]
    return y.reshape(bsz, s, d)


def setup_inputs(seed: int = 0) -> dict:
    key = jax.random.key(seed)
    ks = iter(jax.random.split(key, 40))
    L = DEPTH
    f32 = jnp.float32

    def nrm(shape, scale):
        return jax.random.normal(next(ks), shape, f32) * scale

    def gain(shape):
        return 1.0 + nrm(shape, 0.05)

    x = nrm((BATCH, SEQ, D_MODEL), 1.0)
    c = nrm((BATCH, D_MODEL), 1.0)
    ada_w = nrm((L, D_MODEL, 6 * D_MODEL), 0.5 * D_MODEL ** -0.5)
    ada_b = nrm((L, 6 * D_MODEL), 0.02)
    norm1_g = gain((L, D_MODEL))
    norm2_g = gain((L, D_MODEL))
    w_in = nrm((L, D_MODEL, D_IN), D_MODEL ** -0.5)
    w_out = nrm((L, D_MIX, D_MODEL), D_MIX ** -0.5)
    a_q_gain = gain((L, A_QK_DIM))
    a_k_gain = gain((L, A_QK_DIM))
    a_lam_q1 = nrm((L, A_QK_DIM), 0.1)
    a_lam_k1 = nrm((L, A_QK_DIM), 0.1)
    a_lam_q2 = nrm((L, A_QK_DIM), 0.1)
    a_lam_k2 = nrm((L, A_QK_DIM), 0.1)
    a_out_gain = gain((L, A_V_DIM))
    b_q_gain = gain((L, B_HEAD_DIM))
    b_k_gain = gain((L, B_HEAD_DIM))
    c_w_g2 = nrm((L, GLA_RANK, C_HEADS * C_K_DIM), GLA_RANK ** -0.5)
    c_b_g = nrm((L, C_HEADS * C_K_DIM), 0.1)
    c_out_gain = gain((L, C_V_DIM))
    d_conv_w = nrm((L, CONV_W, D_WIDTH), CONV_W ** -0.5)
    d_conv_b = nrm((L, D_WIDTH), 0.02)
    d_w_a = nrm((L, D_BLOCKS, D_BLOCK_DIM, D_BLOCK_DIM), D_BLOCK_DIM ** -0.5)
    d_b_a = nrm((L, D_WIDTH), 0.02)
    d_w_x = nrm((L, D_BLOCKS, D_BLOCK_DIM, D_BLOCK_DIM), D_BLOCK_DIM ** -0.5)
    d_b_x = nrm((L, D_WIDTH), 0.02)
    u = jax.random.uniform(next(ks), (L, D_WIDTH), f32, 0.9, 0.999)
    p = u ** (1.0 / LRU_C)
    d_lambda = jnp.log(p) - jnp.log1p(-p)
    router_w = nrm((L, D_MODEL, N_EXPERTS), D_MODEL ** -0.5)
    router_b = nrm((L, N_EXPERTS), 0.01)
    exp_w_up = nrm((L, N_EXPERTS, D_MODEL, 2 * D_FF), D_MODEL ** -0.5)
    exp_b_up = nrm((L, N_EXPERTS, 2 * D_FF), 0.01)
    exp_w_down = nrm((L, N_EXPERTS, D_FF, D_MODEL), D_FF ** -0.5)
    exp_b_down = nrm((L, N_EXPERTS, D_MODEL), 0.01)
    return {"x": x, "c": c, "ada_w": ada_w, "ada_b": ada_b, "norm1_g": norm1_g, "norm2_g": norm2_g,
            "w_in": w_in, "w_out": w_out, "a_q_gain": a_q_gain, "a_k_gain": a_k_gain,
            "a_lam_q1": a_lam_q1, "a_lam_k1": a_lam_k1, "a_lam_q2": a_lam_q2, "a_lam_k2": a_lam_k2,
            "a_out_gain": a_out_gain, "b_q_gain": b_q_gain, "b_k_gain": b_k_gain,
            "c_w_g2": c_w_g2, "c_b_g": c_b_g, "c_out_gain": c_out_gain,
            "d_conv_w": d_conv_w, "d_conv_b": d_conv_b, "d_w_a": d_w_a, "d_b_a": d_b_a,
            "d_w_x": d_w_x, "d_b_x": d_b_x, "d_lambda": d_lambda,
            "router_w": router_w, "router_b": router_b, "exp_w_up": exp_w_up, "exp_b_up": exp_b_up,
            "exp_w_down": exp_w_down, "exp_b_down": exp_b_down}


def reference(x, c, ada_w, ada_b, norm1_g, norm2_g, w_in, w_out, a_q_gain, a_k_gain,
              a_lam_q1, a_lam_k1, a_lam_q2, a_lam_k2, a_out_gain, b_q_gain, b_k_gain,
              c_w_g2, c_b_g, c_out_gain, d_conv_w, d_conv_b, d_w_a, d_b_a, d_w_x, d_b_x, d_lambda,
              router_w, router_b, exp_w_up, exp_b_up, exp_w_down, exp_b_down):
    cond = jax.nn.silu(c)
    for l in range(DEPTH):
        mod = cond @ ada_w[l] + ada_b[l]
        sh1, sc1, g1, sh2, sc2, g2 = jnp.split(mod, 6, axis=-1)
        h = rms_norm(x, norm1_g[l]) * (1.0 + sc1[:, None, :]) + sh1[:, None, :]
        y = hybrid_mixer(h, w_in[l], w_out[l], a_q_gain[l], a_k_gain[l], a_lam_q1[l], a_lam_k1[l],
                         a_lam_q2[l], a_lam_k2[l], a_out_gain[l], b_q_gain[l], b_k_gain[l],
                         c_w_g2[l], c_b_g[l], c_out_gain[l], d_conv_w[l], d_conv_b[l],
                         d_w_a[l], d_b_a[l], d_w_x[l], d_b_x[l], d_lambda[l], l)
        x = x + g1[:, None, :] * y
        h = rms_norm(x, norm2_g[l]) * (1.0 + sc2[:, None, :]) + sh2[:, None, :]
        y = moe(h, router_w[l], router_b[l], exp_w_up[l], exp_b_up[l], exp_w_down[l], exp_b_down[l])
        x = x + g2[:, None, :] * y
    return x
```

```python
import functools
import math

import jax
import jax.numpy as jnp
from jax import lax
from jax.experimental import pallas as pl
from jax.experimental.pallas import tpu as pltpu

F32 = jnp.float32
BF16 = jnp.bfloat16
HI = lax.Precision.HIGHEST

EPS = 1e-6
NEG = -1e30

D_MODEL = 1024
N_GROUP_W = 256
A_HEADS, A_QK, A_V = 4, 32, 64
B_HEADS, B_DIM = 4, 64
MOBA_BLOCK, MOBA_TOPK = 256, 3
C_HEADS, C_K, C_V = 4, 32, 64
GLA_RANK, GLA_TAU, GLA_CHUNK = 16, 16.0, 64
D_WIDTH, D_BLOCKS, CONV_W, LRU_C = 256, 4, 4, 8.0
N_EXPERTS, TOP_K, D_FF = 32, 4, 1024
SWIGLU_ALPHA, SWIGLU_LIMIT = 1.702, 7.0

LANE = 128
W_IN_PAD = 2944

ROW_TILE = 512
ATT_TILE = 256
MOE_BLK = 256
DISPATCH_TILE = 512
COMBINE_TILE = 256
VMEM_LIMIT = 56 * 1024 * 1024


def _cparams(sem):
    return pltpu.CompilerParams(dimension_semantics=sem, vmem_limit_bytes=VMEM_LIMIT)


def _log_sigmoid(z):
    return jnp.minimum(z, 0.0) - jnp.log1p(jnp.exp(-jnp.abs(z)))


def _sigmoid(z):
    return 1.0 / (1.0 + jnp.exp(-z))


def _group_rms(t, group):
    n = t.shape[-1]
    shift = int(math.log2(group))
    r = lax.broadcasted_iota(jnp.int32, (n, n), 0) >> shift
    c = lax.broadcasted_iota(jnp.int32, (n, n), 1) >> shift
    bd = jnp.where(r == c, 1.0, 0.0).astype(F32)
    ss = jnp.dot(t * t, bd, precision=HI, preferred_element_type=F32)
    return t * lax.rsqrt(ss * (1.0 / group) + EPS)


def _mod_kernel(c_ref, w_ref, b_ref, o_ref):
    c = c_ref[...]
    cond = c * _sigmoid(c)
    cond8 = jnp.broadcast_to(cond, (8, cond.shape[-1]))
    o_ref[0] = jnp.dot(cond8, w_ref[0], precision=HI, preferred_element_type=F32) + b_ref[0]


def _modulation(c, ada_w, ada_b):
    depth, d, n = ada_w.shape
    tn = 1536
    out = pl.pallas_call(
        _mod_kernel,
        grid=(depth, n // tn),
        in_specs=[pl.BlockSpec((1, d), lambda l, j: (0, 0)),
                  pl.BlockSpec((1, d, tn), lambda l, j: (l, 0, j)),
                  pl.BlockSpec((1, 1, tn), lambda l, j: (l, 0, j))],
        out_specs=pl.BlockSpec((1, 8, tn), lambda l, j: (l, 0, j)),
        out_shape=jax.ShapeDtypeStruct((depth, 8, n), F32),
        compiler_params=_cparams(("arbitrary", "arbitrary")),
    )(c, ada_w, ada_b.reshape(depth, 1, n))
    return out[:, 0, :]


def _inproj_kernel(x_ref, ng_ref, sc_ref, sh_ref, w_ref, aqg_ref, akg_ref, bqg_ref, bkg_ref, wg2_ref, bg_ref,
                   aq_ref, akT_ref, av_ref, bq_ref, bkT_ref, bv_ref, bkm_ref,
                   cq_ref, ck_ref, cv_ref, cr_ref, cg_ref, dx_ref, dg_ref):
    x = x_ref[...]
    tm = x.shape[0]
    h = x * lax.rsqrt(jnp.mean(x * x, axis=-1, keepdims=True) + EPS) * ng_ref[...]
    h = h * (1.0 + sc_ref[...]) + sh_ref[...]
    proj = jnp.dot(h.astype(BF16), w_ref[...], preferred_element_type=F32)

    aq = _group_rms(proj[:, 0:256], A_QK) * aqg_ref[...] * (A_QK ** -0.5)
    for g in range(2 * A_HEADS):
        aq_ref[g] = aq[:, A_QK * g:A_QK * (g + 1)].astype(BF16)
    ak = _group_rms(proj[:, 256:512], A_QK) * akg_ref[...]
    akT_ref[...] = ak.T.astype(BF16)
    av = proj[:, 512:768]
    for hd in range(A_HEADS):
        av_ref[hd] = av[:, A_V * hd:A_V * (hd + 1)].astype(BF16)

    bq = _group_rms(proj[:, 768:1024], B_DIM) * bqg_ref[...] * (B_DIM ** -0.5)
    bk = _group_rms(proj[:, 1024:1280], B_DIM) * bkg_ref[...]
    bv = proj[:, 1280:1536]
    for hd in range(B_HEADS):
        bq_ref[hd] = bq[:, B_DIM * hd:B_DIM * (hd + 1)].astype(BF16)
        bv_ref[hd] = bv[:, B_DIM * hd:B_DIM * (hd + 1)].astype(BF16)
    bkT_ref[...] = bk.T.astype(BF16)
    bkm_ref[0] = jnp.mean(bk.reshape(tm // MOBA_BLOCK, MOBA_BLOCK, N_GROUP_W), axis=1)

    cq_ref[...] = proj[:, 1536:1664] * (C_K ** -0.5)
    ck_ref[...] = proj[:, 1664:1792]
    cv_ref[...] = proj[:, 1792:2048]
    cr_ref[...] = proj[:, 2048:2304]
    z = jnp.dot(proj[:, 2816:2944], wg2_ref[...], precision=HI, preferred_element_type=F32) + bg_ref[...]
    cg_ref[...] = _log_sigmoid(z) * (1.0 / GLA_TAU)

    dx_ref[...] = proj[:, 2304:2560]
    dg_ref[...] = proj[:, 2560:2816]


def _inproj(x, ng, sc, sh, w_cat, aqg, akg, bqg, bkg, wg2p, bg):
    s, d = x.shape
    tm = ROW_TILE
    nt = s // tm
    row = lambda i: (i, 0)
    const2 = lambda i: (0, 0)
    hrow = lambda i: (0, i, 0)
    vec = pl.BlockSpec((1, d), const2)
    gvec = pl.BlockSpec((1, N_GROUP_W), const2)
    out_shape = (
        jax.ShapeDtypeStruct((2 * A_HEADS, s, A_QK), BF16),
        jax.ShapeDtypeStruct((N_GROUP_W, s), BF16),
        jax.ShapeDtypeStruct((A_HEADS, s, A_V), BF16),
        jax.ShapeDtypeStruct((B_HEADS, s, B_DIM), BF16),
        jax.ShapeDtypeStruct((N_GROUP_W, s), BF16),
        jax.ShapeDtypeStruct((B_HEADS, s, B_DIM), BF16),
        jax.ShapeDtypeStruct((nt, tm // MOBA_BLOCK, N_GROUP_W), F32),
        jax.ShapeDtypeStruct((s, 128), F32),
        jax.ShapeDtypeStruct((s, 128), F32),
        jax.ShapeDtypeStruct((s, 256), F32),
        jax.ShapeDtypeStruct((s, 256), F32),
        jax.ShapeDtypeStruct((s, 128), F32),
        jax.ShapeDtypeStruct((s, 256), F32),
        jax.ShapeDtypeStruct((s, 256), F32),
    )
    out_specs = (
        pl.BlockSpec((2 * A_HEADS, tm, A_QK), hrow),
        pl.BlockSpec((N_GROUP_W, tm), lambda i: (0, i)),
        pl.BlockSpec((A_HEADS, tm, A_V), hrow),
        pl.BlockSpec((B_HEADS, tm, B_DIM), hrow),
        pl.BlockSpec((N_GROUP_W, tm), lambda i: (0, i)),
        pl.BlockSpec((B_HEADS, tm, B_DIM), hrow),
        pl.BlockSpec((1, tm // MOBA_BLOCK, N_GROUP_W), lambda i: (i, 0, 0)),
        pl.BlockSpec((tm, 128), row),
        pl.BlockSpec((tm, 128), row),
        pl.BlockSpec((tm, 256), row),
        pl.BlockSpec((tm, 256), row),
        pl.BlockSpec((tm, 128), row),
        pl.BlockSpec((tm, 256), row),
        pl.BlockSpec((tm, 256), row),
    )
    return pl.pallas_call(
        _inproj_kernel,
        grid=(nt,),
        in_specs=[pl.BlockSpec((tm, d), row), vec, vec, vec,
                  pl.BlockSpec((d, W_IN_PAD), const2),
                  gvec, gvec, gvec, gvec,
                  pl.BlockSpec((LANE, LANE), const2),
                  pl.BlockSpec((1, LANE), const2)],
        out_specs=out_specs,
        out_shape=out_shape,
        compiler_params=_cparams(("arbitrary",)),
    )(x, ng, sc, sh, w_cat, aqg, akg, bqg, bkg, wg2p, bg)


def _tile_lanes(t, n):
    if n == LANE:
        return t
    if n < LANE:
        return t[:, :n]
    return jnp.concatenate([t] * (n // LANE), axis=1)


def _softmax_step(s, v, m_ref, l_ref, acc_ref, idx):
    tk = s.shape[-1]
    dv = v.shape[-1]
    m_old = m_ref[idx]
    m_new = jnp.maximum(m_old, jnp.max(s, axis=-1, keepdims=True))
    alpha = jnp.exp(m_old - m_new)
    p = jnp.exp(s - _tile_lanes(m_new, tk))
    l_ref[idx] = alpha * l_ref[idx] + jnp.sum(p, axis=-1, keepdims=True)
    acc_ref[idx] = _tile_lanes(alpha, dv) * acc_ref[idx] + jnp.dot(p.astype(BF16), v, preferred_element_type=F32)
    m_ref[idx] = m_new


def _diffattn_kernel(q_ref, kT_ref, v_ref, lq1_ref, lk1_ref, lq2_ref, lk2_ref, og_ref, o_ref,
                     m_ref, l_ref, acc_ref, *, lam_init):
    t = ATT_TILE
    iq = pl.program_id(1)
    lam = (jnp.exp(jnp.sum(lq1_ref[...] * lk1_ref[...], axis=-1, keepdims=True))
           - jnp.exp(jnp.sum(lq2_ref[...] * lk2_ref[...], axis=-1, keepdims=True)) + lam_init)
    m_ref[...] = jnp.full(m_ref.shape, NEG, F32)
    l_ref[...] = jnp.zeros(l_ref.shape, F32)
    acc_ref[...] = jnp.zeros(acc_ref.shape, F32)
    causal = (lax.broadcasted_iota(jnp.int32, (t, t), 1) <= lax.broadcasted_iota(jnp.int32, (t, t), 0))

    def tile(ik, masked):
        off = pl.multiple_of(ik * t, t)
        v = v_ref[0, pl.ds(off, t), :]
        for i in range(2):
            k = kT_ref[A_QK * i:A_QK * (i + 1), pl.ds(off, t)]
            s = jnp.dot(q_ref[i], k, preferred_element_type=F32)
            if masked:
                s = jnp.where(causal, s, NEG)
            _softmax_step(s, v, m_ref, l_ref, acc_ref, i)

    def body(ik, carry):
        tile(ik, False)
        return carry

    lax.fori_loop(0, iq, body, 0)
    tile(iq, True)

    o1 = acc_ref[0] / l_ref[0][:, :A_V]
    o2 = acc_ref[1] / l_ref[1][:, :A_V]
    o = o1 - lam * o2
    o = o * lax.rsqrt(jnp.mean(o * o, axis=-1, keepdims=True) + EPS) * og_ref[...]
    o_ref[0] = o * (1.0 - lam_init)


def _diff_attention(aq, akT, av, lq1, lk1, lq2, lk2, out_gain, layer_idx):
    s = av.shape[1]
    t = ATT_TILE
    lam_init = 0.8 - 0.6 * math.exp(-0.3 * layer_idx)
    vec = pl.BlockSpec((1, A_QK), lambda h, i: (0, 0))
    return pl.pallas_call(
        functools.partial(_diffattn_kernel, lam_init=lam_init),
        grid=(A_HEADS, s // t),
        in_specs=[pl.BlockSpec((2, t, A_QK), lambda h, i: (h, i, 0)),
                  pl.BlockSpec((2 * A_QK, s), lambda h, i: (h, 0)),
                  pl.BlockSpec((1, s, A_V), lambda h, i: (h, 0, 0)),
                  vec, vec, vec, vec,
                  pl.BlockSpec((1, A_V), lambda h, i: (0, 0))],
        out_specs=pl.BlockSpec((1, t, A_V), lambda h, i: (h, i, 0)),
        out_shape=jax.ShapeDtypeStruct((A_HEADS, s, A_V), F32),
        scratch_shapes=[pltpu.VMEM((2, t, LANE), F32), pltpu.VMEM((2, t, LANE), F32),
                        pltpu.VMEM((2, t, A_V), F32)],
        compiler_params=_cparams(("arbitrary", "arbitrary")),
    )(aq, akT, av, lq1, lk1, lq2, lk2, out_gain)


def _moba_kernel(q_ref, kT_ref, v_ref, kmT_ref, o_ref, m_ref, l_ref, acc_ref):
    t = ATT_TILE
    own = pl.program_id(1)
    q = q_ref[0]
    nb = kmT_ref.shape[-1]
    m_ref[...] = jnp.full(m_ref.shape, NEG, F32)
    l_ref[...] = jnp.zeros(l_ref.shape, F32)
    acc_ref[...] = jnp.zeros(acc_ref.shape, F32)

    gate = jnp.dot(q.astype(F32), kmT_ref[0], precision=HI, preferred_element_type=F32)
    blk = lax.broadcasted_iota(jnp.int32, (t, nb), 1)
    earlier = blk < own
    gate = jnp.where(earlier, gate, NEG)
    sel_bias = jnp.full((t, nb), NEG, F32)
    for _ in range(MOBA_TOPK):
        top = jnp.max(gate, axis=-1, keepdims=True)
        pick = blk == jnp.min(jnp.where(gate == top, blk, nb), axis=-1, keepdims=True)
        sel_bias = jnp.where(pick, jnp.where(earlier, 0.0, NEG), sel_bias)
        gate = jnp.where(pick, -jnp.inf, gate)

    off = pl.multiple_of(own * t, t)
    causal = (lax.broadcasted_iota(jnp.int32, (t, t), 1) <= lax.broadcasted_iota(jnp.int32, (t, t), 0))
    s = jnp.dot(q, kT_ref[:, pl.ds(off, t)], preferred_element_type=F32)
    _softmax_step(jnp.where(causal, s, NEG), v_ref[0, pl.ds(off, t), :], m_ref, l_ref, acc_ref, 0)

    def body(j, carry):
        offj = pl.multiple_of(j * t, t)
        bias = jnp.sum(jnp.where(blk == j, sel_bias, 0.0), axis=-1, keepdims=True)
        sj = jnp.dot(q, kT_ref[:, pl.ds(offj, t)], preferred_element_type=F32) + bias
        _softmax_step(sj, v_ref[0, pl.ds(offj, t), :], m_ref, l_ref, acc_ref, 0)
        return carry

    lax.fori_loop(0, own, body, 0)
    o_ref[0] = acc_ref[0] / l_ref[0][:, :B_DIM]


def _moba(bq, bkT, bv, kmT):
    s = bq.shape[1]
    t = ATT_TILE
    nb = kmT.shape[-1]
    return pl.pallas_call(
        _moba_kernel,
        grid=(B_HEADS, s // t),
        in_specs=[pl.BlockSpec((1, t, B_DIM), lambda h, i: (h, i, 0)),
                  pl.BlockSpec((B_DIM, s), lambda h, i: (h, 0)),
                  pl.BlockSpec((1, s, B_DIM), lambda h, i: (h, 0, 0)),
                  pl.BlockSpec((1, B_DIM, nb), lambda h, i: (h, 0, 0))],
        out_specs=pl.BlockSpec((1, t, B_DIM), lambda h, i: (h, i, 0)),
        out_shape=jax.ShapeDtypeStruct((B_HEADS, s, B_DIM), F32),
        scratch_shapes=[pltpu.VMEM((1, t, LANE), F32), pltpu.VMEM((1, t, LANE), F32),
                        pltpu.VMEM((1, t, B_DIM), F32)],
        compiler_params=_cparams(("arbitrary", "arbitrary")),
    )(bq, bkT, bv, kmT)


def _gla_kernel(q_ref, k_ref, g_ref, v_ref, r_ref, og_ref, o_ref, state_ref, b_ref):
    tm = q_ref.shape[0]
    c = GLA_CHUNK
    shift = int(math.log2(c))

    @pl.when(pl.program_id(0) == 0)
    def _():
        state_ref[...] = jnp.zeros(state_ref.shape, F32)

    ri = lax.broadcasted_iota(jnp.int32, (tm, tm), 0)
    ci = lax.broadcasted_iota(jnp.int32, (tm, tm), 1)
    tri = jnp.where((ri >= ci) & ((ri >> shift) == (ci >> shift)), 1.0, 0.0).astype(F32)
    b_ref[...] = jnp.dot(tri, g_ref[...], precision=HI, preferred_element_type=F32)

    causal = (lax.broadcasted_iota(jnp.int32, (c, c), 0) >= lax.broadcasted_iota(jnp.int32, (c, c), 1))
    og = og_ref[...]

    def chunk(ic, carry):
        rows = pl.ds(pl.multiple_of(ic * c, c), c)
        qc = q_ref[rows, :]
        bc = b_ref[rows, :]
        kT = k_ref[rows, :].T
        bT = bc.T
        vc = v_ref[rows, :]
        rc = r_ref[rows, :]
        outs = []
        for hd in range(C_HEADS):
            ks = slice(C_K * hd, C_K * (hd + 1))
            vs = slice(C_V * hd, C_V * (hd + 1))
            qh, bh, kTh, bTh, vh = qc[:, ks], bc[:, ks], kT[ks, :], bT[ks, :], vc[:, vs].astype(BF16)
            att = jnp.zeros((c, c), F32)
            for d in range(C_K):
                diff = bh[:, d:d + 1] - bTh[d:d + 1, :]
                att = att + (qh[:, d:d + 1] * kTh[d:d + 1, :]) * jnp.exp(jnp.where(causal, diff, NEG))
            state = state_ref[hd]
            o = (jnp.dot((qh * jnp.exp(bh)).astype(BF16), state.astype(BF16), preferred_element_type=F32)
                 + jnp.dot(att.astype(BF16), vh, preferred_element_type=F32))
            b_last = bTh[:, c - 1:c]
            k_dec = kTh * jnp.exp(b_last - bTh)
            state_ref[hd] = jnp.exp(b_last) * state + jnp.dot(k_dec.astype(BF16), vh, preferred_element_type=F32)
            o = o * lax.rsqrt(jnp.mean(o * o, axis=-1, keepdims=True) + EPS) * og
            rh = rc[:, vs]
            outs.append(o * (rh * _sigmoid(rh)))
        o_ref[rows, :] = jnp.concatenate(outs, axis=1)
        return carry

    lax.fori_loop(0, tm // c, chunk, 0)


def _gla(cq, ck, cg, cv, cr, out_gain):
    s = cq.shape[0]
    tm = ROW_TILE
    row = lambda i: (i, 0)
    return pl.pallas_call(
        _gla_kernel,
        grid=(s // tm,),
        in_specs=[pl.BlockSpec((tm, 128), row), pl.BlockSpec((tm, 128), row), pl.BlockSpec((tm, 128), row),
                  pl.BlockSpec((tm, 256), row), pl.BlockSpec((tm, 256), row),
                  pl.BlockSpec((1, C_V), lambda i: (0, 0))],
        out_specs=pl.BlockSpec((tm, 256), row),
        out_shape=jax.ShapeDtypeStruct((s, 256), F32),
        scratch_shapes=[pltpu.VMEM((C_HEADS, C_K, C_V), F32), pltpu.VMEM((tm, 128), F32)],
        compiler_params=_cparams(("arbitrary",)),
    )(cq, ck, cg, cv, cr, out_gain)


def _rglru_kernel(x_ref, gate_ref, cw_ref, cb_ref, wa_ref, ba_ref, wx_ref, bx_ref, lam_ref, o_ref,
                  xbuf_ref, h_ref):
    tm = x_ref.shape[0]
    pad = 8

    @pl.when(pl.program_id(0) == 0)
    def _():
        xbuf_ref[0:pad, :] = jnp.zeros((pad, D_WIDTH), F32)
        h_ref[...] = jnp.zeros(h_ref.shape, F32)

    xbuf_ref[pad:pad + tm, :] = x_ref[...]
    xc = cb_ref[...] + jnp.zeros((tm, D_WIDTH), F32)
    for w in range(CONV_W):
        lag = CONV_W - 1 - w
        xc = xc + cw_ref[w:w + 1, :] * xbuf_ref[pad - lag:pad - lag + tm, :]
    xbuf_ref[0:pad, :] = xbuf_ref[tm:tm + pad, :]

    xcb = xc.astype(BF16)
    r_gate = _sigmoid(jnp.dot(xcb, wa_ref[...], preferred_element_type=F32) + ba_ref[...])
    i_gate = _sigmoid(jnp.dot(xcb, wx_ref[...], preferred_element_type=F32) + bx_ref[...])
    log_a = LRU_C * r_gate * _log_sigmoid(lam_ref[...])
    a = jnp.exp(log_a)
    b = jnp.sqrt(1.0 - jnp.exp(2.0 * log_a)) * (i_gate * xc)

    rows = lax.broadcasted_iota(jnp.int32, (tm, D_WIDTH), 0)
    step = 1
    while step < tm:
        keep = rows >= step
        a_prev = jnp.where(keep, pltpu.roll(a, step, 0), 1.0)
        b_prev = jnp.where(keep, pltpu.roll(b, step, 0), 0.0)
        b = a * b_prev + b
        a = a * a_prev
        step *= 2
    h = a * h_ref[0:1, :] + b
    h_ref[...] = jnp.broadcast_to(h[tm - 1:tm, :], h_ref.shape)

    g = gate_ref[...]
    gelu = g * (0.5 * (1.0 + jnp.tanh(math.sqrt(2.0 / math.pi) * (g + 0.044715 * (g * g * g)))))
    o_ref[...] = h * gelu


def _rglru(dx, dgate, conv_w, conv_b, wa_bd, b_a, wx_bd, b_x, lam):
    s = dx.shape[0]
    tm = ROW_TILE
    row = lambda i: (i, 0)
    const2 = lambda i: (0, 0)
    vec = pl.BlockSpec((1, D_WIDTH), const2)
    mat = pl.BlockSpec((D_WIDTH, D_WIDTH), const2)
    return pl.pallas_call(
        _rglru_kernel,
        grid=(s // tm,),
        in_specs=[pl.BlockSpec((tm, D_WIDTH), row), pl.BlockSpec((tm, D_WIDTH), row),
                  pl.BlockSpec((CONV_W, D_WIDTH), const2), vec, mat, vec, mat, vec, vec],
        out_specs=pl.BlockSpec((tm, D_WIDTH), row),
        out_shape=jax.ShapeDtypeStruct((s, D_WIDTH), F32),
        scratch_shapes=[pltpu.VMEM((tm + 8, D_WIDTH), F32), pltpu.VMEM((8, D_WIDTH), F32)],
        compiler_params=_cparams(("arbitrary",)),
    )(dx, dgate, conv_w, conv_b, wa_bd, b_a, wx_bd, b_x, lam)


def _outproj_kernel(x_ref, oa_ref, ob_ref, oc_ref, od_ref, wab_ref, wcd_ref, g1_ref, ng_ref, sc_ref, sh_ref,
                    rw_ref, rb_ref, xo_ref, h_ref, idx_ref, gate_ref):
    tm = x_ref.shape[0]
    y = jnp.dot(oc_ref[...].astype(BF16), wcd_ref[0], preferred_element_type=F32)
    y = y + jnp.dot(od_ref[...].astype(BF16), wcd_ref[1], preferred_element_type=F32)
    for hd in range(A_HEADS):
        y = y + jnp.dot(oa_ref[hd].astype(BF16), wab_ref[hd], preferred_element_type=F32)
        y = y + jnp.dot(ob_ref[hd].astype(BF16), wab_ref[A_HEADS + hd], preferred_element_type=F32)
    x = x_ref[...] + g1_ref[...] * y
    xo_ref[...] = x
    h = x * lax.rsqrt(jnp.mean(x * x, axis=-1, keepdims=True) + EPS) * ng_ref[...]
    h = h * (1.0 + sc_ref[...]) + sh_ref[...]
    h_ref[...] = h

    logits = jnp.dot(h, rw_ref[...], precision=HI, preferred_element_type=F32) + rb_ref[...]
    e_ids = lax.broadcasted_iota(jnp.int32, (tm, N_EXPERTS), 1)
    lane = lax.broadcasted_iota(jnp.int32, (tm, LANE), 1)
    idx_out = jnp.zeros((tm, LANE), jnp.int32)
    val_out = jnp.zeros((tm, LANE), F32)
    top0 = None
    denom = jnp.zeros((tm, 1), F32)
    for k in range(TOP_K):
        top = jnp.max(logits, axis=-1, keepdims=True)
        arg = jnp.min(jnp.where(logits == top, e_ids, N_EXPERTS), axis=-1, keepdims=True)
        if k == 0:
            top0 = top
        w = jnp.exp(top - top0)
        denom = denom + w
        idx_out = jnp.where(lane == k, arg, idx_out)
        val_out = jnp.where(lane == k, w, val_out)
        logits = jnp.where(e_ids == arg, -jnp.inf, logits)
    idx_ref[...] = idx_out
    gate_ref[...] = val_out / denom


def _outproj(x, oa, ob, oc, od, wab, wcd, g1, ng, sc, sh, rw, rb):
    s, d = x.shape
    tm = ROW_TILE
    row = lambda i: (i, 0)
    const2 = lambda i: (0, 0)
    hrow = lambda i: (0, i, 0)
    vec = pl.BlockSpec((1, d), const2)
    return pl.pallas_call(
        _outproj_kernel,
        grid=(s // tm,),
        in_specs=[pl.BlockSpec((tm, d), row),
                  pl.BlockSpec((A_HEADS, tm, A_V), hrow), pl.BlockSpec((B_HEADS, tm, B_DIM), hrow),
                  pl.BlockSpec((tm, 256), row), pl.BlockSpec((tm, 256), row),
                  pl.BlockSpec((A_HEADS + B_HEADS, A_V, d), lambda i: (0, 0, 0)),
                  pl.BlockSpec((2, 256, d), lambda i: (0, 0, 0)),
                  vec, vec, vec, vec,
                  pl.BlockSpec((d, N_EXPERTS), const2), pl.BlockSpec((1, N_EXPERTS), const2)],
        out_specs=(pl.BlockSpec((tm, d), row), pl.BlockSpec((tm, d), row),
                   pl.BlockSpec((tm, LANE), row), pl.BlockSpec((tm, LANE), row)),
        out_shape=(jax.ShapeDtypeStruct((s, d), F32), jax.ShapeDtypeStruct((s, d), F32),
                   jax.ShapeDtypeStruct((s, LANE), jnp.int32), jax.ShapeDtypeStruct((s, LANE), F32)),
        compiler_params=_cparams(("arbitrary",)),
    )(x, oa, ob, oc, od, wab, wcd, g1, ng, sc, sh, rw, rb)


def _dispatch_kernel(dest_ref, h_ref, zeros_ref, rows_ref, sem):
    del zeros_ref
    n = h_ref.shape[0] * TOP_K

    def row_copy(i):
        return pltpu.make_async_copy(h_ref.at[pl.ds(i >> 2, 1)], rows_ref.at[pl.ds(dest_ref[i], 1)], sem)

    def issue(i, carry):
        row_copy(i).start()
        return carry

    def drain(i, carry):
        row_copy(i).wait()
        return carry

    lax.fori_loop(0, n, issue, 0)
    lax.fori_loop(0, n, drain, 0)


def _dispatch(h, dest_flat, n_rows):
    s, d = h.shape
    tm = DISPATCH_TILE
    return pl.pallas_call(
        _dispatch_kernel,
        grid=(s // tm,),
        in_specs=[pl.BlockSpec((tm * TOP_K,), lambda i: (i,), memory_space=pltpu.SMEM),
                  pl.BlockSpec((tm, d), lambda i: (i, 0)),
                  pl.BlockSpec(memory_space=pl.ANY)],
        out_specs=pl.BlockSpec(memory_space=pl.ANY),
        out_shape=jax.ShapeDtypeStruct((n_rows, d), F32),
        scratch_shapes=[pltpu.SemaphoreType.DMA(())],
        input_output_aliases={2: 0},
        compiler_params=_cparams(("arbitrary",)),
    )(dest_flat, h, jnp.zeros((n_rows, d), F32))


def _ffn_kernel(be_ref, first_ref, nused_ref, x_ref, wu_ref, bu_ref, wd_ref, bd_ref, y_ref, wu_bf, wd_bf):
    b = pl.program_id(0)

    @pl.when(first_ref[b] == 1)
    def _():
        wu_bf[...] = wu_ref[0].astype(BF16)
        wd_bf[...] = wd_ref[0].astype(BF16)

    @pl.when(b < nused_ref[0])
    def _():
        hu = jnp.dot(x_ref[...].astype(BF16), wu_bf[...], preferred_element_type=F32) + bu_ref[0]
        g = jnp.minimum(hu[:, :D_FF], SWIGLU_LIMIT)
        lin = jnp.clip(hu[:, D_FF:], -SWIGLU_LIMIT, SWIGLU_LIMIT)
        act = (lin + 1.0) * (g * _sigmoid(SWIGLU_ALPHA * g))
        y_ref[...] = jnp.dot(act.astype(BF16), wd_bf[...], preferred_element_type=F32) + bd_ref[0]

    @pl.when(b >= nused_ref[0])
    def _():
        y_ref[...] = jnp.zeros(y_ref.shape, F32)


def _ffn(x_rows, blk_expert, first, n_used, w_up, b_up, w_down, b_down):
    n_rows, d = x_rows.shape
    blk = MOE_BLK
    n_exp, _, f2 = w_up.shape
    grid_spec = pltpu.PrefetchScalarGridSpec(
        num_scalar_prefetch=3,
        grid=(n_rows // blk,),
        in_specs=[pl.BlockSpec((blk, d), lambda b, be, fi, nu: (b, 0)),
                  pl.BlockSpec((1, d, f2), lambda b, be, fi, nu: (be[b], 0, 0)),
                  pl.BlockSpec((1, 1, f2), lambda b, be, fi, nu: (be[b], 0, 0)),
                  pl.BlockSpec((1, D_FF, d), lambda b, be, fi, nu: (be[b], 0, 0)),
                  pl.BlockSpec((1, 1, d), lambda b, be, fi, nu: (be[b], 0, 0))],
        out_specs=pl.BlockSpec((blk, d), lambda b, be, fi, nu: (b, 0)),
        scratch_shapes=[pltpu.VMEM((d, f2), BF16), pltpu.VMEM((D_FF, d), BF16)],
    )
    return pl.pallas_call(
        _ffn_kernel,
        grid_spec=grid_spec,
        out_shape=jax.ShapeDtypeStruct((n_rows, d), F32),
        compiler_params=_cparams(("arbitrary",)),
    )(blk_expert, first, n_used, x_rows, w_up, b_up.reshape(n_exp, 1, f2), w_down, b_down.reshape(n_exp, 1, d))


def _combine_kernel(dest_ref, x_ref, gate_ref, g2_ref, rows_ref, o_ref, buf_ref, sem):
    tm = x_ref.shape[0]
    n = tm * TOP_K

    def row_copy(i):
        return pltpu.make_async_copy(rows_ref.at[pl.ds(dest_ref[i], 1)], buf_ref.at[i & 3, pl.ds(i >> 2, 1)], sem)

    def issue(i, carry):
        row_copy(i).start()
        return carry

    def drain(i, carry):
        row_copy(i).wait()
        return carry

    lax.fori_loop(0, n, issue, 0)
    lax.fori_loop(0, n, drain, 0)
    gates = gate_ref[...]
    y = gates[:, 0:1] * buf_ref[0]
    for k in range(1, TOP_K):
        y = y + gates[:, k:k + 1] * buf_ref[k]
    o_ref[...] = x_ref[...] + g2_ref[...] * y


def _combine(x, gates, g2, y_rows, dest_flat):
    s, d = x.shape
    tm = COMBINE_TILE
    row = lambda i: (i, 0)
    return pl.pallas_call(
        _combine_kernel,
        grid=(s // tm,),
        in_specs=[pl.BlockSpec((tm * TOP_K,), lambda i: (i,), memory_space=pltpu.SMEM),
                  pl.BlockSpec((tm, d), row), pl.BlockSpec((tm, LANE), row),
                  pl.BlockSpec((1, d), lambda i: (0, 0)),
                  pl.BlockSpec(memory_space=pl.ANY)],
        out_specs=pl.BlockSpec((tm, d), row),
        out_shape=jax.ShapeDtypeStruct((s, d), F32),
        scratch_shapes=[pltpu.VMEM((TOP_K, tm, d), F32), pltpu.SemaphoreType.DMA(())],
        compiler_params=_cparams(("arbitrary",)),
    )(dest_flat, x, gates, g2, y_rows)


def _moe(x, h, idx_pad, gates_pad, g2, w_up, b_up, w_down, b_down):
    s, d = x.shape
    n = s * TOP_K
    blk = MOE_BLK
    n_blocks = n // blk + N_EXPERTS
    idx = idx_pad[:, :TOP_K]
    onehot = jnp.sum((idx[:, :, None] == jnp.arange(N_EXPERTS, dtype=jnp.int32)).astype(jnp.int32), axis=1)
    rank = jnp.cumsum(onehot, axis=0) - onehot
    counts = jnp.sum(onehot, axis=0)
    padded = (counts + blk - 1) // blk * blk
    pad_ends = jnp.cumsum(padded)
    pad_starts = pad_ends - padded
    dest = (pad_starts[idx] + jnp.take_along_axis(rank, idx, axis=1)).astype(jnp.int32).reshape(n)
    blk_expert = jnp.minimum(jnp.searchsorted(pad_ends, jnp.arange(n_blocks, dtype=jnp.int32) * blk, side='right'),
                             N_EXPERTS - 1).astype(jnp.int32)
    first = jnp.concatenate([jnp.ones((1,), jnp.int32), (blk_expert[1:] != blk_expert[:-1]).astype(jnp.int32)])
    n_used = (pad_ends[-1:] // blk).astype(jnp.int32)

    x_rows = _dispatch(h, dest, n_blocks * blk)
    y_rows = _ffn(x_rows, blk_expert, first, n_used, w_up, b_up, w_down, b_down)
    return _combine(x, gates_pad, g2, y_rows, dest)


def _block_diag(w):
    n, d, _ = w.shape
    eye = jnp.eye(n, dtype=w.dtype)
    return (eye[:, None, :, None] * w[:, :, None, :]).reshape(n * d, n * d)


def kernel(x, c, ada_w, ada_b, norm1_g, norm2_g, w_in, w_out, a_q_gain, a_k_gain, a_lam_q1, a_lam_k1, a_lam_q2, a_lam_k2, a_out_gain, b_q_gain, b_k_gain, c_w_g2, c_b_g, c_out_gain, d_conv_w, d_conv_b, d_w_a, d_b_a, d_w_x, d_b_x, d_lambda, router_w, router_b, exp_w_up, exp_b_up, exp_w_down, exp_b_down):
    bsz, s, d = x.shape
    assert bsz == 1 and d == D_MODEL and s % ROW_TILE == 0
    depth = ada_w.shape[0]
    mod = _modulation(c, ada_w, ada_b)
    xs = x.reshape(s, d)
    for l in range(depth):
        sh1, sc1, g1, sh2, sc2, g2 = [mod[l, i * d:(i + 1) * d].reshape(1, d) for i in range(6)]
        w = w_in[l]
        w_cat = jnp.concatenate([w[:, :2048], w[:, 2064:2832], w[:, 2048:2064],
                                 jnp.zeros((d, W_IN_PAD - 2832), w.dtype)], axis=1).astype(BF16)
        wg2p = jnp.zeros((LANE, LANE), F32).at[:GLA_RANK, :].set(c_w_g2[l])
        (aq, akT, av, bq, bkT, bv, bkm, cq, ck, cv, cr, cg, dx, dgate) = _inproj(
            xs, norm1_g[l].reshape(1, d), sc1, sh1, w_cat,
            jnp.tile(a_q_gain[l], 2 * A_HEADS).reshape(1, -1), jnp.tile(a_k_gain[l], 2 * A_HEADS).reshape(1, -1),
            jnp.tile(b_q_gain[l], B_HEADS).reshape(1, -1), jnp.tile(b_k_gain[l], B_HEADS).reshape(1, -1),
            wg2p, c_b_g[l].reshape(1, -1))
        o_a = _diff_attention(aq, akT, av, a_lam_q1[l].reshape(1, -1), a_lam_k1[l].reshape(1, -1),
                              a_lam_q2[l].reshape(1, -1), a_lam_k2[l].reshape(1, -1),
                              a_out_gain[l].reshape(1, -1), l)
        kmT = bkm.reshape(s // MOBA_BLOCK, B_HEADS, B_DIM).transpose(1, 2, 0)
        o_b = _moba(bq, bkT, bv, kmT)
        o_c = _gla(cq, ck, cg, cv, cr, c_out_gain[l].reshape(1, -1))
        o_d = _rglru(dx, dgate, d_conv_w[l], d_conv_b[l].reshape(1, -1),
                     _block_diag(d_w_a[l]).astype(BF16), d_b_a[l].reshape(1, -1),
                     _block_diag(d_w_x[l]).astype(BF16), d_b_x[l].reshape(1, -1), d_lambda[l].reshape(1, -1))
        wo = w_out[l].astype(BF16)
        wab = wo[:512].reshape(A_HEADS + B_HEADS, A_V, d)
        wcd = wo[512:].reshape(2, 256, d)
        xs, h2, idx_pad, gates_pad = _outproj(xs, o_a, o_b, o_c, o_d, wab, wcd, g1, norm2_g[l].reshape(1, d),
                                              sc2, sh2, router_w[l], router_b[l].reshape(1, -1))
        xs = _moe(xs, h2, idx_pad, gates_pad, g2, exp_w_up[l], exp_b_up[l], exp_w_down[l], exp_b_down[l])
    return xs.reshape(bsz, s, d)
```

```python
import functools
import math

import jax
import jax.numpy as jnp
from jax import lax
from jax.experimental import pallas as pl
from jax.experimental.pallas import tpu as pltpu

F32 = jnp.float32
BF16 = jnp.bfloat16
HI = lax.Precision.HIGHEST

EPS = 1e-6
NEG = -1e30

D_MODEL = 1024
N_GROUP_W = 256
A_HEADS, A_QK, A_V = 4, 32, 64
B_HEADS, B_DIM = 4, 64
MOBA_BLOCK, MOBA_TOPK = 256, 3
C_HEADS, C_K, C_V = 4, 32, 64
GLA_RANK, GLA_TAU, GLA_CHUNK = 16, 16.0, 64
D_WIDTH, D_BLOCKS, CONV_W, LRU_C = 256, 4, 4, 8.0
N_EXPERTS, TOP_K, D_FF = 32, 4, 1024
SWIGLU_ALPHA, SWIGLU_LIMIT = 1.702, 7.0

LANE = 128
W_IN_PAD = 2944

ROW_TILE = 512
ATT_TILE = 256
MOE_BLK = 256
DISPATCH_TILE = 512
COMBINE_TILE = 256
VMEM_LIMIT = 56 * 1024 * 1024


def _cparams(sem):
    return pltpu.CompilerParams(dimension_semantics=sem, vmem_limit_bytes=VMEM_LIMIT)


def _log_sigmoid(z):
    return jnp.minimum(z, 0.0) - jnp.log1p(jnp.exp(-jnp.abs(z)))


def _sigmoid(z):
    return 1.0 / (1.0 + jnp.exp(-z))


def _group_rms(t, group):
    n = t.shape[-1]
    shift = int(math.log2(group))
    r = lax.broadcasted_iota(jnp.int32, (n, n), 0) >> shift
    c = lax.broadcasted_iota(jnp.int32, (n, n), 1) >> shift
    bd = jnp.where(r == c, 1.0, 0.0).astype(F32)
    ss = jnp.dot(t * t, bd, precision=HI, preferred_element_type=F32)
    return t * lax.rsqrt(ss * (1.0 / group) + EPS)


def _mod_kernel(c_ref, w_ref, b_ref, o_ref):
    c = c_ref[...]
    cond = c * _sigmoid(c)
    cond8 = jnp.broadcast_to(cond, (8, cond.shape[-1]))
    o_ref[0] = jnp.dot(cond8, w_ref[0], precision=HI, preferred_element_type=F32) + b_ref[0]


def _modulation(c, ada_w, ada_b):
    depth, d, n = ada_w.shape
    tn = 1536
    out = pl.pallas_call(
        _mod_kernel,
        grid=(depth, n // tn),
        in_specs=[pl.BlockSpec((1, d), lambda l, j: (0, 0)),
                  pl.BlockSpec((1, d, tn), lambda l, j: (l, 0, j)),
                  pl.BlockSpec((1, 1, tn), lambda l, j: (l, 0, j))],
        out_specs=pl.BlockSpec((1, 8, tn), lambda l, j: (l, 0, j)),
        out_shape=jax.ShapeDtypeStruct((depth, 8, n), F32),
        compiler_params=_cparams(("arbitrary", "arbitrary")),
    )(c, ada_w, ada_b.reshape(depth, 1, n))
    return out[:, 0, :]


def _inproj_kernel(x_ref, ng_ref, sc_ref, sh_ref, w_ref, aqg_ref, akg_ref, bqg_ref, bkg_ref, wg2_ref, bg_ref,
                   aqT_ref, ak_ref, avT_ref, bqT_ref, bk_ref, bvT_ref, bkm_ref,
                   cq_ref, ck_ref, cv_ref, cr_ref, cg_ref, dx_ref, dg_ref):
    x = x_ref[...]
    tm = x.shape[0]
    h = x * lax.rsqrt(jnp.mean(x * x, axis=-1, keepdims=True) + EPS) * ng_ref[...]
    h = h * (1.0 + sc_ref[...]) + sh_ref[...]
    proj = jnp.dot(h.astype(BF16), w_ref[...], preferred_element_type=F32)

    aq = _group_rms(proj[:, 0:256], A_QK) * aqg_ref[...] * (A_QK ** -0.5)
    aqT_ref[...] = aq.T.astype(BF16)
    ak = _group_rms(proj[:, 256:512], A_QK) * akg_ref[...]
    ak_ref[...] = ak.astype(BF16)
    avT_ref[...] = proj[:, 512:768].T.astype(BF16)

    bq = _group_rms(proj[:, 768:1024], B_DIM) * bqg_ref[...] * (B_DIM ** -0.5)
    bqT_ref[...] = bq.T.astype(BF16)
    bk = _group_rms(proj[:, 1024:1280], B_DIM) * bkg_ref[...]
    bk_ref[...] = bk.astype(BF16)
    bvT_ref[...] = proj[:, 1280:1536].T.astype(BF16)
    bkm_ref[0] = jnp.mean(bk.reshape(tm // MOBA_BLOCK, MOBA_BLOCK, N_GROUP_W), axis=1)

    cq_ref[...] = proj[:, 1536:1664] * (C_K ** -0.5)
    ck_ref[...] = proj[:, 1664:1792]
    cv_ref[...] = proj[:, 1792:2048]
    cr_ref[...] = proj[:, 2048:2304]
    z = jnp.dot(proj[:, 2816:2944], wg2_ref[...], precision=HI, preferred_element_type=F32) + bg_ref[...]
    cg_ref[...] = _log_sigmoid(z) * (1.0 / GLA_TAU)

    dx_ref[...] = proj[:, 2304:2560]
    dg_ref[...] = proj[:, 2560:2816]


def _inproj(x, ng, sc, sh, w_cat, aqg, akg, bqg, bkg, wg2p, bg):
    s, d = x.shape
    tm = ROW_TILE
    nt = s // tm
    row = lambda i: (i, 0)
    const2 = lambda i: (0, 0)
    vec = pl.BlockSpec((1, d), const2)
    gvec = pl.BlockSpec((1, N_GROUP_W), const2)
    out_shape = (
        jax.ShapeDtypeStruct((N_GROUP_W, s), BF16),
        jax.ShapeDtypeStruct((s, N_GROUP_W), BF16),
        jax.ShapeDtypeStruct((N_GROUP_W, s), BF16),
        jax.ShapeDtypeStruct((N_GROUP_W, s), BF16),
        jax.ShapeDtypeStruct((s, N_GROUP_W), BF16),
        jax.ShapeDtypeStruct((N_GROUP_W, s), BF16),
        jax.ShapeDtypeStruct((nt, tm // MOBA_BLOCK, N_GROUP_W), F32),
        jax.ShapeDtypeStruct((s, 128), F32),
        jax.ShapeDtypeStruct((s, 128), F32),
        jax.ShapeDtypeStruct((s, 256), F32),
        jax.ShapeDtypeStruct((s, 256), F32),
        jax.ShapeDtypeStruct((s, 128), F32),
        jax.ShapeDtypeStruct((s, 256), F32),
        jax.ShapeDtypeStruct((s, 256), F32),
    )
    out_specs = (
        pl.BlockSpec((N_GROUP_W, tm), lambda i: (0, i)),
        pl.BlockSpec((tm, N_GROUP_W), row),
        pl.BlockSpec((N_GROUP_W, tm), lambda i: (0, i)),
        pl.BlockSpec((N_GROUP_W, tm), lambda i: (0, i)),
        pl.BlockSpec((tm, N_GROUP_W), row),
        pl.BlockSpec((N_GROUP_W, tm), lambda i: (0, i)),
        pl.BlockSpec((1, tm // MOBA_BLOCK, N_GROUP_W), lambda i: (i, 0, 0)),
        pl.BlockSpec((tm, 128), row),
        pl.BlockSpec((tm, 128), row),
        pl.BlockSpec((tm, 256), row),
        pl.BlockSpec((tm, 256), row),
        pl.BlockSpec((tm, 128), row),
        pl.BlockSpec((tm, 256), row),
        pl.BlockSpec((tm, 256), row),
    )
    return pl.pallas_call(
        _inproj_kernel,
        grid=(nt,),
        in_specs=[pl.BlockSpec((tm, d), row), vec, vec, vec,
                  pl.BlockSpec((d, W_IN_PAD), const2),
                  gvec, gvec, gvec, gvec,
                  pl.BlockSpec((LANE, LANE), const2),
                  pl.BlockSpec((1, LANE), const2)],
        out_specs=out_specs,
        out_shape=out_shape,
        compiler_params=_cparams(("arbitrary",)),
    )(x, ng, sc, sh, w_cat, aqg, akg, bqg, bkg, wg2p, bg)


def _pad_query_rows(qT_ref, qpad_ref, width):
    per = LANE // width
    rows = lax.broadcasted_iota(jnp.int32, qpad_ref.shape[1:], 0)
    for g in range(qpad_ref.shape[0]):
        blk = qT_ref[LANE * (g // per):LANE * (g // per + 1), :]
        lo = width * (g % per)
        qpad_ref[g] = jnp.where((rows >= lo) & (rows < lo + width), blk, jnp.zeros_like(blk))


def _softmax_tiles(k_slabs, qpad_ref, v_tiles, adjust, m_ref, l_ref, acc_ref):
    n = len(k_slabs)
    scores = [jnp.dot(k_slabs[g], qpad_ref[g], preferred_element_type=F32) for g in range(n)]
    probs, alphas = [], []
    for g in range(n):
        sT = adjust(g, scores[g])
        m_old = m_ref[g]
        m_new = jnp.maximum(m_old, jnp.max(sT, axis=0, keepdims=True))
        alpha = jnp.exp(m_old - m_new)
        p = jnp.exp(sT - m_new)
        l_ref[g] = alpha * l_ref[g] + jnp.sum(p, axis=0, keepdims=True)
        m_ref[g] = m_new
        probs.append(p.astype(BF16))
        alphas.append(alpha)
    pv = [jnp.dot(v_tiles[g], probs[g], preferred_element_type=F32) for g in range(n)]
    for g in range(n):
        acc_ref[g] = alphas[g] * acc_ref[g] + pv[g]


def _diffattn_kernel(qT_ref, k_ref, vT_ref, lq1_ref, lk1_ref, lq2_ref, lk2_ref, og_ref, o_ref,
                     m_ref, l_ref, acc_ref, qpad_ref, *, lam_init):
    t = ATT_TILE
    n_maps = 2 * A_HEADS
    iq = pl.program_id(0)
    lam = (jnp.exp(jnp.sum(lq1_ref[...] * lk1_ref[...], axis=-1, keepdims=True))
           - jnp.exp(jnp.sum(lq2_ref[...] * lk2_ref[...], axis=-1, keepdims=True)) + lam_init)
    m_ref[...] = jnp.full(m_ref.shape, NEG, F32)
    l_ref[...] = jnp.zeros(l_ref.shape, F32)
    acc_ref[...] = jnp.zeros(acc_ref.shape, F32)
    _pad_query_rows(qT_ref, qpad_ref, A_QK)
    causal = (lax.broadcasted_iota(jnp.int32, (t, t), 0) <= lax.broadcasted_iota(jnp.int32, (t, t), 1))

    def tile(ik, adjust):
        keys = pl.ds(pl.multiple_of(ik * t, t), t)
        k_slabs = [k_ref[keys, LANE * (g // 4):LANE * (g // 4 + 1)] for g in range(n_maps)]
        v_tiles = [vT_ref[A_V * (g // 2):A_V * (g // 2 + 1), keys] for g in range(n_maps)]
        _softmax_tiles(k_slabs, qpad_ref, v_tiles, adjust, m_ref, l_ref, acc_ref)

    def body(ik, carry):
        tile(ik, lambda g, sT: sT)
        return carry

    lax.fori_loop(0, iq, body, 0)
    tile(iq, lambda g, sT: jnp.where(causal, sT, NEG))

    for hd in range(A_HEADS):
        o = acc_ref[2 * hd] / l_ref[2 * hd] - lam * (acc_ref[2 * hd + 1] / l_ref[2 * hd + 1])
        o = o * lax.rsqrt(jnp.mean(o * o, axis=0, keepdims=True) + EPS) * og_ref[...]
        o_ref[A_V * hd:A_V * (hd + 1), :] = o * (1.0 - lam_init)


def _diff_attention(aqT, ak, avT, lq1, lk1, lq2, lk2, out_gain_col, layer_idx):
    s = aqT.shape[1]
    t = ATT_TILE
    n_maps = 2 * A_HEADS
    lam_init = 0.8 - 0.6 * math.exp(-0.3 * layer_idx)
    vec = pl.BlockSpec((1, A_QK), lambda i: (0, 0))
    return pl.pallas_call(
        functools.partial(_diffattn_kernel, lam_init=lam_init),
        grid=(s // t,),
        in_specs=[pl.BlockSpec((N_GROUP_W, t), lambda i: (0, i)),
                  pl.BlockSpec((s, N_GROUP_W), lambda i: (0, 0)),
                  pl.BlockSpec((N_GROUP_W, s), lambda i: (0, 0)),
                  vec, vec, vec, vec,
                  pl.BlockSpec((A_V, 1), lambda i: (0, 0))],
        out_specs=pl.BlockSpec((N_GROUP_W, t), lambda i: (0, i)),
        out_shape=jax.ShapeDtypeStruct((N_GROUP_W, s), F32),
        scratch_shapes=[pltpu.VMEM((n_maps, 1, t), F32), pltpu.VMEM((n_maps, 1, t), F32),
                        pltpu.VMEM((n_maps, A_V, t), F32), pltpu.VMEM((n_maps, LANE, t), BF16)],
        compiler_params=_cparams(("arbitrary",)),
    )(aqT, ak, avT, lq1, lk1, lq2, lk2, out_gain_col)


def _moba_kernel(qT_ref, k_ref, vT_ref, km_ref, o_ref, m_ref, l_ref, acc_ref, bias_ref, qpad_ref):
    t = ATT_TILE
    own = pl.program_id(0)
    nb = km_ref.shape[1]
    m_ref[...] = jnp.full(m_ref.shape, NEG, F32)
    l_ref[...] = jnp.zeros(l_ref.shape, F32)
    acc_ref[...] = jnp.zeros(acc_ref.shape, F32)
    _pad_query_rows(qT_ref, qpad_ref, B_DIM)

    blk = lax.broadcasted_iota(jnp.int32, (nb, t), 0)
    earlier = blk < own
    for hd in range(B_HEADS):
        qT = qT_ref[B_DIM * hd:B_DIM * (hd + 1), :].astype(F32)
        gate = jnp.dot(km_ref[hd], qT, precision=HI, preferred_element_type=F32)
        gate = jnp.where(earlier, gate, NEG)
        sel_bias = jnp.full((nb, t), NEG, F32)
        for _ in range(MOBA_TOPK):
            top = jnp.max(gate, axis=0, keepdims=True)
            pick = blk == jnp.min(jnp.where(gate == top, blk, nb), axis=0, keepdims=True)
            sel_bias = jnp.where(pick, jnp.where(earlier, 0.0, NEG), sel_bias)
            gate = jnp.where(pick, -jnp.inf, gate)
        bias_ref[hd] = sel_bias

    causal = (lax.broadcasted_iota(jnp.int32, (t, t), 0) <= lax.broadcasted_iota(jnp.int32, (t, t), 1))

    def tile(j, adjust):
        keys = pl.ds(pl.multiple_of(j * t, t), t)
        k_slabs = [k_ref[keys, LANE * (hd // 2):LANE * (hd // 2 + 1)] for hd in range(B_HEADS)]
        v_tiles = [vT_ref[B_DIM * hd:B_DIM * (hd + 1), keys] for hd in range(B_HEADS)]
        _softmax_tiles(k_slabs, qpad_ref, v_tiles, adjust, m_ref, l_ref, acc_ref)

    tile(own, lambda hd, sT: jnp.where(causal, sT, NEG))

    def body(j, carry):
        tile(j, lambda hd, sT: sT + bias_ref[hd, pl.ds(j, 1), :])
        return carry

    lax.fori_loop(0, own, body, 0)
    for hd in range(B_HEADS):
        o_ref[B_DIM * hd:B_DIM * (hd + 1), :] = acc_ref[hd] / l_ref[hd]


def _moba(bqT, bk, bvT, km):
    s = bqT.shape[1]
    t = ATT_TILE
    nb = km.shape[1]
    return pl.pallas_call(
        _moba_kernel,
        grid=(s // t,),
        in_specs=[pl.BlockSpec((N_GROUP_W, t), lambda i: (0, i)),
                  pl.BlockSpec((s, N_GROUP_W), lambda i: (0, 0)),
                  pl.BlockSpec((N_GROUP_W, s), lambda i: (0, 0)),
                  pl.BlockSpec((B_HEADS, nb, B_DIM), lambda i: (0, 0, 0))],
        out_specs=pl.BlockSpec((N_GROUP_W, t), lambda i: (0, i)),
        out_shape=jax.ShapeDtypeStruct((N_GROUP_W, s), F32),
        scratch_shapes=[pltpu.VMEM((B_HEADS, 1, t), F32), pltpu.VMEM((B_HEADS, 1, t), F32),
                        pltpu.VMEM((B_HEADS, B_DIM, t), F32), pltpu.VMEM((B_HEADS, nb, t), F32),
                        pltpu.VMEM((B_HEADS, LANE, t), BF16)],
        compiler_params=_cparams(("arbitrary",)),
    )(bqT, bk, bvT, km)


def _gla_kernel(q_ref, k_ref, g_ref, v_ref, r_ref, og_ref, o_ref, state_ref, b_ref):
    tm = q_ref.shape[0]
    c = GLA_CHUNK
    shift = int(math.log2(c))

    @pl.when(pl.program_id(0) == 0)
    def _():
        state_ref[...] = jnp.zeros(state_ref.shape, F32)

    ri = lax.broadcasted_iota(jnp.int32, (tm, tm), 0)
    ci = lax.broadcasted_iota(jnp.int32, (tm, tm), 1)
    tri = jnp.where((ri >= ci) & ((ri >> shift) == (ci >> shift)), 1.0, 0.0).astype(F32)
    b_ref[...] = jnp.dot(tri, g_ref[...], precision=HI, preferred_element_type=F32)

    causal = (lax.broadcasted_iota(jnp.int32, (c, c), 0) >= lax.broadcasted_iota(jnp.int32, (c, c), 1))
    og = og_ref[...]

    def chunk(ic, carry):
        rows = pl.ds(pl.multiple_of(ic * c, c), c)
        qc = q_ref[rows, :]
        bc = b_ref[rows, :]
        kT = k_ref[rows, :].T
        bT = bc.T
        vc = v_ref[rows, :]
        rc = r_ref[rows, :]
        outs = []
        for hd in range(C_HEADS):
            ks = slice(C_K * hd, C_K * (hd + 1))
            vs = slice(C_V * hd, C_V * (hd + 1))
            qh, bh, kTh, bTh, vh = qc[:, ks], bc[:, ks], kT[ks, :], bT[ks, :], vc[:, vs].astype(BF16)
            att = jnp.zeros((c, c), F32)
            for d in range(C_K):
                diff = bh[:, d:d + 1] - bTh[d:d + 1, :]
                att = att + (qh[:, d:d + 1] * kTh[d:d + 1, :]) * jnp.exp(jnp.where(causal, diff, NEG))
            state = state_ref[hd]
            o = (jnp.dot((qh * jnp.exp(bh)).astype(BF16), state.astype(BF16), preferred_element_type=F32)
                 + jnp.dot(att.astype(BF16), vh, preferred_element_type=F32))
            b_last = bTh[:, c - 1:c]
            k_dec = kTh * jnp.exp(b_last - bTh)
            state_ref[hd] = jnp.exp(b_last) * state + jnp.dot(k_dec.astype(BF16), vh, preferred_element_type=F32)
            o = o * lax.rsqrt(jnp.mean(o * o, axis=-1, keepdims=True) + EPS) * og
            rh = rc[:, vs]
            outs.append(o * (rh * _sigmoid(rh)))
        o_ref[rows, :] = jnp.concatenate(outs, axis=1)
        return carry

    lax.fori_loop(0, tm // c, chunk, 0)


def _gla(cq, ck, cg, cv, cr, out_gain):
    s = cq.shape[0]
    tm = ROW_TILE
    row = lambda i: (i, 0)
    return pl.pallas_call(
        _gla_kernel,
        grid=(s // tm,),
        in_specs=[pl.BlockSpec((tm, 128), row), pl.BlockSpec((tm, 128), row), pl.BlockSpec((tm, 128), row),
                  pl.BlockSpec((tm, 256), row), pl.BlockSpec((tm, 256), row),
                  pl.BlockSpec((1, C_V), lambda i: (0, 0))],
        out_specs=pl.BlockSpec((tm, 256), row),
        out_shape=jax.ShapeDtypeStruct((s, 256), F32),
        scratch_shapes=[pltpu.VMEM((C_HEADS, C_K, C_V), F32), pltpu.VMEM((tm, 128), F32)],
        compiler_params=_cparams(("arbitrary",)),
    )(cq, ck, cg, cv, cr, out_gain)


def _rglru_kernel(x_ref, gate_ref, cw_ref, cb_ref, wa_ref, ba_ref, wx_ref, bx_ref, lam_ref, o_ref,
                  xbuf_ref, h_ref):
    tm = x_ref.shape[0]
    pad = 8

    @pl.when(pl.program_id(0) == 0)
    def _():
        xbuf_ref[0:pad, :] = jnp.zeros((pad, D_WIDTH), F32)
        h_ref[...] = jnp.zeros(h_ref.shape, F32)

    xbuf_ref[pad:pad + tm, :] = x_ref[...]
    xc = cb_ref[...] + jnp.zeros((tm, D_WIDTH), F32)
    for w in range(CONV_W):
        lag = CONV_W - 1 - w
        xc = xc + cw_ref[w:w + 1, :] * xbuf_ref[pad - lag:pad - lag + tm, :]
    xbuf_ref[0:pad, :] = xbuf_ref[tm:tm + pad, :]

    xcb = xc.astype(BF16)
    r_gate = _sigmoid(jnp.dot(xcb, wa_ref[...], preferred_element_type=F32) + ba_ref[...])
    i_gate = _sigmoid(jnp.dot(xcb, wx_ref[...], preferred_element_type=F32) + bx_ref[...])
    log_a = LRU_C * r_gate * _log_sigmoid(lam_ref[...])
    a = jnp.exp(log_a)
    b = jnp.sqrt(1.0 - jnp.exp(2.0 * log_a)) * (i_gate * xc)

    rows = lax.broadcasted_iota(jnp.int32, (tm, D_WIDTH), 0)
    step = 1
    while step < tm:
        keep = rows >= step
        a_prev = jnp.where(keep, pltpu.roll(a, step, 0), 1.0)
        b_prev = jnp.where(keep, pltpu.roll(b, step, 0), 0.0)
        b = a * b_prev + b
        a = a * a_prev
        step *= 2
    h = a * h_ref[0:1, :] + b
    h_ref[...] = jnp.broadcast_to(h[tm - 1:tm, :], h_ref.shape)

    g = gate_ref[...]
    gelu = g * (0.5 * (1.0 + jnp.tanh(math.sqrt(2.0 / math.pi) * (g + 0.044715 * (g * g * g)))))
    o_ref[...] = h * gelu


def _rglru(dx, dgate, conv_w, conv_b, wa_bd, b_a, wx_bd, b_x, lam):
    s = dx.shape[0]
    tm = ROW_TILE
    row = lambda i: (i, 0)
    const2 = lambda i: (0, 0)
    vec = pl.BlockSpec((1, D_WIDTH), const2)
    mat = pl.BlockSpec((D_WIDTH, D_WIDTH), const2)
    return pl.pallas_call(
        _rglru_kernel,
        grid=(s // tm,),
        in_specs=[pl.BlockSpec((tm, D_WIDTH), row), pl.BlockSpec((tm, D_WIDTH), row),
                  pl.BlockSpec((CONV_W, D_WIDTH), const2), vec, mat, vec, mat, vec, vec],
        out_specs=pl.BlockSpec((tm, D_WIDTH), row),
        out_shape=jax.ShapeDtypeStruct((s, D_WIDTH), F32),
        scratch_shapes=[pltpu.VMEM((tm + 8, D_WIDTH), F32), pltpu.VMEM((8, D_WIDTH), F32)],
        compiler_params=_cparams(("arbitrary",)),
    )(dx, dgate, conv_w, conv_b, wa_bd, b_a, wx_bd, b_x, lam)


def _outproj_kernel(x_ref, oaT_ref, obT_ref, oc_ref, od_ref, w_ref, g1_ref, ng_ref, sc_ref, sh_ref,
                    rw_ref, rb_ref, xo_ref, h_ref, idx_ref, gate_ref):
    tm = x_ref.shape[0]
    y = jnp.dot(oaT_ref[...].T.astype(BF16), w_ref[0], preferred_element_type=F32)
    y = y + jnp.dot(obT_ref[...].T.astype(BF16), w_ref[1], preferred_element_type=F32)
    y = y + jnp.dot(oc_ref[...].astype(BF16), w_ref[2], preferred_element_type=F32)
    y = y + jnp.dot(od_ref[...].astype(BF16), w_ref[3], preferred_element_type=F32)
    x = x_ref[...] + g1_ref[...] * y
    xo_ref[...] = x
    h = x * lax.rsqrt(jnp.mean(x * x, axis=-1, keepdims=True) + EPS) * ng_ref[...]
    h = h * (1.0 + sc_ref[...]) + sh_ref[...]
    h_ref[...] = h

    logits = jnp.dot(h, rw_ref[...], precision=HI, preferred_element_type=F32) + rb_ref[...]
    e_ids = lax.broadcasted_iota(jnp.int32, (tm, N_EXPERTS), 1)
    lane = lax.broadcasted_iota(jnp.int32, (tm, LANE), 1)
    idx_out = jnp.zeros((tm, LANE), jnp.int32)
    val_out = jnp.zeros((tm, LANE), F32)
    top0 = None
    denom = jnp.zeros((tm, 1), F32)
    for k in range(TOP_K):
        top = jnp.max(logits, axis=-1, keepdims=True)
        arg = jnp.min(jnp.where(logits == top, e_ids, N_EXPERTS), axis=-1, keepdims=True)
        if k == 0:
            top0 = top
        w = jnp.exp(top - top0)
        denom = denom + w
        idx_out = jnp.where(lane == k, arg, idx_out)
        val_out = jnp.where(lane == k, w, val_out)
        logits = jnp.where(e_ids == arg, -jnp.inf, logits)
    idx_ref[...] = idx_out
    gate_ref[...] = val_out / denom


def _outproj(x, oaT, obT, oc, od, w4, g1, ng, sc, sh, rw, rb):
    s, d = x.shape
    tm = ROW_TILE
    row = lambda i: (i, 0)
    col = lambda i: (0, i)
    const2 = lambda i: (0, 0)
    vec = pl.BlockSpec((1, d), const2)
    return pl.pallas_call(
        _outproj_kernel,
        grid=(s // tm,),
        in_specs=[pl.BlockSpec((tm, d), row),
                  pl.BlockSpec((N_GROUP_W, tm), col), pl.BlockSpec((N_GROUP_W, tm), col),
                  pl.BlockSpec((tm, 256), row), pl.BlockSpec((tm, 256), row),
                  pl.BlockSpec((4, N_GROUP_W, d), lambda i: (0, 0, 0)),
                  vec, vec, vec, vec,
                  pl.BlockSpec((d, N_EXPERTS), const2), pl.BlockSpec((1, N_EXPERTS), const2)],
        out_specs=(pl.BlockSpec((tm, d), row), pl.BlockSpec((tm, d), row),
                   pl.BlockSpec((tm, LANE), row), pl.BlockSpec((tm, LANE), row)),
        out_shape=(jax.ShapeDtypeStruct((s, d), F32), jax.ShapeDtypeStruct((s, d), F32),
                   jax.ShapeDtypeStruct((s, LANE), jnp.int32), jax.ShapeDtypeStruct((s, LANE), F32)),
        compiler_params=_cparams(("arbitrary",)),
    )(x, oaT, obT, oc, od, w4, g1, ng, sc, sh, rw, rb)


def _dispatch_kernel(dest_ref, h_ref, zeros_ref, rows_ref, sem):
    del zeros_ref
    n = h_ref.shape[0] * TOP_K

    def row_copy(i):
        return pltpu.make_async_copy(h_ref.at[pl.ds(i >> 2, 1)], rows_ref.at[pl.ds(dest_ref[i], 1)], sem)

    def issue(i, carry):
        row_copy(i).start()
        return carry

    def drain(i, carry):
        row_copy(i).wait()
        return carry

    lax.fori_loop(0, n, issue, 0)
    lax.fori_loop(0, n, drain, 0)


def _dispatch(h, dest_flat, n_rows):
    s, d = h.shape
    tm = DISPATCH_TILE
    return pl.pallas_call(
        _dispatch_kernel,
        grid=(s // tm,),
        in_specs=[pl.BlockSpec((tm * TOP_K,), lambda i: (i,), memory_space=pltpu.SMEM),
                  pl.BlockSpec((tm, d), lambda i: (i, 0)),
                  pl.BlockSpec(memory_space=pl.ANY)],
        out_specs=pl.BlockSpec(memory_space=pl.ANY),
        out_shape=jax.ShapeDtypeStruct((n_rows, d), F32),
        scratch_shapes=[pltpu.SemaphoreType.DMA(())],
        input_output_aliases={2: 0},
        compiler_params=_cparams(("arbitrary",)),
    )(dest_flat, h, jnp.zeros((n_rows, d), F32))


def _ffn_kernel(be_ref, first_ref, nused_ref, x_ref, wu_ref, bu_ref, wd_ref, bd_ref, y_ref, wu_bf, wd_bf):
    b = pl.program_id(0)

    @pl.when(first_ref[b] == 1)
    def _():
        wu_bf[...] = wu_ref[0].astype(BF16)
        wd_bf[...] = wd_ref[0].astype(BF16)

    @pl.when(b < nused_ref[0])
    def _():
        hu = jnp.dot(x_ref[...].astype(BF16), wu_bf[...], preferred_element_type=F32) + bu_ref[0]
        g = jnp.minimum(hu[:, :D_FF], SWIGLU_LIMIT)
        lin = jnp.clip(hu[:, D_FF:], -SWIGLU_LIMIT, SWIGLU_LIMIT)
        act = (lin + 1.0) * (g * _sigmoid(SWIGLU_ALPHA * g))
        y_ref[...] = jnp.dot(act.astype(BF16), wd_bf[...], preferred_element_type=F32) + bd_ref[0]

    @pl.when(b >= nused_ref[0])
    def _():
        y_ref[...] = jnp.zeros(y_ref.shape, F32)


def _ffn(x_rows, blk_expert, first, n_used, w_up, b_up, w_down, b_down):
    n_rows, d = x_rows.shape
    blk = MOE_BLK
    n_exp, _, f2 = w_up.shape
    grid_spec = pltpu.PrefetchScalarGridSpec(
        num_scalar_prefetch=3,
        grid=(n_rows // blk,),
        in_specs=[pl.BlockSpec((blk, d), lambda b, be, fi, nu: (b, 0)),
                  pl.BlockSpec((1, d, f2), lambda b, be, fi, nu: (be[b], 0, 0)),
                  pl.BlockSpec((1, 1, f2), lambda b, be, fi, nu: (be[b], 0, 0)),
                  pl.BlockSpec((1, D_FF, d), lambda b, be, fi, nu: (be[b], 0, 0)),
                  pl.BlockSpec((1, 1, d), lambda b, be, fi, nu: (be[b], 0, 0))],
        out_specs=pl.BlockSpec((blk, d), lambda b, be, fi, nu: (b, 0)),
        scratch_shapes=[pltpu.VMEM((d, f2), BF16), pltpu.VMEM((D_FF, d), BF16)],
    )
    return pl.pallas_call(
        _ffn_kernel,
        grid_spec=grid_spec,
        out_shape=jax.ShapeDtypeStruct((n_rows, d), F32),
        compiler_params=_cparams(("arbitrary",)),
    )(blk_expert, first, n_used, x_rows, w_up, b_up.reshape(n_exp, 1, f2), w_down, b_down.reshape(n_exp, 1, d))


def _combine_kernel(dest_ref, x_ref, gate_ref, g2_ref, rows_ref, o_ref, buf_ref, sem):
    tm = x_ref.shape[0]
    n = tm * TOP_K

    def row_copy(i):
        return pltpu.make_async_copy(rows_ref.at[pl.ds(dest_ref[i], 1)], buf_ref.at[i & 3, pl.ds(i >> 2, 1)], sem)

    def issue(i, carry):
        row_copy(i).start()
        return carry

    def drain(i, carry):
        row_copy(i).wait()
        return carry

    lax.fori_loop(0, n, issue, 0)
    lax.fori_loop(0, n, drain, 0)
    gates = gate_ref[...]
    y = gates[:, 0:1] * buf_ref[0]
    for k in range(1, TOP_K):
        y = y + gates[:, k:k + 1] * buf_ref[k]
    o_ref[...] = x_ref[...] + g2_ref[...] * y


def _combine(x, gates, g2, y_rows, dest_flat):
    s, d = x.shape
    tm = COMBINE_TILE
    row = lambda i: (i, 0)
    return pl.pallas_call(
        _combine_kernel,
        grid=(s // tm,),
        in_specs=[pl.BlockSpec((tm * TOP_K,), lambda i: (i,), memory_space=pltpu.SMEM),
                  pl.BlockSpec((tm, d), row), pl.BlockSpec((tm, LANE), row),
                  pl.BlockSpec((1, d), lambda i: (0, 0)),
                  pl.BlockSpec(memory_space=pl.ANY)],
        out_specs=pl.BlockSpec((tm, d), row),
        out_shape=jax.ShapeDtypeStruct((s, d), F32),
        scratch_shapes=[pltpu.VMEM((TOP_K, tm, d), F32), pltpu.SemaphoreType.DMA(())],
        compiler_params=_cparams(("arbitrary",)),
    )(dest_flat, x, gates, g2, y_rows)


def _moe(x, h, idx_pad, gates_pad, g2, w_up, b_up, w_down, b_down, layer_idx):
    s, d = x.shape
    n = s * TOP_K
    blk = MOE_BLK
    n_blocks = n // blk + N_EXPERTS
    idx = idx_pad[:, :TOP_K]
    onehot = jnp.sum((idx[:, :, None] == jnp.arange(N_EXPERTS, dtype=jnp.int32)).astype(jnp.int32), axis=1)
    rank = jnp.cumsum(onehot, axis=0) - onehot
    counts = jnp.sum(onehot, axis=0)
    padded = (counts + blk - 1) // blk * blk
    pad_ends = jnp.cumsum(padded)
    pad_starts = pad_ends - padded
    dest = (pad_starts[idx] + jnp.take_along_axis(rank, idx, axis=1)).astype(jnp.int32).reshape(n)
    blk_start = jnp.arange(n_blocks, dtype=jnp.int32) * blk
    blk_expert = jnp.minimum(jnp.sum((pad_ends[None, :] <= blk_start[:, None]).astype(jnp.int32), axis=1),
                             N_EXPERTS - 1).astype(jnp.int32)
    first = jnp.concatenate([jnp.ones((1,), jnp.int32), (blk_expert[1:] != blk_expert[:-1]).astype(jnp.int32)])
    n_used = (pad_ends[-1:] // blk).astype(jnp.int32)

    x_rows = _dispatch(h, dest, n_blocks * blk)
    y_rows = _ffn(x_rows, blk_expert + layer_idx * N_EXPERTS, first, n_used, w_up, b_up, w_down, b_down)
    return _combine(x, gates_pad, g2, y_rows, dest)


def _block_diag(w):
    n, d, _ = w.shape
    eye = jnp.eye(n, dtype=w.dtype)
    return (eye[:, None, :, None] * w[:, :, None, :]).reshape(n * d, n * d)


def kernel(x, c, ada_w, ada_b, norm1_g, norm2_g, w_in, w_out, a_q_gain, a_k_gain, a_lam_q1, a_lam_k1, a_lam_q2, a_lam_k2, a_out_gain, b_q_gain, b_k_gain, c_w_g2, c_b_g, c_out_gain, d_conv_w, d_conv_b, d_w_a, d_b_a, d_w_x, d_b_x, d_lambda, router_w, router_b, exp_w_up, exp_b_up, exp_w_down, exp_b_down):
    bsz, s, d = x.shape
    assert bsz == 1 and d == D_MODEL and s % ROW_TILE == 0
    depth = ada_w.shape[0]
    mod = _modulation(c, ada_w, ada_b)
    xs = x.reshape(s, d)
    w_up_all = exp_w_up.reshape(depth * N_EXPERTS, d, 2 * D_FF)
    b_up_all = exp_b_up.reshape(depth * N_EXPERTS, 2 * D_FF)
    w_down_all = exp_w_down.reshape(depth * N_EXPERTS, D_FF, d)
    b_down_all = exp_b_down.reshape(depth * N_EXPERTS, d)
    for l in range(depth):
        sh1, sc1, g1, sh2, sc2, g2 = [mod[l, i * d:(i + 1) * d].reshape(1, d) for i in range(6)]
        w = w_in[l]
        w_cat = jnp.concatenate([w[:, :2048], w[:, 2064:2832], w[:, 2048:2064],
                                 jnp.zeros((d, W_IN_PAD - 2832), w.dtype)], axis=1).astype(BF16)
        wg2p = jnp.zeros((LANE, LANE), F32).at[:GLA_RANK, :].set(c_w_g2[l])
        (aqT, ak, avT, bqT, bk, bvT, bkm, cq, ck, cv, cr, cg, dx, dgate) = _inproj(
            xs, norm1_g[l].reshape(1, d), sc1, sh1, w_cat,
            jnp.tile(a_q_gain[l], 2 * A_HEADS).reshape(1, -1), jnp.tile(a_k_gain[l], 2 * A_HEADS).reshape(1, -1),
            jnp.tile(b_q_gain[l], B_HEADS).reshape(1, -1), jnp.tile(b_k_gain[l], B_HEADS).reshape(1, -1),
            wg2p, c_b_g[l].reshape(1, -1))
        o_aT = _diff_attention(aqT, ak, avT, a_lam_q1[l].reshape(1, -1), a_lam_k1[l].reshape(1, -1),
                               a_lam_q2[l].reshape(1, -1), a_lam_k2[l].reshape(1, -1),
                               a_out_gain[l].reshape(-1, 1), l)
        km = bkm.reshape(s // MOBA_BLOCK, B_HEADS, B_DIM).transpose(1, 0, 2)
        o_bT = _moba(bqT, bk, bvT, km)
        o_c = _gla(cq, ck, cg, cv, cr, c_out_gain[l].reshape(1, -1))
        o_d = _rglru(dx, dgate, d_conv_w[l], d_conv_b[l].reshape(1, -1),
                     _block_diag(d_w_a[l]).astype(BF16), d_b_a[l].reshape(1, -1),
                     _block_diag(d_w_x[l]).astype(BF16), d_b_x[l].reshape(1, -1), d_lambda[l].reshape(1, -1))
        w4 = w_out[l].astype(BF16).reshape(4, N_GROUP_W, d)
        xs, h2, idx_pad, gates_pad = _outproj(xs, o_aT, o_bT, o_c, o_d, w4, g1, norm2_g[l].reshape(1, d),
                                              sc2, sh2, router_w[l], router_b[l].reshape(1, -1))
        xs = _moe(xs, h2, idx_pad, gates_pad, g2, w_up_all, b_up_all, w_down_all, b_down_all, l)
    return xs.reshape(bsz, s, d)
```

```python
import functools
import math

import jax
import jax.numpy as jnp
from jax import lax
from jax.experimental import pallas as pl
from jax.experimental.pallas import tpu as pltpu

F32 = jnp.float32
BF16 = jnp.bfloat16
HI = lax.Precision.HIGHEST

EPS = 1e-6
NEG = -1e30

D_MODEL = 1024
N_GROUP_W = 256
A_HEADS, A_QK, A_V = 4, 32, 64
B_HEADS, B_DIM = 4, 64
MOBA_BLOCK, MOBA_TOPK = 256, 3
C_HEADS, C_K, C_V = 4, 32, 64
GLA_RANK, GLA_TAU, GLA_CHUNK = 16, 16.0, 64
D_WIDTH, D_BLOCKS, CONV_W, LRU_C = 256, 4, 4, 8.0
N_EXPERTS, TOP_K, D_FF = 32, 4, 1024
SWIGLU_ALPHA, SWIGLU_LIMIT = 1.702, 7.0

LANE = 128
SUBLANE = 8
W_IN_PAD = 2944

ROW_TILE = 512
ATT_TILE = 256
MOE_BLK = 256
MOE_TILE = 256
MOE_SORT_ROWS = MOE_TILE * TOP_K + N_EXPERTS * SUBLANE
VMEM_LIMIT = 56 * 1024 * 1024


def _cparams(sem):
    return pltpu.CompilerParams(dimension_semantics=sem, vmem_limit_bytes=VMEM_LIMIT)


def _log_sigmoid(z):
    return jnp.minimum(z, 0.0) - jnp.log1p(jnp.exp(-jnp.abs(z)))


def _sigmoid(z):
    return 1.0 / (1.0 + jnp.exp(-z))


def _group_rms(t, group):
    n = t.shape[-1]
    shift = int(math.log2(group))
    r = lax.broadcasted_iota(jnp.int32, (n, n), 0) >> shift
    c = lax.broadcasted_iota(jnp.int32, (n, n), 1) >> shift
    bd = jnp.where(r == c, 1.0, 0.0).astype(F32)
    ss = jnp.dot(t * t, bd, precision=HI, preferred_element_type=F32)
    return t * lax.rsqrt(ss * (1.0 / group) + EPS)


def _mod_kernel(c_ref, w_ref, b_ref, o_ref):
    c = c_ref[...]
    cond = c * _sigmoid(c)
    cond8 = jnp.broadcast_to(cond, (8, cond.shape[-1]))
    o_ref[0] = jnp.dot(cond8, w_ref[0], precision=HI, preferred_element_type=F32) + b_ref[0]


def _modulation(c, ada_w, ada_b):
    depth, d, n = ada_w.shape
    tn = 1536
    out = pl.pallas_call(
        _mod_kernel,
        grid=(depth, n // tn),
        in_specs=[pl.BlockSpec((1, d), lambda l, j: (0, 0)),
                  pl.BlockSpec((1, d, tn), lambda l, j: (l, 0, j)),
                  pl.BlockSpec((1, 1, tn), lambda l, j: (l, 0, j))],
        out_specs=pl.BlockSpec((1, 8, tn), lambda l, j: (l, 0, j)),
        out_shape=jax.ShapeDtypeStruct((depth, 8, n), F32),
        compiler_params=_cparams(("arbitrary", "arbitrary")),
    )(c, ada_w, ada_b.reshape(depth, 1, n))
    return out[:, 0, :]


def _inproj_kernel(x_ref, ng_ref, sc_ref, sh_ref, w_ref, aqg_ref, akg_ref, bqg_ref, bkg_ref, wg2_ref, bg_ref,
                   aqT_ref, ak_ref, avT_ref, bqT_ref, bk_ref, bvT_ref, bkm_ref,
                   cq_ref, ck_ref, cv_ref, cr_ref, cg_ref, dx_ref, dg_ref):
    x = x_ref[...]
    tm = x.shape[0]
    h = x * lax.rsqrt(jnp.mean(x * x, axis=-1, keepdims=True) + EPS) * ng_ref[...]
    h = h * (1.0 + sc_ref[...]) + sh_ref[...]
    proj = jnp.dot(h.astype(BF16), w_ref[...], preferred_element_type=F32)

    aq = _group_rms(proj[:, 0:256], A_QK) * aqg_ref[...] * (A_QK ** -0.5)
    aqT_ref[...] = aq.T.astype(BF16)
    ak = _group_rms(proj[:, 256:512], A_QK) * akg_ref[...]
    ak_ref[...] = ak.astype(BF16)
    avT_ref[...] = proj[:, 512:768].T.astype(BF16)

    bq = _group_rms(proj[:, 768:1024], B_DIM) * bqg_ref[...] * (B_DIM ** -0.5)
    bqT_ref[...] = bq.T.astype(BF16)
    bk = _group_rms(proj[:, 1024:1280], B_DIM) * bkg_ref[...]
    bk_ref[...] = bk.astype(BF16)
    bvT_ref[...] = proj[:, 1280:1536].T.astype(BF16)
    bkm_ref[0] = jnp.mean(bk.reshape(tm // MOBA_BLOCK, MOBA_BLOCK, N_GROUP_W), axis=1)

    cq_ref[...] = proj[:, 1536:1664] * (C_K ** -0.5)
    ck_ref[...] = proj[:, 1664:1792]
    cv_ref[...] = proj[:, 1792:2048]
    cr_ref[...] = proj[:, 2048:2304]
    z = jnp.dot(proj[:, 2816:2944], wg2_ref[...], precision=HI, preferred_element_type=F32) + bg_ref[...]
    cg_ref[...] = _log_sigmoid(z) * (1.0 / GLA_TAU)

    dx_ref[...] = proj[:, 2304:2560]
    dg_ref[...] = proj[:, 2560:2816]


def _inproj(x, ng, sc, sh, w_cat, aqg, akg, bqg, bkg, wg2p, bg):
    s, d = x.shape
    tm = ROW_TILE
    nt = s // tm
    row = lambda i: (i, 0)
    const2 = lambda i: (0, 0)
    vec = pl.BlockSpec((1, d), const2)
    gvec = pl.BlockSpec((1, N_GROUP_W), const2)
    out_shape = (
        jax.ShapeDtypeStruct((N_GROUP_W, s), BF16),
        jax.ShapeDtypeStruct((s, N_GROUP_W), BF16),
        jax.ShapeDtypeStruct((N_GROUP_W, s), BF16),
        jax.ShapeDtypeStruct((N_GROUP_W, s), BF16),
        jax.ShapeDtypeStruct((s, N_GROUP_W), BF16),
        jax.ShapeDtypeStruct((N_GROUP_W, s), BF16),
        jax.ShapeDtypeStruct((nt, tm // MOBA_BLOCK, N_GROUP_W), F32),
        jax.ShapeDtypeStruct((s, 128), F32),
        jax.ShapeDtypeStruct((s, 128), F32),
        jax.ShapeDtypeStruct((s, 256), F32),
        jax.ShapeDtypeStruct((s, 256), F32),
        jax.ShapeDtypeStruct((s, 128), F32),
        jax.ShapeDtypeStruct((s, 256), F32),
        jax.ShapeDtypeStruct((s, 256), F32),
    )
    out_specs = (
        pl.BlockSpec((N_GROUP_W, tm), lambda i: (0, i)),
        pl.BlockSpec((tm, N_GROUP_W), row),
        pl.BlockSpec((N_GROUP_W, tm), lambda i: (0, i)),
        pl.BlockSpec((N_GROUP_W, tm), lambda i: (0, i)),
        pl.BlockSpec((tm, N_GROUP_W), row),
        pl.BlockSpec((N_GROUP_W, tm), lambda i: (0, i)),
        pl.BlockSpec((1, tm // MOBA_BLOCK, N_GROUP_W), lambda i: (i, 0, 0)),
        pl.BlockSpec((tm, 128), row),
        pl.BlockSpec((tm, 128), row),
        pl.BlockSpec((tm, 256), row),
        pl.BlockSpec((tm, 256), row),
        pl.BlockSpec((tm, 128), row),
        pl.BlockSpec((tm, 256), row),
        pl.BlockSpec((tm, 256), row),
    )
    return pl.pallas_call(
        _inproj_kernel,
        grid=(nt,),
        in_specs=[pl.BlockSpec((tm, d), row), vec, vec, vec,
                  pl.BlockSpec((d, W_IN_PAD), const2),
                  gvec, gvec, gvec, gvec,
                  pl.BlockSpec((LANE, LANE), const2),
                  pl.BlockSpec((1, LANE), const2)],
        out_specs=out_specs,
        out_shape=out_shape,
        compiler_params=_cparams(("arbitrary",)),
    )(x, ng, sc, sh, w_cat, aqg, akg, bqg, bkg, wg2p, bg)


def _pad_query_rows(qT_ref, qpad_ref, width):
    per = LANE // width
    rows = lax.broadcasted_iota(jnp.int32, qpad_ref.shape[1:], 0)
    for g in range(qpad_ref.shape[0]):
        blk = qT_ref[LANE * (g // per):LANE * (g // per + 1), :]
        lo = width * (g % per)
        qpad_ref[g] = jnp.where((rows >= lo) & (rows < lo + width), blk, jnp.zeros_like(blk))


def _softmax_tiles(k_slabs, qpad_ref, v_tiles, adjust, m_ref, l_ref, acc_ref):
    n = len(k_slabs)
    scores = [jnp.dot(k_slabs[g], qpad_ref[g], preferred_element_type=F32) for g in range(n)]
    probs, alphas = [], []
    for g in range(n):
        sT = adjust(g, scores[g])
        m_old = m_ref[g]
        m_new = jnp.maximum(m_old, jnp.max(sT, axis=0, keepdims=True))
        alpha = jnp.exp(m_old - m_new)
        p = jnp.exp(sT - m_new)
        l_ref[g] = alpha * l_ref[g] + jnp.sum(p, axis=0, keepdims=True)
        m_ref[g] = m_new
        probs.append(p.astype(BF16))
        alphas.append(alpha)
    pv = [jnp.dot(v_tiles[g], probs[g], preferred_element_type=F32) for g in range(n)]
    for g in range(n):
        acc_ref[g] = alphas[g] * acc_ref[g] + pv[g]


def _diffattn_kernel(qT_ref, k_ref, vT_ref, lq1_ref, lk1_ref, lq2_ref, lk2_ref, og_ref, o_ref,
                     m_ref, l_ref, acc_ref, qpad_ref, *, lam_init):
    t = ATT_TILE
    n_maps = 2 * A_HEADS
    iq = pl.program_id(0)
    lam = (jnp.exp(jnp.sum(lq1_ref[...] * lk1_ref[...], axis=-1, keepdims=True))
           - jnp.exp(jnp.sum(lq2_ref[...] * lk2_ref[...], axis=-1, keepdims=True)) + lam_init)
    m_ref[...] = jnp.full(m_ref.shape, NEG, F32)
    l_ref[...] = jnp.zeros(l_ref.shape, F32)
    acc_ref[...] = jnp.zeros(acc_ref.shape, F32)
    _pad_query_rows(qT_ref, qpad_ref, A_QK)
    causal = (lax.broadcasted_iota(jnp.int32, (t, t), 0) <= lax.broadcasted_iota(jnp.int32, (t, t), 1))

    def tile(ik, adjust):
        keys = pl.ds(pl.multiple_of(ik * t, t), t)
        k_slabs = [k_ref[keys, LANE * (g // 4):LANE * (g // 4 + 1)] for g in range(n_maps)]
        v_tiles = [vT_ref[A_V * (g // 2):A_V * (g // 2 + 1), keys] for g in range(n_maps)]
        _softmax_tiles(k_slabs, qpad_ref, v_tiles, adjust, m_ref, l_ref, acc_ref)

    def body(ik, carry):
        tile(ik, lambda g, sT: sT)
        return carry

    lax.fori_loop(0, iq, body, 0)
    tile(iq, lambda g, sT: jnp.where(causal, sT, NEG))

    for hd in range(A_HEADS):
        o = acc_ref[2 * hd] / l_ref[2 * hd] - lam * (acc_ref[2 * hd + 1] / l_ref[2 * hd + 1])
        o = o * lax.rsqrt(jnp.mean(o * o, axis=0, keepdims=True) + EPS) * og_ref[...]
        o_ref[A_V * hd:A_V * (hd + 1), :] = o * (1.0 - lam_init)


def _diff_attention(aqT, ak, avT, lq1, lk1, lq2, lk2, out_gain_col, layer_idx):
    s = aqT.shape[1]
    t = ATT_TILE
    n_maps = 2 * A_HEADS
    lam_init = 0.8 - 0.6 * math.exp(-0.3 * layer_idx)
    vec = pl.BlockSpec((1, A_QK), lambda i: (0, 0))
    return pl.pallas_call(
        functools.partial(_diffattn_kernel, lam_init=lam_init),
        grid=(s // t,),
        in_specs=[pl.BlockSpec((N_GROUP_W, t), lambda i: (0, i)),
                  pl.BlockSpec((s, N_GROUP_W), lambda i: (0, 0)),
                  pl.BlockSpec((N_GROUP_W, s), lambda i: (0, 0)),
                  vec, vec, vec, vec,
                  pl.BlockSpec((A_V, 1), lambda i: (0, 0))],
        out_specs=pl.BlockSpec((N_GROUP_W, t), lambda i: (0, i)),
        out_shape=jax.ShapeDtypeStruct((N_GROUP_W, s), F32),
        scratch_shapes=[pltpu.VMEM((n_maps, 1, t), F32), pltpu.VMEM((n_maps, 1, t), F32),
                        pltpu.VMEM((n_maps, A_V, t), F32), pltpu.VMEM((n_maps, LANE, t), BF16)],
        compiler_params=_cparams(("arbitrary",)),
    )(aqT, ak, avT, lq1, lk1, lq2, lk2, out_gain_col)


def _moba_kernel(qT_ref, k_ref, vT_ref, km_ref, o_ref, m_ref, l_ref, acc_ref, bias_ref, qpad_ref):
    t = ATT_TILE
    own = pl.program_id(0)
    nb = km_ref.shape[1]
    m_ref[...] = jnp.full(m_ref.shape, NEG, F32)
    l_ref[...] = jnp.zeros(l_ref.shape, F32)
    acc_ref[...] = jnp.zeros(acc_ref.shape, F32)
    _pad_query_rows(qT_ref, qpad_ref, B_DIM)

    blk = lax.broadcasted_iota(jnp.int32, (nb, t), 0)
    earlier = blk < own
    for hd in range(B_HEADS):
        qT = qT_ref[B_DIM * hd:B_DIM * (hd + 1), :].astype(F32)
        gate = jnp.dot(km_ref[hd], qT, precision=HI, preferred_element_type=F32)
        gate = jnp.where(earlier, gate, NEG)
        sel_bias = jnp.full((nb, t), NEG, F32)
        for _ in range(MOBA_TOPK):
            top = jnp.max(gate, axis=0, keepdims=True)
            pick = blk == jnp.min(jnp.where(gate == top, blk, nb), axis=0, keepdims=True)
            sel_bias = jnp.where(pick, jnp.where(earlier, 0.0, NEG), sel_bias)
            gate = jnp.where(pick, -jnp.inf, gate)
        bias_ref[hd] = sel_bias

    causal = (lax.broadcasted_iota(jnp.int32, (t, t), 0) <= lax.broadcasted_iota(jnp.int32, (t, t), 1))

    def tile(j, adjust):
        keys = pl.ds(pl.multiple_of(j * t, t), t)
        k_slabs = [k_ref[keys, LANE * (hd // 2):LANE * (hd // 2 + 1)] for hd in range(B_HEADS)]
        v_tiles = [vT_ref[B_DIM * hd:B_DIM * (hd + 1), keys] for hd in range(B_HEADS)]
        _softmax_tiles(k_slabs, qpad_ref, v_tiles, adjust, m_ref, l_ref, acc_ref)

    tile(own, lambda hd, sT: jnp.where(causal, sT, NEG))

    def body(j, carry):
        tile(j, lambda hd, sT: sT + bias_ref[hd, pl.ds(j, 1), :])
        return carry

    lax.fori_loop(0, own, body, 0)
    for hd in range(B_HEADS):
        o_ref[B_DIM * hd:B_DIM * (hd + 1), :] = acc_ref[hd] / l_ref[hd]


def _moba(bqT, bk, bvT, km):
    s = bqT.shape[1]
    t = ATT_TILE
    nb = km.shape[1]
    return pl.pallas_call(
        _moba_kernel,
        grid=(s // t,),
        in_specs=[pl.BlockSpec((N_GROUP_W, t), lambda i: (0, i)),
                  pl.BlockSpec((s, N_GROUP_W), lambda i: (0, 0)),
                  pl.BlockSpec((N_GROUP_W, s), lambda i: (0, 0)),
                  pl.BlockSpec((B_HEADS, nb, B_DIM), lambda i: (0, 0, 0))],
        out_specs=pl.BlockSpec((N_GROUP_W, t), lambda i: (0, i)),
        out_shape=jax.ShapeDtypeStruct((N_GROUP_W, s), F32),
        scratch_shapes=[pltpu.VMEM((B_HEADS, 1, t), F32), pltpu.VMEM((B_HEADS, 1, t), F32),
                        pltpu.VMEM((B_HEADS, B_DIM, t), F32), pltpu.VMEM((B_HEADS, nb, t), F32),
                        pltpu.VMEM((B_HEADS, LANE, t), BF16)],
        compiler_params=_cparams(("arbitrary",)),
    )(bqT, bk, bvT, km)


def _gla_kernel(q_ref, k_ref, g_ref, v_ref, r_ref, og_ref, o_ref, state_ref, b_ref):
    tm = q_ref.shape[0]
    c = GLA_CHUNK
    shift = int(math.log2(c))

    @pl.when(pl.program_id(0) == 0)
    def _():
        state_ref[...] = jnp.zeros(state_ref.shape, F32)

    ri = lax.broadcasted_iota(jnp.int32, (tm, tm), 0)
    ci = lax.broadcasted_iota(jnp.int32, (tm, tm), 1)
    tri = jnp.where((ri >= ci) & ((ri >> shift) == (ci >> shift)), 1.0, 0.0).astype(F32)
    b_ref[...] = jnp.dot(tri, g_ref[...], precision=HI, preferred_element_type=F32)

    causal = (lax.broadcasted_iota(jnp.int32, (c, c), 0) >= lax.broadcasted_iota(jnp.int32, (c, c), 1))
    og = og_ref[...]

    def chunk(ic, carry):
        rows = pl.ds(pl.multiple_of(ic * c, c), c)
        qc = q_ref[rows, :]
        bc = b_ref[rows, :]
        kT = k_ref[rows, :].T
        bT = bc.T
        vc = v_ref[rows, :]
        rc = r_ref[rows, :]
        outs = []
        for hd in range(C_HEADS):
            ks = slice(C_K * hd, C_K * (hd + 1))
            vs = slice(C_V * hd, C_V * (hd + 1))
            qh, bh, kTh, bTh, vh = qc[:, ks], bc[:, ks], kT[ks, :], bT[ks, :], vc[:, vs].astype(BF16)
            att = jnp.zeros((c, c), F32)
            for d in range(C_K):
                diff = bh[:, d:d + 1] - bTh[d:d + 1, :]
                att = att + (qh[:, d:d + 1] * kTh[d:d + 1, :]) * jnp.exp(jnp.where(causal, diff, NEG))
            state = state_ref[hd]
            o = (jnp.dot((qh * jnp.exp(bh)).astype(BF16), state.astype(BF16), preferred_element_type=F32)
                 + jnp.dot(att.astype(BF16), vh, preferred_element_type=F32))
            b_last = bTh[:, c - 1:c]
            k_dec = kTh * jnp.exp(b_last - bTh)
            state_ref[hd] = jnp.exp(b_last) * state + jnp.dot(k_dec.astype(BF16), vh, preferred_element_type=F32)
            o = o * lax.rsqrt(jnp.mean(o * o, axis=-1, keepdims=True) + EPS) * og
            rh = rc[:, vs]
            outs.append(o * (rh * _sigmoid(rh)))
        o_ref[rows, :] = jnp.concatenate(outs, axis=1)
        return carry

    lax.fori_loop(0, tm // c, chunk, 0)


def _gla(cq, ck, cg, cv, cr, out_gain):
    s = cq.shape[0]
    tm = ROW_TILE
    row = lambda i: (i, 0)
    return pl.pallas_call(
        _gla_kernel,
        grid=(s // tm,),
        in_specs=[pl.BlockSpec((tm, 128), row), pl.BlockSpec((tm, 128), row), pl.BlockSpec((tm, 128), row),
                  pl.BlockSpec((tm, 256), row), pl.BlockSpec((tm, 256), row),
                  pl.BlockSpec((1, C_V), lambda i: (0, 0))],
        out_specs=pl.BlockSpec((tm, 256), row),
        out_shape=jax.ShapeDtypeStruct((s, 256), F32),
        scratch_shapes=[pltpu.VMEM((C_HEADS, C_K, C_V), F32), pltpu.VMEM((tm, 128), F32)],
        compiler_params=_cparams(("arbitrary",)),
    )(cq, ck, cg, cv, cr, out_gain)


def _rglru_kernel(x_ref, gate_ref, cw_ref, cb_ref, wa_ref, ba_ref, wx_ref, bx_ref, lam_ref, o_ref,
                  xbuf_ref, h_ref):
    tm = x_ref.shape[0]
    pad = 8

    @pl.when(pl.program_id(0) == 0)
    def _():
        xbuf_ref[0:pad, :] = jnp.zeros((pad, D_WIDTH), F32)
        h_ref[...] = jnp.zeros(h_ref.shape, F32)

    xbuf_ref[pad:pad + tm, :] = x_ref[...]
    xc = cb_ref[...] + jnp.zeros((tm, D_WIDTH), F32)
    for w in range(CONV_W):
        lag = CONV_W - 1 - w
        xc = xc + cw_ref[w:w + 1, :] * xbuf_ref[pad - lag:pad - lag + tm, :]
    xbuf_ref[0:pad, :] = xbuf_ref[tm:tm + pad, :]

    xcb = xc.astype(BF16)
    r_gate = _sigmoid(jnp.dot(xcb, wa_ref[...], preferred_element_type=F32) + ba_ref[...])
    i_gate = _sigmoid(jnp.dot(xcb, wx_ref[...], preferred_element_type=F32) + bx_ref[...])
    log_a = LRU_C * r_gate * _log_sigmoid(lam_ref[...])
    a = jnp.exp(log_a)
    b = jnp.sqrt(1.0 - jnp.exp(2.0 * log_a)) * (i_gate * xc)

    rows = lax.broadcasted_iota(jnp.int32, (tm, D_WIDTH), 0)
    step = 1
    while step < tm:
        keep = rows >= step
        a_prev = jnp.where(keep, pltpu.roll(a, step, 0), 1.0)
        b_prev = jnp.where(keep, pltpu.roll(b, step, 0), 0.0)
        b = a * b_prev + b
        a = a * a_prev
        step *= 2
    h = a * h_ref[0:1, :] + b
    h_ref[...] = jnp.broadcast_to(h[tm - 1:tm, :], h_ref.shape)

    g = gate_ref[...]
    gelu = g * (0.5 * (1.0 + jnp.tanh(math.sqrt(2.0 / math.pi) * (g + 0.044715 * (g * g * g)))))
    o_ref[...] = h * gelu


def _rglru(dx, dgate, conv_w, conv_b, wa_bd, b_a, wx_bd, b_x, lam):
    s = dx.shape[0]
    tm = ROW_TILE
    row = lambda i: (i, 0)
    const2 = lambda i: (0, 0)
    vec = pl.BlockSpec((1, D_WIDTH), const2)
    mat = pl.BlockSpec((D_WIDTH, D_WIDTH), const2)
    return pl.pallas_call(
        _rglru_kernel,
        grid=(s // tm,),
        in_specs=[pl.BlockSpec((tm, D_WIDTH), row), pl.BlockSpec((tm, D_WIDTH), row),
                  pl.BlockSpec((CONV_W, D_WIDTH), const2), vec, mat, vec, mat, vec, vec],
        out_specs=pl.BlockSpec((tm, D_WIDTH), row),
        out_shape=jax.ShapeDtypeStruct((s, D_WIDTH), F32),
        scratch_shapes=[pltpu.VMEM((tm + 8, D_WIDTH), F32), pltpu.VMEM((8, D_WIDTH), F32)],
        compiler_params=_cparams(("arbitrary",)),
    )(dx, dgate, conv_w, conv_b, wa_bd, b_a, wx_bd, b_x, lam)


def _outproj_kernel(x_ref, oaT_ref, obT_ref, oc_ref, od_ref, w_ref, g1_ref, ng_ref, sc_ref, sh_ref,
                    rw_ref, rb_ref, xo_ref, h_ref, idx_ref, gate_ref):
    tm = x_ref.shape[0]
    y = jnp.dot(oaT_ref[...].T.astype(BF16), w_ref[0], preferred_element_type=F32)
    y = y + jnp.dot(obT_ref[...].T.astype(BF16), w_ref[1], preferred_element_type=F32)
    y = y + jnp.dot(oc_ref[...].astype(BF16), w_ref[2], preferred_element_type=F32)
    y = y + jnp.dot(od_ref[...].astype(BF16), w_ref[3], preferred_element_type=F32)
    x = x_ref[...] + g1_ref[...] * y
    xo_ref[...] = x
    h = x * lax.rsqrt(jnp.mean(x * x, axis=-1, keepdims=True) + EPS) * ng_ref[...]
    h = h * (1.0 + sc_ref[...]) + sh_ref[...]
    h_ref[...] = h

    logits = jnp.dot(h, rw_ref[...], precision=HI, preferred_element_type=F32) + rb_ref[...]
    e_ids = lax.broadcasted_iota(jnp.int32, (tm, N_EXPERTS), 1)
    lane = lax.broadcasted_iota(jnp.int32, (tm, LANE), 1)
    idx_out = jnp.zeros((tm, LANE), jnp.int32)
    val_out = jnp.zeros((tm, LANE), F32)
    top0 = None
    denom = jnp.zeros((tm, 1), F32)
    for k in range(TOP_K):
        top = jnp.max(logits, axis=-1, keepdims=True)
        arg = jnp.min(jnp.where(logits == top, e_ids, N_EXPERTS), axis=-1, keepdims=True)
        if k == 0:
            top0 = top
        w = jnp.exp(top - top0)
        denom = denom + w
        idx_out = jnp.where(lane == k, arg, idx_out)
        val_out = jnp.where(lane == k, w, val_out)
        logits = jnp.where(e_ids == arg, -jnp.inf, logits)
    idx_ref[...] = idx_out
    gate_ref[...] = val_out / denom


def _outproj(x, oaT, obT, oc, od, w4, g1, ng, sc, sh, rw, rb):
    s, d = x.shape
    tm = ROW_TILE
    row = lambda i: (i, 0)
    col = lambda i: (0, i)
    const2 = lambda i: (0, 0)
    vec = pl.BlockSpec((1, d), const2)
    return pl.pallas_call(
        _outproj_kernel,
        grid=(s // tm,),
        in_specs=[pl.BlockSpec((tm, d), row),
                  pl.BlockSpec((N_GROUP_W, tm), col), pl.BlockSpec((N_GROUP_W, tm), col),
                  pl.BlockSpec((tm, 256), row), pl.BlockSpec((tm, 256), row),
                  pl.BlockSpec((4, N_GROUP_W, d), lambda i: (0, 0, 0)),
                  vec, vec, vec, vec,
                  pl.BlockSpec((d, N_EXPERTS), const2), pl.BlockSpec((1, N_EXPERTS), const2)],
        out_specs=(pl.BlockSpec((tm, d), row), pl.BlockSpec((tm, d), row),
                   pl.BlockSpec((tm, LANE), row), pl.BlockSpec((tm, LANE), row)),
        out_shape=(jax.ShapeDtypeStruct((s, d), F32), jax.ShapeDtypeStruct((s, d), F32),
                   jax.ShapeDtypeStruct((s, LANE), jnp.int32), jax.ShapeDtypeStruct((s, LANE), F32)),
        compiler_params=_cparams(("arbitrary",)),
    )(x, oaT, obT, oc, od, w4, g1, ng, sc, sh, rw, rb)


def _segment_copies(tile, cnt_ref, seg_ref, off_ref, make_copy, start):
    def per_expert(e, carry):
        j = tile * N_EXPERTS + e
        cnt, a, b = cnt_ref[j], seg_ref[j], off_ref[j]
        size = MOE_TILE
        while size >= SUBLANE:
            bit = cnt & size

            @pl.when(bit != 0)
            def _():
                cp = make_copy(pl.multiple_of(a, SUBLANE), pl.multiple_of(b, SUBLANE), size)
                if start:
                    cp.start()
                else:
                    cp.wait()

            a, b = a + bit, b + bit
            size //= 2
        return carry

    lax.fori_loop(0, N_EXPERTS, per_expert, 0)


def _dispatch_kernel(cnt_ref, seg_ref, off_ref, h_ref, slotT_ref, zeros_ref, rows_ref, sorted_ref, sem):
    del zeros_ref
    tile = pl.program_id(0)
    tt = h_ref.shape[0]
    n = sorted_ref.shape[0]
    r_ids = lax.broadcasted_iota(jnp.int32, (n, tt), 0)
    perm = jnp.where(r_ids == slotT_ref[0:1, :], 1.0, 0.0)
    for k in range(1, TOP_K):
        perm = perm + jnp.where(r_ids == slotT_ref[k:k + 1, :], 1.0, 0.0)
    perm = perm.astype(BF16)
    sorted_ref[...] = jnp.dot(perm, h_ref[...].astype(BF16), preferred_element_type=F32)

    def make_copy(a, b, size):
        return pltpu.make_async_copy(sorted_ref.at[pl.ds(a, size)], rows_ref.at[pl.ds(b, size)], sem)

    _segment_copies(tile, cnt_ref, seg_ref, off_ref, make_copy, True)
    _segment_copies(tile, cnt_ref, seg_ref, off_ref, make_copy, False)


def _dispatch(h, slotT, tile_cnt, tile_seg, tile_off, n_rows):
    s, d = h.shape
    tt = MOE_TILE
    grid_spec = pltpu.PrefetchScalarGridSpec(
        num_scalar_prefetch=3,
        grid=(s // tt,),
        in_specs=[pl.BlockSpec((tt, d), lambda i, c, sg, of: (i, 0)),
                  pl.BlockSpec((TOP_K, tt), lambda i, c, sg, of: (0, i)),
                  pl.BlockSpec(memory_space=pl.ANY)],
        out_specs=pl.BlockSpec(memory_space=pl.ANY),
        scratch_shapes=[pltpu.VMEM((MOE_SORT_ROWS, d), F32), pltpu.SemaphoreType.DMA(())],
    )
    return pl.pallas_call(
        _dispatch_kernel,
        grid_spec=grid_spec,
        out_shape=jax.ShapeDtypeStruct((n_rows, d), F32),
        input_output_aliases={5: 0},
        compiler_params=_cparams(("arbitrary",)),
    )(tile_cnt, tile_seg, tile_off, h, slotT, jnp.zeros((n_rows, d), F32))


def _ffn_kernel(be_ref, first_ref, nused_ref, x_ref, wu_ref, bu_ref, wd_ref, bd_ref, y_ref, wu_bf, wd_bf):
    b = pl.program_id(0)

    @pl.when(first_ref[b] == 1)
    def _():
        wu_bf[...] = wu_ref[0].astype(BF16)
        wd_bf[...] = wd_ref[0].astype(BF16)

    @pl.when(b < nused_ref[0])
    def _():
        hu = jnp.dot(x_ref[...].astype(BF16), wu_bf[...], preferred_element_type=F32) + bu_ref[0]
        g = jnp.minimum(hu[:, :D_FF], SWIGLU_LIMIT)
        lin = jnp.clip(hu[:, D_FF:], -SWIGLU_LIMIT, SWIGLU_LIMIT)
        act = (lin + 1.0) * (g * _sigmoid(SWIGLU_ALPHA * g))
        y_ref[...] = jnp.dot(act.astype(BF16), wd_bf[...], preferred_element_type=F32) + bd_ref[0]

    @pl.when(b >= nused_ref[0])
    def _():
        y_ref[...] = jnp.zeros(y_ref.shape, F32)


def _ffn(x_rows, blk_expert, first, n_used, w_up, b_up, w_down, b_down):
    n_rows, d = x_rows.shape
    blk = MOE_BLK
    n_exp, _, f2 = w_up.shape
    grid_spec = pltpu.PrefetchScalarGridSpec(
        num_scalar_prefetch=3,
        grid=(n_rows // blk,),
        in_specs=[pl.BlockSpec((blk, d), lambda b, be, fi, nu: (b, 0)),
                  pl.BlockSpec((1, d, f2), lambda b, be, fi, nu: (be[b], 0, 0)),
                  pl.BlockSpec((1, 1, f2), lambda b, be, fi, nu: (be[b], 0, 0)),
                  pl.BlockSpec((1, D_FF, d), lambda b, be, fi, nu: (be[b], 0, 0)),
                  pl.BlockSpec((1, 1, d), lambda b, be, fi, nu: (be[b], 0, 0))],
        out_specs=pl.BlockSpec((blk, d), lambda b, be, fi, nu: (b, 0)),
        scratch_shapes=[pltpu.VMEM((d, f2), BF16), pltpu.VMEM((D_FF, d), BF16)],
    )
    return pl.pallas_call(
        _ffn_kernel,
        grid_spec=grid_spec,
        out_shape=jax.ShapeDtypeStruct((n_rows, d), F32),
        compiler_params=_cparams(("arbitrary",)),
    )(blk_expert, first, n_used, x_rows, w_up, b_up.reshape(n_exp, 1, f2), w_down, b_down.reshape(n_exp, 1, d))


def _split_bf16(t):
    hi = t.astype(BF16)
    return hi, (t - hi.astype(F32)).astype(BF16)


def _combine_kernel(cnt_ref, seg_ref, off_ref, x_ref, slot_ref, gate_ref, g2_ref, rows_ref, o_ref, buf_ref, sem):
    tile = pl.program_id(0)
    tt = x_ref.shape[0]
    n = buf_ref.shape[0]

    def make_copy(a, b, size):
        return pltpu.make_async_copy(rows_ref.at[pl.ds(b, size)], buf_ref.at[pl.ds(a, size)], sem)

    @pl.when(tile == 0)
    def _():
        buf_ref[...] = jnp.zeros(buf_ref.shape, F32)

    _segment_copies(tile, cnt_ref, seg_ref, off_ref, make_copy, True)
    slots, gates = slot_ref[...], gate_ref[...]
    r_ids = lax.broadcasted_iota(jnp.int32, (tt, n), 1)
    sel = jnp.where(r_ids == slots[:, 0:1], gates[:, 0:1], 0.0)
    for k in range(1, TOP_K):
        sel = sel + jnp.where(r_ids == slots[:, k:k + 1], gates[:, k:k + 1], 0.0)
    sel_hi, sel_lo = _split_bf16(sel)
    _segment_copies(tile, cnt_ref, seg_ref, off_ref, make_copy, False)
    y_hi, y_lo = _split_bf16(buf_ref[...])
    both = jnp.dot(jnp.concatenate([sel_hi, sel_lo], axis=0), y_hi, preferred_element_type=F32)
    y = both[:tt] + both[tt:] + jnp.dot(sel_hi, y_lo, preferred_element_type=F32)
    o_ref[...] = x_ref[...] + g2_ref[...] * y


def _combine(x, slots, gates, g2, y_rows, tile_cnt, tile_seg, tile_off):
    s, d = x.shape
    tt = MOE_TILE
    row = lambda i, c, sg, of: (i, 0)
    grid_spec = pltpu.PrefetchScalarGridSpec(
        num_scalar_prefetch=3,
        grid=(s // tt,),
        in_specs=[pl.BlockSpec((tt, d), row), pl.BlockSpec((tt, LANE), row), pl.BlockSpec((tt, LANE), row),
                  pl.BlockSpec((1, d), lambda i, c, sg, of: (0, 0)),
                  pl.BlockSpec(memory_space=pl.ANY)],
        out_specs=pl.BlockSpec((tt, d), row),
        scratch_shapes=[pltpu.VMEM((MOE_SORT_ROWS, d), F32), pltpu.SemaphoreType.DMA(())],
    )
    return pl.pallas_call(
        _combine_kernel,
        grid_spec=grid_spec,
        out_shape=jax.ShapeDtypeStruct((s, d), F32),
        compiler_params=_cparams(("arbitrary",)),
    )(tile_cnt, tile_seg, tile_off, x, slots, gates, g2, y_rows)


def _moe(x, h, idx_pad, gates_pad, g2, w_up, b_up, w_down, b_down, layer_idx):
    s, d = x.shape
    n = s * TOP_K
    blk = MOE_BLK
    tt = MOE_TILE
    nt = s // tt
    n_blocks = -(-(n + nt * N_EXPERTS * SUBLANE) // blk) + N_EXPERTS
    i32 = jnp.int32
    idx = idx_pad[:, :TOP_K]
    onehot = jnp.sum((idx[:, :, None] == jnp.arange(N_EXPERTS, dtype=i32)).astype(i32), axis=1)
    onehot_t = onehot.reshape(nt, tt, N_EXPERTS)
    rank_in_tile = (jnp.cumsum(onehot_t, axis=1) - onehot_t).reshape(s, N_EXPERTS)
    tile_cnt = (jnp.sum(onehot_t, axis=1) + SUBLANE - 1) // SUBLANE * SUBLANE
    tile_base = jnp.cumsum(tile_cnt, axis=0) - tile_cnt
    tile_seg = jnp.cumsum(tile_cnt, axis=1) - tile_cnt
    counts = jnp.sum(tile_cnt, axis=0)
    padded = (counts + blk - 1) // blk * blk
    pad_ends = jnp.cumsum(padded)
    pad_starts = pad_ends - padded
    tile_off = pad_starts[None, :] + tile_base
    slot = (jnp.take_along_axis(jnp.repeat(tile_seg, tt, axis=0) + rank_in_tile, idx, axis=1)).astype(i32)
    slot_pad = jnp.pad(slot, ((0, 0), (0, LANE - TOP_K)))
    blk_start = jnp.arange(n_blocks, dtype=i32) * blk
    blk_expert = jnp.minimum(jnp.sum((pad_ends[None, :] <= blk_start[:, None]).astype(i32), axis=1),
                             N_EXPERTS - 1).astype(i32)
    first = jnp.concatenate([jnp.ones((1,), i32), (blk_expert[1:] != blk_expert[:-1]).astype(i32)])
    n_used = (pad_ends[-1:] // blk).astype(i32)
    tile_cnt, tile_seg, tile_off = [t.astype(i32).reshape(-1) for t in (tile_cnt, tile_seg, tile_off)]

    x_rows = _dispatch(h, slot.T, tile_cnt, tile_seg, tile_off, n_blocks * blk)
    y_rows = _ffn(x_rows, blk_expert + layer_idx * N_EXPERTS, first, n_used, w_up, b_up, w_down, b_down)
    return _combine(x, slot_pad, gates_pad, g2, y_rows, tile_cnt, tile_seg, tile_off)


def _block_diag(w):
    n, d, _ = w.shape
    eye = jnp.eye(n, dtype=w.dtype)
    return (eye[:, None, :, None] * w[:, :, None, :]).reshape(n * d, n * d)


def kernel(x, c, ada_w, ada_b, norm1_g, norm2_g, w_in, w_out, a_q_gain, a_k_gain, a_lam_q1, a_lam_k1, a_lam_q2, a_lam_k2, a_out_gain, b_q_gain, b_k_gain, c_w_g2, c_b_g, c_out_gain, d_conv_w, d_conv_b, d_w_a, d_b_a, d_w_x, d_b_x, d_lambda, router_w, router_b, exp_w_up, exp_b_up, exp_w_down, exp_b_down):
    bsz, s, d = x.shape
    assert bsz == 1 and d == D_MODEL and s % ROW_TILE == 0
    depth = ada_w.shape[0]
    mod = _modulation(c, ada_w, ada_b)
    xs = x.reshape(s, d)
    w_up_all = exp_w_up.reshape(depth * N_EXPERTS, d, 2 * D_FF)
    b_up_all = exp_b_up.reshape(depth * N_EXPERTS, 2 * D_FF)
    w_down_all = exp_w_down.reshape(depth * N_EXPERTS, D_FF, d)
    b_down_all = exp_b_down.reshape(depth * N_EXPERTS, d)
    for l in range(depth):
        sh1, sc1, g1, sh2, sc2, g2 = [mod[l, i * d:(i + 1) * d].reshape(1, d) for i in range(6)]
        w = w_in[l]
        w_cat = jnp.concatenate([w[:, :2048], w[:, 2064:2832], w[:, 2048:2064],
                                 jnp.zeros((d, W_IN_PAD - 2832), w.dtype)], axis=1).astype(BF16)
        wg2p = jnp.zeros((LANE, LANE), F32).at[:GLA_RANK, :].set(c_w_g2[l])
        (aqT, ak, avT, bqT, bk, bvT, bkm, cq, ck, cv, cr, cg, dx, dgate) = _inproj(
            xs, norm1_g[l].reshape(1, d), sc1, sh1, w_cat,
            jnp.tile(a_q_gain[l], 2 * A_HEADS).reshape(1, -1), jnp.tile(a_k_gain[l], 2 * A_HEADS).reshape(1, -1),
            jnp.tile(b_q_gain[l], B_HEADS).reshape(1, -1), jnp.tile(b_k_gain[l], B_HEADS).reshape(1, -1),
            wg2p, c_b_g[l].reshape(1, -1))
        o_aT = _diff_attention(aqT, ak, avT, a_lam_q1[l].reshape(1, -1), a_lam_k1[l].reshape(1, -1),
                               a_lam_q2[l].reshape(1, -1), a_lam_k2[l].reshape(1, -1),
                               a_out_gain[l].reshape(-1, 1), l)
        km = bkm.reshape(s // MOBA_BLOCK, B_HEADS, B_DIM).transpose(1, 0, 2)
        o_bT = _moba(bqT, bk, bvT, km)
        o_c = _gla(cq, ck, cg, cv, cr, c_out_gain[l].reshape(1, -1))
        o_d = _rglru(dx, dgate, d_conv_w[l], d_conv_b[l].reshape(1, -1),
                     _block_diag(d_w_a[l]).astype(BF16), d_b_a[l].reshape(1, -1),
                     _block_diag(d_w_x[l]).astype(BF16), d_b_x[l].reshape(1, -1), d_lambda[l].reshape(1, -1))
        w4 = w_out[l].astype(BF16).reshape(4, N_GROUP_W, d)
        xs, h2, idx_pad, gates_pad = _outproj(xs, o_aT, o_bT, o_c, o_d, w4, g1, norm2_g[l].reshape(1, d),
                                              sc2, sh2, router_w[l], router_b[l].reshape(1, -1))
        xs = _moe(xs, h2, idx_pad, gates_pad, g2, w_up_all, b_up_all, w_down_all, b_down_all, l)
    return xs.reshape(bsz, s, d)
```

```python
import functools
import math

import jax
import jax.numpy as jnp
from jax import lax
from jax.experimental import pallas as pl
from jax.experimental.pallas import tpu as pltpu

F32 = jnp.float32
BF16 = jnp.bfloat16
HI = lax.Precision.HIGHEST

EPS = 1e-6
NEG = -1e30
M_INIT = -1e30
MASKED = -2e30
LOG2E = 1.4426950408889634

D_MODEL = 1024
N_GROUP_W = 256
A_HEADS, A_QK, A_V = 4, 32, 64
B_HEADS, B_DIM = 4, 64
MOBA_BLOCK, MOBA_TOPK = 256, 3
C_HEADS, C_K, C_V = 4, 32, 64
GLA_RANK, GLA_TAU, GLA_CHUNK = 16, 16.0, 64
D_WIDTH, D_BLOCKS, CONV_W, LRU_C = 256, 4, 4, 8.0
N_EXPERTS, TOP_K, D_FF = 32, 4, 1024
SWIGLU_ALPHA, SWIGLU_LIMIT = 1.702, 7.0

LANE = 128
SUBLANE = 8
W_IN_PAD = 2944

ROW_TILE = 512
ATT_TILE = 256
KEY_TILE = 2 * ATT_TILE
SUM_ROWS = 16
MOE_BLK = 256
MOE_TILE = 256
MOE_SORT_ROWS = MOE_TILE * TOP_K + N_EXPERTS * SUBLANE
VMEM_LIMIT = 56 * 1024 * 1024


def _cparams(sem):
    return pltpu.CompilerParams(dimension_semantics=sem, vmem_limit_bytes=VMEM_LIMIT)


def _log_sigmoid(z):
    return jnp.minimum(z, 0.0) - jnp.log1p(jnp.exp(-jnp.abs(z)))


def _sigmoid(z):
    return 1.0 / (1.0 + jnp.exp(-z))


def _group_rms(t, group):
    n = t.shape[-1]
    shift = int(math.log2(group))
    r = lax.broadcasted_iota(jnp.int32, (n, n), 0) >> shift
    c = lax.broadcasted_iota(jnp.int32, (n, n), 1) >> shift
    bd = jnp.where(r == c, 1.0, 0.0).astype(F32)
    ss = jnp.dot(t * t, bd, precision=HI, preferred_element_type=F32)
    return t * lax.rsqrt(ss * (1.0 / group) + EPS)


def _mod_kernel(c_ref, w_ref, b_ref, o_ref):
    c = c_ref[...]
    cond = c * _sigmoid(c)
    cond8 = jnp.broadcast_to(cond, (8, cond.shape[-1]))
    o_ref[0] = jnp.dot(cond8, w_ref[0], precision=HI, preferred_element_type=F32) + b_ref[0]


def _modulation(c, ada_w, ada_b):
    depth, d, n = ada_w.shape
    tn = 1536
    out = pl.pallas_call(
        _mod_kernel,
        grid=(depth, n // tn),
        in_specs=[pl.BlockSpec((1, d), lambda l, j: (0, 0)),
                  pl.BlockSpec((1, d, tn), lambda l, j: (l, 0, j)),
                  pl.BlockSpec((1, 1, tn), lambda l, j: (l, 0, j))],
        out_specs=pl.BlockSpec((1, 8, tn), lambda l, j: (l, 0, j)),
        out_shape=jax.ShapeDtypeStruct((depth, 8, n), F32),
        compiler_params=_cparams(("arbitrary", "arbitrary")),
    )(c, ada_w, ada_b.reshape(depth, 1, n))
    return out[:, 0, :]


def _inproj_kernel(x_ref, ng_ref, sc_ref, sh_ref, w_ref, aqg_ref, akg_ref, bqg_ref, bkg_ref, wg2_ref, bg_ref,
                   aqT_ref, ak_ref, avT_ref, bqT_ref, bk_ref, bvT_ref, bkm_ref,
                   cq_ref, ck_ref, cv_ref, cr_ref, cg_ref, dx_ref, dg_ref):
    x = x_ref[...]
    tm = x.shape[0]
    h = x * lax.rsqrt(jnp.mean(x * x, axis=-1, keepdims=True) + EPS) * ng_ref[...]
    h = h * (1.0 + sc_ref[...]) + sh_ref[...]
    proj = jnp.dot(h.astype(BF16), w_ref[...], preferred_element_type=F32)

    aq = _group_rms(proj[:, 0:256], A_QK) * aqg_ref[...] * (A_QK ** -0.5 * LOG2E)
    aqT_ref[...] = aq.T.astype(BF16)
    ak = _group_rms(proj[:, 256:512], A_QK) * akg_ref[...]
    ak_ref[...] = ak.astype(BF16)
    avT_ref[...] = proj[:, 512:768].T.astype(BF16)

    bq = _group_rms(proj[:, 768:1024], B_DIM) * bqg_ref[...] * (B_DIM ** -0.5 * LOG2E)
    bqT_ref[...] = bq.T.astype(BF16)
    bk = _group_rms(proj[:, 1024:1280], B_DIM) * bkg_ref[...]
    bk_ref[...] = bk.astype(BF16)
    bvT_ref[...] = proj[:, 1280:1536].T.astype(BF16)
    bkm_ref[0] = jnp.mean(bk.reshape(tm // MOBA_BLOCK, MOBA_BLOCK, N_GROUP_W), axis=1)

    cq_ref[...] = proj[:, 1536:1664] * (C_K ** -0.5)
    ck_ref[...] = proj[:, 1664:1792]
    cv_ref[...] = proj[:, 1792:2048]
    cr_ref[...] = proj[:, 2048:2304]
    z = jnp.dot(proj[:, 2816:2944], wg2_ref[...], precision=HI, preferred_element_type=F32) + bg_ref[...]
    cg_ref[...] = _log_sigmoid(z) * (1.0 / GLA_TAU)

    dx_ref[...] = proj[:, 2304:2560]
    dg_ref[...] = proj[:, 2560:2816]


def _inproj(x, ng, sc, sh, w_cat, aqg, akg, bqg, bkg, wg2p, bg):
    s, d = x.shape
    tm = ROW_TILE
    nt = s // tm
    row = lambda i: (i, 0)
    const2 = lambda i: (0, 0)
    vec = pl.BlockSpec((1, d), const2)
    gvec = pl.BlockSpec((1, N_GROUP_W), const2)
    out_shape = (
        jax.ShapeDtypeStruct((N_GROUP_W, s), BF16),
        jax.ShapeDtypeStruct((s, N_GROUP_W), BF16),
        jax.ShapeDtypeStruct((N_GROUP_W, s), BF16),
        jax.ShapeDtypeStruct((N_GROUP_W, s), BF16),
        jax.ShapeDtypeStruct((s, N_GROUP_W), BF16),
        jax.ShapeDtypeStruct((N_GROUP_W, s), BF16),
        jax.ShapeDtypeStruct((nt, tm // MOBA_BLOCK, N_GROUP_W), F32),
        jax.ShapeDtypeStruct((s, 128), F32),
        jax.ShapeDtypeStruct((s, 128), F32),
        jax.ShapeDtypeStruct((s, 256), F32),
        jax.ShapeDtypeStruct((s, 256), F32),
        jax.ShapeDtypeStruct((s, 128), F32),
        jax.ShapeDtypeStruct((s, 256), F32),
        jax.ShapeDtypeStruct((s, 256), F32),
    )
    out_specs = (
        pl.BlockSpec((N_GROUP_W, tm), lambda i: (0, i)),
        pl.BlockSpec((tm, N_GROUP_W), row),
        pl.BlockSpec((N_GROUP_W, tm), lambda i: (0, i)),
        pl.BlockSpec((N_GROUP_W, tm), lambda i: (0, i)),
        pl.BlockSpec((tm, N_GROUP_W), row),
        pl.BlockSpec((N_GROUP_W, tm), lambda i: (0, i)),
        pl.BlockSpec((1, tm // MOBA_BLOCK, N_GROUP_W), lambda i: (i, 0, 0)),
        pl.BlockSpec((tm, 128), row),
        pl.BlockSpec((tm, 128), row),
        pl.BlockSpec((tm, 256), row),
        pl.BlockSpec((tm, 256), row),
        pl.BlockSpec((tm, 128), row),
        pl.BlockSpec((tm, 256), row),
        pl.BlockSpec((tm, 256), row),
    )
    return pl.pallas_call(
        _inproj_kernel,
        grid=(nt,),
        in_specs=[pl.BlockSpec((tm, d), row), vec, vec, vec,
                  pl.BlockSpec((d, W_IN_PAD), const2),
                  gvec, gvec, gvec, gvec,
                  pl.BlockSpec((LANE, LANE), const2),
                  pl.BlockSpec((1, LANE), const2)],
        out_specs=out_specs,
        out_shape=out_shape,
        compiler_params=_cparams(("arbitrary",)),
    )(x, ng, sc, sh, w_cat, aqg, akg, bqg, bkg, wg2p, bg)


def _pad_query_rows(qT_ref, qpad_ref, width):
    per = LANE // width
    rows = lax.broadcasted_iota(jnp.int32, qpad_ref.shape[1:], 0)
    for g in range(qpad_ref.shape[0]):
        blk = qT_ref[LANE * (g // per):LANE * (g // per + 1), :]
        lo = width * (g % per)
        qpad_ref[g] = jnp.where((rows >= lo) & (rows < lo + width), blk, jnp.zeros_like(blk))


def _with_sum_rows(vT):
    return jnp.concatenate([vT, jnp.ones((SUM_ROWS, vT.shape[1]), vT.dtype)], axis=0)


def _softmax_tiles(k_slabs, qpad_ref, v_tiles, adjust, m_ref, acc_ref):
    n = len(k_slabs)
    scores = [jnp.dot(k_slabs[g], qpad_ref[g], preferred_element_type=F32) for g in range(n)]
    probs, alphas = [], []
    for g in range(n):
        sT = adjust(g, scores[g])
        m_old = m_ref[g]
        m_new = jnp.maximum(m_old, jnp.max(sT, axis=0, keepdims=True))
        alphas.append(jnp.exp2(m_old - m_new))
        probs.append(jnp.exp2(sT - m_new).astype(BF16))
        m_ref[g] = m_new
    pv = [jnp.dot(v_tiles[g], probs[g], preferred_element_type=F32) for g in range(n)]
    for g in range(n):
        acc_ref[g] = alphas[g] * acc_ref[g] + pv[g]


def _diffattn_kernel(qT_ref, k_ref, vT_ref, lq1_ref, lk1_ref, lq2_ref, lk2_ref, og_ref, o_ref,
                     m_ref, acc_ref, qpad_ref, *, lam_init):
    t, tk = ATT_TILE, KEY_TILE
    n_maps = 2 * A_HEADS
    iq = pl.program_id(0)
    lam = (jnp.exp(jnp.sum(lq1_ref[...] * lk1_ref[...], axis=-1, keepdims=True))
           - jnp.exp(jnp.sum(lq2_ref[...] * lk2_ref[...], axis=-1, keepdims=True)) + lam_init)
    m_ref[...] = jnp.full(m_ref.shape, M_INIT, F32)
    acc_ref[...] = jnp.zeros(acc_ref.shape, F32)
    _pad_query_rows(qT_ref, qpad_ref, A_QK)

    def tile(ik, adjust):
        keys = pl.ds(pl.multiple_of(ik * tk, tk), tk)
        k_slabs = [k_ref[keys, LANE * (g // 4):LANE * (g // 4 + 1)] for g in range(n_maps)]
        v_heads = [_with_sum_rows(vT_ref[A_V * hd:A_V * (hd + 1), keys]) for hd in range(A_HEADS)]
        _softmax_tiles(k_slabs, qpad_ref, [v_heads[g // 2] for g in range(n_maps)], adjust, m_ref, acc_ref)

    def body(ik, carry):
        tile(ik, lambda g, sT: sT)
        return carry

    n_full = iq // 2
    lax.fori_loop(0, n_full, body, 0)
    causal = (lax.broadcasted_iota(jnp.int32, (tk, t), 0) - lax.broadcasted_iota(jnp.int32, (tk, t), 1)
              <= (iq - 2 * n_full) * t)
    tile(n_full, lambda g, sT: jnp.where(causal, sT, MASKED))

    for hd in range(A_HEADS):
        a1, a2 = acc_ref[2 * hd], acc_ref[2 * hd + 1]
        o = a1[:A_V] / a1[A_V:A_V + 1] - lam * (a2[:A_V] / a2[A_V:A_V + 1])
        o = o * lax.rsqrt(jnp.mean(o * o, axis=0, keepdims=True) + EPS) * og_ref[...]
        o_ref[A_V * hd:A_V * (hd + 1), :] = o * (1.0 - lam_init)


def _diff_attention(aqT, ak, avT, lq1, lk1, lq2, lk2, out_gain_col, layer_idx):
    s = aqT.shape[1]
    t = ATT_TILE
    n_maps = 2 * A_HEADS
    lam_init = 0.8 - 0.6 * math.exp(-0.3 * layer_idx)
    vec = pl.BlockSpec((1, A_QK), lambda i: (0, 0))
    return pl.pallas_call(
        functools.partial(_diffattn_kernel, lam_init=lam_init),
        grid=(s // t,),
        in_specs=[pl.BlockSpec((N_GROUP_W, t), lambda i: (0, i)),
                  pl.BlockSpec((s, N_GROUP_W), lambda i: (0, 0)),
                  pl.BlockSpec((N_GROUP_W, s), lambda i: (0, 0)),
                  vec, vec, vec, vec,
                  pl.BlockSpec((A_V, 1), lambda i: (0, 0))],
        out_specs=pl.BlockSpec((N_GROUP_W, t), lambda i: (0, i)),
        out_shape=jax.ShapeDtypeStruct((N_GROUP_W, s), F32),
        scratch_shapes=[pltpu.VMEM((n_maps, 1, t), F32), pltpu.VMEM((n_maps, A_V + SUM_ROWS, t), F32),
                        pltpu.VMEM((n_maps, LANE, t), BF16)],
        compiler_params=_cparams(("arbitrary",)),
    )(aqT, ak, avT, lq1, lk1, lq2, lk2, out_gain_col)


def _moba_kernel(qT_ref, k_ref, vT_ref, km_ref, o_ref, m_ref, acc_ref, bias_ref, qpad_ref):
    t, tk = ATT_TILE, KEY_TILE
    own = pl.program_id(0)
    nb = km_ref.shape[1]
    m_ref[...] = jnp.full(m_ref.shape, M_INIT, F32)
    acc_ref[...] = jnp.zeros(acc_ref.shape, F32)
    _pad_query_rows(qT_ref, qpad_ref, B_DIM)

    blk = lax.broadcasted_iota(jnp.int32, (nb, t), 0)
    earlier = blk < own
    for hd in range(B_HEADS):
        qT = qT_ref[B_DIM * hd:B_DIM * (hd + 1), :].astype(F32)
        gate = jnp.dot(km_ref[hd], qT, precision=HI, preferred_element_type=F32)
        gate = jnp.where(earlier, gate, NEG)
        sel_bias = jnp.full((nb, t), MASKED, F32)
        for _ in range(MOBA_TOPK):
            top = jnp.max(gate, axis=0, keepdims=True)
            pick = blk == jnp.min(jnp.where(gate == top, blk, nb), axis=0, keepdims=True)
            sel_bias = jnp.where(pick, jnp.where(earlier, 0.0, MASKED), sel_bias)
            gate = jnp.where(pick, -jnp.inf, gate)
        bias_ref[hd] = sel_bias

    def tile(j, adjust):
        keys = pl.ds(pl.multiple_of(j * tk, tk), tk)
        k_slabs = [k_ref[keys, LANE * (hd // 2):LANE * (hd // 2 + 1)] for hd in range(B_HEADS)]
        v_tiles = [_with_sum_rows(vT_ref[B_DIM * hd:B_DIM * (hd + 1), keys]) for hd in range(B_HEADS)]
        _softmax_tiles(k_slabs, qpad_ref, v_tiles, adjust, m_ref, acc_ref)

    def block_bias(hd, b):
        return bias_ref[hd, pl.ds(b, 1), :]

    def body(j, carry):
        tile(j, lambda hd, sT: jnp.concatenate([sT[:t] + block_bias(hd, 2 * j),
                                                sT[t:] + block_bias(hd, 2 * j + 1)], axis=0))
        return carry

    n_full = own // 2
    lax.fori_loop(0, n_full, body, 0)

    causal = (lax.broadcasted_iota(jnp.int32, (t, t), 0) <= lax.broadcasted_iota(jnp.int32, (t, t), 1))

    def last(hd, sT):
        halves = []
        for half in range(2):
            b = 2 * n_full + half
            sh = sT[t * half:t * (half + 1)]
            other = sh + jnp.where(b < own, block_bias(hd, b), MASKED)
            halves.append(jnp.where(b == own, jnp.where(causal, sh, MASKED), other))
        return jnp.concatenate(halves, axis=0)

    tile(n_full, last)
    for hd in range(B_HEADS):
        a = acc_ref[hd]
        o_ref[B_DIM * hd:B_DIM * (hd + 1), :] = a[:B_DIM] / a[B_DIM:B_DIM + 1]


def _moba(bqT, bk, bvT, km):
    s = bqT.shape[1]
    t = ATT_TILE
    nb = km.shape[1]
    return pl.pallas_call(
        _moba_kernel,
        grid=(s // t,),
        in_specs=[pl.BlockSpec((N_GROUP_W, t), lambda i: (0, i)),
                  pl.BlockSpec((s, N_GROUP_W), lambda i: (0, 0)),
                  pl.BlockSpec((N_GROUP_W, s), lambda i: (0, 0)),
                  pl.BlockSpec((B_HEADS, nb, B_DIM), lambda i: (0, 0, 0))],
        out_specs=pl.BlockSpec((N_GROUP_W, t), lambda i: (0, i)),
        out_shape=jax.ShapeDtypeStruct((N_GROUP_W, s), F32),
        scratch_shapes=[pltpu.VMEM((B_HEADS, 1, t), F32),
                        pltpu.VMEM((B_HEADS, B_DIM + SUM_ROWS, t), F32), pltpu.VMEM((B_HEADS, nb, t), F32),
                        pltpu.VMEM((B_HEADS, LANE, t), BF16)],
        compiler_params=_cparams(("arbitrary",)),
    )(bqT, bk, bvT, km)


def _gla_kernel(q_ref, k_ref, g_ref, v_ref, r_ref, og_ref, o_ref, state_ref, b_ref):
    tm = q_ref.shape[0]
    c = GLA_CHUNK
    shift = int(math.log2(c))

    @pl.when(pl.program_id(0) == 0)
    def _():
        state_ref[...] = jnp.zeros(state_ref.shape, F32)

    ri = lax.broadcasted_iota(jnp.int32, (tm, tm), 0)
    ci = lax.broadcasted_iota(jnp.int32, (tm, tm), 1)
    tri = jnp.where((ri >= ci) & ((ri >> shift) == (ci >> shift)), 1.0, 0.0).astype(F32)
    b_ref[...] = jnp.dot(tri, g_ref[...], precision=HI, preferred_element_type=F32)

    causal = (lax.broadcasted_iota(jnp.int32, (c, c), 0) >= lax.broadcasted_iota(jnp.int32, (c, c), 1))
    og = og_ref[...]

    def chunk(ic, carry):
        rows = pl.ds(pl.multiple_of(ic * c, c), c)
        qc = q_ref[rows, :]
        bc = b_ref[rows, :]
        kT = k_ref[rows, :].T
        bT = bc.T
        vc = v_ref[rows, :]
        rc = r_ref[rows, :]
        outs = []
        for hd in range(C_HEADS):
            ks = slice(C_K * hd, C_K * (hd + 1))
            vs = slice(C_V * hd, C_V * (hd + 1))
            qh, bh, kTh, bTh, vh = qc[:, ks], bc[:, ks], kT[ks, :], bT[ks, :], vc[:, vs].astype(BF16)
            att = jnp.zeros((c, c), F32)
            for d in range(C_K):
                diff = bh[:, d:d + 1] - bTh[d:d + 1, :]
                att = att + (qh[:, d:d + 1] * kTh[d:d + 1, :]) * jnp.exp(jnp.where(causal, diff, NEG))
            state = state_ref[hd]
            o = (jnp.dot((qh * jnp.exp(bh)).astype(BF16), state.astype(BF16), preferred_element_type=F32)
                 + jnp.dot(att.astype(BF16), vh, preferred_element_type=F32))
            b_last = bTh[:, c - 1:c]
            k_dec = kTh * jnp.exp(b_last - bTh)
            state_ref[hd] = jnp.exp(b_last) * state + jnp.dot(k_dec.astype(BF16), vh, preferred_element_type=F32)
            o = o * lax.rsqrt(jnp.mean(o * o, axis=-1, keepdims=True) + EPS) * og
            rh = rc[:, vs]
            outs.append(o * (rh * _sigmoid(rh)))
        o_ref[rows, :] = jnp.concatenate(outs, axis=1)
        return carry

    lax.fori_loop(0, tm // c, chunk, 0)


def _gla(cq, ck, cg, cv, cr, out_gain):
    s = cq.shape[0]
    tm = ROW_TILE
    row = lambda i: (i, 0)
    return pl.pallas_call(
        _gla_kernel,
        grid=(s // tm,),
        in_specs=[pl.BlockSpec((tm, 128), row), pl.BlockSpec((tm, 128), row), pl.BlockSpec((tm, 128), row),
                  pl.BlockSpec((tm, 256), row), pl.BlockSpec((tm, 256), row),
                  pl.BlockSpec((1, C_V), lambda i: (0, 0))],
        out_specs=pl.BlockSpec((tm, 256), row),
        out_shape=jax.ShapeDtypeStruct((s, 256), F32),
        scratch_shapes=[pltpu.VMEM((C_HEADS, C_K, C_V), F32), pltpu.VMEM((tm, 128), F32)],
        compiler_params=_cparams(("arbitrary",)),
    )(cq, ck, cg, cv, cr, out_gain)


def _rglru_kernel(x_ref, gate_ref, cw_ref, cb_ref, wa_ref, ba_ref, wx_ref, bx_ref, lam_ref, o_ref,
                  xbuf_ref, h_ref):
    tm = x_ref.shape[0]
    pad = 8

    @pl.when(pl.program_id(0) == 0)
    def _():
        xbuf_ref[0:pad, :] = jnp.zeros((pad, D_WIDTH), F32)
        h_ref[...] = jnp.zeros(h_ref.shape, F32)

    xbuf_ref[pad:pad + tm, :] = x_ref[...]
    xc = cb_ref[...] + jnp.zeros((tm, D_WIDTH), F32)
    for w in range(CONV_W):
        lag = CONV_W - 1 - w
        xc = xc + cw_ref[w:w + 1, :] * xbuf_ref[pad - lag:pad - lag + tm, :]
    xbuf_ref[0:pad, :] = xbuf_ref[tm:tm + pad, :]

    xcb = xc.astype(BF16)
    r_gate = _sigmoid(jnp.dot(xcb, wa_ref[...], preferred_element_type=F32) + ba_ref[...])
    i_gate = _sigmoid(jnp.dot(xcb, wx_ref[...], preferred_element_type=F32) + bx_ref[...])
    log_a = LRU_C * r_gate * _log_sigmoid(lam_ref[...])
    a = jnp.exp(log_a)
    b = jnp.sqrt(1.0 - jnp.exp(2.0 * log_a)) * (i_gate * xc)

    rows = lax.broadcasted_iota(jnp.int32, (tm, D_WIDTH), 0)
    step = 1
    while step < tm:
        keep = rows >= step
        a_prev = jnp.where(keep, pltpu.roll(a, step, 0), 1.0)
        b_prev = jnp.where(keep, pltpu.roll(b, step, 0), 0.0)
        b = a * b_prev + b
        a = a * a_prev
        step *= 2
    h = a * h_ref[0:1, :] + b
    h_ref[...] = jnp.broadcast_to(h[tm - 1:tm, :], h_ref.shape)

    g = gate_ref[...]
    gelu = g * (0.5 * (1.0 + jnp.tanh(math.sqrt(2.0 / math.pi) * (g + 0.044715 * (g * g * g)))))
    o_ref[...] = h * gelu


def _rglru(dx, dgate, conv_w, conv_b, wa_bd, b_a, wx_bd, b_x, lam):
    s = dx.shape[0]
    tm = ROW_TILE
    row = lambda i: (i, 0)
    const2 = lambda i: (0, 0)
    vec = pl.BlockSpec((1, D_WIDTH), const2)
    mat = pl.BlockSpec((D_WIDTH, D_WIDTH), const2)
    return pl.pallas_call(
        _rglru_kernel,
        grid=(s // tm,),
        in_specs=[pl.BlockSpec((tm, D_WIDTH), row), pl.BlockSpec((tm, D_WIDTH), row),
                  pl.BlockSpec((CONV_W, D_WIDTH), const2), vec, mat, vec, mat, vec, vec],
        out_specs=pl.BlockSpec((tm, D_WIDTH), row),
        out_shape=jax.ShapeDtypeStruct((s, D_WIDTH), F32),
        scratch_shapes=[pltpu.VMEM((tm + 8, D_WIDTH), F32), pltpu.VMEM((8, D_WIDTH), F32)],
        compiler_params=_cparams(("arbitrary",)),
    )(dx, dgate, conv_w, conv_b, wa_bd, b_a, wx_bd, b_x, lam)


def _outproj_kernel(x_ref, oaT_ref, obT_ref, oc_ref, od_ref, w_ref, g1_ref, ng_ref, sc_ref, sh_ref,
                    rw_ref, rb_ref, xo_ref, h_ref, idx_ref, gate_ref):
    tm = x_ref.shape[0]
    y = jnp.dot(oaT_ref[...].T.astype(BF16), w_ref[0], preferred_element_type=F32)
    y = y + jnp.dot(obT_ref[...].T.astype(BF16), w_ref[1], preferred_element_type=F32)
    y = y + jnp.dot(oc_ref[...].astype(BF16), w_ref[2], preferred_element_type=F32)
    y = y + jnp.dot(od_ref[...].astype(BF16), w_ref[3], preferred_element_type=F32)
    x = x_ref[...] + g1_ref[...] * y
    xo_ref[...] = x
    h = x * lax.rsqrt(jnp.mean(x * x, axis=-1, keepdims=True) + EPS) * ng_ref[...]
    h = h * (1.0 + sc_ref[...]) + sh_ref[...]
    h_ref[...] = h

    logits = jnp.dot(h, rw_ref[...], precision=HI, preferred_element_type=F32) + rb_ref[...]
    e_ids = lax.broadcasted_iota(jnp.int32, (tm, N_EXPERTS), 1)
    lane = lax.broadcasted_iota(jnp.int32, (tm, LANE), 1)
    idx_out = jnp.zeros((tm, LANE), jnp.int32)
    val_out = jnp.zeros((tm, LANE), F32)
    top0 = None
    denom = jnp.zeros((tm, 1), F32)
    for k in range(TOP_K):
        top = jnp.max(logits, axis=-1, keepdims=True)
        arg = jnp.min(jnp.where(logits == top, e_ids, N_EXPERTS), axis=-1, keepdims=True)
        if k == 0:
            top0 = top
        w = jnp.exp(top - top0)
        denom = denom + w
        idx_out = jnp.where(lane == k, arg, idx_out)
        val_out = jnp.where(lane == k, w, val_out)
        logits = jnp.where(e_ids == arg, -jnp.inf, logits)
    idx_ref[...] = idx_out
    gate_ref[...] = val_out / denom


def _outproj(x, oaT, obT, oc, od, w4, g1, ng, sc, sh, rw, rb):
    s, d = x.shape
    tm = ROW_TILE
    row = lambda i: (i, 0)
    col = lambda i: (0, i)
    const2 = lambda i: (0, 0)
    vec = pl.BlockSpec((1, d), const2)
    return pl.pallas_call(
        _outproj_kernel,
        grid=(s // tm,),
        in_specs=[pl.BlockSpec((tm, d), row),
                  pl.BlockSpec((N_GROUP_W, tm), col), pl.BlockSpec((N_GROUP_W, tm), col),
                  pl.BlockSpec((tm, 256), row), pl.BlockSpec((tm, 256), row),
                  pl.BlockSpec((4, N_GROUP_W, d), lambda i: (0, 0, 0)),
                  vec, vec, vec, vec,
                  pl.BlockSpec((d, N_EXPERTS), const2), pl.BlockSpec((1, N_EXPERTS), const2)],
        out_specs=(pl.BlockSpec((tm, d), row), pl.BlockSpec((tm, d), row),
                   pl.BlockSpec((tm, LANE), row), pl.BlockSpec((tm, LANE), row)),
        out_shape=(jax.ShapeDtypeStruct((s, d), F32), jax.ShapeDtypeStruct((s, d), F32),
                   jax.ShapeDtypeStruct((s, LANE), jnp.int32), jax.ShapeDtypeStruct((s, LANE), F32)),
        compiler_params=_cparams(("arbitrary",)),
    )(x, oaT, obT, oc, od, w4, g1, ng, sc, sh, rw, rb)


def _segment_copies(tile, cnt_ref, seg_ref, off_ref, make_copy, start):
    def per_expert(e, carry):
        j = tile * N_EXPERTS + e
        cnt, a, b = cnt_ref[j], seg_ref[j], off_ref[j]
        size = MOE_TILE
        while size >= SUBLANE:
            bit = cnt & size

            @pl.when(bit != 0)
            def _():
                cp = make_copy(pl.multiple_of(a, SUBLANE), pl.multiple_of(b, SUBLANE), size)
                if start:
                    cp.start()
                else:
                    cp.wait()

            a, b = a + bit, b + bit
            size //= 2
        return carry

    lax.fori_loop(0, N_EXPERTS, per_expert, 0)


def _dispatch_kernel(cnt_ref, seg_ref, off_ref, h_ref, slotT_ref, zeros_ref, rows_ref, sorted_ref, sem):
    del zeros_ref
    tile = pl.program_id(0)
    tt = h_ref.shape[0]
    n = sorted_ref.shape[0]
    r_ids = lax.broadcasted_iota(jnp.int32, (n, tt), 0)
    perm = jnp.where(r_ids == slotT_ref[0:1, :], 1.0, 0.0)
    for k in range(1, TOP_K):
        perm = perm + jnp.where(r_ids == slotT_ref[k:k + 1, :], 1.0, 0.0)
    perm = perm.astype(BF16)
    sorted_ref[...] = jnp.dot(perm, h_ref[...].astype(BF16), preferred_element_type=F32)

    def make_copy(a, b, size):
        return pltpu.make_async_copy(sorted_ref.at[pl.ds(a, size)], rows_ref.at[pl.ds(b, size)], sem)

    _segment_copies(tile, cnt_ref, seg_ref, off_ref, make_copy, True)
    _segment_copies(tile, cnt_ref, seg_ref, off_ref, make_copy, False)


def _dispatch(h, slotT, tile_cnt, tile_seg, tile_off, n_rows):
    s, d = h.shape
    tt = MOE_TILE
    grid_spec = pltpu.PrefetchScalarGridSpec(
        num_scalar_prefetch=3,
        grid=(s // tt,),
        in_specs=[pl.BlockSpec((tt, d), lambda i, c, sg, of: (i, 0)),
                  pl.BlockSpec((TOP_K, tt), lambda i, c, sg, of: (0, i)),
                  pl.BlockSpec(memory_space=pl.ANY)],
        out_specs=pl.BlockSpec(memory_space=pl.ANY),
        scratch_shapes=[pltpu.VMEM((MOE_SORT_ROWS, d), F32), pltpu.SemaphoreType.DMA(())],
    )
    return pl.pallas_call(
        _dispatch_kernel,
        grid_spec=grid_spec,
        out_shape=jax.ShapeDtypeStruct((n_rows, d), F32),
        input_output_aliases={5: 0},
        compiler_params=_cparams(("arbitrary",)),
    )(tile_cnt, tile_seg, tile_off, h, slotT, jnp.zeros((n_rows, d), F32))


def _ffn_kernel(be_ref, first_ref, nused_ref, x_ref, wu_ref, bu_ref, wd_ref, bd_ref, y_ref, wu_bf, wd_bf):
    b = pl.program_id(0)

    @pl.when(first_ref[b] == 1)
    def _():
        wu_bf[...] = wu_ref[0].astype(BF16)
        wd_bf[...] = wd_ref[0].astype(BF16)

    @pl.when(b < nused_ref[0])
    def _():
        hu = jnp.dot(x_ref[...].astype(BF16), wu_bf[...], preferred_element_type=F32) + bu_ref[0]
        g = jnp.minimum(hu[:, :D_FF], SWIGLU_LIMIT)
        lin = jnp.clip(hu[:, D_FF:], -SWIGLU_LIMIT, SWIGLU_LIMIT)
        act = (lin + 1.0) * (g * _sigmoid(SWIGLU_ALPHA * g))
        y_ref[...] = jnp.dot(act.astype(BF16), wd_bf[...], preferred_element_type=F32) + bd_ref[0]

    @pl.when(b >= nused_ref[0])
    def _():
        y_ref[...] = jnp.zeros(y_ref.shape, F32)


def _ffn(x_rows, blk_expert, first, n_used, w_up, b_up, w_down, b_down):
    n_rows, d = x_rows.shape
    blk = MOE_BLK
    n_exp, _, f2 = w_up.shape
    grid_spec = pltpu.PrefetchScalarGridSpec(
        num_scalar_prefetch=3,
        grid=(n_rows // blk,),
        in_specs=[pl.BlockSpec((blk, d), lambda b, be, fi, nu: (b, 0)),
                  pl.BlockSpec((1, d, f2), lambda b, be, fi, nu: (be[b], 0, 0)),
                  pl.BlockSpec((1, 1, f2), lambda b, be, fi, nu: (be[b], 0, 0)),
                  pl.BlockSpec((1, D_FF, d), lambda b, be, fi, nu: (be[b], 0, 0)),
                  pl.BlockSpec((1, 1, d), lambda b, be, fi, nu: (be[b], 0, 0))],
        out_specs=pl.BlockSpec((blk, d), lambda b, be, fi, nu: (b, 0)),
        scratch_shapes=[pltpu.VMEM((d, f2), BF16), pltpu.VMEM((D_FF, d), BF16)],
    )
    return pl.pallas_call(
        _ffn_kernel,
        grid_spec=grid_spec,
        out_shape=jax.ShapeDtypeStruct((n_rows, d), F32),
        compiler_params=_cparams(("arbitrary",)),
    )(blk_expert, first, n_used, x_rows, w_up, b_up.reshape(n_exp, 1, f2), w_down, b_down.reshape(n_exp, 1, d))


def _split_bf16(t):
    hi = t.astype(BF16)
    return hi, (t - hi.astype(F32)).astype(BF16)


def _combine_kernel(cnt_ref, seg_ref, off_ref, x_ref, slot_ref, gate_ref, g2_ref, rows_ref, o_ref, buf_ref, sem):
    tile = pl.program_id(0)
    tt = x_ref.shape[0]
    n = buf_ref.shape[0]

    def make_copy(a, b, size):
        return pltpu.make_async_copy(rows_ref.at[pl.ds(b, size)], buf_ref.at[pl.ds(a, size)], sem)

    @pl.when(tile == 0)
    def _():
        buf_ref[...] = jnp.zeros(buf_ref.shape, F32)

    _segment_copies(tile, cnt_ref, seg_ref, off_ref, make_copy, True)
    slots, gates = slot_ref[...], gate_ref[...]
    r_ids = lax.broadcasted_iota(jnp.int32, (tt, n), 1)
    sel = jnp.where(r_ids == slots[:, 0:1], gates[:, 0:1], 0.0)
    for k in range(1, TOP_K):
        sel = sel + jnp.where(r_ids == slots[:, k:k + 1], gates[:, k:k + 1], 0.0)
    sel_hi, sel_lo = _split_bf16(sel)
    _segment_copies(tile, cnt_ref, seg_ref, off_ref, make_copy, False)
    y_hi, y_lo = _split_bf16(buf_ref[...])
    both = jnp.dot(jnp.concatenate([sel_hi, sel_lo], axis=0), y_hi, preferred_element_type=F32)
    y = both[:tt] + both[tt:] + jnp.dot(sel_hi, y_lo, preferred_element_type=F32)
    o_ref[...] = x_ref[...] + g2_ref[...] * y


def _combine(x, slots, gates, g2, y_rows, tile_cnt, tile_seg, tile_off):
    s, d = x.shape
    tt = MOE_TILE
    row = lambda i, c, sg, of: (i, 0)
    grid_spec = pltpu.PrefetchScalarGridSpec(
        num_scalar_prefetch=3,
        grid=(s // tt,),
        in_specs=[pl.BlockSpec((tt, d), row), pl.BlockSpec((tt, LANE), row), pl.BlockSpec((tt, LANE), row),
                  pl.BlockSpec((1, d), lambda i, c, sg, of: (0, 0)),
                  pl.BlockSpec(memory_space=pl.ANY)],
        out_specs=pl.BlockSpec((tt, d), row),
        scratch_shapes=[pltpu.VMEM((MOE_SORT_ROWS, d), F32), pltpu.SemaphoreType.DMA(())],
    )
    return pl.pallas_call(
        _combine_kernel,
        grid_spec=grid_spec,
        out_shape=jax.ShapeDtypeStruct((s, d), F32),
        compiler_params=_cparams(("arbitrary",)),
    )(tile_cnt, tile_seg, tile_off, x, slots, gates, g2, y_rows)


def _moe(x, h, idx_pad, gates_pad, g2, w_up, b_up, w_down, b_down, layer_idx):
    s, d = x.shape
    n = s * TOP_K
    blk = MOE_BLK
    tt = MOE_TILE
    nt = s // tt
    n_blocks = -(-(n + nt * N_EXPERTS * SUBLANE) // blk) + N_EXPERTS
    i32 = jnp.int32
    idx = idx_pad[:, :TOP_K]
    onehot = jnp.sum((idx[:, :, None] == jnp.arange(N_EXPERTS, dtype=i32)).astype(i32), axis=1)
    onehot_t = onehot.reshape(nt, tt, N_EXPERTS)
    rank_in_tile = (jnp.cumsum(onehot_t, axis=1) - onehot_t).reshape(s, N_EXPERTS)
    tile_cnt = (jnp.sum(onehot_t, axis=1) + SUBLANE - 1) // SUBLANE * SUBLANE
    tile_base = jnp.cumsum(tile_cnt, axis=0) - tile_cnt
    tile_seg = jnp.cumsum(tile_cnt, axis=1) - tile_cnt
    counts = jnp.sum(tile_cnt, axis=0)
    padded = (counts + blk - 1) // blk * blk
    pad_ends = jnp.cumsum(padded)
    pad_starts = pad_ends - padded
    tile_off = pad_starts[None, :] + tile_base
    slot = (jnp.take_along_axis(jnp.repeat(tile_seg, tt, axis=0) + rank_in_tile, idx, axis=1)).astype(i32)
    slot_pad = jnp.pad(slot, ((0, 0), (0, LANE - TOP_K)))
    blk_start = jnp.arange(n_blocks, dtype=i32) * blk
    blk_expert = jnp.minimum(jnp.sum((pad_ends[None, :] <= blk_start[:, None]).astype(i32), axis=1),
                             N_EXPERTS - 1).astype(i32)
    first = jnp.concatenate([jnp.ones((1,), i32), (blk_expert[1:] != blk_expert[:-1]).astype(i32)])
    n_used = (pad_ends[-1:] // blk).astype(i32)
    tile_cnt, tile_seg, tile_off = [t.astype(i32).reshape(-1) for t in (tile_cnt, tile_seg, tile_off)]

    x_rows = _dispatch(h, slot.T, tile_cnt, tile_seg, tile_off, n_blocks * blk)
    y_rows = _ffn(x_rows, blk_expert + layer_idx * N_EXPERTS, first, n_used, w_up, b_up, w_down, b_down)
    return _combine(x, slot_pad, gates_pad, g2, y_rows, tile_cnt, tile_seg, tile_off)


def _block_diag(w):
    n, d, _ = w.shape
    eye = jnp.eye(n, dtype=w.dtype)
    return (eye[:, None, :, None] * w[:, :, None, :]).reshape(n * d, n * d)


def kernel(x, c, ada_w, ada_b, norm1_g, norm2_g, w_in, w_out, a_q_gain, a_k_gain, a_lam_q1, a_lam_k1, a_lam_q2, a_lam_k2, a_out_gain, b_q_gain, b_k_gain, c_w_g2, c_b_g, c_out_gain, d_conv_w, d_conv_b, d_w_a, d_b_a, d_w_x, d_b_x, d_lambda, router_w, router_b, exp_w_up, exp_b_up, exp_w_down, exp_b_down):
    bsz, s, d = x.shape
    assert bsz == 1 and d == D_MODEL and s % ROW_TILE == 0
    depth = ada_w.shape[0]
    mod = _modulation(c, ada_w, ada_b)
    xs = x.reshape(s, d)
    w_up_all = exp_w_up.reshape(depth * N_EXPERTS, d, 2 * D_FF)
    b_up_all = exp_b_up.reshape(depth * N_EXPERTS, 2 * D_FF)
    w_down_all = exp_w_down.reshape(depth * N_EXPERTS, D_FF, d)
    b_down_all = exp_b_down.reshape(depth * N_EXPERTS, d)
    for l in range(depth):
        sh1, sc1, g1, sh2, sc2, g2 = [mod[l, i * d:(i + 1) * d].reshape(1, d) for i in range(6)]
        w = w_in[l]
        w_cat = jnp.concatenate([w[:, :2048], w[:, 2064:2832], w[:, 2048:2064],
                                 jnp.zeros((d, W_IN_PAD - 2832), w.dtype)], axis=1).astype(BF16)
        wg2p = jnp.zeros((LANE, LANE), F32).at[:GLA_RANK, :].set(c_w_g2[l])
        (aqT, ak, avT, bqT, bk, bvT, bkm, cq, ck, cv, cr, cg, dx, dgate) = _inproj(
            xs, norm1_g[l].reshape(1, d), sc1, sh1, w_cat,
            jnp.tile(a_q_gain[l], 2 * A_HEADS).reshape(1, -1), jnp.tile(a_k_gain[l], 2 * A_HEADS).reshape(1, -1),
            jnp.tile(b_q_gain[l], B_HEADS).reshape(1, -1), jnp.tile(b_k_gain[l], B_HEADS).reshape(1, -1),
            wg2p, c_b_g[l].reshape(1, -1))
        o_aT = _diff_attention(aqT, ak, avT, a_lam_q1[l].reshape(1, -1), a_lam_k1[l].reshape(1, -1),
                               a_lam_q2[l].reshape(1, -1), a_lam_k2[l].reshape(1, -1),
                               a_out_gain[l].reshape(-1, 1), l)
        km = bkm.reshape(s // MOBA_BLOCK, B_HEADS, B_DIM).transpose(1, 0, 2)
        o_bT = _moba(bqT, bk, bvT, km)
        o_c = _gla(cq, ck, cg, cv, cr, c_out_gain[l].reshape(1, -1))
        o_d = _rglru(dx, dgate, d_conv_w[l], d_conv_b[l].reshape(1, -1),
                     _block_diag(d_w_a[l]).astype(BF16), d_b_a[l].reshape(1, -1),
                     _block_diag(d_w_x[l]).astype(BF16), d_b_x[l].reshape(1, -1), d_lambda[l].reshape(1, -1))
        w4 = w_out[l].astype(BF16).reshape(4, N_GROUP_W, d)
        xs, h2, idx_pad, gates_pad = _outproj(xs, o_aT, o_bT, o_c, o_d, w4, g1, norm2_g[l].reshape(1, d),
                                              sc2, sh2, router_w[l], router_b[l].reshape(1, -1))
        xs = _moe(xs, h2, idx_pad, gates_pad, g2, w_up_all, b_up_all, w_down_all, b_down_all, l)
    return xs.reshape(bsz, s, d)
```

```python
import functools
import math

import jax
import jax.numpy as jnp
from jax import lax
from jax.experimental import pallas as pl
from jax.experimental.pallas import tpu as pltpu

F32 = jnp.float32
BF16 = jnp.bfloat16
HI = lax.Precision.HIGHEST

EPS = 1e-6
NEG = -1e30
M_INIT = -1e30
MASKED = -2e30
LOG2E = 1.4426950408889634

D_MODEL = 1024
N_GROUP_W = 256
A_HEADS, A_QK, A_V = 4, 32, 64
B_HEADS, B_DIM = 4, 64
MOBA_BLOCK, MOBA_TOPK = 256, 3
C_HEADS, C_K, C_V = 4, 32, 64
GLA_RANK, GLA_TAU = 16, 16.0
GLA_SUB = 16
D_WIDTH, D_BLOCKS, CONV_W, LRU_C = 256, 4, 4, 8.0
N_EXPERTS, TOP_K, D_FF = 32, 4, 1024
SWIGLU_ALPHA, SWIGLU_LIMIT = 1.702, 7.0

LANE = 128
SUBLANE = 8
W_IN_PAD = 2944

ROW_TILE = 512
ATT_TILE = 256
KEY_TILE = 2 * ATT_TILE
SUM_ROWS = 16
MOE_BLK = 256
MOE_TILE = 256
MOE_SORT_ROWS = MOE_TILE * TOP_K + N_EXPERTS * SUBLANE
VMEM_LIMIT = 56 * 1024 * 1024


def _cparams(sem):
    return pltpu.CompilerParams(dimension_semantics=sem, vmem_limit_bytes=VMEM_LIMIT)


def _log_sigmoid(z):
    return jnp.minimum(z, 0.0) - jnp.log1p(jnp.exp(-jnp.abs(z)))


def _sigmoid(z):
    return 1.0 / (1.0 + jnp.exp(-z))


def _group_rms(t, group):
    n = t.shape[-1]
    shift = int(math.log2(group))
    r = lax.broadcasted_iota(jnp.int32, (n, n), 0) >> shift
    c = lax.broadcasted_iota(jnp.int32, (n, n), 1) >> shift
    bd = jnp.where(r == c, 1.0, 0.0).astype(F32)
    ss = jnp.dot(t * t, bd, precision=HI, preferred_element_type=F32)
    return t * lax.rsqrt(ss * (1.0 / group) + EPS)


def _mod_kernel(c_ref, w_ref, b_ref, o_ref):
    c = c_ref[...]
    cond = c * _sigmoid(c)
    cond8 = jnp.broadcast_to(cond, (8, cond.shape[-1]))
    o_ref[0] = jnp.dot(cond8, w_ref[0], precision=HI, preferred_element_type=F32) + b_ref[0]


def _modulation(c, ada_w, ada_b):
    depth, d, n = ada_w.shape
    tn = 1536
    out = pl.pallas_call(
        _mod_kernel,
        grid=(depth, n // tn),
        in_specs=[pl.BlockSpec((1, d), lambda l, j: (0, 0)),
                  pl.BlockSpec((1, d, tn), lambda l, j: (l, 0, j)),
                  pl.BlockSpec((1, 1, tn), lambda l, j: (l, 0, j))],
        out_specs=pl.BlockSpec((1, 8, tn), lambda l, j: (l, 0, j)),
        out_shape=jax.ShapeDtypeStruct((depth, 8, n), F32),
        compiler_params=_cparams(("arbitrary", "arbitrary")),
    )(c, ada_w, ada_b.reshape(depth, 1, n))
    return out[:, 0, :]


def _inproj_kernel(x_ref, ng_ref, sc_ref, sh_ref, w_ref, aqg_ref, akg_ref, bqg_ref, bkg_ref, wg2_ref, bg_ref,
                   aqT_ref, ak_ref, avT_ref, bqT_ref, bk_ref, bvT_ref, bkm_ref,
                   cq_ref, ck_ref, cv_ref, cr_ref, cg_ref, dx_ref, dg_ref):
    x = x_ref[...]
    tm = x.shape[0]
    h = x * lax.rsqrt(jnp.mean(x * x, axis=-1, keepdims=True) + EPS) * ng_ref[...]
    h = h * (1.0 + sc_ref[...]) + sh_ref[...]
    proj = jnp.dot(h.astype(BF16), w_ref[...], preferred_element_type=F32)

    aq = _group_rms(proj[:, 0:256], A_QK) * aqg_ref[...] * (A_QK ** -0.5 * LOG2E)
    aqT_ref[...] = aq.T.astype(BF16)
    ak = _group_rms(proj[:, 256:512], A_QK) * akg_ref[...]
    ak_ref[...] = ak.astype(BF16)
    avT_ref[...] = proj[:, 512:768].T.astype(BF16)

    bq = _group_rms(proj[:, 768:1024], B_DIM) * bqg_ref[...] * (B_DIM ** -0.5 * LOG2E)
    bqT_ref[...] = bq.T.astype(BF16)
    bk = _group_rms(proj[:, 1024:1280], B_DIM) * bkg_ref[...]
    bk_ref[...] = bk.astype(BF16)
    bvT_ref[...] = proj[:, 1280:1536].T.astype(BF16)
    bkm_ref[0] = jnp.mean(bk.reshape(tm // MOBA_BLOCK, MOBA_BLOCK, N_GROUP_W), axis=1)

    cq_ref[...] = proj[:, 1536:1664] * (C_K ** -0.5)
    ck_ref[...] = proj[:, 1664:1792]
    cv_ref[...] = proj[:, 1792:2048]
    cr_ref[...] = proj[:, 2048:2304]
    z = jnp.dot(proj[:, 2816:2944], wg2_ref[...], precision=HI, preferred_element_type=F32) + bg_ref[...]
    cg_ref[...] = _log_sigmoid(z) * (1.0 / GLA_TAU)

    dx_ref[...] = proj[:, 2304:2560]
    dg_ref[...] = proj[:, 2560:2816]


def _inproj(x, ng, sc, sh, w_cat, aqg, akg, bqg, bkg, wg2p, bg):
    s, d = x.shape
    tm = ROW_TILE
    nt = s // tm
    row = lambda i: (i, 0)
    const2 = lambda i: (0, 0)
    vec = pl.BlockSpec((1, d), const2)
    gvec = pl.BlockSpec((1, N_GROUP_W), const2)
    out_shape = (
        jax.ShapeDtypeStruct((N_GROUP_W, s), BF16),
        jax.ShapeDtypeStruct((s, N_GROUP_W), BF16),
        jax.ShapeDtypeStruct((N_GROUP_W, s), BF16),
        jax.ShapeDtypeStruct((N_GROUP_W, s), BF16),
        jax.ShapeDtypeStruct((s, N_GROUP_W), BF16),
        jax.ShapeDtypeStruct((N_GROUP_W, s), BF16),
        jax.ShapeDtypeStruct((nt, tm // MOBA_BLOCK, N_GROUP_W), F32),
        jax.ShapeDtypeStruct((s, 128), F32),
        jax.ShapeDtypeStruct((s, 128), F32),
        jax.ShapeDtypeStruct((s, 256), F32),
        jax.ShapeDtypeStruct((s, 256), F32),
        jax.ShapeDtypeStruct((s, 128), F32),
        jax.ShapeDtypeStruct((s, 256), F32),
        jax.ShapeDtypeStruct((s, 256), F32),
    )
    out_specs = (
        pl.BlockSpec((N_GROUP_W, tm), lambda i: (0, i)),
        pl.BlockSpec((tm, N_GROUP_W), row),
        pl.BlockSpec((N_GROUP_W, tm), lambda i: (0, i)),
        pl.BlockSpec((N_GROUP_W, tm), lambda i: (0, i)),
        pl.BlockSpec((tm, N_GROUP_W), row),
        pl.BlockSpec((N_GROUP_W, tm), lambda i: (0, i)),
        pl.BlockSpec((1, tm // MOBA_BLOCK, N_GROUP_W), lambda i: (i, 0, 0)),
        pl.BlockSpec((tm, 128), row),
        pl.BlockSpec((tm, 128), row),
        pl.BlockSpec((tm, 256), row),
        pl.BlockSpec((tm, 256), row),
        pl.BlockSpec((tm, 128), row),
        pl.BlockSpec((tm, 256), row),
        pl.BlockSpec((tm, 256), row),
    )
    return pl.pallas_call(
        _inproj_kernel,
        grid=(nt,),
        in_specs=[pl.BlockSpec((tm, d), row), vec, vec, vec,
                  pl.BlockSpec((d, W_IN_PAD), const2),
                  gvec, gvec, gvec, gvec,
                  pl.BlockSpec((LANE, LANE), const2),
                  pl.BlockSpec((1, LANE), const2)],
        out_specs=out_specs,
        out_shape=out_shape,
        compiler_params=_cparams(("arbitrary",)),
    )(x, ng, sc, sh, w_cat, aqg, akg, bqg, bkg, wg2p, bg)


def _pad_query_rows(qT_ref, qpad_ref, width):
    per = LANE // width
    rows = lax.broadcasted_iota(jnp.int32, qpad_ref.shape[1:], 0)
    for g in range(qpad_ref.shape[0]):
        blk = qT_ref[LANE * (g // per):LANE * (g // per + 1), :]
        lo = width * (g % per)
        qpad_ref[g] = jnp.where((rows >= lo) & (rows < lo + width), blk, jnp.zeros_like(blk))


def _with_sum_rows(vT):
    return jnp.concatenate([vT, jnp.ones((SUM_ROWS, vT.shape[1]), vT.dtype)], axis=0)


def _softmax_init(m_ref, acc_ref, p_ref, alpha_ref):
    m_ref[...] = jnp.full(m_ref.shape, M_INIT, F32)
    acc_ref[...] = jnp.zeros(acc_ref.shape, F32)
    p_ref[...] = jnp.zeros(p_ref.shape, p_ref.dtype)
    alpha_ref[...] = jnp.ones(alpha_ref.shape, F32)


def _softmax_flush(v_pending, acc_ref, p_ref, alpha_ref):
    n = len(v_pending)
    pv = [jnp.dot(v_pending[g], p_ref[g], preferred_element_type=F32) for g in range(n)]
    for g in range(n):
        acc_ref[g] = alpha_ref[g] * acc_ref[g] + pv[g]


def _softmax_tiles(k_slabs, qpad_ref, v_pending, adjust, m_ref, acc_ref, p_ref, alpha_ref):
    n = len(k_slabs)
    pv = [jnp.dot(v_pending[g], p_ref[g], preferred_element_type=F32) for g in range(n)]
    scores = [jnp.dot(k_slabs[g], qpad_ref[g], preferred_element_type=F32) for g in range(n)]
    probs, alphas = [], []
    for g in range(n):
        sT = adjust(g, scores[g])
        m_old = m_ref[g]
        m_new = jnp.maximum(m_old, jnp.max(sT, axis=0, keepdims=True))
        alphas.append(jnp.exp2(m_old - m_new))
        probs.append(jnp.exp2(sT - m_new).astype(BF16))
        m_ref[g] = m_new
    for g in range(n):
        acc_ref[g] = alpha_ref[g] * acc_ref[g] + pv[g]
    for g in range(n):
        p_ref[g] = probs[g]
        alpha_ref[g] = alphas[g]


def _diffattn_kernel(qT_ref, k_ref, vT_ref, lq1_ref, lk1_ref, lq2_ref, lk2_ref, og_ref, o_ref,
                     m_ref, acc_ref, qpad_ref, p_ref, alpha_ref, *, lam_init):
    t, tk = ATT_TILE, KEY_TILE
    n_maps = 2 * A_HEADS
    iq = pl.program_id(0)
    lam = (jnp.exp(jnp.sum(lq1_ref[...] * lk1_ref[...], axis=-1, keepdims=True))
           - jnp.exp(jnp.sum(lq2_ref[...] * lk2_ref[...], axis=-1, keepdims=True)) + lam_init)
    _softmax_init(m_ref, acc_ref, p_ref, alpha_ref)
    _pad_query_rows(qT_ref, qpad_ref, A_QK)

    def values(ik):
        keys = pl.ds(pl.multiple_of(ik * tk, tk), tk)
        v_heads = [_with_sum_rows(vT_ref[A_V * hd:A_V * (hd + 1), keys]) for hd in range(A_HEADS)]
        return [v_heads[g // 2] for g in range(n_maps)]

    def tile(ik, adjust):
        keys = pl.ds(pl.multiple_of(ik * tk, tk), tk)
        k_slabs = [k_ref[keys, LANE * (g // 4):LANE * (g // 4 + 1)] for g in range(n_maps)]
        _softmax_tiles(k_slabs, qpad_ref, values(jnp.maximum(ik - 1, 0)), adjust, m_ref, acc_ref, p_ref, alpha_ref)

    def body(ik, carry):
        tile(ik, lambda g, sT: sT)
        return carry

    n_full = iq // 2
    lax.fori_loop(0, n_full, body, 0)
    causal = (lax.broadcasted_iota(jnp.int32, (tk, t), 0) - lax.broadcasted_iota(jnp.int32, (tk, t), 1)
              <= (iq - 2 * n_full) * t)
    tile(n_full, lambda g, sT: jnp.where(causal, sT, MASKED))
    _softmax_flush(values(n_full), acc_ref, p_ref, alpha_ref)

    for hd in range(A_HEADS):
        a1, a2 = acc_ref[2 * hd], acc_ref[2 * hd + 1]
        o = a1[:A_V] / a1[A_V:A_V + 1] - lam * (a2[:A_V] / a2[A_V:A_V + 1])
        o = o * lax.rsqrt(jnp.mean(o * o, axis=0, keepdims=True) + EPS) * og_ref[...]
        o_ref[A_V * hd:A_V * (hd + 1), :] = o * (1.0 - lam_init)


def _diff_attention(aqT, ak, avT, lq1, lk1, lq2, lk2, out_gain_col, layer_idx):
    s = aqT.shape[1]
    t = ATT_TILE
    n_maps = 2 * A_HEADS
    lam_init = 0.8 - 0.6 * math.exp(-0.3 * layer_idx)
    vec = pl.BlockSpec((1, A_QK), lambda i: (0, 0))
    return pl.pallas_call(
        functools.partial(_diffattn_kernel, lam_init=lam_init),
        grid=(s // t,),
        in_specs=[pl.BlockSpec((N_GROUP_W, t), lambda i: (0, i)),
                  pl.BlockSpec((s, N_GROUP_W), lambda i: (0, 0)),
                  pl.BlockSpec((N_GROUP_W, s), lambda i: (0, 0)),
                  vec, vec, vec, vec,
                  pl.BlockSpec((A_V, 1), lambda i: (0, 0))],
        out_specs=pl.BlockSpec((N_GROUP_W, t), lambda i: (0, i)),
        out_shape=jax.ShapeDtypeStruct((N_GROUP_W, s), F32),
        scratch_shapes=[pltpu.VMEM((n_maps, 1, t), F32), pltpu.VMEM((n_maps, A_V + SUM_ROWS, t), F32),
                        pltpu.VMEM((n_maps, LANE, t), BF16), pltpu.VMEM((n_maps, KEY_TILE, t), BF16),
                        pltpu.VMEM((n_maps, 1, t), F32)],
        compiler_params=_cparams(("arbitrary",)),
    )(aqT, ak, avT, lq1, lk1, lq2, lk2, out_gain_col)


def _moba_kernel(qT_ref, k_ref, vT_ref, km_ref, o_ref, m_ref, acc_ref, bias_ref, qpad_ref, p_ref, alpha_ref):
    t, tk = ATT_TILE, KEY_TILE
    own = pl.program_id(0)
    nb = km_ref.shape[1]
    _softmax_init(m_ref, acc_ref, p_ref, alpha_ref)
    _pad_query_rows(qT_ref, qpad_ref, B_DIM)

    blk = lax.broadcasted_iota(jnp.int32, (nb, t), 0)
    earlier = blk < own
    for hd in range(B_HEADS):
        qT = qT_ref[B_DIM * hd:B_DIM * (hd + 1), :].astype(F32)
        gate = jnp.dot(km_ref[hd], qT, precision=HI, preferred_element_type=F32)
        gate = jnp.where(earlier, gate, NEG)
        sel_bias = jnp.full((nb, t), MASKED, F32)
        for _ in range(MOBA_TOPK):
            top = jnp.max(gate, axis=0, keepdims=True)
            pick = blk == jnp.min(jnp.where(gate == top, blk, nb), axis=0, keepdims=True)
            sel_bias = jnp.where(pick, jnp.where(earlier, 0.0, MASKED), sel_bias)
            gate = jnp.where(pick, -jnp.inf, gate)
        bias_ref[hd] = sel_bias

    def values(j):
        keys = pl.ds(pl.multiple_of(j * tk, tk), tk)
        return [_with_sum_rows(vT_ref[B_DIM * hd:B_DIM * (hd + 1), keys]) for hd in range(B_HEADS)]

    def tile(j, adjust):
        keys = pl.ds(pl.multiple_of(j * tk, tk), tk)
        k_slabs = [k_ref[keys, LANE * (hd // 2):LANE * (hd // 2 + 1)] for hd in range(B_HEADS)]
        _softmax_tiles(k_slabs, qpad_ref, values(jnp.maximum(j - 1, 0)), adjust, m_ref, acc_ref, p_ref, alpha_ref)

    def block_bias(hd, b):
        return bias_ref[hd, pl.ds(b, 1), :]

    def body(j, carry):
        tile(j, lambda hd, sT: jnp.concatenate([sT[:t] + block_bias(hd, 2 * j),
                                                sT[t:] + block_bias(hd, 2 * j + 1)], axis=0))
        return carry

    n_full = own // 2
    lax.fori_loop(0, n_full, body, 0)

    causal = (lax.broadcasted_iota(jnp.int32, (t, t), 0) <= lax.broadcasted_iota(jnp.int32, (t, t), 1))

    def last(hd, sT):
        halves = []
        for half in range(2):
            b = 2 * n_full + half
            sh = sT[t * half:t * (half + 1)]
            other = sh + jnp.where(b < own, block_bias(hd, b), MASKED)
            halves.append(jnp.where(b == own, jnp.where(causal, sh, MASKED), other))
        return jnp.concatenate(halves, axis=0)

    tile(n_full, last)
    _softmax_flush(values(n_full), acc_ref, p_ref, alpha_ref)
    for hd in range(B_HEADS):
        a = acc_ref[hd]
        o_ref[B_DIM * hd:B_DIM * (hd + 1), :] = a[:B_DIM] / a[B_DIM:B_DIM + 1]


def _moba(bqT, bk, bvT, km):
    s = bqT.shape[1]
    t = ATT_TILE
    nb = km.shape[1]
    return pl.pallas_call(
        _moba_kernel,
        grid=(s // t,),
        in_specs=[pl.BlockSpec((N_GROUP_W, t), lambda i: (0, i)),
                  pl.BlockSpec((s, N_GROUP_W), lambda i: (0, 0)),
                  pl.BlockSpec((N_GROUP_W, s), lambda i: (0, 0)),
                  pl.BlockSpec((B_HEADS, nb, B_DIM), lambda i: (0, 0, 0))],
        out_specs=pl.BlockSpec((N_GROUP_W, t), lambda i: (0, i)),
        out_shape=jax.ShapeDtypeStruct((N_GROUP_W, s), F32),
        scratch_shapes=[pltpu.VMEM((B_HEADS, 1, t), F32),
                        pltpu.VMEM((B_HEADS, B_DIM + SUM_ROWS, t), F32), pltpu.VMEM((B_HEADS, nb, t), F32),
                        pltpu.VMEM((B_HEADS, LANE, t), BF16), pltpu.VMEM((B_HEADS, KEY_TILE, t), BF16),
                        pltpu.VMEM((B_HEADS, 1, t), F32)],
        compiler_params=_cparams(("arbitrary",)),
    )(bqT, bk, bvT, km)


def _gla_kernel(q_ref, k_ref, g_ref, v_ref, r_ref, og_ref, o_ref, stateT_ref, kbuf_ref, bbuf_ref, vbuf_ref):
    tm = q_ref.shape[0]
    c = GLA_SUB
    shift = int(math.log2(c))

    @pl.when(pl.program_id(0) == 0)
    def _():
        stateT_ref[...] = jnp.zeros(stateT_ref.shape, F32)

    q, k, v = q_ref[...], k_ref[...], v_ref[...]
    ri = lax.broadcasted_iota(jnp.int32, (tm, tm), 0)
    ci = lax.broadcasted_iota(jnp.int32, (tm, tm), 1)
    same = (ri >> shift) == (ci >> shift)
    g = g_ref[...]
    b = jnp.dot(jnp.where(same & (ri >= ci), 1.0, 0.0).astype(F32), g, precision=HI, preferred_element_type=F32)
    b_last = jnp.dot(jnp.where(same, 1.0, 0.0).astype(F32), g, precision=HI, preferred_element_type=F32)

    k_shift, v_shift = int(math.log2(C_K)), int(math.log2(C_V))
    head_sum = jnp.where((lax.broadcasted_iota(jnp.int32, (C_HEADS * C_K, C_HEADS * C_V), 0) >> k_shift)
                         == (lax.broadcasted_iota(jnp.int32, (C_HEADS * C_K, C_HEADS * C_V), 1) >> v_shift),
                         1.0, 0.0).astype(BF16)

    for buf, val in ((kbuf_ref, k), (bbuf_ref, b), (vbuf_ref, v)):
        buf[0:c, :] = jnp.zeros((c, val.shape[1]), F32)
        buf[c:c + tm, :] = val
    pos = lax.broadcasted_iota(jnp.int32, (tm, C_HEADS * C_K), 0) & (c - 1)
    o = jnp.zeros((tm, C_HEADS * C_V), F32)
    for delta in range(c):
        k_d = kbuf_ref[c - delta:c - delta + tm, :]
        b_d = bbuf_ref[c - delta:c - delta + tm, :]
        v_d = vbuf_ref[c - delta:c - delta + tm, :]
        w = q * k_d * jnp.exp(jnp.where(pos >= delta, b - b_d, NEG))
        w_hi, w_lo = _split_bf16(w)
        att = (jnp.dot(w_hi, head_sum, preferred_element_type=F32)
               + jnp.dot(w_lo, head_sum, preferred_element_type=F32))
        o = o + att * v_d

    q_dec = (q * jnp.exp(b)).astype(BF16)
    k_dec = (k * jnp.exp(b_last - b)).astype(BF16)
    decay = jnp.exp(b_last)
    v_bf = v.astype(BF16)
    own_head = ((lax.broadcasted_iota(jnp.int32, (C_HEADS * C_V, C_HEADS * C_K), 0) >> v_shift)
                == (lax.broadcasted_iota(jnp.int32, (C_HEADS * C_V, C_HEADS * C_K), 1) >> k_shift))
    nt_dims = (((1,), (1,)), ((), ()))
    tn_dims = (((0,), (0,)), ((), ()))
    updates = [lax.dot_general(v_bf[c * i:c * (i + 1)], k_dec[c * i:c * (i + 1)], tn_dims,
                               preferred_element_type=F32) for i in range(tm // c)]
    stateT = stateT_ref[...]
    far = []
    for i in range(tm // c):
        far.append(lax.dot_general(q_dec[c * i:c * (i + 1)], stateT.astype(BF16), nt_dims,
                                   preferred_element_type=F32))
        stateT = stateT * decay[c * i:c * i + 1, :] + jnp.where(own_head, updates[i], 0.0)
    stateT_ref[...] = stateT
    o = o + jnp.concatenate(far, axis=0)

    r = r_ref[...]
    o_ref[...] = _group_rms(o, C_V) * og_ref[...] * (r * _sigmoid(r))


def _gla(cq, ck, cg, cv, cr, out_gain):
    s = cq.shape[0]
    tm = ROW_TILE
    row = lambda i: (i, 0)
    return pl.pallas_call(
        _gla_kernel,
        grid=(s // tm,),
        in_specs=[pl.BlockSpec((tm, 128), row), pl.BlockSpec((tm, 128), row), pl.BlockSpec((tm, 128), row),
                  pl.BlockSpec((tm, 256), row), pl.BlockSpec((tm, 256), row),
                  pl.BlockSpec((1, C_HEADS * C_V), lambda i: (0, 0))],
        out_specs=pl.BlockSpec((tm, 256), row),
        out_shape=jax.ShapeDtypeStruct((s, 256), F32),
        scratch_shapes=[pltpu.VMEM((C_HEADS * C_V, C_HEADS * C_K), F32),
                        pltpu.VMEM((tm + GLA_SUB, C_HEADS * C_K), F32),
                        pltpu.VMEM((tm + GLA_SUB, C_HEADS * C_K), F32),
                        pltpu.VMEM((tm + GLA_SUB, C_HEADS * C_V), F32)],
        compiler_params=_cparams(("arbitrary",)),
    )(cq, ck, cg, cv, cr, out_gain)


def _rglru_kernel(x_ref, gate_ref, cw_ref, cb_ref, wa_ref, ba_ref, wx_ref, bx_ref, lam_ref, o_ref,
                  xbuf_ref, h_ref):
    tm = x_ref.shape[0]
    pad = 8

    @pl.when(pl.program_id(0) == 0)
    def _():
        xbuf_ref[0:pad, :] = jnp.zeros((pad, D_WIDTH), F32)
        h_ref[...] = jnp.zeros(h_ref.shape, F32)

    xbuf_ref[pad:pad + tm, :] = x_ref[...]
    xc = cb_ref[...] + jnp.zeros((tm, D_WIDTH), F32)
    for w in range(CONV_W):
        lag = CONV_W - 1 - w
        xc = xc + cw_ref[w:w + 1, :] * xbuf_ref[pad - lag:pad - lag + tm, :]
    xbuf_ref[0:pad, :] = xbuf_ref[tm:tm + pad, :]

    xcb = xc.astype(BF16)
    r_gate = _sigmoid(jnp.dot(xcb, wa_ref[...], preferred_element_type=F32) + ba_ref[...])
    i_gate = _sigmoid(jnp.dot(xcb, wx_ref[...], preferred_element_type=F32) + bx_ref[...])
    log_a = LRU_C * r_gate * _log_sigmoid(lam_ref[...])
    a = jnp.exp(log_a)
    b = jnp.sqrt(1.0 - jnp.exp(2.0 * log_a)) * (i_gate * xc)

    rows = lax.broadcasted_iota(jnp.int32, (tm, D_WIDTH), 0)
    step = 1
    while step < tm:
        keep = rows >= step
        a_prev = jnp.where(keep, pltpu.roll(a, step, 0), 1.0)
        b_prev = jnp.where(keep, pltpu.roll(b, step, 0), 0.0)
        b = a * b_prev + b
        a = a * a_prev
        step *= 2
    h = a * h_ref[0:1, :] + b
    h_ref[...] = jnp.broadcast_to(h[tm - 1:tm, :], h_ref.shape)

    g = gate_ref[...]
    gelu = g * (0.5 * (1.0 + jnp.tanh(math.sqrt(2.0 / math.pi) * (g + 0.044715 * (g * g * g)))))
    o_ref[...] = h * gelu


def _rglru(dx, dgate, conv_w, conv_b, wa_bd, b_a, wx_bd, b_x, lam):
    s = dx.shape[0]
    tm = ROW_TILE
    row = lambda i: (i, 0)
    const2 = lambda i: (0, 0)
    vec = pl.BlockSpec((1, D_WIDTH), const2)
    mat = pl.BlockSpec((D_WIDTH, D_WIDTH), const2)
    return pl.pallas_call(
        _rglru_kernel,
        grid=(s // tm,),
        in_specs=[pl.BlockSpec((tm, D_WIDTH), row), pl.BlockSpec((tm, D_WIDTH), row),
                  pl.BlockSpec((CONV_W, D_WIDTH), const2), vec, mat, vec, mat, vec, vec],
        out_specs=pl.BlockSpec((tm, D_WIDTH), row),
        out_shape=jax.ShapeDtypeStruct((s, D_WIDTH), F32),
        scratch_shapes=[pltpu.VMEM((tm + 8, D_WIDTH), F32), pltpu.VMEM((8, D_WIDTH), F32)],
        compiler_params=_cparams(("arbitrary",)),
    )(dx, dgate, conv_w, conv_b, wa_bd, b_a, wx_bd, b_x, lam)


def _outproj_kernel(x_ref, oaT_ref, obT_ref, oc_ref, od_ref, w_ref, g1_ref, ng_ref, sc_ref, sh_ref,
                    rw_ref, rb_ref, xo_ref, h_ref, idx_ref, gate_ref):
    tm = x_ref.shape[0]
    y = jnp.dot(oaT_ref[...].T.astype(BF16), w_ref[0], preferred_element_type=F32)
    y = y + jnp.dot(obT_ref[...].T.astype(BF16), w_ref[1], preferred_element_type=F32)
    y = y + jnp.dot(oc_ref[...].astype(BF16), w_ref[2], preferred_element_type=F32)
    y = y + jnp.dot(od_ref[...].astype(BF16), w_ref[3], preferred_element_type=F32)
    x = x_ref[...] + g1_ref[...] * y
    xo_ref[...] = x
    h = x * lax.rsqrt(jnp.mean(x * x, axis=-1, keepdims=True) + EPS) * ng_ref[...]
    h = h * (1.0 + sc_ref[...]) + sh_ref[...]
    h_ref[...] = h

    logits = jnp.dot(h, rw_ref[...], precision=HI, preferred_element_type=F32) + rb_ref[...]
    e_ids = lax.broadcasted_iota(jnp.int32, (tm, N_EXPERTS), 1)
    lane = lax.broadcasted_iota(jnp.int32, (tm, LANE), 1)
    idx_out = jnp.zeros((tm, LANE), jnp.int32)
    val_out = jnp.zeros((tm, LANE), F32)
    top0 = None
    denom = jnp.zeros((tm, 1), F32)
    for k in range(TOP_K):
        top = jnp.max(logits, axis=-1, keepdims=True)
        arg = jnp.min(jnp.where(logits == top, e_ids, N_EXPERTS), axis=-1, keepdims=True)
        if k == 0:
            top0 = top
        w = jnp.exp(top - top0)
        denom = denom + w
        idx_out = jnp.where(lane == k, arg, idx_out)
        val_out = jnp.where(lane == k, w, val_out)
        logits = jnp.where(e_ids == arg, -jnp.inf, logits)
    idx_ref[...] = idx_out
    gate_ref[...] = val_out / denom


def _outproj(x, oaT, obT, oc, od, w4, g1, ng, sc, sh, rw, rb):
    s, d = x.shape
    tm = ROW_TILE
    row = lambda i: (i, 0)
    col = lambda i: (0, i)
    const2 = lambda i: (0, 0)
    vec = pl.BlockSpec((1, d), const2)
    return pl.pallas_call(
        _outproj_kernel,
        grid=(s // tm,),
        in_specs=[pl.BlockSpec((tm, d), row),
                  pl.BlockSpec((N_GROUP_W, tm), col), pl.BlockSpec((N_GROUP_W, tm), col),
                  pl.BlockSpec((tm, 256), row), pl.BlockSpec((tm, 256), row),
                  pl.BlockSpec((4, N_GROUP_W, d), lambda i: (0, 0, 0)),
                  vec, vec, vec, vec,
                  pl.BlockSpec((d, N_EXPERTS), const2), pl.BlockSpec((1, N_EXPERTS), const2)],
        out_specs=(pl.BlockSpec((tm, d), row), pl.BlockSpec((tm, d), row),
                   pl.BlockSpec((tm, LANE), row), pl.BlockSpec((tm, LANE), row)),
        out_shape=(jax.ShapeDtypeStruct((s, d), F32), jax.ShapeDtypeStruct((s, d), F32),
                   jax.ShapeDtypeStruct((s, LANE), jnp.int32), jax.ShapeDtypeStruct((s, LANE), F32)),
        compiler_params=_cparams(("arbitrary",)),
    )(x, oaT, obT, oc, od, w4, g1, ng, sc, sh, rw, rb)


def _segment_copies(tile, cnt_ref, seg_ref, off_ref, make_copy, start):
    def per_expert(e, carry):
        j = tile * N_EXPERTS + e
        cnt, a, b = cnt_ref[j], seg_ref[j], off_ref[j]
        size = MOE_TILE
        while size >= SUBLANE:
            bit = cnt & size

            @pl.when(bit != 0)
            def _():
                cp = make_copy(pl.multiple_of(a, SUBLANE), pl.multiple_of(b, SUBLANE), size)
                if start:
                    cp.start()
                else:
                    cp.wait()

            a, b = a + bit, b + bit
            size //= 2
        return carry

    lax.fori_loop(0, N_EXPERTS, per_expert, 0)


def _dispatch_kernel(cnt_ref, seg_ref, off_ref, h_ref, slotT_ref, zeros_ref, rows_ref, sorted_ref, sem):
    del zeros_ref
    tile = pl.program_id(0)
    tt = h_ref.shape[0]
    n = sorted_ref.shape[0]
    r_ids = lax.broadcasted_iota(jnp.int32, (n, tt), 0)
    perm = jnp.where(r_ids == slotT_ref[0:1, :], 1.0, 0.0)
    for k in range(1, TOP_K):
        perm = perm + jnp.where(r_ids == slotT_ref[k:k + 1, :], 1.0, 0.0)
    perm = perm.astype(BF16)
    sorted_ref[...] = jnp.dot(perm, h_ref[...].astype(BF16), preferred_element_type=F32)

    def make_copy(a, b, size):
        return pltpu.make_async_copy(sorted_ref.at[pl.ds(a, size)], rows_ref.at[pl.ds(b, size)], sem)

    _segment_copies(tile, cnt_ref, seg_ref, off_ref, make_copy, True)
    _segment_copies(tile, cnt_ref, seg_ref, off_ref, make_copy, False)


def _dispatch(h, slotT, tile_cnt, tile_seg, tile_off, n_rows):
    s, d = h.shape
    tt = MOE_TILE
    grid_spec = pltpu.PrefetchScalarGridSpec(
        num_scalar_prefetch=3,
        grid=(s // tt,),
        in_specs=[pl.BlockSpec((tt, d), lambda i, c, sg, of: (i, 0)),
                  pl.BlockSpec((TOP_K, tt), lambda i, c, sg, of: (0, i)),
                  pl.BlockSpec(memory_space=pl.ANY)],
        out_specs=pl.BlockSpec(memory_space=pl.ANY),
        scratch_shapes=[pltpu.VMEM((MOE_SORT_ROWS, d), F32), pltpu.SemaphoreType.DMA(())],
    )
    return pl.pallas_call(
        _dispatch_kernel,
        grid_spec=grid_spec,
        out_shape=jax.ShapeDtypeStruct((n_rows, d), F32),
        input_output_aliases={5: 0},
        compiler_params=_cparams(("arbitrary",)),
    )(tile_cnt, tile_seg, tile_off, h, slotT, jnp.zeros((n_rows, d), F32))


def _ffn_kernel(be_ref, first_ref, nused_ref, x_ref, wu_ref, bu_ref, wd_ref, bd_ref, y_ref, wu_bf, wd_bf):
    b = pl.program_id(0)

    @pl.when(first_ref[b] == 1)
    def _():
        wu_bf[...] = wu_ref[0].astype(BF16)
        wd_bf[...] = wd_ref[0].astype(BF16)

    @pl.when(b < nused_ref[0])
    def _():
        hu = jnp.dot(x_ref[...].astype(BF16), wu_bf[...], preferred_element_type=F32) + bu_ref[0]
        g = jnp.minimum(hu[:, :D_FF], SWIGLU_LIMIT)
        lin = jnp.clip(hu[:, D_FF:], -SWIGLU_LIMIT, SWIGLU_LIMIT)
        act = (lin + 1.0) * (g * _sigmoid(SWIGLU_ALPHA * g))
        y_ref[...] = jnp.dot(act.astype(BF16), wd_bf[...], preferred_element_type=F32) + bd_ref[0]

    @pl.when(b >= nused_ref[0])
    def _():
        y_ref[...] = jnp.zeros(y_ref.shape, F32)


def _ffn(x_rows, blk_expert, first, n_used, w_up, b_up, w_down, b_down):
    n_rows, d = x_rows.shape
    blk = MOE_BLK
    n_exp, _, f2 = w_up.shape
    grid_spec = pltpu.PrefetchScalarGridSpec(
        num_scalar_prefetch=3,
        grid=(n_rows // blk,),
        in_specs=[pl.BlockSpec((blk, d), lambda b, be, fi, nu: (b, 0)),
                  pl.BlockSpec((1, d, f2), lambda b, be, fi, nu: (be[b], 0, 0)),
                  pl.BlockSpec((1, 1, f2), lambda b, be, fi, nu: (be[b], 0, 0)),
                  pl.BlockSpec((1, D_FF, d), lambda b, be, fi, nu: (be[b], 0, 0)),
                  pl.BlockSpec((1, 1, d), lambda b, be, fi, nu: (be[b], 0, 0))],
        out_specs=pl.BlockSpec((blk, d), lambda b, be, fi, nu: (b, 0)),
        scratch_shapes=[pltpu.VMEM((d, f2), BF16), pltpu.VMEM((D_FF, d), BF16)],
    )
    return pl.pallas_call(
        _ffn_kernel,
        grid_spec=grid_spec,
        out_shape=jax.ShapeDtypeStruct((n_rows, d), F32),
        compiler_params=_cparams(("arbitrary",)),
    )(blk_expert, first, n_used, x_rows, w_up, b_up.reshape(n_exp, 1, f2), w_down, b_down.reshape(n_exp, 1, d))


def _split_bf16(t):
    hi = t.astype(BF16)
    return hi, (t - hi.astype(F32)).astype(BF16)


def _combine_kernel(cnt_ref, seg_ref, off_ref, x_ref, slot_ref, gate_ref, g2_ref, rows_ref, o_ref, buf_ref, sem):
    tile = pl.program_id(0)
    tt = x_ref.shape[0]
    n = buf_ref.shape[0]

    def make_copy(a, b, size):
        return pltpu.make_async_copy(rows_ref.at[pl.ds(b, size)], buf_ref.at[pl.ds(a, size)], sem)

    @pl.when(tile == 0)
    def _():
        buf_ref[...] = jnp.zeros(buf_ref.shape, F32)

    _segment_copies(tile, cnt_ref, seg_ref, off_ref, make_copy, True)
    slots, gates = slot_ref[...], gate_ref[...]
    r_ids = lax.broadcasted_iota(jnp.int32, (tt, n), 1)
    sel = jnp.where(r_ids == slots[:, 0:1], gates[:, 0:1], 0.0)
    for k in range(1, TOP_K):
        sel = sel + jnp.where(r_ids == slots[:, k:k + 1], gates[:, k:k + 1], 0.0)
    sel_hi, sel_lo = _split_bf16(sel)
    _segment_copies(tile, cnt_ref, seg_ref, off_ref, make_copy, False)
    y_hi, y_lo = _split_bf16(buf_ref[...])
    both = jnp.dot(jnp.concatenate([sel_hi, sel_lo], axis=0), y_hi, preferred_element_type=F32)
    y = both[:tt] + both[tt:] + jnp.dot(sel_hi, y_lo, preferred_element_type=F32)
    o_ref[...] = x_ref[...] + g2_ref[...] * y


def _combine(x, slots, gates, g2, y_rows, tile_cnt, tile_seg, tile_off):
    s, d = x.shape
    tt = MOE_TILE
    row = lambda i, c, sg, of: (i, 0)
    grid_spec = pltpu.PrefetchScalarGridSpec(
        num_scalar_prefetch=3,
        grid=(s // tt,),
        in_specs=[pl.BlockSpec((tt, d), row), pl.BlockSpec((tt, LANE), row), pl.BlockSpec((tt, LANE), row),
                  pl.BlockSpec((1, d), lambda i, c, sg, of: (0, 0)),
                  pl.BlockSpec(memory_space=pl.ANY)],
        out_specs=pl.BlockSpec((tt, d), row),
        scratch_shapes=[pltpu.VMEM((MOE_SORT_ROWS, d), F32), pltpu.SemaphoreType.DMA(())],
    )
    return pl.pallas_call(
        _combine_kernel,
        grid_spec=grid_spec,
        out_shape=jax.ShapeDtypeStruct((s, d), F32),
        compiler_params=_cparams(("arbitrary",)),
    )(tile_cnt, tile_seg, tile_off, x, slots, gates, g2, y_rows)


def _moe(x, h, idx_pad, gates_pad, g2, w_up, b_up, w_down, b_down, layer_idx):
    s, d = x.shape
    n = s * TOP_K
    blk = MOE_BLK
    tt = MOE_TILE
    nt = s // tt
    n_blocks = -(-(n + nt * N_EXPERTS * SUBLANE) // blk) + N_EXPERTS
    i32 = jnp.int32
    idx = idx_pad[:, :TOP_K]
    onehot = jnp.sum((idx[:, :, None] == jnp.arange(N_EXPERTS, dtype=i32)).astype(i32), axis=1)
    onehot_t = onehot.reshape(nt, tt, N_EXPERTS)
    rank_in_tile = (jnp.cumsum(onehot_t, axis=1) - onehot_t).reshape(s, N_EXPERTS)
    tile_cnt = (jnp.sum(onehot_t, axis=1) + SUBLANE - 1) // SUBLANE * SUBLANE
    tile_base = jnp.cumsum(tile_cnt, axis=0) - tile_cnt
    tile_seg = jnp.cumsum(tile_cnt, axis=1) - tile_cnt
    counts = jnp.sum(tile_cnt, axis=0)
    padded = (counts + blk - 1) // blk * blk
    pad_ends = jnp.cumsum(padded)
    pad_starts = pad_ends - padded
    tile_off = pad_starts[None, :] + tile_base
    slot = (jnp.take_along_axis(jnp.repeat(tile_seg, tt, axis=0) + rank_in_tile, idx, axis=1)).astype(i32)
    slot_pad = jnp.pad(slot, ((0, 0), (0, LANE - TOP_K)))
    blk_start = jnp.arange(n_blocks, dtype=i32) * blk
    blk_expert = jnp.minimum(jnp.sum((pad_ends[None, :] <= blk_start[:, None]).astype(i32), axis=1),
                             N_EXPERTS - 1).astype(i32)
    first = jnp.concatenate([jnp.ones((1,), i32), (blk_expert[1:] != blk_expert[:-1]).astype(i32)])
    n_used = (pad_ends[-1:] // blk).astype(i32)
    tile_cnt, tile_seg, tile_off = [t.astype(i32).reshape(-1) for t in (tile_cnt, tile_seg, tile_off)]

    x_rows = _dispatch(h, slot.T, tile_cnt, tile_seg, tile_off, n_blocks * blk)
    y_rows = _ffn(x_rows, blk_expert + layer_idx * N_EXPERTS, first, n_used, w_up, b_up, w_down, b_down)
    return _combine(x, slot_pad, gates_pad, g2, y_rows, tile_cnt, tile_seg, tile_off)


def _block_diag(w):
    n, d, _ = w.shape
    eye = jnp.eye(n, dtype=w.dtype)
    return (eye[:, None, :, None] * w[:, :, None, :]).reshape(n * d, n * d)


def kernel(x, c, ada_w, ada_b, norm1_g, norm2_g, w_in, w_out, a_q_gain, a_k_gain, a_lam_q1, a_lam_k1, a_lam_q2, a_lam_k2, a_out_gain, b_q_gain, b_k_gain, c_w_g2, c_b_g, c_out_gain, d_conv_w, d_conv_b, d_w_a, d_b_a, d_w_x, d_b_x, d_lambda, router_w, router_b, exp_w_up, exp_b_up, exp_w_down, exp_b_down):
    bsz, s, d = x.shape
    assert bsz == 1 and d == D_MODEL and s % ROW_TILE == 0
    depth = ada_w.shape[0]
    mod = _modulation(c, ada_w, ada_b)
    xs = x.reshape(s, d)
    w_up_all = exp_w_up.reshape(depth * N_EXPERTS, d, 2 * D_FF)
    b_up_all = exp_b_up.reshape(depth * N_EXPERTS, 2 * D_FF)
    w_down_all = exp_w_down.reshape(depth * N_EXPERTS, D_FF, d)
    b_down_all = exp_b_down.reshape(depth * N_EXPERTS, d)
    for l in range(depth):
        sh1, sc1, g1, sh2, sc2, g2 = [mod[l, i * d:(i + 1) * d].reshape(1, d) for i in range(6)]
        w = w_in[l]
        w_cat = jnp.concatenate([w[:, :2048], w[:, 2064:2832], w[:, 2048:2064],
                                 jnp.zeros((d, W_IN_PAD - 2832), w.dtype)], axis=1).astype(BF16)
        wg2p = jnp.zeros((LANE, LANE), F32).at[:GLA_RANK, :].set(c_w_g2[l])
        (aqT, ak, avT, bqT, bk, bvT, bkm, cq, ck, cv, cr, cg, dx, dgate) = _inproj(
            xs, norm1_g[l].reshape(1, d), sc1, sh1, w_cat,
            jnp.tile(a_q_gain[l], 2 * A_HEADS).reshape(1, -1), jnp.tile(a_k_gain[l], 2 * A_HEADS).reshape(1, -1),
            jnp.tile(b_q_gain[l], B_HEADS).reshape(1, -1), jnp.tile(b_k_gain[l], B_HEADS).reshape(1, -1),
            wg2p, c_b_g[l].reshape(1, -1))
        o_aT = _diff_attention(aqT, ak, avT, a_lam_q1[l].reshape(1, -1), a_lam_k1[l].reshape(1, -1),
                               a_lam_q2[l].reshape(1, -1), a_lam_k2[l].reshape(1, -1),
                               a_out_gain[l].reshape(-1, 1), l)
        km = bkm.reshape(s // MOBA_BLOCK, B_HEADS, B_DIM).transpose(1, 0, 2)
        o_bT = _moba(bqT, bk, bvT, km)
        o_c = _gla(cq, ck, cg, cv, cr, jnp.tile(c_out_gain[l], C_HEADS).reshape(1, -1))
        o_d = _rglru(dx, dgate, d_conv_w[l], d_conv_b[l].reshape(1, -1),
                     _block_diag(d_w_a[l]).astype(BF16), d_b_a[l].reshape(1, -1),
                     _block_diag(d_w_x[l]).astype(BF16), d_b_x[l].reshape(1, -1), d_lambda[l].reshape(1, -1))
        w4 = w_out[l].astype(BF16).reshape(4, N_GROUP_W, d)
        xs, h2, idx_pad, gates_pad = _outproj(xs, o_aT, o_bT, o_c, o_d, w4, g1, norm2_g[l].reshape(1, d),
                                              sc2, sh2, router_w[l], router_b[l].reshape(1, -1))
        xs = _moe(xs, h2, idx_pad, gates_pad, g2, w_up_all, b_up_all, w_down_all, b_down_all, l)
    return xs.reshape(bsz, s, d)
```

```python
import functools
import math

import jax
import jax.numpy as jnp
from jax import lax
from jax.experimental import pallas as pl
from jax.experimental.pallas import tpu as pltpu

F32 = jnp.float32
BF16 = jnp.bfloat16
HI = lax.Precision.HIGHEST

EPS = 1e-6
NEG = -1e30
M_INIT = -1e30
MASKED = -2e30
LOG2E = 1.4426950408889634

D_MODEL = 1024
N_GROUP_W = 256
A_HEADS, A_QK, A_V = 4, 32, 64
B_HEADS, B_DIM = 4, 64
MOBA_BLOCK, MOBA_TOPK = 256, 3
C_HEADS, C_K, C_V = 4, 32, 64
GLA_RANK, GLA_TAU = 16, 16.0
GLA_SUB = 16
D_WIDTH, D_BLOCKS, CONV_W, LRU_C = 256, 4, 4, 8.0
N_EXPERTS, TOP_K, D_FF = 32, 4, 1024
SWIGLU_ALPHA, SWIGLU_LIMIT = 1.702, 7.0

LANE = 128
SUBLANE = 8
W_IN_PAD = 2944

ROW_TILE = 512
ATT_TILE = 256
KEY_TILE = 2 * ATT_TILE
SUM_ROWS = 16
MOE_BLK = 256
MOE_TILE = 256
MOE_SORT_ROWS = MOE_TILE * TOP_K + N_EXPERTS * SUBLANE
VMEM_LIMIT = 56 * 1024 * 1024


def _cparams(sem):
    return pltpu.CompilerParams(dimension_semantics=sem, vmem_limit_bytes=VMEM_LIMIT)


def _log_sigmoid(z):
    return jnp.minimum(z, 0.0) - jnp.log1p(jnp.exp(-jnp.abs(z)))


def _sigmoid(z):
    return 1.0 / (1.0 + jnp.exp(-z))


def _group_rms(t, group):
    n = t.shape[-1]
    shift = int(math.log2(group))
    r = lax.broadcasted_iota(jnp.int32, (n, n), 0) >> shift
    c = lax.broadcasted_iota(jnp.int32, (n, n), 1) >> shift
    bd = jnp.where(r == c, 1.0, 0.0).astype(F32)
    ss = jnp.dot(t * t, bd, precision=HI, preferred_element_type=F32)
    return t * lax.rsqrt(ss * (1.0 / group) + EPS)


def _mod_kernel(c_ref, w_ref, b_ref, o_ref):
    c = c_ref[...]
    cond = c * _sigmoid(c)
    cond8 = jnp.broadcast_to(cond, (8, cond.shape[-1]))
    o_ref[0] = jnp.dot(cond8, w_ref[0], precision=HI, preferred_element_type=F32) + b_ref[0]


def _modulation(c, ada_w, ada_b):
    depth, d, n = ada_w.shape
    tn = 1536
    out = pl.pallas_call(
        _mod_kernel,
        grid=(depth, n // tn),
        in_specs=[pl.BlockSpec((1, d), lambda l, j: (0, 0)),
                  pl.BlockSpec((1, d, tn), lambda l, j: (l, 0, j)),
                  pl.BlockSpec((1, 1, tn), lambda l, j: (l, 0, j))],
        out_specs=pl.BlockSpec((1, 8, tn), lambda l, j: (l, 0, j)),
        out_shape=jax.ShapeDtypeStruct((depth, 8, n), F32),
        compiler_params=_cparams(("arbitrary", "arbitrary")),
    )(c, ada_w, ada_b.reshape(depth, 1, n))
    return out[:, 0, :]


def _inproj_kernel(x_ref, ng_ref, sc_ref, sh_ref, w_ref, aqg_ref, akg_ref, bqg_ref, bkg_ref, wg2_ref, bg_ref,
                   aqT_ref, ak_ref, avT_ref, bqT_ref, bk_ref, bvT_ref, bkm_ref,
                   cq_ref, ck_ref, cv_ref, cr_ref, cg_ref, dx_ref, dg_ref):
    x = x_ref[...]
    tm = x.shape[0]
    h = x * lax.rsqrt(jnp.mean(x * x, axis=-1, keepdims=True) + EPS) * ng_ref[...]
    h = h * (1.0 + sc_ref[...]) + sh_ref[...]
    proj = jnp.dot(h.astype(BF16), w_ref[...], preferred_element_type=F32)

    aq = _group_rms(proj[:, 0:256], A_QK) * aqg_ref[...] * (A_QK ** -0.5 * LOG2E)
    aqT_ref[...] = aq.T.astype(BF16)
    ak = _group_rms(proj[:, 256:512], A_QK) * akg_ref[...]
    ak_ref[...] = ak.astype(BF16)
    avT_ref[...] = proj[:, 512:768].T.astype(BF16)

    bq = _group_rms(proj[:, 768:1024], B_DIM) * bqg_ref[...] * (B_DIM ** -0.5 * LOG2E)
    bqT_ref[...] = bq.T.astype(BF16)
    bk = _group_rms(proj[:, 1024:1280], B_DIM) * bkg_ref[...]
    bk_ref[...] = bk.astype(BF16)
    bvT_ref[...] = proj[:, 1280:1536].T.astype(BF16)
    bkm_ref[0] = jnp.mean(bk.reshape(tm // MOBA_BLOCK, MOBA_BLOCK, N_GROUP_W), axis=1)

    cq_ref[...] = proj[:, 1536:1664] * (C_K ** -0.5)
    ck_ref[...] = proj[:, 1664:1792]
    cv_ref[...] = proj[:, 1792:2048]
    cr_ref[...] = proj[:, 2048:2304]
    z = jnp.dot(proj[:, 2816:2944], wg2_ref[...], precision=HI, preferred_element_type=F32) + bg_ref[...]
    cg_ref[...] = _log_sigmoid(z) * (1.0 / GLA_TAU)

    dx_ref[...] = proj[:, 2304:2560]
    dg_ref[...] = proj[:, 2560:2816]


def _inproj(x, ng, sc, sh, w_cat, aqg, akg, bqg, bkg, wg2p, bg):
    s, d = x.shape
    tm = ROW_TILE
    nt = s // tm
    row = lambda i: (i, 0)
    const2 = lambda i: (0, 0)
    vec = pl.BlockSpec((1, d), const2)
    gvec = pl.BlockSpec((1, N_GROUP_W), const2)
    out_shape = (
        jax.ShapeDtypeStruct((N_GROUP_W, s), BF16),
        jax.ShapeDtypeStruct((s, N_GROUP_W), BF16),
        jax.ShapeDtypeStruct((N_GROUP_W, s), BF16),
        jax.ShapeDtypeStruct((N_GROUP_W, s), BF16),
        jax.ShapeDtypeStruct((s, N_GROUP_W), BF16),
        jax.ShapeDtypeStruct((N_GROUP_W, s), BF16),
        jax.ShapeDtypeStruct((nt, tm // MOBA_BLOCK, N_GROUP_W), F32),
        jax.ShapeDtypeStruct((s, 128), F32),
        jax.ShapeDtypeStruct((s, 128), F32),
        jax.ShapeDtypeStruct((s, 256), F32),
        jax.ShapeDtypeStruct((s, 256), F32),
        jax.ShapeDtypeStruct((s, 128), F32),
        jax.ShapeDtypeStruct((s, 256), F32),
        jax.ShapeDtypeStruct((s, 256), F32),
    )
    out_specs = (
        pl.BlockSpec((N_GROUP_W, tm), lambda i: (0, i)),
        pl.BlockSpec((tm, N_GROUP_W), row),
        pl.BlockSpec((N_GROUP_W, tm), lambda i: (0, i)),
        pl.BlockSpec((N_GROUP_W, tm), lambda i: (0, i)),
        pl.BlockSpec((tm, N_GROUP_W), row),
        pl.BlockSpec((N_GROUP_W, tm), lambda i: (0, i)),
        pl.BlockSpec((1, tm // MOBA_BLOCK, N_GROUP_W), lambda i: (i, 0, 0)),
        pl.BlockSpec((tm, 128), row),
        pl.BlockSpec((tm, 128), row),
        pl.BlockSpec((tm, 256), row),
        pl.BlockSpec((tm, 256), row),
        pl.BlockSpec((tm, 128), row),
        pl.BlockSpec((tm, 256), row),
        pl.BlockSpec((tm, 256), row),
    )
    return pl.pallas_call(
        _inproj_kernel,
        grid=(nt,),
        in_specs=[pl.BlockSpec((tm, d), row), vec, vec, vec,
                  pl.BlockSpec((d, W_IN_PAD), const2),
                  gvec, gvec, gvec, gvec,
                  pl.BlockSpec((LANE, LANE), const2),
                  pl.BlockSpec((1, LANE), const2)],
        out_specs=out_specs,
        out_shape=out_shape,
        compiler_params=_cparams(("arbitrary",)),
    )(x, ng, sc, sh, w_cat, aqg, akg, bqg, bkg, wg2p, bg)


def _pad_query_rows(qT_ref, qpad_ref, width):
    per = LANE // width
    rows = lax.broadcasted_iota(jnp.int32, qpad_ref.shape[1:], 0)
    for g in range(qpad_ref.shape[0]):
        blk = qT_ref[LANE * (g // per):LANE * (g // per + 1), :]
        lo = width * (g % per)
        qpad_ref[g] = jnp.where((rows >= lo) & (rows < lo + width), blk, jnp.zeros_like(blk))


def _with_sum_rows(vT):
    return jnp.concatenate([vT, jnp.ones((SUM_ROWS, vT.shape[1]), vT.dtype)], axis=0)


def _softmax_tiles(k_slabs, qpad_ref, v_tiles, adjust, m_ref, acc_ref):
    n = len(k_slabs)
    scores = [jnp.dot(k_slabs[g], qpad_ref[g], preferred_element_type=F32) for g in range(n)]
    probs, alphas = [], []
    for g in range(n):
        sT = adjust(g, scores[g])
        m_old = m_ref[g]
        m_new = jnp.maximum(m_old, jnp.max(sT, axis=0, keepdims=True))
        alphas.append(jnp.exp2(m_old - m_new))
        probs.append(jnp.exp2(sT - m_new).astype(BF16))
        m_ref[g] = m_new
    pv = [jnp.dot(v_tiles[g], probs[g], preferred_element_type=F32) for g in range(n)]
    for g in range(n):
        acc_ref[g] = alphas[g] * acc_ref[g] + pv[g]


def _diffattn_kernel(qT_ref, k_ref, vT_ref, lq1_ref, lk1_ref, lq2_ref, lk2_ref, og_ref, o_ref,
                     m_ref, acc_ref, qpad_ref, *, lam_init):
    t, tk = ATT_TILE, KEY_TILE
    n_maps = 2 * A_HEADS
    iq = pl.program_id(0)
    lam = (jnp.exp(jnp.sum(lq1_ref[...] * lk1_ref[...], axis=-1, keepdims=True))
           - jnp.exp(jnp.sum(lq2_ref[...] * lk2_ref[...], axis=-1, keepdims=True)) + lam_init)
    m_ref[...] = jnp.full(m_ref.shape, M_INIT, F32)
    acc_ref[...] = jnp.zeros(acc_ref.shape, F32)
    _pad_query_rows(qT_ref, qpad_ref, A_QK)

    def tile(ik, adjust):
        keys = pl.ds(pl.multiple_of(ik * tk, tk), tk)
        k_slabs = [k_ref[keys, LANE * (g // 4):LANE * (g // 4 + 1)] for g in range(n_maps)]
        v_heads = [_with_sum_rows(vT_ref[A_V * hd:A_V * (hd + 1), keys]) for hd in range(A_HEADS)]
        _softmax_tiles(k_slabs, qpad_ref, [v_heads[g // 2] for g in range(n_maps)], adjust, m_ref, acc_ref)

    def body(ik, carry):
        tile(ik, lambda g, sT: sT)
        return carry

    n_full = iq // 2
    lax.fori_loop(0, n_full, body, 0)
    causal = (lax.broadcasted_iota(jnp.int32, (tk, t), 0) - lax.broadcasted_iota(jnp.int32, (tk, t), 1)
              <= (iq - 2 * n_full) * t)
    tile(n_full, lambda g, sT: jnp.where(causal, sT, MASKED))

    for hd in range(A_HEADS):
        a1, a2 = acc_ref[2 * hd], acc_ref[2 * hd + 1]
        o = a1[:A_V] / a1[A_V:A_V + 1] - lam * (a2[:A_V] / a2[A_V:A_V + 1])
        o = o * lax.rsqrt(jnp.mean(o * o, axis=0, keepdims=True) + EPS) * og_ref[...]
        o_ref[A_V * hd:A_V * (hd + 1), :] = o * (1.0 - lam_init)


def _diff_attention(aqT, ak, avT, lq1, lk1, lq2, lk2, out_gain_col, layer_idx):
    s = aqT.shape[1]
    t = ATT_TILE
    n_maps = 2 * A_HEADS
    lam_init = 0.8 - 0.6 * math.exp(-0.3 * layer_idx)
    vec = pl.BlockSpec((1, A_QK), lambda i: (0, 0))
    return pl.pallas_call(
        functools.partial(_diffattn_kernel, lam_init=lam_init),
        grid=(s // t,),
        in_specs=[pl.BlockSpec((N_GROUP_W, t), lambda i: (0, i)),
                  pl.BlockSpec((s, N_GROUP_W), lambda i: (0, 0)),
                  pl.BlockSpec((N_GROUP_W, s), lambda i: (0, 0)),
                  vec, vec, vec, vec,
                  pl.BlockSpec((A_V, 1), lambda i: (0, 0))],
        out_specs=pl.BlockSpec((N_GROUP_W, t), lambda i: (0, i)),
        out_shape=jax.ShapeDtypeStruct((N_GROUP_W, s), F32),
        scratch_shapes=[pltpu.VMEM((n_maps, 1, t), F32), pltpu.VMEM((n_maps, A_V + SUM_ROWS, t), F32),
                        pltpu.VMEM((n_maps, LANE, t), BF16)],
        compiler_params=_cparams(("arbitrary",)),
    )(aqT, ak, avT, lq1, lk1, lq2, lk2, out_gain_col)


def _moba_kernel(qT_ref, k_ref, vT_ref, km_ref, o_ref, m_ref, acc_ref, bias_ref, qpad_ref):
    t, tk = ATT_TILE, KEY_TILE
    own = pl.program_id(0)
    nb = km_ref.shape[1]
    m_ref[...] = jnp.full(m_ref.shape, M_INIT, F32)
    acc_ref[...] = jnp.zeros(acc_ref.shape, F32)
    _pad_query_rows(qT_ref, qpad_ref, B_DIM)

    blk = lax.broadcasted_iota(jnp.int32, (nb, t), 0)
    earlier = blk < own
    for hd in range(B_HEADS):
        qT = qT_ref[B_DIM * hd:B_DIM * (hd + 1), :].astype(F32)
        gate = jnp.dot(km_ref[hd], qT, precision=HI, preferred_element_type=F32)
        gate = jnp.where(earlier, gate, NEG)
        sel_bias = jnp.full((nb, t), MASKED, F32)
        for _ in range(MOBA_TOPK):
            top = jnp.max(gate, axis=0, keepdims=True)
            pick = blk == jnp.min(jnp.where(gate == top, blk, nb), axis=0, keepdims=True)
            sel_bias = jnp.where(pick, jnp.where(earlier, 0.0, MASKED), sel_bias)
            gate = jnp.where(pick, -jnp.inf, gate)
        bias_ref[hd] = sel_bias

    def tile(j, adjust):
        keys = pl.ds(pl.multiple_of(j * tk, tk), tk)
        k_slabs = [k_ref[keys, LANE * (hd // 2):LANE * (hd // 2 + 1)] for hd in range(B_HEADS)]
        v_tiles = [_with_sum_rows(vT_ref[B_DIM * hd:B_DIM * (hd + 1), keys]) for hd in range(B_HEADS)]
        _softmax_tiles(k_slabs, qpad_ref, v_tiles, adjust, m_ref, acc_ref)

    def block_bias(hd, b):
        return bias_ref[hd, pl.ds(b, 1), :]

    def body(j, carry):
        tile(j, lambda hd, sT: jnp.concatenate([sT[:t] + block_bias(hd, 2 * j),
                                                sT[t:] + block_bias(hd, 2 * j + 1)], axis=0))
        return carry

    n_full = own // 2
    lax.fori_loop(0, n_full, body, 0)

    causal = (lax.broadcasted_iota(jnp.int32, (t, t), 0) <= lax.broadcasted_iota(jnp.int32, (t, t), 1))

    def last(hd, sT):
        halves = []
        for half in range(2):
            b = 2 * n_full + half
            sh = sT[t * half:t * (half + 1)]
            other = sh + jnp.where(b < own, block_bias(hd, b), MASKED)
            halves.append(jnp.where(b == own, jnp.where(causal, sh, MASKED), other))
        return jnp.concatenate(halves, axis=0)

    tile(n_full, last)
    for hd in range(B_HEADS):
        a = acc_ref[hd]
        o_ref[B_DIM * hd:B_DIM * (hd + 1), :] = a[:B_DIM] / a[B_DIM:B_DIM + 1]


def _moba(bqT, bk, bvT, km):
    s = bqT.shape[1]
    t = ATT_TILE
    nb = km.shape[1]
    return pl.pallas_call(
        _moba_kernel,
        grid=(s // t,),
        in_specs=[pl.BlockSpec((N_GROUP_W, t), lambda i: (0, i)),
                  pl.BlockSpec((s, N_GROUP_W), lambda i: (0, 0)),
                  pl.BlockSpec((N_GROUP_W, s), lambda i: (0, 0)),
                  pl.BlockSpec((B_HEADS, nb, B_DIM), lambda i: (0, 0, 0))],
        out_specs=pl.BlockSpec((N_GROUP_W, t), lambda i: (0, i)),
        out_shape=jax.ShapeDtypeStruct((N_GROUP_W, s), F32),
        scratch_shapes=[pltpu.VMEM((B_HEADS, 1, t), F32),
                        pltpu.VMEM((B_HEADS, B_DIM + SUM_ROWS, t), F32), pltpu.VMEM((B_HEADS, nb, t), F32),
                        pltpu.VMEM((B_HEADS, LANE, t), BF16)],
        compiler_params=_cparams(("arbitrary",)),
    )(bqT, bk, bvT, km)


def _gla_kernel(q_ref, k_ref, g_ref, v_ref, r_ref, og_ref, o_ref, stateT_ref, kbuf_ref, bbuf_ref, vbuf_ref):
    tm = q_ref.shape[0]
    c = GLA_SUB
    shift = int(math.log2(c))

    @pl.when(pl.program_id(0) == 0)
    def _():
        stateT_ref[...] = jnp.zeros(stateT_ref.shape, F32)

    q, k, v = q_ref[...], k_ref[...], v_ref[...]
    ri = lax.broadcasted_iota(jnp.int32, (tm, tm), 0)
    ci = lax.broadcasted_iota(jnp.int32, (tm, tm), 1)
    same = (ri >> shift) == (ci >> shift)
    g = g_ref[...]
    b = jnp.dot(jnp.where(same & (ri >= ci), 1.0, 0.0).astype(F32), g, precision=HI, preferred_element_type=F32)
    b_last = jnp.dot(jnp.where(same, 1.0, 0.0).astype(F32), g, precision=HI, preferred_element_type=F32)

    k_shift, v_shift = int(math.log2(C_K)), int(math.log2(C_V))
    head_sum = jnp.where((lax.broadcasted_iota(jnp.int32, (C_HEADS * C_K, C_HEADS * C_V), 0) >> k_shift)
                         == (lax.broadcasted_iota(jnp.int32, (C_HEADS * C_K, C_HEADS * C_V), 1) >> v_shift),
                         1.0, 0.0).astype(BF16)

    for buf, val in ((kbuf_ref, k), (bbuf_ref, b), (vbuf_ref, v)):
        buf[0:c, :] = jnp.zeros((c, val.shape[1]), F32)
        buf[c:c + tm, :] = val
    pos = lax.broadcasted_iota(jnp.int32, (tm, C_HEADS * C_K), 0) & (c - 1)
    o = jnp.zeros((tm, C_HEADS * C_V), F32)
    for delta in range(c):
        k_d = kbuf_ref[c - delta:c - delta + tm, :]
        b_d = bbuf_ref[c - delta:c - delta + tm, :]
        v_d = vbuf_ref[c - delta:c - delta + tm, :]
        w = q * k_d * jnp.exp(jnp.where(pos >= delta, b - b_d, NEG))
        w_hi, w_lo = _split_bf16(w)
        att = (jnp.dot(w_hi, head_sum, preferred_element_type=F32)
               + jnp.dot(w_lo, head_sum, preferred_element_type=F32))
        o = o + att * v_d

    q_dec = (q * jnp.exp(b)).astype(BF16)
    k_dec = (k * jnp.exp(b_last - b)).astype(BF16)
    decay = jnp.exp(b_last)
    v_bf = v.astype(BF16)
    own_head = ((lax.broadcasted_iota(jnp.int32, (C_HEADS * C_V, C_HEADS * C_K), 0) >> v_shift)
                == (lax.broadcasted_iota(jnp.int32, (C_HEADS * C_V, C_HEADS * C_K), 1) >> k_shift))
    nt_dims = (((1,), (1,)), ((), ()))
    tn_dims = (((0,), (0,)), ((), ()))
    updates = [lax.dot_general(v_bf[c * i:c * (i + 1)], k_dec[c * i:c * (i + 1)], tn_dims,
                               preferred_element_type=F32) for i in range(tm // c)]
    stateT = stateT_ref[...]
    far = []
    for i in range(tm // c):
        far.append(lax.dot_general(q_dec[c * i:c * (i + 1)], stateT.astype(BF16), nt_dims,
                                   preferred_element_type=F32))
        stateT = stateT * decay[c * i:c * i + 1, :] + jnp.where(own_head, updates[i], 0.0)
    stateT_ref[...] = stateT
    o = o + jnp.concatenate(far, axis=0)

    r = r_ref[...]
    o_ref[...] = _group_rms(o, C_V) * og_ref[...] * (r * _sigmoid(r))


def _gla(cq, ck, cg, cv, cr, out_gain):
    s = cq.shape[0]
    tm = ROW_TILE
    row = lambda i: (i, 0)
    return pl.pallas_call(
        _gla_kernel,
        grid=(s // tm,),
        in_specs=[pl.BlockSpec((tm, 128), row), pl.BlockSpec((tm, 128), row), pl.BlockSpec((tm, 128), row),
                  pl.BlockSpec((tm, 256), row), pl.BlockSpec((tm, 256), row),
                  pl.BlockSpec((1, C_HEADS * C_V), lambda i: (0, 0))],
        out_specs=pl.BlockSpec((tm, 256), row),
        out_shape=jax.ShapeDtypeStruct((s, 256), F32),
        scratch_shapes=[pltpu.VMEM((C_HEADS * C_V, C_HEADS * C_K), F32),
                        pltpu.VMEM((tm + GLA_SUB, C_HEADS * C_K), F32),
                        pltpu.VMEM((tm + GLA_SUB, C_HEADS * C_K), F32),
                        pltpu.VMEM((tm + GLA_SUB, C_HEADS * C_V), F32)],
        compiler_params=_cparams(("arbitrary",)),
    )(cq, ck, cg, cv, cr, out_gain)


def _rglru_kernel(x_ref, gate_ref, cw_ref, cb_ref, wa_ref, ba_ref, wx_ref, bx_ref, lam_ref, o_ref,
                  xbuf_ref, h_ref):
    tm = x_ref.shape[0]
    pad = 8

    @pl.when(pl.program_id(0) == 0)
    def _():
        xbuf_ref[0:pad, :] = jnp.zeros((pad, D_WIDTH), F32)
        h_ref[...] = jnp.zeros(h_ref.shape, F32)

    xbuf_ref[pad:pad + tm, :] = x_ref[...]
    xc = cb_ref[...] + jnp.zeros((tm, D_WIDTH), F32)
    for w in range(CONV_W):
        lag = CONV_W - 1 - w
        xc = xc + cw_ref[w:w + 1, :] * xbuf_ref[pad - lag:pad - lag + tm, :]
    xbuf_ref[0:pad, :] = xbuf_ref[tm:tm + pad, :]

    xcb = xc.astype(BF16)
    r_gate = _sigmoid(jnp.dot(xcb, wa_ref[...], preferred_element_type=F32) + ba_ref[...])
    i_gate = _sigmoid(jnp.dot(xcb, wx_ref[...], preferred_element_type=F32) + bx_ref[...])
    log_a = LRU_C * r_gate * _log_sigmoid(lam_ref[...])
    a = jnp.exp(log_a)
    b = jnp.sqrt(1.0 - jnp.exp(2.0 * log_a)) * (i_gate * xc)

    rows = lax.broadcasted_iota(jnp.int32, (tm, D_WIDTH), 0)
    step = 1
    while step < tm:
        keep = rows >= step
        a_prev = jnp.where(keep, pltpu.roll(a, step, 0), 1.0)
        b_prev = jnp.where(keep, pltpu.roll(b, step, 0), 0.0)
        b = a * b_prev + b
        a = a * a_prev
        step *= 2
    h = a * h_ref[0:1, :] + b
    h_ref[...] = jnp.broadcast_to(h[tm - 1:tm, :], h_ref.shape)

    g = gate_ref[...]
    gelu = g * (0.5 * (1.0 + jnp.tanh(math.sqrt(2.0 / math.pi) * (g + 0.044715 * (g * g * g)))))
    o_ref[...] = h * gelu


def _rglru(dx, dgate, conv_w, conv_b, wa_bd, b_a, wx_bd, b_x, lam):
    s = dx.shape[0]
    tm = ROW_TILE
    row = lambda i: (i, 0)
    const2 = lambda i: (0, 0)
    vec = pl.BlockSpec((1, D_WIDTH), const2)
    mat = pl.BlockSpec((D_WIDTH, D_WIDTH), const2)
    return pl.pallas_call(
        _rglru_kernel,
        grid=(s // tm,),
        in_specs=[pl.BlockSpec((tm, D_WIDTH), row), pl.BlockSpec((tm, D_WIDTH), row),
                  pl.BlockSpec((CONV_W, D_WIDTH), const2), vec, mat, vec, mat, vec, vec],
        out_specs=pl.BlockSpec((tm, D_WIDTH), row),
        out_shape=jax.ShapeDtypeStruct((s, D_WIDTH), F32),
        scratch_shapes=[pltpu.VMEM((tm + 8, D_WIDTH), F32), pltpu.VMEM((8, D_WIDTH), F32)],
        compiler_params=_cparams(("arbitrary",)),
    )(dx, dgate, conv_w, conv_b, wa_bd, b_a, wx_bd, b_x, lam)


def _outproj_kernel(x_ref, oaT_ref, obT_ref, oc_ref, od_ref, w_ref, g1_ref, ng_ref, sc_ref, sh_ref,
                    rw_ref, rb_ref, xo_ref, h_ref, idx_ref, gate_ref):
    tm = x_ref.shape[0]
    y = jnp.dot(oaT_ref[...].T.astype(BF16), w_ref[0], preferred_element_type=F32)
    y = y + jnp.dot(obT_ref[...].T.astype(BF16), w_ref[1], preferred_element_type=F32)
    y = y + jnp.dot(oc_ref[...].astype(BF16), w_ref[2], preferred_element_type=F32)
    y = y + jnp.dot(od_ref[...].astype(BF16), w_ref[3], preferred_element_type=F32)
    x = x_ref[...] + g1_ref[...] * y
    xo_ref[...] = x
    h = x * lax.rsqrt(jnp.mean(x * x, axis=-1, keepdims=True) + EPS) * ng_ref[...]
    h = h * (1.0 + sc_ref[...]) + sh_ref[...]
    h_ref[...] = h

    logits = jnp.dot(h, rw_ref[...], precision=HI, preferred_element_type=F32) + rb_ref[...]
    e_ids = lax.broadcasted_iota(jnp.int32, (tm, N_EXPERTS), 1)
    lane = lax.broadcasted_iota(jnp.int32, (tm, LANE), 1)
    idx_out = jnp.zeros((tm, LANE), jnp.int32)
    val_out = jnp.zeros((tm, LANE), F32)
    top0 = None
    denom = jnp.zeros((tm, 1), F32)
    for k in range(TOP_K):
        top = jnp.max(logits, axis=-1, keepdims=True)
        arg = jnp.min(jnp.where(logits == top, e_ids, N_EXPERTS), axis=-1, keepdims=True)
        if k == 0:
            top0 = top
        w = jnp.exp(top - top0)
        denom = denom + w
        idx_out = jnp.where(lane == k, arg, idx_out)
        val_out = jnp.where(lane == k, w, val_out)
        logits = jnp.where(e_ids == arg, -jnp.inf, logits)
    idx_ref[...] = idx_out
    gate_ref[...] = val_out / denom


def _outproj(x, oaT, obT, oc, od, w4, g1, ng, sc, sh, rw, rb):
    s, d = x.shape
    tm = ROW_TILE
    row = lambda i: (i, 0)
    col = lambda i: (0, i)
    const2 = lambda i: (0, 0)
    vec = pl.BlockSpec((1, d), const2)
    return pl.pallas_call(
        _outproj_kernel,
        grid=(s // tm,),
        in_specs=[pl.BlockSpec((tm, d), row),
                  pl.BlockSpec((N_GROUP_W, tm), col), pl.BlockSpec((N_GROUP_W, tm), col),
                  pl.BlockSpec((tm, 256), row), pl.BlockSpec((tm, 256), row),
                  pl.BlockSpec((4, N_GROUP_W, d), lambda i: (0, 0, 0)),
                  vec, vec, vec, vec,
                  pl.BlockSpec((d, N_EXPERTS), const2), pl.BlockSpec((1, N_EXPERTS), const2)],
        out_specs=(pl.BlockSpec((tm, d), row), pl.BlockSpec((tm, d), row),
                   pl.BlockSpec((tm, LANE), row), pl.BlockSpec((tm, LANE), row)),
        out_shape=(jax.ShapeDtypeStruct((s, d), F32), jax.ShapeDtypeStruct((s, d), F32),
                   jax.ShapeDtypeStruct((s, LANE), jnp.int32), jax.ShapeDtypeStruct((s, LANE), F32)),
        compiler_params=_cparams(("arbitrary",)),
    )(x, oaT, obT, oc, od, w4, g1, ng, sc, sh, rw, rb)


def _segment_copies(tile, cnt_ref, seg_ref, off_ref, make_copy, start):
    def per_expert(e, carry):
        j = tile * N_EXPERTS + e
        cnt, a, b = cnt_ref[j], seg_ref[j], off_ref[j]
        size = MOE_TILE
        while size >= SUBLANE:
            bit = cnt & size

            @pl.when(bit != 0)
            def _():
                cp = make_copy(pl.multiple_of(a, SUBLANE), pl.multiple_of(b, SUBLANE), size)
                if start:
                    cp.start()
                else:
                    cp.wait()

            a, b = a + bit, b + bit
            size //= 2
        return carry

    lax.fori_loop(0, N_EXPERTS, per_expert, 0)


def _dispatch_kernel(cnt_ref, seg_ref, off_ref, nused_ref, h_ref, slotT_ref, rows_ref, sorted_ref, zero_ref, sem):
    tile = pl.program_id(0)
    tt = h_ref.shape[0]
    n = sorted_ref.shape[0]
    r_ids = lax.broadcasted_iota(jnp.int32, (n, tt), 0)
    perm = jnp.where(r_ids == slotT_ref[0:1, :], 1.0, 0.0)
    for k in range(1, TOP_K):
        perm = perm + jnp.where(r_ids == slotT_ref[k:k + 1, :], 1.0, 0.0)
    perm = perm.astype(BF16)
    sorted_ref[...] = jnp.dot(perm, h_ref[...].astype(BF16), preferred_element_type=F32)

    def make_copy(a, b, size):
        return pltpu.make_async_copy(sorted_ref.at[pl.ds(a, size)], rows_ref.at[pl.ds(b, size)], sem)

    _segment_copies(tile, cnt_ref, seg_ref, off_ref, make_copy, True)
    _segment_copies(tile, cnt_ref, seg_ref, off_ref, make_copy, False)

    last = pl.num_programs(0) - 1

    @pl.when(tile == last)
    def _():
        zero_ref[...] = jnp.zeros(zero_ref.shape, F32)

        def make_zero_copy(a, b, size):
            return pltpu.make_async_copy(zero_ref.at[pl.ds(a, size)], rows_ref.at[pl.ds(b, size)], sem)

        _segment_copies(last + 1, cnt_ref, seg_ref, off_ref, make_zero_copy, True)
        _segment_copies(last + 1, cnt_ref, seg_ref, off_ref, make_zero_copy, False)

        def block_copy(b):
            return pltpu.make_async_copy(zero_ref, rows_ref.at[pl.ds(pl.multiple_of(b * MOE_BLK, MOE_BLK), MOE_BLK)],
                                         sem)

        def issue(b, carry):
            block_copy(b).start()
            return carry

        def drain(b, carry):
            block_copy(b).wait()
            return carry

        n_blocks = rows_ref.shape[0] // MOE_BLK
        lax.fori_loop(nused_ref[0], n_blocks, issue, 0)
        lax.fori_loop(nused_ref[0], n_blocks, drain, 0)


def _dispatch(h, slotT, tile_cnt, tile_seg, tile_off, n_used, n_rows):
    s, d = h.shape
    tt = MOE_TILE
    grid_spec = pltpu.PrefetchScalarGridSpec(
        num_scalar_prefetch=4,
        grid=(s // tt,),
        in_specs=[pl.BlockSpec((tt, d), lambda i, c, sg, of, nu: (i, 0)),
                  pl.BlockSpec((TOP_K, tt), lambda i, c, sg, of, nu: (0, i))],
        out_specs=pl.BlockSpec(memory_space=pl.ANY),
        scratch_shapes=[pltpu.VMEM((MOE_SORT_ROWS, d), F32), pltpu.VMEM((MOE_BLK, d), F32),
                        pltpu.SemaphoreType.DMA(())],
    )
    return pl.pallas_call(
        _dispatch_kernel,
        grid_spec=grid_spec,
        out_shape=jax.ShapeDtypeStruct((n_rows, d), F32),
        compiler_params=_cparams(("arbitrary",)),
    )(tile_cnt, tile_seg, tile_off, n_used, h, slotT)


def _ffn_kernel(be_ref, first_ref, nused_ref, x_ref, wu_ref, bu_ref, wd_ref, bd_ref, y_ref, wu_bf, wd_bf):
    b = pl.program_id(0)

    @pl.when(first_ref[b] == 1)
    def _():
        wu_bf[...] = wu_ref[0].astype(BF16)
        wd_bf[...] = wd_ref[0].astype(BF16)

    @pl.when(b < nused_ref[0])
    def _():
        hu = jnp.dot(x_ref[...].astype(BF16), wu_bf[...], preferred_element_type=F32) + bu_ref[0]
        g = jnp.minimum(hu[:, :D_FF], SWIGLU_LIMIT)
        lin = jnp.clip(hu[:, D_FF:], -SWIGLU_LIMIT, SWIGLU_LIMIT)
        act = (lin + 1.0) * (g * _sigmoid(SWIGLU_ALPHA * g))
        y_ref[...] = jnp.dot(act.astype(BF16), wd_bf[...], preferred_element_type=F32) + bd_ref[0]

    @pl.when(b >= nused_ref[0])
    def _():
        y_ref[...] = jnp.zeros(y_ref.shape, F32)


def _ffn(x_rows, blk_expert, first, n_used, w_up, b_up, w_down, b_down):
    n_rows, d = x_rows.shape
    blk = MOE_BLK
    n_exp, _, f2 = w_up.shape
    grid_spec = pltpu.PrefetchScalarGridSpec(
        num_scalar_prefetch=3,
        grid=(n_rows // blk,),
        in_specs=[pl.BlockSpec((blk, d), lambda b, be, fi, nu: (jnp.minimum(b, nu[0] - 1), 0)),
                  pl.BlockSpec((1, d, f2), lambda b, be, fi, nu: (be[b], 0, 0)),
                  pl.BlockSpec((1, 1, f2), lambda b, be, fi, nu: (be[b], 0, 0)),
                  pl.BlockSpec((1, D_FF, d), lambda b, be, fi, nu: (be[b], 0, 0)),
                  pl.BlockSpec((1, 1, d), lambda b, be, fi, nu: (be[b], 0, 0))],
        out_specs=pl.BlockSpec((blk, d), lambda b, be, fi, nu: (b, 0)),
        scratch_shapes=[pltpu.VMEM((d, f2), BF16), pltpu.VMEM((D_FF, d), BF16)],
    )
    return pl.pallas_call(
        _ffn_kernel,
        grid_spec=grid_spec,
        out_shape=jax.ShapeDtypeStruct((n_rows, d), F32),
        compiler_params=_cparams(("arbitrary",)),
    )(blk_expert, first, n_used, x_rows, w_up, b_up.reshape(n_exp, 1, f2), w_down, b_down.reshape(n_exp, 1, d))


def _split_bf16(t):
    hi = t.astype(BF16)
    return hi, (t - hi.astype(F32)).astype(BF16)


def _combine_kernel(cnt_ref, seg_ref, off_ref, x_ref, slot_ref, gate_ref, g2_ref, rows_ref, o_ref, buf_ref, sem):
    tile = pl.program_id(0)
    tt = x_ref.shape[0]
    n = buf_ref.shape[0]

    def make_copy(a, b, size):
        return pltpu.make_async_copy(rows_ref.at[pl.ds(b, size)], buf_ref.at[pl.ds(a, size)], sem)

    @pl.when(tile == 0)
    def _():
        buf_ref[...] = jnp.zeros(buf_ref.shape, F32)

    _segment_copies(tile, cnt_ref, seg_ref, off_ref, make_copy, True)
    slots, gates = slot_ref[...], gate_ref[...]
    r_ids = lax.broadcasted_iota(jnp.int32, (tt, n), 1)
    sel = jnp.where(r_ids == slots[:, 0:1], gates[:, 0:1], 0.0)
    for k in range(1, TOP_K):
        sel = sel + jnp.where(r_ids == slots[:, k:k + 1], gates[:, k:k + 1], 0.0)
    sel_hi, sel_lo = _split_bf16(sel)
    _segment_copies(tile, cnt_ref, seg_ref, off_ref, make_copy, False)
    y_hi, y_lo = _split_bf16(buf_ref[...])
    both = jnp.dot(jnp.concatenate([sel_hi, sel_lo], axis=0), y_hi, preferred_element_type=F32)
    y = both[:tt] + both[tt:] + jnp.dot(sel_hi, y_lo, preferred_element_type=F32)
    o_ref[...] = x_ref[...] + g2_ref[...] * y


def _combine(x, slots, gates, g2, y_rows, tile_cnt, tile_seg, tile_off):
    s, d = x.shape
    tt = MOE_TILE
    row = lambda i, c, sg, of: (i, 0)
    grid_spec = pltpu.PrefetchScalarGridSpec(
        num_scalar_prefetch=3,
        grid=(s // tt,),
        in_specs=[pl.BlockSpec((tt, d), row), pl.BlockSpec((tt, LANE), row), pl.BlockSpec((tt, LANE), row),
                  pl.BlockSpec((1, d), lambda i, c, sg, of: (0, 0)),
                  pl.BlockSpec(memory_space=pl.ANY)],
        out_specs=pl.BlockSpec((tt, d), row),
        scratch_shapes=[pltpu.VMEM((MOE_SORT_ROWS, d), F32), pltpu.SemaphoreType.DMA(())],
    )
    return pl.pallas_call(
        _combine_kernel,
        grid_spec=grid_spec,
        out_shape=jax.ShapeDtypeStruct((s, d), F32),
        compiler_params=_cparams(("arbitrary",)),
    )(tile_cnt, tile_seg, tile_off, x, slots, gates, g2, y_rows)


def _moe(x, h, idx_pad, gates_pad, g2, w_up, b_up, w_down, b_down, layer_idx):
    s, d = x.shape
    n = s * TOP_K
    blk = MOE_BLK
    tt = MOE_TILE
    nt = s // tt
    n_blocks = -(-(n + nt * N_EXPERTS * SUBLANE) // blk) + N_EXPERTS
    i32 = jnp.int32
    idx = idx_pad[:, :TOP_K]
    onehot = jnp.sum((idx[:, :, None] == jnp.arange(N_EXPERTS, dtype=i32)).astype(i32), axis=1)
    onehot_t = onehot.reshape(nt, tt, N_EXPERTS)
    rank_in_tile = (jnp.cumsum(onehot_t, axis=1) - onehot_t).reshape(s, N_EXPERTS)
    tile_cnt = (jnp.sum(onehot_t, axis=1) + SUBLANE - 1) // SUBLANE * SUBLANE
    tile_base = jnp.cumsum(tile_cnt, axis=0) - tile_cnt
    tile_seg = jnp.cumsum(tile_cnt, axis=1) - tile_cnt
    counts = jnp.sum(tile_cnt, axis=0)
    padded = (counts + blk - 1) // blk * blk
    pad_ends = jnp.cumsum(padded)
    pad_starts = pad_ends - padded
    tile_off = pad_starts[None, :] + tile_base
    slot = (jnp.take_along_axis(jnp.repeat(tile_seg, tt, axis=0) + rank_in_tile, idx, axis=1)).astype(i32)
    slot_pad = jnp.pad(slot, ((0, 0), (0, LANE - TOP_K)))
    blk_start = jnp.arange(n_blocks, dtype=i32) * blk
    blk_expert = jnp.minimum(jnp.sum((pad_ends[None, :] <= blk_start[:, None]).astype(i32), axis=1),
                             N_EXPERTS - 1).astype(i32)
    first = jnp.concatenate([jnp.ones((1,), i32), (blk_expert[1:] != blk_expert[:-1]).astype(i32)])
    n_used = (pad_ends[-1:] // blk).astype(i32)
    tile_cnt = jnp.concatenate([tile_cnt, (padded - counts)[None, :]], axis=0)
    tile_seg = jnp.concatenate([tile_seg, jnp.zeros((1, N_EXPERTS), tile_seg.dtype)], axis=0)
    tile_off = jnp.concatenate([tile_off, (pad_starts + counts)[None, :]], axis=0)
    tile_cnt, tile_seg, tile_off = [t.astype(i32).reshape(-1) for t in (tile_cnt, tile_seg, tile_off)]

    x_rows = _dispatch(h, slot.T, tile_cnt, tile_seg, tile_off, n_used, n_blocks * blk)
    y_rows = _ffn(x_rows, blk_expert + layer_idx * N_EXPERTS, first, n_used, w_up, b_up, w_down, b_down)
    return _combine(x, slot_pad, gates_pad, g2, y_rows, tile_cnt, tile_seg, tile_off)


def _block_diag(w):
    n, d, _ = w.shape
    eye = jnp.eye(n, dtype=w.dtype)
    return (eye[:, None, :, None] * w[:, :, None, :]).reshape(n * d, n * d)


def kernel(x, c, ada_w, ada_b, norm1_g, norm2_g, w_in, w_out, a_q_gain, a_k_gain, a_lam_q1, a_lam_k1, a_lam_q2, a_lam_k2, a_out_gain, b_q_gain, b_k_gain, c_w_g2, c_b_g, c_out_gain, d_conv_w, d_conv_b, d_w_a, d_b_a, d_w_x, d_b_x, d_lambda, router_w, router_b, exp_w_up, exp_b_up, exp_w_down, exp_b_down):
    bsz, s, d = x.shape
    assert bsz == 1 and d == D_MODEL and s % ROW_TILE == 0
    depth = ada_w.shape[0]
    mod = _modulation(c, ada_w, ada_b)
    xs = x.reshape(s, d)
    w_up_all = exp_w_up.reshape(depth * N_EXPERTS, d, 2 * D_FF)
    b_up_all = exp_b_up.reshape(depth * N_EXPERTS, 2 * D_FF)
    w_down_all = exp_w_down.reshape(depth * N_EXPERTS, D_FF, d)
    b_down_all = exp_b_down.reshape(depth * N_EXPERTS, d)
    for l in range(depth):
        sh1, sc1, g1, sh2, sc2, g2 = [mod[l, i * d:(i + 1) * d].reshape(1, d) for i in range(6)]
        w = w_in[l]
        w_cat = jnp.concatenate([w[:, :2048], w[:, 2064:2832], w[:, 2048:2064],
                                 jnp.zeros((d, W_IN_PAD - 2832), w.dtype)], axis=1).astype(BF16)
        wg2p = jnp.zeros((LANE, LANE), F32).at[:GLA_RANK, :].set(c_w_g2[l])
        (aqT, ak, avT, bqT, bk, bvT, bkm, cq, ck, cv, cr, cg, dx, dgate) = _inproj(
            xs, norm1_g[l].reshape(1, d), sc1, sh1, w_cat,
            jnp.tile(a_q_gain[l], 2 * A_HEADS).reshape(1, -1), jnp.tile(a_k_gain[l], 2 * A_HEADS).reshape(1, -1),
            jnp.tile(b_q_gain[l], B_HEADS).reshape(1, -1), jnp.tile(b_k_gain[l], B_HEADS).reshape(1, -1),
            wg2p, c_b_g[l].reshape(1, -1))
        o_aT = _diff_attention(aqT, ak, avT, a_lam_q1[l].reshape(1, -1), a_lam_k1[l].reshape(1, -1),
                               a_lam_q2[l].reshape(1, -1), a_lam_k2[l].reshape(1, -1),
                               a_out_gain[l].reshape(-1, 1), l)
        km = bkm.reshape(s // MOBA_BLOCK, B_HEADS, B_DIM).transpose(1, 0, 2)
        o_bT = _moba(bqT, bk, bvT, km)
        o_c = _gla(cq, ck, cg, cv, cr, jnp.tile(c_out_gain[l], C_HEADS).reshape(1, -1))
        o_d = _rglru(dx, dgate, d_conv_w[l], d_conv_b[l].reshape(1, -1),
                     _block_diag(d_w_a[l]).astype(BF16), d_b_a[l].reshape(1, -1),
                     _block_diag(d_w_x[l]).astype(BF16), d_b_x[l].reshape(1, -1), d_lambda[l].reshape(1, -1))
        w4 = w_out[l].astype(BF16).reshape(4, N_GROUP_W, d)
        xs, h2, idx_pad, gates_pad = _outproj(xs, o_aT, o_bT, o_c, o_d, w4, g1, norm2_g[l].reshape(1, d),
                                              sc2, sh2, router_w[l], router_b[l].reshape(1, -1))
        xs = _moe(xs, h2, idx_pad, gates_pad, g2, w_up_all, b_up_all, w_down_all, b_down_all, l)
    return xs.reshape(bsz, s, d)
```

```python
import functools
import math

import jax
import jax.numpy as jnp
from jax import lax
from jax.experimental import pallas as pl
from jax.experimental.pallas import tpu as pltpu

F32 = jnp.float32
BF16 = jnp.bfloat16
HI = lax.Precision.HIGHEST

EPS = 1e-6
NEG = -1e30
M_INIT = -1e30
MASKED = -2e30
LOG2E = 1.4426950408889634

D_MODEL = 1024
N_GROUP_W = 256
A_HEADS, A_QK, A_V = 4, 32, 64
B_HEADS, B_DIM = 4, 64
MOBA_BLOCK, MOBA_TOPK = 256, 3
C_HEADS, C_K, C_V = 4, 32, 64
GLA_RANK, GLA_TAU = 16, 16.0
GLA_SUB = 16
D_WIDTH, D_BLOCKS, CONV_W, LRU_C = 256, 4, 4, 8.0
N_EXPERTS, TOP_K, D_FF = 32, 4, 1024
SWIGLU_ALPHA, SWIGLU_LIMIT = 1.702, 7.0

LANE = 128
SUBLANE = 8
W_IN_PAD = 2944

ROW_TILE = 512
ATT_TILE = 256
A_KEY_BLOCKS = 2
B_KEY_BLOCKS = 4
SUM_ROWS = 16
MOE_BLK = 256
MOE_TILE = 256
MOE_SORT_ROWS = MOE_TILE * TOP_K + N_EXPERTS * SUBLANE
VMEM_LIMIT = 56 * 1024 * 1024


def _cparams(sem):
    return pltpu.CompilerParams(dimension_semantics=sem, vmem_limit_bytes=VMEM_LIMIT)


def _log_sigmoid(z):
    return jnp.minimum(z, 0.0) - jnp.log1p(jnp.exp(-jnp.abs(z)))


def _sigmoid(z):
    return 1.0 / (1.0 + jnp.exp(-z))


def _group_rms(t, group):
    n = t.shape[-1]
    shift = int(math.log2(group))
    r = lax.broadcasted_iota(jnp.int32, (n, n), 0) >> shift
    c = lax.broadcasted_iota(jnp.int32, (n, n), 1) >> shift
    bd = jnp.where(r == c, 1.0, 0.0).astype(F32)
    ss = jnp.dot(t * t, bd, precision=HI, preferred_element_type=F32)
    return t * lax.rsqrt(ss * (1.0 / group) + EPS)


def _mod_kernel(c_ref, w_ref, b_ref, o_ref):
    c = c_ref[...]
    cond = c * _sigmoid(c)
    cond8 = jnp.broadcast_to(cond, (8, cond.shape[-1]))
    o_ref[0] = jnp.dot(cond8, w_ref[0], precision=HI, preferred_element_type=F32) + b_ref[0]


def _modulation(c, ada_w, ada_b):
    depth, d, n = ada_w.shape
    tn = 1536
    out = pl.pallas_call(
        _mod_kernel,
        grid=(depth, n // tn),
        in_specs=[pl.BlockSpec((1, d), lambda l, j: (0, 0)),
                  pl.BlockSpec((1, d, tn), lambda l, j: (l, 0, j)),
                  pl.BlockSpec((1, 1, tn), lambda l, j: (l, 0, j))],
        out_specs=pl.BlockSpec((1, 8, tn), lambda l, j: (l, 0, j)),
        out_shape=jax.ShapeDtypeStruct((depth, 8, n), F32),
        compiler_params=_cparams(("arbitrary", "arbitrary")),
    )(c, ada_w, ada_b.reshape(depth, 1, n))
    return out[:, 0, :]


def _inproj_kernel(x_ref, ng_ref, sc_ref, sh_ref, w_ref, aqg_ref, akg_ref, bqg_ref, bkg_ref, wg2_ref, bg_ref,
                   aqT_ref, ak_ref, avT_ref, bqT_ref, bk_ref, bvT_ref, bkm_ref,
                   cq_ref, ck_ref, cv_ref, cr_ref, cg_ref, dx_ref, dg_ref):
    x = x_ref[...]
    tm = x.shape[0]
    h = x * lax.rsqrt(jnp.mean(x * x, axis=-1, keepdims=True) + EPS) * ng_ref[...]
    h = h * (1.0 + sc_ref[...]) + sh_ref[...]
    proj = jnp.dot(h.astype(BF16), w_ref[...], preferred_element_type=F32)

    aq = _group_rms(proj[:, 0:256], A_QK) * aqg_ref[...] * (A_QK ** -0.5 * LOG2E)
    aqT_ref[...] = aq.T.astype(BF16)
    ak = _group_rms(proj[:, 256:512], A_QK) * akg_ref[...]
    ak_ref[...] = ak.astype(BF16)
    avT_ref[...] = proj[:, 512:768].T.astype(BF16)

    bq = _group_rms(proj[:, 768:1024], B_DIM) * bqg_ref[...] * (B_DIM ** -0.5 * LOG2E)
    bqT_ref[...] = bq.T.astype(BF16)
    bk = _group_rms(proj[:, 1024:1280], B_DIM) * bkg_ref[...]
    bk_ref[...] = bk.astype(BF16)
    bvT_ref[...] = proj[:, 1280:1536].T.astype(BF16)
    bkm_ref[0] = jnp.mean(bk.reshape(tm // MOBA_BLOCK, MOBA_BLOCK, N_GROUP_W), axis=1)

    cq_ref[...] = proj[:, 1536:1664] * (C_K ** -0.5)
    ck_ref[...] = proj[:, 1664:1792]
    cv_ref[...] = proj[:, 1792:2048]
    cr_ref[...] = proj[:, 2048:2304]
    z = jnp.dot(proj[:, 2816:2944], wg2_ref[...], precision=HI, preferred_element_type=F32) + bg_ref[...]
    cg_ref[...] = _log_sigmoid(z) * (1.0 / GLA_TAU)

    dx_ref[...] = proj[:, 2304:2560]
    dg_ref[...] = proj[:, 2560:2816]


def _inproj(x, ng, sc, sh, w_cat, aqg, akg, bqg, bkg, wg2p, bg):
    s, d = x.shape
    tm = ROW_TILE
    nt = s // tm
    row = lambda i: (i, 0)
    const2 = lambda i: (0, 0)
    vec = pl.BlockSpec((1, d), const2)
    gvec = pl.BlockSpec((1, N_GROUP_W), const2)
    out_shape = (
        jax.ShapeDtypeStruct((N_GROUP_W, s), BF16),
        jax.ShapeDtypeStruct((s, N_GROUP_W), BF16),
        jax.ShapeDtypeStruct((N_GROUP_W, s), BF16),
        jax.ShapeDtypeStruct((N_GROUP_W, s), BF16),
        jax.ShapeDtypeStruct((s, N_GROUP_W), BF16),
        jax.ShapeDtypeStruct((N_GROUP_W, s), BF16),
        jax.ShapeDtypeStruct((nt, tm // MOBA_BLOCK, N_GROUP_W), F32),
        jax.ShapeDtypeStruct((s, 128), F32),
        jax.ShapeDtypeStruct((s, 128), F32),
        jax.ShapeDtypeStruct((s, 256), F32),
        jax.ShapeDtypeStruct((s, 256), F32),
        jax.ShapeDtypeStruct((s, 128), F32),
        jax.ShapeDtypeStruct((s, 256), F32),
        jax.ShapeDtypeStruct((s, 256), F32),
    )
    out_specs = (
        pl.BlockSpec((N_GROUP_W, tm), lambda i: (0, i)),
        pl.BlockSpec((tm, N_GROUP_W), row),
        pl.BlockSpec((N_GROUP_W, tm), lambda i: (0, i)),
        pl.BlockSpec((N_GROUP_W, tm), lambda i: (0, i)),
        pl.BlockSpec((tm, N_GROUP_W), row),
        pl.BlockSpec((N_GROUP_W, tm), lambda i: (0, i)),
        pl.BlockSpec((1, tm // MOBA_BLOCK, N_GROUP_W), lambda i: (i, 0, 0)),
        pl.BlockSpec((tm, 128), row),
        pl.BlockSpec((tm, 128), row),
        pl.BlockSpec((tm, 256), row),
        pl.BlockSpec((tm, 256), row),
        pl.BlockSpec((tm, 128), row),
        pl.BlockSpec((tm, 256), row),
        pl.BlockSpec((tm, 256), row),
    )
    return pl.pallas_call(
        _inproj_kernel,
        grid=(nt,),
        in_specs=[pl.BlockSpec((tm, d), row), vec, vec, vec,
                  pl.BlockSpec((d, W_IN_PAD), const2),
                  gvec, gvec, gvec, gvec,
                  pl.BlockSpec((LANE, LANE), const2),
                  pl.BlockSpec((1, LANE), const2)],
        out_specs=out_specs,
        out_shape=out_shape,
        compiler_params=_cparams(("arbitrary",)),
    )(x, ng, sc, sh, w_cat, aqg, akg, bqg, bkg, wg2p, bg)


def _pad_query_rows(qT_ref, qpad_ref, width):
    per = LANE // width
    rows = lax.broadcasted_iota(jnp.int32, qpad_ref.shape[1:], 0)
    for g in range(qpad_ref.shape[0]):
        blk = qT_ref[LANE * (g // per):LANE * (g // per + 1), :]
        lo = width * (g % per)
        qpad_ref[g] = jnp.where((rows >= lo) & (rows < lo + width), blk, jnp.zeros_like(blk))


def _with_sum_rows(vT):
    return jnp.concatenate([vT, jnp.ones((SUM_ROWS, vT.shape[1]), vT.dtype)], axis=0)


def _softmax_tiles(k_slabs, qpad_ref, v_tiles, adjust, m_ref, acc_ref):
    n = len(k_slabs)
    scores = [jnp.dot(k_slabs[g], qpad_ref[g], preferred_element_type=F32) for g in range(n)]
    probs, alphas = [], []
    for g in range(n):
        sT = adjust(g, scores[g])
        m_old = m_ref[g]
        m_new = jnp.maximum(m_old, jnp.max(sT, axis=0, keepdims=True))
        alphas.append(jnp.exp2(m_old - m_new))
        probs.append(jnp.exp2(sT - m_new).astype(BF16))
        m_ref[g] = m_new
    pv = [jnp.dot(v_tiles[g], probs[g], preferred_element_type=F32) for g in range(n)]
    for g in range(n):
        acc_ref[g] = alphas[g] * acc_ref[g] + pv[g]


def _diffattn_kernel(qT_ref, k_ref, vT_ref, lq1_ref, lk1_ref, lq2_ref, lk2_ref, og_ref, o_ref,
                     m_ref, acc_ref, qpad_ref, *, lam_init):
    KEY_BLOCKS = A_KEY_BLOCKS
    t, tk = ATT_TILE, KEY_BLOCKS * ATT_TILE
    n_maps = 2 * A_HEADS
    iq = pl.program_id(0)
    lam = (jnp.exp(jnp.sum(lq1_ref[...] * lk1_ref[...], axis=-1, keepdims=True))
           - jnp.exp(jnp.sum(lq2_ref[...] * lk2_ref[...], axis=-1, keepdims=True)) + lam_init)
    m_ref[...] = jnp.full(m_ref.shape, M_INIT, F32)
    acc_ref[...] = jnp.zeros(acc_ref.shape, F32)
    _pad_query_rows(qT_ref, qpad_ref, A_QK)

    def tile(ik, adjust):
        keys = pl.ds(pl.multiple_of(ik * tk, tk), tk)
        k_slabs = [k_ref[keys, LANE * (g // 4):LANE * (g // 4 + 1)] for g in range(n_maps)]
        v_heads = [_with_sum_rows(vT_ref[A_V * hd:A_V * (hd + 1), keys]) for hd in range(A_HEADS)]
        _softmax_tiles(k_slabs, qpad_ref, [v_heads[g // 2] for g in range(n_maps)], adjust, m_ref, acc_ref)

    def body(ik, carry):
        tile(ik, lambda g, sT: sT)
        return carry

    n_full = iq // KEY_BLOCKS
    lax.fori_loop(0, n_full, body, 0)
    causal = (lax.broadcasted_iota(jnp.int32, (tk, t), 0) - lax.broadcasted_iota(jnp.int32, (tk, t), 1)
              <= (iq - KEY_BLOCKS * n_full) * t)
    tile(n_full, lambda g, sT: jnp.where(causal, sT, MASKED))

    for hd in range(A_HEADS):
        a1, a2 = acc_ref[2 * hd], acc_ref[2 * hd + 1]
        o = a1[:A_V] / a1[A_V:A_V + 1] - lam * (a2[:A_V] / a2[A_V:A_V + 1])
        o = o * lax.rsqrt(jnp.mean(o * o, axis=0, keepdims=True) + EPS) * og_ref[...]
        o_ref[A_V * hd:A_V * (hd + 1), :] = o * (1.0 - lam_init)


def _diff_attention(aqT, ak, avT, lq1, lk1, lq2, lk2, out_gain_col, layer_idx):
    s = aqT.shape[1]
    t = ATT_TILE
    n_maps = 2 * A_HEADS
    lam_init = 0.8 - 0.6 * math.exp(-0.3 * layer_idx)
    vec = pl.BlockSpec((1, A_QK), lambda i: (0, 0))
    return pl.pallas_call(
        functools.partial(_diffattn_kernel, lam_init=lam_init),
        grid=(s // t,),
        in_specs=[pl.BlockSpec((N_GROUP_W, t), lambda i: (0, i)),
                  pl.BlockSpec((s, N_GROUP_W), lambda i: (0, 0)),
                  pl.BlockSpec((N_GROUP_W, s), lambda i: (0, 0)),
                  vec, vec, vec, vec,
                  pl.BlockSpec((A_V, 1), lambda i: (0, 0))],
        out_specs=pl.BlockSpec((N_GROUP_W, t), lambda i: (0, i)),
        out_shape=jax.ShapeDtypeStruct((N_GROUP_W, s), F32),
        scratch_shapes=[pltpu.VMEM((n_maps, 1, t), F32), pltpu.VMEM((n_maps, A_V + SUM_ROWS, t), F32),
                        pltpu.VMEM((n_maps, LANE, t), BF16)],
        compiler_params=_cparams(("arbitrary",)),
    )(aqT, ak, avT, lq1, lk1, lq2, lk2, out_gain_col)


def _moba_kernel(qT_ref, k_ref, vT_ref, km_ref, o_ref, m_ref, acc_ref, bias_ref, qpad_ref):
    KEY_BLOCKS = B_KEY_BLOCKS
    t, tk = ATT_TILE, KEY_BLOCKS * ATT_TILE
    own = pl.program_id(0)
    nb = km_ref.shape[1]
    m_ref[...] = jnp.full(m_ref.shape, M_INIT, F32)
    acc_ref[...] = jnp.zeros(acc_ref.shape, F32)
    _pad_query_rows(qT_ref, qpad_ref, B_DIM)

    blk = lax.broadcasted_iota(jnp.int32, (nb, t), 0)
    earlier = blk < own
    for hd in range(B_HEADS):
        qT = qT_ref[B_DIM * hd:B_DIM * (hd + 1), :].astype(F32)
        gate = jnp.dot(km_ref[hd], qT, precision=HI, preferred_element_type=F32)
        gate = jnp.where(earlier, gate, NEG)
        sel_bias = jnp.full((nb, t), MASKED, F32)
        for _ in range(MOBA_TOPK):
            top = jnp.max(gate, axis=0, keepdims=True)
            pick = blk == jnp.min(jnp.where(gate == top, blk, nb), axis=0, keepdims=True)
            sel_bias = jnp.where(pick, jnp.where(earlier, 0.0, MASKED), sel_bias)
            gate = jnp.where(pick, -jnp.inf, gate)
        bias_ref[hd] = sel_bias

    def tile(j, adjust):
        keys = pl.ds(pl.multiple_of(j * tk, tk), tk)
        k_slabs = [k_ref[keys, LANE * (hd // 2):LANE * (hd // 2 + 1)] for hd in range(B_HEADS)]
        v_tiles = [_with_sum_rows(vT_ref[B_DIM * hd:B_DIM * (hd + 1), keys]) for hd in range(B_HEADS)]
        _softmax_tiles(k_slabs, qpad_ref, v_tiles, adjust, m_ref, acc_ref)

    def block_bias(hd, b):
        return bias_ref[hd, pl.ds(b, 1), :]

    def body(j, carry):
        tile(j, lambda hd, sT: jnp.concatenate(
            [sT[t * part:t * (part + 1)] + block_bias(hd, KEY_BLOCKS * j + part) for part in range(KEY_BLOCKS)],
            axis=0))
        return carry

    n_full = own // KEY_BLOCKS
    lax.fori_loop(0, n_full, body, 0)

    causal = (lax.broadcasted_iota(jnp.int32, (t, t), 0) <= lax.broadcasted_iota(jnp.int32, (t, t), 1))

    def last(hd, sT):
        parts = []
        for part in range(KEY_BLOCKS):
            b = KEY_BLOCKS * n_full + part
            sp = sT[t * part:t * (part + 1)]
            other = sp + jnp.where(b < own, block_bias(hd, b), MASKED)
            parts.append(jnp.where(b == own, jnp.where(causal, sp, MASKED), other))
        return jnp.concatenate(parts, axis=0)

    tile(n_full, last)
    for hd in range(B_HEADS):
        a = acc_ref[hd]
        o_ref[B_DIM * hd:B_DIM * (hd + 1), :] = a[:B_DIM] / a[B_DIM:B_DIM + 1]


def _moba(bqT, bk, bvT, km):
    s = bqT.shape[1]
    t = ATT_TILE
    nb = km.shape[1]
    return pl.pallas_call(
        _moba_kernel,
        grid=(s // t,),
        in_specs=[pl.BlockSpec((N_GROUP_W, t), lambda i: (0, i)),
                  pl.BlockSpec((s, N_GROUP_W), lambda i: (0, 0)),
                  pl.BlockSpec((N_GROUP_W, s), lambda i: (0, 0)),
                  pl.BlockSpec((B_HEADS, nb, B_DIM), lambda i: (0, 0, 0))],
        out_specs=pl.BlockSpec((N_GROUP_W, t), lambda i: (0, i)),
        out_shape=jax.ShapeDtypeStruct((N_GROUP_W, s), F32),
        scratch_shapes=[pltpu.VMEM((B_HEADS, 1, t), F32),
                        pltpu.VMEM((B_HEADS, B_DIM + SUM_ROWS, t), F32), pltpu.VMEM((B_HEADS, nb, t), F32),
                        pltpu.VMEM((B_HEADS, LANE, t), BF16)],
        compiler_params=_cparams(("arbitrary",)),
    )(bqT, bk, bvT, km)


def _gla_kernel(q_ref, k_ref, g_ref, v_ref, r_ref, og_ref, o_ref, stateT_ref, kbuf_ref, bbuf_ref, vbuf_ref):
    tm = q_ref.shape[0]
    c = GLA_SUB
    shift = int(math.log2(c))

    @pl.when(pl.program_id(0) == 0)
    def _():
        stateT_ref[...] = jnp.zeros(stateT_ref.shape, F32)

    q, k, v = q_ref[...], k_ref[...], v_ref[...]
    ri = lax.broadcasted_iota(jnp.int32, (tm, tm), 0)
    ci = lax.broadcasted_iota(jnp.int32, (tm, tm), 1)
    same = (ri >> shift) == (ci >> shift)
    g = g_ref[...]
    b = jnp.dot(jnp.where(same & (ri >= ci), 1.0, 0.0).astype(F32), g, precision=HI, preferred_element_type=F32)
    b_last = jnp.dot(jnp.where(same, 1.0, 0.0).astype(F32), g, precision=HI, preferred_element_type=F32)

    k_shift, v_shift = int(math.log2(C_K)), int(math.log2(C_V))
    head_sum = jnp.where((lax.broadcasted_iota(jnp.int32, (C_HEADS * C_K, C_HEADS * C_V), 0) >> k_shift)
                         == (lax.broadcasted_iota(jnp.int32, (C_HEADS * C_K, C_HEADS * C_V), 1) >> v_shift),
                         1.0, 0.0).astype(BF16)

    for buf, val in ((kbuf_ref, k), (bbuf_ref, b), (vbuf_ref, v)):
        buf[0:c, :] = jnp.zeros((c, val.shape[1]), F32)
        buf[c:c + tm, :] = val
    pos = lax.broadcasted_iota(jnp.int32, (tm, C_HEADS * C_K), 0) & (c - 1)
    o = jnp.zeros((tm, C_HEADS * C_V), F32)
    for delta in range(c):
        k_d = kbuf_ref[c - delta:c - delta + tm, :]
        b_d = bbuf_ref[c - delta:c - delta + tm, :]
        v_d = vbuf_ref[c - delta:c - delta + tm, :]
        w = q * k_d * jnp.exp(jnp.where(pos >= delta, b - b_d, NEG))
        w_hi, w_lo = _split_bf16(w)
        att = (jnp.dot(w_hi, head_sum, preferred_element_type=F32)
               + jnp.dot(w_lo, head_sum, preferred_element_type=F32))
        o = o + att * v_d

    q_dec = (q * jnp.exp(b)).astype(BF16)
    k_dec = (k * jnp.exp(b_last - b)).astype(BF16)
    decay = jnp.exp(b_last)
    v_bf = v.astype(BF16)
    own_head = ((lax.broadcasted_iota(jnp.int32, (C_HEADS * C_V, C_HEADS * C_K), 0) >> v_shift)
                == (lax.broadcasted_iota(jnp.int32, (C_HEADS * C_V, C_HEADS * C_K), 1) >> k_shift))
    nt_dims = (((1,), (1,)), ((), ()))
    tn_dims = (((0,), (0,)), ((), ()))
    updates = [lax.dot_general(v_bf[c * i:c * (i + 1)], k_dec[c * i:c * (i + 1)], tn_dims,
                               preferred_element_type=F32) for i in range(tm // c)]
    stateT = stateT_ref[...]
    far = []
    for i in range(tm // c):
        far.append(lax.dot_general(q_dec[c * i:c * (i + 1)], stateT.astype(BF16), nt_dims,
                                   preferred_element_type=F32))
        stateT = stateT * decay[c * i:c * i + 1, :] + jnp.where(own_head, updates[i], 0.0)
    stateT_ref[...] = stateT
    o = o + jnp.concatenate(far, axis=0)

    r = r_ref[...]
    o_ref[...] = _group_rms(o, C_V) * og_ref[...] * (r * _sigmoid(r))


def _gla(cq, ck, cg, cv, cr, out_gain):
    s = cq.shape[0]
    tm = ROW_TILE
    row = lambda i: (i, 0)
    return pl.pallas_call(
        _gla_kernel,
        grid=(s // tm,),
        in_specs=[pl.BlockSpec((tm, 128), row), pl.BlockSpec((tm, 128), row), pl.BlockSpec((tm, 128), row),
                  pl.BlockSpec((tm, 256), row), pl.BlockSpec((tm, 256), row),
                  pl.BlockSpec((1, C_HEADS * C_V), lambda i: (0, 0))],
        out_specs=pl.BlockSpec((tm, 256), row),
        out_shape=jax.ShapeDtypeStruct((s, 256), F32),
        scratch_shapes=[pltpu.VMEM((C_HEADS * C_V, C_HEADS * C_K), F32),
                        pltpu.VMEM((tm + GLA_SUB, C_HEADS * C_K), F32),
                        pltpu.VMEM((tm + GLA_SUB, C_HEADS * C_K), F32),
                        pltpu.VMEM((tm + GLA_SUB, C_HEADS * C_V), F32)],
        compiler_params=_cparams(("arbitrary",)),
    )(cq, ck, cg, cv, cr, out_gain)


def _rglru_kernel(x_ref, gate_ref, cw_ref, cb_ref, wa_ref, ba_ref, wx_ref, bx_ref, lam_ref, o_ref,
                  xbuf_ref, h_ref):
    tm = x_ref.shape[0]
    pad = 8

    @pl.when(pl.program_id(0) == 0)
    def _():
        xbuf_ref[0:pad, :] = jnp.zeros((pad, D_WIDTH), F32)
        h_ref[...] = jnp.zeros(h_ref.shape, F32)

    xbuf_ref[pad:pad + tm, :] = x_ref[...]
    xc = cb_ref[...] + jnp.zeros((tm, D_WIDTH), F32)
    for w in range(CONV_W):
        lag = CONV_W - 1 - w
        xc = xc + cw_ref[w:w + 1, :] * xbuf_ref[pad - lag:pad - lag + tm, :]
    xbuf_ref[0:pad, :] = xbuf_ref[tm:tm + pad, :]

    xcb = xc.astype(BF16)
    r_gate = _sigmoid(jnp.dot(xcb, wa_ref[...], preferred_element_type=F32) + ba_ref[...])
    i_gate = _sigmoid(jnp.dot(xcb, wx_ref[...], preferred_element_type=F32) + bx_ref[...])
    log_a = LRU_C * r_gate * _log_sigmoid(lam_ref[...])
    a = jnp.exp(log_a)
    b = jnp.sqrt(1.0 - jnp.exp(2.0 * log_a)) * (i_gate * xc)

    rows = lax.broadcasted_iota(jnp.int32, (tm, D_WIDTH), 0)
    step = 1
    while step < tm:
        keep = rows >= step
        a_prev = jnp.where(keep, pltpu.roll(a, step, 0), 1.0)
        b_prev = jnp.where(keep, pltpu.roll(b, step, 0), 0.0)
        b = a * b_prev + b
        a = a * a_prev
        step *= 2
    h = a * h_ref[0:1, :] + b
    h_ref[...] = jnp.broadcast_to(h[tm - 1:tm, :], h_ref.shape)

    g = gate_ref[...]
    gelu = g * (0.5 * (1.0 + jnp.tanh(math.sqrt(2.0 / math.pi) * (g + 0.044715 * (g * g * g)))))
    o_ref[...] = h * gelu


def _rglru(dx, dgate, conv_w, conv_b, wa_bd, b_a, wx_bd, b_x, lam):
    s = dx.shape[0]
    tm = ROW_TILE
    row = lambda i: (i, 0)
    const2 = lambda i: (0, 0)
    vec = pl.BlockSpec((1, D_WIDTH), const2)
    mat = pl.BlockSpec((D_WIDTH, D_WIDTH), const2)
    return pl.pallas_call(
        _rglru_kernel,
        grid=(s // tm,),
        in_specs=[pl.BlockSpec((tm, D_WIDTH), row), pl.BlockSpec((tm, D_WIDTH), row),
                  pl.BlockSpec((CONV_W, D_WIDTH), const2), vec, mat, vec, mat, vec, vec],
        out_specs=pl.BlockSpec((tm, D_WIDTH), row),
        out_shape=jax.ShapeDtypeStruct((s, D_WIDTH), F32),
        scratch_shapes=[pltpu.VMEM((tm + 8, D_WIDTH), F32), pltpu.VMEM((8, D_WIDTH), F32)],
        compiler_params=_cparams(("arbitrary",)),
    )(dx, dgate, conv_w, conv_b, wa_bd, b_a, wx_bd, b_x, lam)


def _outproj_kernel(x_ref, oaT_ref, obT_ref, oc_ref, od_ref, w_ref, g1_ref, ng_ref, sc_ref, sh_ref,
                    rw_ref, rb_ref, xo_ref, h_ref, idx_ref, gate_ref):
    tm = x_ref.shape[0]
    y = jnp.dot(oaT_ref[...].T.astype(BF16), w_ref[0], preferred_element_type=F32)
    y = y + jnp.dot(obT_ref[...].T.astype(BF16), w_ref[1], preferred_element_type=F32)
    y = y + jnp.dot(oc_ref[...].astype(BF16), w_ref[2], preferred_element_type=F32)
    y = y + jnp.dot(od_ref[...].astype(BF16), w_ref[3], preferred_element_type=F32)
    x = x_ref[...] + g1_ref[...] * y
    xo_ref[...] = x
    h = x * lax.rsqrt(jnp.mean(x * x, axis=-1, keepdims=True) + EPS) * ng_ref[...]
    h = h * (1.0 + sc_ref[...]) + sh_ref[...]
    h_ref[...] = h

    logits = jnp.dot(h, rw_ref[...], precision=HI, preferred_element_type=F32) + rb_ref[...]
    e_ids = lax.broadcasted_iota(jnp.int32, (tm, N_EXPERTS), 1)
    lane = lax.broadcasted_iota(jnp.int32, (tm, LANE), 1)
    idx_out = jnp.zeros((tm, LANE), jnp.int32)
    val_out = jnp.zeros((tm, LANE), F32)
    top0 = None
    denom = jnp.zeros((tm, 1), F32)
    for k in range(TOP_K):
        top = jnp.max(logits, axis=-1, keepdims=True)
        arg = jnp.min(jnp.where(logits == top, e_ids, N_EXPERTS), axis=-1, keepdims=True)
        if k == 0:
            top0 = top
        w = jnp.exp(top - top0)
        denom = denom + w
        idx_out = jnp.where(lane == k, arg, idx_out)
        val_out = jnp.where(lane == k, w, val_out)
        logits = jnp.where(e_ids == arg, -jnp.inf, logits)
    idx_ref[...] = idx_out
    gate_ref[...] = val_out / denom


def _outproj(x, oaT, obT, oc, od, w4, g1, ng, sc, sh, rw, rb):
    s, d = x.shape
    tm = ROW_TILE
    row = lambda i: (i, 0)
    col = lambda i: (0, i)
    const2 = lambda i: (0, 0)
    vec = pl.BlockSpec((1, d), const2)
    return pl.pallas_call(
        _outproj_kernel,
        grid=(s // tm,),
        in_specs=[pl.BlockSpec((tm, d), row),
                  pl.BlockSpec((N_GROUP_W, tm), col), pl.BlockSpec((N_GROUP_W, tm), col),
                  pl.BlockSpec((tm, 256), row), pl.BlockSpec((tm, 256), row),
                  pl.BlockSpec((4, N_GROUP_W, d), lambda i: (0, 0, 0)),
                  vec, vec, vec, vec,
                  pl.BlockSpec((d, N_EXPERTS), const2), pl.BlockSpec((1, N_EXPERTS), const2)],
        out_specs=(pl.BlockSpec((tm, d), row), pl.BlockSpec((tm, d), row),
                   pl.BlockSpec((tm, LANE), row), pl.BlockSpec((tm, LANE), row)),
        out_shape=(jax.ShapeDtypeStruct((s, d), F32), jax.ShapeDtypeStruct((s, d), F32),
                   jax.ShapeDtypeStruct((s, LANE), jnp.int32), jax.ShapeDtypeStruct((s, LANE), F32)),
        compiler_params=_cparams(("arbitrary",)),
    )(x, oaT, obT, oc, od, w4, g1, ng, sc, sh, rw, rb)


def _segment_copies(tile, cnt_ref, seg_ref, off_ref, make_copy, start):
    def per_expert(e, carry):
        j = tile * N_EXPERTS + e
        cnt, a, b = cnt_ref[j], seg_ref[j], off_ref[j]
        size = MOE_TILE
        while size >= SUBLANE:
            bit = cnt & size

            @pl.when(bit != 0)
            def _():
                cp = make_copy(pl.multiple_of(a, SUBLANE), pl.multiple_of(b, SUBLANE), size)
                if start:
                    cp.start()
                else:
                    cp.wait()

            a, b = a + bit, b + bit
            size //= 2
        return carry

    lax.fori_loop(0, N_EXPERTS, per_expert, 0)


def _dispatch_kernel(cnt_ref, seg_ref, off_ref, nused_ref, h_ref, slotT_ref, rows_ref, sorted_ref, zero_ref, sem):
    tile = pl.program_id(0)
    tt = h_ref.shape[0]
    n = sorted_ref.shape[1]
    par = tile & 1
    r_ids = lax.broadcasted_iota(jnp.int32, (n, tt), 0)
    perm = jnp.where(r_ids == slotT_ref[0:1, :], 1.0, 0.0)
    for k in range(1, TOP_K):
        perm = perm + jnp.where(r_ids == slotT_ref[k:k + 1, :], 1.0, 0.0)
    perm = perm.astype(BF16)
    sorted_ref[par] = jnp.dot(perm, h_ref[...].astype(BF16), preferred_element_type=F32)

    def copies_from(p):
        def make_copy(a, b, size):
            return pltpu.make_async_copy(sorted_ref.at[p, pl.ds(a, size)], rows_ref.at[pl.ds(b, size)], sem.at[p])
        return make_copy

    _segment_copies(tile, cnt_ref, seg_ref, off_ref, copies_from(par), True)

    @pl.when(tile > 0)
    def _():
        _segment_copies(tile - 1, cnt_ref, seg_ref, off_ref, copies_from(1 - par), False)

    last = pl.num_programs(0) - 1

    @pl.when(tile == last)
    def _():
        _segment_copies(tile, cnt_ref, seg_ref, off_ref, copies_from(par), False)
        zero_ref[...] = jnp.zeros(zero_ref.shape, F32)

        def make_zero_copy(a, b, size):
            return pltpu.make_async_copy(zero_ref.at[pl.ds(a, size)], rows_ref.at[pl.ds(b, size)], sem.at[0])

        _segment_copies(last + 1, cnt_ref, seg_ref, off_ref, make_zero_copy, True)
        _segment_copies(last + 1, cnt_ref, seg_ref, off_ref, make_zero_copy, False)

        def block_copy(b):
            return pltpu.make_async_copy(zero_ref, rows_ref.at[pl.ds(pl.multiple_of(b * MOE_BLK, MOE_BLK), MOE_BLK)],
                                         sem.at[0])

        def issue(b, carry):
            block_copy(b).start()
            return carry

        def drain(b, carry):
            block_copy(b).wait()
            return carry

        n_blocks = rows_ref.shape[0] // MOE_BLK
        lax.fori_loop(nused_ref[0], n_blocks, issue, 0)
        lax.fori_loop(nused_ref[0], n_blocks, drain, 0)


def _dispatch(h, slotT, tile_cnt, tile_seg, tile_off, n_used, n_rows):
    s, d = h.shape
    tt = MOE_TILE
    grid_spec = pltpu.PrefetchScalarGridSpec(
        num_scalar_prefetch=4,
        grid=(s // tt,),
        in_specs=[pl.BlockSpec((tt, d), lambda i, c, sg, of, nu: (i, 0)),
                  pl.BlockSpec((TOP_K, tt), lambda i, c, sg, of, nu: (0, i))],
        out_specs=pl.BlockSpec(memory_space=pl.ANY),
        scratch_shapes=[pltpu.VMEM((2, MOE_SORT_ROWS, d), F32), pltpu.VMEM((MOE_BLK, d), F32),
                        pltpu.SemaphoreType.DMA((2,))],
    )
    return pl.pallas_call(
        _dispatch_kernel,
        grid_spec=grid_spec,
        out_shape=jax.ShapeDtypeStruct((n_rows, d), F32),
        compiler_params=_cparams(("arbitrary",)),
    )(tile_cnt, tile_seg, tile_off, n_used, h, slotT)


def _ffn_kernel(be_ref, first_ref, nused_ref, x_ref, wu_ref, bu_ref, wd_ref, bd_ref, y_ref, wu_bf, wd_bf):
    b = pl.program_id(0)

    @pl.when(first_ref[b] == 1)
    def _():
        wu_bf[...] = wu_ref[0].astype(BF16)
        wd_bf[...] = wd_ref[0].astype(BF16)

    @pl.when(b < nused_ref[0])
    def _():
        hu = jnp.dot(x_ref[...].astype(BF16), wu_bf[...], preferred_element_type=F32) + bu_ref[0]
        g = jnp.minimum(hu[:, :D_FF], SWIGLU_LIMIT)
        lin = jnp.clip(hu[:, D_FF:], -SWIGLU_LIMIT, SWIGLU_LIMIT)
        act = (lin + 1.0) * (g * _sigmoid(SWIGLU_ALPHA * g))
        y_ref[...] = jnp.dot(act.astype(BF16), wd_bf[...], preferred_element_type=F32) + bd_ref[0]

    @pl.when(b >= nused_ref[0])
    def _():
        y_ref[...] = jnp.zeros(y_ref.shape, F32)


def _ffn(x_rows, blk_expert, first, n_used, w_up, b_up, w_down, b_down):
    n_rows, d = x_rows.shape
    blk = MOE_BLK
    n_exp, _, f2 = w_up.shape
    grid_spec = pltpu.PrefetchScalarGridSpec(
        num_scalar_prefetch=3,
        grid=(n_rows // blk,),
        in_specs=[pl.BlockSpec((blk, d), lambda b, be, fi, nu: (jnp.minimum(b, nu[0] - 1), 0)),
                  pl.BlockSpec((1, d, f2), lambda b, be, fi, nu: (be[b], 0, 0)),
                  pl.BlockSpec((1, 1, f2), lambda b, be, fi, nu: (be[b], 0, 0)),
                  pl.BlockSpec((1, D_FF, d), lambda b, be, fi, nu: (be[b], 0, 0)),
                  pl.BlockSpec((1, 1, d), lambda b, be, fi, nu: (be[b], 0, 0))],
        out_specs=pl.BlockSpec((blk, d), lambda b, be, fi, nu: (b, 0)),
        scratch_shapes=[pltpu.VMEM((d, f2), BF16), pltpu.VMEM((D_FF, d), BF16)],
    )
    return pl.pallas_call(
        _ffn_kernel,
        grid_spec=grid_spec,
        out_shape=jax.ShapeDtypeStruct((n_rows, d), F32),
        compiler_params=_cparams(("arbitrary",)),
    )(blk_expert, first, n_used, x_rows, w_up, b_up.reshape(n_exp, 1, f2), w_down, b_down.reshape(n_exp, 1, d))


def _split_bf16(t):
    hi = t.astype(BF16)
    return hi, (t - hi.astype(F32)).astype(BF16)


def _combine_kernel(cnt_ref, seg_ref, off_ref, x_ref, slot_ref, gate_ref, g2_ref, rows_ref, o_ref, buf_ref, sem):
    tile = pl.program_id(0)
    tt = x_ref.shape[0]
    n = buf_ref.shape[1]
    par = tile & 1

    def copies_into(p):
        def make_copy(a, b, size):
            return pltpu.make_async_copy(rows_ref.at[pl.ds(b, size)], buf_ref.at[p, pl.ds(a, size)], sem.at[p])
        return make_copy

    @pl.when(tile == 0)
    def _():
        buf_ref[...] = jnp.zeros(buf_ref.shape, F32)
        _segment_copies(tile, cnt_ref, seg_ref, off_ref, copies_into(par), True)

    @pl.when(tile + 1 < pl.num_programs(0))
    def _():
        _segment_copies(tile + 1, cnt_ref, seg_ref, off_ref, copies_into(1 - par), True)

    slots, gates = slot_ref[...], gate_ref[...]
    r_ids = lax.broadcasted_iota(jnp.int32, (tt, n), 1)
    sel = jnp.where(r_ids == slots[:, 0:1], gates[:, 0:1], 0.0)
    for k in range(1, TOP_K):
        sel = sel + jnp.where(r_ids == slots[:, k:k + 1], gates[:, k:k + 1], 0.0)
    sel_hi, sel_lo = _split_bf16(sel)
    _segment_copies(tile, cnt_ref, seg_ref, off_ref, copies_into(par), False)
    y_hi, y_lo = _split_bf16(buf_ref[par])
    both = jnp.dot(jnp.concatenate([sel_hi, sel_lo], axis=0), y_hi, preferred_element_type=F32)
    y = both[:tt] + both[tt:] + jnp.dot(sel_hi, y_lo, preferred_element_type=F32)
    o_ref[...] = x_ref[...] + g2_ref[...] * y


def _combine(x, slots, gates, g2, y_rows, tile_cnt, tile_seg, tile_off):
    s, d = x.shape
    tt = MOE_TILE
    row = lambda i, c, sg, of: (i, 0)
    grid_spec = pltpu.PrefetchScalarGridSpec(
        num_scalar_prefetch=3,
        grid=(s // tt,),
        in_specs=[pl.BlockSpec((tt, d), row), pl.BlockSpec((tt, LANE), row), pl.BlockSpec((tt, LANE), row),
                  pl.BlockSpec((1, d), lambda i, c, sg, of: (0, 0)),
                  pl.BlockSpec(memory_space=pl.ANY)],
        out_specs=pl.BlockSpec((tt, d), row),
        scratch_shapes=[pltpu.VMEM((2, MOE_SORT_ROWS, d), F32), pltpu.SemaphoreType.DMA((2,))],
    )
    return pl.pallas_call(
        _combine_kernel,
        grid_spec=grid_spec,
        out_shape=jax.ShapeDtypeStruct((s, d), F32),
        compiler_params=_cparams(("arbitrary",)),
    )(tile_cnt, tile_seg, tile_off, x, slots, gates, g2, y_rows)


def _moe(x, h, idx_pad, gates_pad, g2, w_up, b_up, w_down, b_down, layer_idx):
    s, d = x.shape
    n = s * TOP_K
    blk = MOE_BLK
    tt = MOE_TILE
    nt = s // tt
    n_blocks = -(-(n + nt * N_EXPERTS * SUBLANE) // blk) + N_EXPERTS
    i32 = jnp.int32
    idx = idx_pad[:, :TOP_K]
    onehot = jnp.sum((idx[:, :, None] == jnp.arange(N_EXPERTS, dtype=i32)).astype(i32), axis=1)
    onehot_t = onehot.reshape(nt, tt, N_EXPERTS)
    rank_in_tile = (jnp.cumsum(onehot_t, axis=1) - onehot_t).reshape(s, N_EXPERTS)
    tile_cnt = (jnp.sum(onehot_t, axis=1) + SUBLANE - 1) // SUBLANE * SUBLANE
    tile_base = jnp.cumsum(tile_cnt, axis=0) - tile_cnt
    tile_seg = jnp.cumsum(tile_cnt, axis=1) - tile_cnt
    counts = jnp.sum(tile_cnt, axis=0)
    padded = (counts + blk - 1) // blk * blk
    pad_ends = jnp.cumsum(padded)
    pad_starts = pad_ends - padded
    tile_off = pad_starts[None, :] + tile_base
    slot = (jnp.take_along_axis(jnp.repeat(tile_seg, tt, axis=0) + rank_in_tile, idx, axis=1)).astype(i32)
    slot_pad = jnp.pad(slot, ((0, 0), (0, LANE - TOP_K)))
    blk_start = jnp.arange(n_blocks, dtype=i32) * blk
    blk_expert = jnp.minimum(jnp.sum((pad_ends[None, :] <= blk_start[:, None]).astype(i32), axis=1),
                             N_EXPERTS - 1).astype(i32)
    first = jnp.concatenate([jnp.ones((1,), i32), (blk_expert[1:] != blk_expert[:-1]).astype(i32)])
    n_used = (pad_ends[-1:] // blk).astype(i32)
    tile_cnt = jnp.concatenate([tile_cnt, (padded - counts)[None, :]], axis=0)
    tile_seg = jnp.concatenate([tile_seg, jnp.zeros((1, N_EXPERTS), tile_seg.dtype)], axis=0)
    tile_off = jnp.concatenate([tile_off, (pad_starts + counts)[None, :]], axis=0)
    tile_cnt, tile_seg, tile_off = [t.astype(i32).reshape(-1) for t in (tile_cnt, tile_seg, tile_off)]

    x_rows = _dispatch(h, slot.T, tile_cnt, tile_seg, tile_off, n_used, n_blocks * blk)
    y_rows = _ffn(x_rows, blk_expert + layer_idx * N_EXPERTS, first, n_used, w_up, b_up, w_down, b_down)
    return _combine(x, slot_pad, gates_pad, g2, y_rows, tile_cnt, tile_seg, tile_off)


def _block_diag(w):
    n, d, _ = w.shape
    eye = jnp.eye(n, dtype=w.dtype)
    return (eye[:, None, :, None] * w[:, :, None, :]).reshape(n * d, n * d)


def kernel(x, c, ada_w, ada_b, norm1_g, norm2_g, w_in, w_out, a_q_gain, a_k_gain, a_lam_q1, a_lam_k1, a_lam_q2, a_lam_k2, a_out_gain, b_q_gain, b_k_gain, c_w_g2, c_b_g, c_out_gain, d_conv_w, d_conv_b, d_w_a, d_b_a, d_w_x, d_b_x, d_lambda, router_w, router_b, exp_w_up, exp_b_up, exp_w_down, exp_b_down):
    bsz, s, d = x.shape
    assert bsz == 1 and d == D_MODEL and s % ROW_TILE == 0 and s % (max(A_KEY_BLOCKS, B_KEY_BLOCKS) * ATT_TILE) == 0
    depth = ada_w.shape[0]
    mod = _modulation(c, ada_w, ada_b)
    xs = x.reshape(s, d)
    w_up_all = exp_w_up.reshape(depth * N_EXPERTS, d, 2 * D_FF)
    b_up_all = exp_b_up.reshape(depth * N_EXPERTS, 2 * D_FF)
    w_down_all = exp_w_down.reshape(depth * N_EXPERTS, D_FF, d)
    b_down_all = exp_b_down.reshape(depth * N_EXPERTS, d)
    for l in range(depth):
        sh1, sc1, g1, sh2, sc2, g2 = [mod[l, i * d:(i + 1) * d].reshape(1, d) for i in range(6)]
        w = w_in[l]
        w_cat = jnp.concatenate([w[:, :2048], w[:, 2064:2832], w[:, 2048:2064],
                                 jnp.zeros((d, W_IN_PAD - 2832), w.dtype)], axis=1).astype(BF16)
        wg2p = jnp.zeros((LANE, LANE), F32).at[:GLA_RANK, :].set(c_w_g2[l])
        (aqT, ak, avT, bqT, bk, bvT, bkm, cq, ck, cv, cr, cg, dx, dgate) = _inproj(
            xs, norm1_g[l].reshape(1, d), sc1, sh1, w_cat,
            jnp.tile(a_q_gain[l], 2 * A_HEADS).reshape(1, -1), jnp.tile(a_k_gain[l], 2 * A_HEADS).reshape(1, -1),
            jnp.tile(b_q_gain[l], B_HEADS).reshape(1, -1), jnp.tile(b_k_gain[l], B_HEADS).reshape(1, -1),
            wg2p, c_b_g[l].reshape(1, -1))
        o_aT = _diff_attention(aqT, ak, avT, a_lam_q1[l].reshape(1, -1), a_lam_k1[l].reshape(1, -1),
                               a_lam_q2[l].reshape(1, -1), a_lam_k2[l].reshape(1, -1),
                               a_out_gain[l].reshape(-1, 1), l)
        km = bkm.reshape(s // MOBA_BLOCK, B_HEADS, B_DIM).transpose(1, 0, 2)
        o_bT = _moba(bqT, bk, bvT, km)
        o_c = _gla(cq, ck, cg, cv, cr, jnp.tile(c_out_gain[l], C_HEADS).reshape(1, -1))
        o_d = _rglru(dx, dgate, d_conv_w[l], d_conv_b[l].reshape(1, -1),
                     _block_diag(d_w_a[l]).astype(BF16), d_b_a[l].reshape(1, -1),
                     _block_diag(d_w_x[l]).astype(BF16), d_b_x[l].reshape(1, -1), d_lambda[l].reshape(1, -1))
        w4 = w_out[l].astype(BF16).reshape(4, N_GROUP_W, d)
        xs, h2, idx_pad, gates_pad = _outproj(xs, o_aT, o_bT, o_c, o_d, w4, g1, norm2_g[l].reshape(1, d),
                                              sc2, sh2, router_w[l], router_b[l].reshape(1, -1))
        xs = _moe(xs, h2, idx_pad, gates_pad, g2, w_up_all, b_up_all, w_down_all, b_down_all, l)
    return xs.reshape(bsz, s, d)
```

```python
import functools
import math

import jax
import jax.numpy as jnp
from jax import lax
from jax.experimental import pallas as pl
from jax.experimental.pallas import tpu as pltpu

F32 = jnp.float32
BF16 = jnp.bfloat16
HI = lax.Precision.HIGHEST

EPS = 1e-6
NEG = -1e30
M_INIT = -1e30
MASKED = -2e30
LOG2E = 1.4426950408889634

D_MODEL = 1024
N_GROUP_W = 256
A_HEADS, A_QK, A_V = 4, 32, 64
B_HEADS, B_DIM = 4, 64
MOBA_BLOCK, MOBA_TOPK = 256, 3
C_HEADS, C_K, C_V = 4, 32, 64
GLA_RANK, GLA_TAU = 16, 16.0
GLA_SUB = 16
D_WIDTH, D_BLOCKS, CONV_W, LRU_C = 256, 4, 4, 8.0
N_EXPERTS, TOP_K, D_FF = 32, 4, 1024
SWIGLU_ALPHA, SWIGLU_LIMIT = 1.702, 7.0

LANE = 128
SUBLANE = 8
W_IN_PAD = 2944

ROW_TILE = 512
ATT_TILE = 256
A_KEY_BLOCKS = 2
B_KEY_BLOCKS = 4
SUM_ROWS = 16
MOE_BLK = 256
MOE_TILE = 256
MOE_SORT_ROWS = MOE_TILE * TOP_K + N_EXPERTS * SUBLANE
VMEM_LIMIT = 56 * 1024 * 1024


def _cparams(sem):
    return pltpu.CompilerParams(dimension_semantics=sem, vmem_limit_bytes=VMEM_LIMIT)


def _log_sigmoid(z):
    return jnp.minimum(z, 0.0) - jnp.log1p(jnp.exp(-jnp.abs(z)))


def _sigmoid(z):
    return 1.0 / (1.0 + jnp.exp(-z))


def _split_bf16(t):
    hi = t.astype(BF16)
    return hi, (t - hi.astype(F32)).astype(BF16)


def _dot_exact_rhs(t, w_bf16, parts=2):
    out, rest = None, t
    for _ in range(parts):
        piece = rest.astype(BF16)
        term = jnp.dot(piece, w_bf16, preferred_element_type=F32)
        out = term if out is None else out + term
        rest = rest - piece.astype(F32)
    return out


def _dot_exact_lhs(w_bf16, t, parts=3):
    out, rest = None, t
    for _ in range(parts):
        piece = rest.astype(BF16)
        term = jnp.dot(w_bf16, piece, preferred_element_type=F32)
        out = term if out is None else out + term
        rest = rest - piece.astype(F32)
    return out


def _group_rms(t, group):
    n = t.shape[-1]
    shift = int(math.log2(group))
    r = lax.broadcasted_iota(jnp.int32, (n, n), 0) >> shift
    c = lax.broadcasted_iota(jnp.int32, (n, n), 1) >> shift
    bd = jnp.where(r == c, 1.0, 0.0).astype(BF16)
    ss = _dot_exact_rhs(t * t, bd)
    return t * lax.rsqrt(ss * (1.0 / group) + EPS)


def _mod_kernel(c_ref, w_ref, b_ref, o_ref):
    c = c_ref[...]
    cond = c * _sigmoid(c)
    cond8 = jnp.broadcast_to(cond, (8, cond.shape[-1]))
    o_ref[0] = jnp.dot(cond8, w_ref[0], precision=HI, preferred_element_type=F32) + b_ref[0]


def _modulation(c, ada_w, ada_b):
    depth, d, n = ada_w.shape
    tn = 1536
    out = pl.pallas_call(
        _mod_kernel,
        grid=(depth, n // tn),
        in_specs=[pl.BlockSpec((1, d), lambda l, j: (0, 0)),
                  pl.BlockSpec((1, d, tn), lambda l, j: (l, 0, j)),
                  pl.BlockSpec((1, 1, tn), lambda l, j: (l, 0, j))],
        out_specs=pl.BlockSpec((1, 8, tn), lambda l, j: (l, 0, j)),
        out_shape=jax.ShapeDtypeStruct((depth, 8, n), F32),
        compiler_params=_cparams(("arbitrary", "arbitrary")),
    )(c, ada_w, ada_b.reshape(depth, 1, n))
    return out[:, 0, :]


def _inproj_kernel(x_ref, ng_ref, sc_ref, sh_ref, w_ref, aqg_ref, akg_ref, bqg_ref, bkg_ref, wg2_ref, bg_ref,
                   aqT_ref, ak_ref, avT_ref, bqT_ref, bk_ref, bvT_ref, bkm_ref,
                   cq_ref, ck_ref, cv_ref, cr_ref, cg_ref, dx_ref, dg_ref):
    x = x_ref[...]
    tm = x.shape[0]
    h = x * lax.rsqrt(jnp.mean(x * x, axis=-1, keepdims=True) + EPS) * ng_ref[...]
    h = h * (1.0 + sc_ref[...]) + sh_ref[...]
    proj = jnp.dot(h.astype(BF16), w_ref[...], preferred_element_type=F32)

    aq = _group_rms(proj[:, 0:256], A_QK) * aqg_ref[...] * (A_QK ** -0.5 * LOG2E)
    aqT_ref[...] = aq.T.astype(BF16)
    ak = _group_rms(proj[:, 256:512], A_QK) * akg_ref[...]
    ak_ref[...] = ak.astype(BF16)
    avT_ref[...] = proj[:, 512:768].T.astype(BF16)

    bq = _group_rms(proj[:, 768:1024], B_DIM) * bqg_ref[...] * (B_DIM ** -0.5 * LOG2E)
    bqT_ref[...] = bq.T.astype(BF16)
    bk = _group_rms(proj[:, 1024:1280], B_DIM) * bkg_ref[...]
    bk_ref[...] = bk.astype(BF16)
    bvT_ref[...] = proj[:, 1280:1536].T.astype(BF16)
    bkm_ref[0] = jnp.mean(bk.reshape(tm // MOBA_BLOCK, MOBA_BLOCK, N_GROUP_W), axis=1)

    cq_ref[...] = proj[:, 1536:1664] * (C_K ** -0.5)
    ck_ref[...] = proj[:, 1664:1792]
    cv_ref[...] = proj[:, 1792:2048]
    cr_ref[...] = proj[:, 2048:2304]
    z = jnp.dot(proj[:, 2816:2944], wg2_ref[...], precision=HI, preferred_element_type=F32) + bg_ref[...]
    cg_ref[...] = _log_sigmoid(z) * (1.0 / GLA_TAU)

    dx_ref[...] = proj[:, 2304:2560]
    dg_ref[...] = proj[:, 2560:2816]


def _inproj(x, ng, sc, sh, w_cat, aqg, akg, bqg, bkg, wg2p, bg):
    s, d = x.shape
    tm = ROW_TILE
    nt = s // tm
    row = lambda i: (i, 0)
    const2 = lambda i: (0, 0)
    vec = pl.BlockSpec((1, d), const2)
    gvec = pl.BlockSpec((1, N_GROUP_W), const2)
    out_shape = (
        jax.ShapeDtypeStruct((N_GROUP_W, s), BF16),
        jax.ShapeDtypeStruct((s, N_GROUP_W), BF16),
        jax.ShapeDtypeStruct((N_GROUP_W, s), BF16),
        jax.ShapeDtypeStruct((N_GROUP_W, s), BF16),
        jax.ShapeDtypeStruct((s, N_GROUP_W), BF16),
        jax.ShapeDtypeStruct((N_GROUP_W, s), BF16),
        jax.ShapeDtypeStruct((nt, tm // MOBA_BLOCK, N_GROUP_W), F32),
        jax.ShapeDtypeStruct((s, 128), F32),
        jax.ShapeDtypeStruct((s, 128), F32),
        jax.ShapeDtypeStruct((s, 256), F32),
        jax.ShapeDtypeStruct((s, 256), F32),
        jax.ShapeDtypeStruct((s, 128), F32),
        jax.ShapeDtypeStruct((s, 256), F32),
        jax.ShapeDtypeStruct((s, 256), F32),
    )
    out_specs = (
        pl.BlockSpec((N_GROUP_W, tm), lambda i: (0, i)),
        pl.BlockSpec((tm, N_GROUP_W), row),
        pl.BlockSpec((N_GROUP_W, tm), lambda i: (0, i)),
        pl.BlockSpec((N_GROUP_W, tm), lambda i: (0, i)),
        pl.BlockSpec((tm, N_GROUP_W), row),
        pl.BlockSpec((N_GROUP_W, tm), lambda i: (0, i)),
        pl.BlockSpec((1, tm // MOBA_BLOCK, N_GROUP_W), lambda i: (i, 0, 0)),
        pl.BlockSpec((tm, 128), row),
        pl.BlockSpec((tm, 128), row),
        pl.BlockSpec((tm, 256), row),
        pl.BlockSpec((tm, 256), row),
        pl.BlockSpec((tm, 128), row),
        pl.BlockSpec((tm, 256), row),
        pl.BlockSpec((tm, 256), row),
    )
    return pl.pallas_call(
        _inproj_kernel,
        grid=(nt,),
        in_specs=[pl.BlockSpec((tm, d), row), vec, vec, vec,
                  pl.BlockSpec((d, W_IN_PAD), const2),
                  gvec, gvec, gvec, gvec,
                  pl.BlockSpec((LANE, LANE), const2),
                  pl.BlockSpec((1, LANE), const2)],
        out_specs=out_specs,
        out_shape=out_shape,
        compiler_params=_cparams(("arbitrary",)),
    )(x, ng, sc, sh, w_cat, aqg, akg, bqg, bkg, wg2p, bg)


def _pad_query_rows(qT_ref, qpad_ref, width):
    per = LANE // width
    rows = lax.broadcasted_iota(jnp.int32, qpad_ref.shape[1:], 0)
    for g in range(qpad_ref.shape[0]):
        blk = qT_ref[LANE * (g // per):LANE * (g // per + 1), :]
        lo = width * (g % per)
        qpad_ref[g] = jnp.where((rows >= lo) & (rows < lo + width), blk, jnp.zeros_like(blk))


def _with_sum_rows(vT):
    return jnp.concatenate([vT, jnp.ones((SUM_ROWS, vT.shape[1]), vT.dtype)], axis=0)


def _softmax_tiles(k_slabs, qpad_ref, v_tiles, adjust, m_ref, acc_ref):
    n = len(k_slabs)
    scores = [jnp.dot(k_slabs[g], qpad_ref[g], preferred_element_type=F32) for g in range(n)]
    probs, alphas = [], []
    for g in range(n):
        sT = adjust(g, scores[g])
        m_old = m_ref[g]
        m_new = jnp.maximum(m_old, jnp.max(sT, axis=0, keepdims=True))
        alphas.append(jnp.exp2(m_old - m_new))
        probs.append(jnp.exp2(sT - m_new).astype(BF16))
        m_ref[g] = m_new
    pv = [jnp.dot(v_tiles[g], probs[g], preferred_element_type=F32) for g in range(n)]
    for g in range(n):
        acc_ref[g] = alphas[g] * acc_ref[g] + pv[g]


def _diffattn_kernel(qT_ref, k_ref, vT_ref, lq1_ref, lk1_ref, lq2_ref, lk2_ref, og_ref, o_ref,
                     m_ref, acc_ref, qpad_ref, *, lam_init):
    KEY_BLOCKS = A_KEY_BLOCKS
    t, tk = ATT_TILE, KEY_BLOCKS * ATT_TILE
    n_maps = 2 * A_HEADS
    iq = pl.program_id(0)
    lam = (jnp.exp(jnp.sum(lq1_ref[...] * lk1_ref[...], axis=-1, keepdims=True))
           - jnp.exp(jnp.sum(lq2_ref[...] * lk2_ref[...], axis=-1, keepdims=True)) + lam_init)
    m_ref[...] = jnp.full(m_ref.shape, M_INIT, F32)
    acc_ref[...] = jnp.zeros(acc_ref.shape, F32)
    _pad_query_rows(qT_ref, qpad_ref, A_QK)

    def tile(ik, adjust):
        keys = pl.ds(pl.multiple_of(ik * tk, tk), tk)
        k_slabs = [k_ref[keys, LANE * (g // 4):LANE * (g // 4 + 1)] for g in range(n_maps)]
        v_heads = [_with_sum_rows(vT_ref[A_V * hd:A_V * (hd + 1), keys]) for hd in range(A_HEADS)]
        _softmax_tiles(k_slabs, qpad_ref, [v_heads[g // 2] for g in range(n_maps)], adjust, m_ref, acc_ref)

    def body(ik, carry):
        tile(ik, lambda g, sT: sT)
        return carry

    n_full = iq // KEY_BLOCKS
    lax.fori_loop(0, n_full, body, 0)
    causal = (lax.broadcasted_iota(jnp.int32, (tk, t), 0) - lax.broadcasted_iota(jnp.int32, (tk, t), 1)
              <= (iq - KEY_BLOCKS * n_full) * t)
    tile(n_full, lambda g, sT: jnp.where(causal, sT, MASKED))

    for hd in range(A_HEADS):
        a1, a2 = acc_ref[2 * hd], acc_ref[2 * hd + 1]
        o = a1[:A_V] / a1[A_V:A_V + 1] - lam * (a2[:A_V] / a2[A_V:A_V + 1])
        o = o * lax.rsqrt(jnp.mean(o * o, axis=0, keepdims=True) + EPS) * og_ref[...]
        o_ref[A_V * hd:A_V * (hd + 1), :] = o * (1.0 - lam_init)


def _diff_attention(aqT, ak, avT, lq1, lk1, lq2, lk2, out_gain_col, layer_idx):
    s = aqT.shape[1]
    t = ATT_TILE
    n_maps = 2 * A_HEADS
    lam_init = 0.8 - 0.6 * math.exp(-0.3 * layer_idx)
    vec = pl.BlockSpec((1, A_QK), lambda i: (0, 0))
    return pl.pallas_call(
        functools.partial(_diffattn_kernel, lam_init=lam_init),
        grid=(s // t,),
        in_specs=[pl.BlockSpec((N_GROUP_W, t), lambda i: (0, i)),
                  pl.BlockSpec((s, N_GROUP_W), lambda i: (0, 0)),
                  pl.BlockSpec((N_GROUP_W, s), lambda i: (0, 0)),
                  vec, vec, vec, vec,
                  pl.BlockSpec((A_V, 1), lambda i: (0, 0))],
        out_specs=pl.BlockSpec((N_GROUP_W, t), lambda i: (0, i)),
        out_shape=jax.ShapeDtypeStruct((N_GROUP_W, s), F32),
        scratch_shapes=[pltpu.VMEM((n_maps, 1, t), F32), pltpu.VMEM((n_maps, A_V + SUM_ROWS, t), F32),
                        pltpu.VMEM((n_maps, LANE, t), BF16)],
        compiler_params=_cparams(("arbitrary",)),
    )(aqT, ak, avT, lq1, lk1, lq2, lk2, out_gain_col)


def _moba_kernel(qT_ref, k_ref, vT_ref, km_ref, o_ref, m_ref, acc_ref, bias_ref, qpad_ref):
    KEY_BLOCKS = B_KEY_BLOCKS
    t, tk = ATT_TILE, KEY_BLOCKS * ATT_TILE
    own = pl.program_id(0)
    nb = km_ref.shape[1]
    m_ref[...] = jnp.full(m_ref.shape, M_INIT, F32)
    acc_ref[...] = jnp.zeros(acc_ref.shape, F32)
    _pad_query_rows(qT_ref, qpad_ref, B_DIM)

    blk = lax.broadcasted_iota(jnp.int32, (nb, t), 0)
    earlier = blk < own
    for hd in range(B_HEADS):
        qT = qT_ref[B_DIM * hd:B_DIM * (hd + 1), :].astype(F32)
        gate = jnp.dot(km_ref[hd], qT, precision=HI, preferred_element_type=F32)
        gate = jnp.where(earlier, gate, NEG)
        sel_bias = jnp.full((nb, t), MASKED, F32)
        for _ in range(MOBA_TOPK):
            top = jnp.max(gate, axis=0, keepdims=True)
            pick = blk == jnp.min(jnp.where(gate == top, blk, nb), axis=0, keepdims=True)
            sel_bias = jnp.where(pick, jnp.where(earlier, 0.0, MASKED), sel_bias)
            gate = jnp.where(pick, -jnp.inf, gate)
        bias_ref[hd] = sel_bias

    def tile(j, adjust):
        keys = pl.ds(pl.multiple_of(j * tk, tk), tk)
        k_slabs = [k_ref[keys, LANE * (hd // 2):LANE * (hd // 2 + 1)] for hd in range(B_HEADS)]
        v_tiles = [_with_sum_rows(vT_ref[B_DIM * hd:B_DIM * (hd + 1), keys]) for hd in range(B_HEADS)]
        _softmax_tiles(k_slabs, qpad_ref, v_tiles, adjust, m_ref, acc_ref)

    def block_bias(hd, b):
        return bias_ref[hd, pl.ds(b, 1), :]

    def body(j, carry):
        tile(j, lambda hd, sT: jnp.concatenate(
            [sT[t * part:t * (part + 1)] + block_bias(hd, KEY_BLOCKS * j + part) for part in range(KEY_BLOCKS)],
            axis=0))
        return carry

    n_full = own // KEY_BLOCKS
    lax.fori_loop(0, n_full, body, 0)

    causal = (lax.broadcasted_iota(jnp.int32, (t, t), 0) <= lax.broadcasted_iota(jnp.int32, (t, t), 1))

    def last(hd, sT):
        parts = []
        for part in range(KEY_BLOCKS):
            b = KEY_BLOCKS * n_full + part
            sp = sT[t * part:t * (part + 1)]
            other = sp + jnp.where(b < own, block_bias(hd, b), MASKED)
            parts.append(jnp.where(b == own, jnp.where(causal, sp, MASKED), other))
        return jnp.concatenate(parts, axis=0)

    tile(n_full, last)
    for hd in range(B_HEADS):
        a = acc_ref[hd]
        o_ref[B_DIM * hd:B_DIM * (hd + 1), :] = a[:B_DIM] / a[B_DIM:B_DIM + 1]


def _moba(bqT, bk, bvT, km):
    s = bqT.shape[1]
    t = ATT_TILE
    nb = km.shape[1]
    return pl.pallas_call(
        _moba_kernel,
        grid=(s // t,),
        in_specs=[pl.BlockSpec((N_GROUP_W, t), lambda i: (0, i)),
                  pl.BlockSpec((s, N_GROUP_W), lambda i: (0, 0)),
                  pl.BlockSpec((N_GROUP_W, s), lambda i: (0, 0)),
                  pl.BlockSpec((B_HEADS, nb, B_DIM), lambda i: (0, 0, 0))],
        out_specs=pl.BlockSpec((N_GROUP_W, t), lambda i: (0, i)),
        out_shape=jax.ShapeDtypeStruct((N_GROUP_W, s), F32),
        scratch_shapes=[pltpu.VMEM((B_HEADS, 1, t), F32),
                        pltpu.VMEM((B_HEADS, B_DIM + SUM_ROWS, t), F32), pltpu.VMEM((B_HEADS, nb, t), F32),
                        pltpu.VMEM((B_HEADS, LANE, t), BF16)],
        compiler_params=_cparams(("arbitrary",)),
    )(bqT, bk, bvT, km)


def _gla_kernel(q_ref, k_ref, g_ref, v_ref, r_ref, og_ref, o_ref, stateT_ref, kbuf_ref, bbuf_ref, vbuf_ref):
    tm = q_ref.shape[0]
    c = GLA_SUB
    shift = int(math.log2(c))

    @pl.when(pl.program_id(0) == 0)
    def _():
        stateT_ref[...] = jnp.zeros(stateT_ref.shape, F32)

    q, k, v = q_ref[...], k_ref[...], v_ref[...]
    ri = lax.broadcasted_iota(jnp.int32, (tm, tm), 0)
    ci = lax.broadcasted_iota(jnp.int32, (tm, tm), 1)
    same = (ri >> shift) == (ci >> shift)
    g = g_ref[...]
    b = _dot_exact_lhs(jnp.where(same & (ri >= ci), 1.0, 0.0).astype(BF16), g)
    b_last = _dot_exact_lhs(jnp.where(same, 1.0, 0.0).astype(BF16), g)

    k_shift, v_shift = int(math.log2(C_K)), int(math.log2(C_V))
    head_sum = jnp.where((lax.broadcasted_iota(jnp.int32, (C_HEADS * C_K, C_HEADS * C_V), 0) >> k_shift)
                         == (lax.broadcasted_iota(jnp.int32, (C_HEADS * C_K, C_HEADS * C_V), 1) >> v_shift),
                         1.0, 0.0).astype(BF16)

    for buf, val in ((kbuf_ref, k), (bbuf_ref, b), (vbuf_ref, v)):
        buf[0:c, :] = jnp.zeros((c, val.shape[1]), F32)
        buf[c:c + tm, :] = val
    pos = lax.broadcasted_iota(jnp.int32, (tm, C_HEADS * C_K), 0) & (c - 1)
    o = jnp.zeros((tm, C_HEADS * C_V), F32)
    for delta in range(c):
        k_d = kbuf_ref[c - delta:c - delta + tm, :]
        b_d = bbuf_ref[c - delta:c - delta + tm, :]
        v_d = vbuf_ref[c - delta:c - delta + tm, :]
        w = q * k_d * jnp.exp(jnp.where(pos >= delta, b - b_d, NEG))
        w_hi, w_lo = _split_bf16(w)
        att = (jnp.dot(w_hi, head_sum, preferred_element_type=F32)
               + jnp.dot(w_lo, head_sum, preferred_element_type=F32))
        o = o + att * v_d

    q_dec = (q * jnp.exp(b)).astype(BF16)
    k_dec = (k * jnp.exp(b_last - b)).astype(BF16)
    decay = jnp.exp(b_last)
    v_bf = v.astype(BF16)
    own_head = ((lax.broadcasted_iota(jnp.int32, (C_HEADS * C_V, C_HEADS * C_K), 0) >> v_shift)
                == (lax.broadcasted_iota(jnp.int32, (C_HEADS * C_V, C_HEADS * C_K), 1) >> k_shift))
    nt_dims = (((1,), (1,)), ((), ()))
    tn_dims = (((0,), (0,)), ((), ()))
    updates = [lax.dot_general(v_bf[c * i:c * (i + 1)], k_dec[c * i:c * (i + 1)], tn_dims,
                               preferred_element_type=F32) for i in range(tm // c)]
    stateT = stateT_ref[...]
    far = []
    for i in range(tm // c):
        far.append(lax.dot_general(q_dec[c * i:c * (i + 1)], stateT.astype(BF16), nt_dims,
                                   preferred_element_type=F32))
        stateT = stateT * decay[c * i:c * i + 1, :] + jnp.where(own_head, updates[i], 0.0)
    stateT_ref[...] = stateT
    o = o + jnp.concatenate(far, axis=0)

    r = r_ref[...]
    o_ref[...] = _group_rms(o, C_V) * og_ref[...] * (r * _sigmoid(r))


def _gla(cq, ck, cg, cv, cr, out_gain):
    s = cq.shape[0]
    tm = ROW_TILE
    row = lambda i: (i, 0)
    return pl.pallas_call(
        _gla_kernel,
        grid=(s // tm,),
        in_specs=[pl.BlockSpec((tm, 128), row), pl.BlockSpec((tm, 128), row), pl.BlockSpec((tm, 128), row),
                  pl.BlockSpec((tm, 256), row), pl.BlockSpec((tm, 256), row),
                  pl.BlockSpec((1, C_HEADS * C_V), lambda i: (0, 0))],
        out_specs=pl.BlockSpec((tm, 256), row),
        out_shape=jax.ShapeDtypeStruct((s, 256), F32),
        scratch_shapes=[pltpu.VMEM((C_HEADS * C_V, C_HEADS * C_K), F32),
                        pltpu.VMEM((tm + GLA_SUB, C_HEADS * C_K), F32),
                        pltpu.VMEM((tm + GLA_SUB, C_HEADS * C_K), F32),
                        pltpu.VMEM((tm + GLA_SUB, C_HEADS * C_V), F32)],
        compiler_params=_cparams(("arbitrary",)),
    )(cq, ck, cg, cv, cr, out_gain)


def _rglru_kernel(x_ref, gate_ref, cw_ref, cb_ref, wa_ref, ba_ref, wx_ref, bx_ref, lam_ref, o_ref,
                  xbuf_ref, h_ref):
    tm = x_ref.shape[0]
    pad = 8

    @pl.when(pl.program_id(0) == 0)
    def _():
        xbuf_ref[0:pad, :] = jnp.zeros((pad, D_WIDTH), F32)
        h_ref[...] = jnp.zeros(h_ref.shape, F32)

    xbuf_ref[pad:pad + tm, :] = x_ref[...]
    xc = cb_ref[...] + jnp.zeros((tm, D_WIDTH), F32)
    for w in range(CONV_W):
        lag = CONV_W - 1 - w
        xc = xc + cw_ref[w:w + 1, :] * xbuf_ref[pad - lag:pad - lag + tm, :]
    xbuf_ref[0:pad, :] = xbuf_ref[tm:tm + pad, :]

    xcb = xc.astype(BF16)
    r_gate = _sigmoid(jnp.dot(xcb, wa_ref[...], preferred_element_type=F32) + ba_ref[...])
    i_gate = _sigmoid(jnp.dot(xcb, wx_ref[...], preferred_element_type=F32) + bx_ref[...])
    log_a = LRU_C * r_gate * _log_sigmoid(lam_ref[...])
    a = jnp.exp(log_a)
    b = jnp.sqrt(1.0 - jnp.exp(2.0 * log_a)) * (i_gate * xc)

    rows = lax.broadcasted_iota(jnp.int32, (tm, D_WIDTH), 0)
    step = 1
    while step < tm:
        keep = rows >= step
        a_prev = jnp.where(keep, pltpu.roll(a, step, 0), 1.0)
        b_prev = jnp.where(keep, pltpu.roll(b, step, 0), 0.0)
        b = a * b_prev + b
        a = a * a_prev
        step *= 2
    h = a * h_ref[0:1, :] + b
    h_ref[...] = jnp.broadcast_to(h[tm - 1:tm, :], h_ref.shape)

    g = gate_ref[...]
    gelu = g * (0.5 * (1.0 + jnp.tanh(math.sqrt(2.0 / math.pi) * (g + 0.044715 * (g * g * g)))))
    o_ref[...] = h * gelu


def _rglru(dx, dgate, conv_w, conv_b, wa_bd, b_a, wx_bd, b_x, lam):
    s = dx.shape[0]
    tm = ROW_TILE
    row = lambda i: (i, 0)
    const2 = lambda i: (0, 0)
    vec = pl.BlockSpec((1, D_WIDTH), const2)
    mat = pl.BlockSpec((D_WIDTH, D_WIDTH), const2)
    return pl.pallas_call(
        _rglru_kernel,
        grid=(s // tm,),
        in_specs=[pl.BlockSpec((tm, D_WIDTH), row), pl.BlockSpec((tm, D_WIDTH), row),
                  pl.BlockSpec((CONV_W, D_WIDTH), const2), vec, mat, vec, mat, vec, vec],
        out_specs=pl.BlockSpec((tm, D_WIDTH), row),
        out_shape=jax.ShapeDtypeStruct((s, D_WIDTH), F32),
        scratch_shapes=[pltpu.VMEM((tm + 8, D_WIDTH), F32), pltpu.VMEM((8, D_WIDTH), F32)],
        compiler_params=_cparams(("arbitrary",)),
    )(dx, dgate, conv_w, conv_b, wa_bd, b_a, wx_bd, b_x, lam)


def _outproj_kernel(x_ref, oaT_ref, obT_ref, oc_ref, od_ref, w_ref, g1_ref, ng_ref, sc_ref, sh_ref,
                    rw_ref, rb_ref, xo_ref, h_ref, idx_ref, gate_ref):
    tm = x_ref.shape[0]
    y = jnp.dot(oaT_ref[...].T.astype(BF16), w_ref[0], preferred_element_type=F32)
    y = y + jnp.dot(obT_ref[...].T.astype(BF16), w_ref[1], preferred_element_type=F32)
    y = y + jnp.dot(oc_ref[...].astype(BF16), w_ref[2], preferred_element_type=F32)
    y = y + jnp.dot(od_ref[...].astype(BF16), w_ref[3], preferred_element_type=F32)
    x = x_ref[...] + g1_ref[...] * y
    xo_ref[...] = x
    h = x * lax.rsqrt(jnp.mean(x * x, axis=-1, keepdims=True) + EPS) * ng_ref[...]
    h = h * (1.0 + sc_ref[...]) + sh_ref[...]
    h_ref[...] = h

    h_hi, h_lo = _split_bf16(h)
    w_hi, w_lo = _split_bf16(rw_ref[...])
    logits = (jnp.dot(h_hi, w_hi, preferred_element_type=F32) + jnp.dot(h_lo, w_hi, preferred_element_type=F32)
              + jnp.dot(h_hi, w_lo, preferred_element_type=F32)) + rb_ref[...]
    e_ids = lax.broadcasted_iota(jnp.int32, (tm, N_EXPERTS), 1)
    lane = lax.broadcasted_iota(jnp.int32, (tm, LANE), 1)
    idx_out = jnp.zeros((tm, LANE), jnp.int32)
    val_out = jnp.zeros((tm, LANE), F32)
    top0 = None
    denom = jnp.zeros((tm, 1), F32)
    for k in range(TOP_K):
        top = jnp.max(logits, axis=-1, keepdims=True)
        arg = jnp.min(jnp.where(logits == top, e_ids, N_EXPERTS), axis=-1, keepdims=True)
        if k == 0:
            top0 = top
        w = jnp.exp(top - top0)
        denom = denom + w
        idx_out = jnp.where(lane == k, arg, idx_out)
        val_out = jnp.where(lane == k, w, val_out)
        logits = jnp.where(e_ids == arg, -jnp.inf, logits)
    idx_ref[...] = idx_out
    gate_ref[...] = val_out / denom


def _outproj(x, oaT, obT, oc, od, w4, g1, ng, sc, sh, rw, rb):
    s, d = x.shape
    tm = ROW_TILE
    row = lambda i: (i, 0)
    col = lambda i: (0, i)
    const2 = lambda i: (0, 0)
    vec = pl.BlockSpec((1, d), const2)
    return pl.pallas_call(
        _outproj_kernel,
        grid=(s // tm,),
        in_specs=[pl.BlockSpec((tm, d), row),
                  pl.BlockSpec((N_GROUP_W, tm), col), pl.BlockSpec((N_GROUP_W, tm), col),
                  pl.BlockSpec((tm, 256), row), pl.BlockSpec((tm, 256), row),
                  pl.BlockSpec((4, N_GROUP_W, d), lambda i: (0, 0, 0)),
                  vec, vec, vec, vec,
                  pl.BlockSpec((d, N_EXPERTS), const2), pl.BlockSpec((1, N_EXPERTS), const2)],
        out_specs=(pl.BlockSpec((tm, d), row), pl.BlockSpec((tm, d), row),
                   pl.BlockSpec((tm, LANE), row), pl.BlockSpec((tm, LANE), row)),
        out_shape=(jax.ShapeDtypeStruct((s, d), F32), jax.ShapeDtypeStruct((s, d), F32),
                   jax.ShapeDtypeStruct((s, LANE), jnp.int32), jax.ShapeDtypeStruct((s, LANE), F32)),
        compiler_params=_cparams(("arbitrary",)),
    )(x, oaT, obT, oc, od, w4, g1, ng, sc, sh, rw, rb)


def _segment_copies(tile, cnt_ref, seg_ref, off_ref, make_copy, start):
    def per_expert(e, carry):
        j = tile * N_EXPERTS + e
        cnt, a, b = cnt_ref[j], seg_ref[j], off_ref[j]
        size = MOE_TILE
        while size >= SUBLANE:
            bit = cnt & size

            @pl.when(bit != 0)
            def _():
                cp = make_copy(pl.multiple_of(a, SUBLANE), pl.multiple_of(b, SUBLANE), size)
                if start:
                    cp.start()
                else:
                    cp.wait()

            a, b = a + bit, b + bit
            size //= 2
        return carry

    lax.fori_loop(0, N_EXPERTS, per_expert, 0)


def _dispatch_kernel(cnt_ref, seg_ref, off_ref, nused_ref, h_ref, slotT_ref, rows_ref, sorted_ref, zero_ref, sem):
    tile = pl.program_id(0)
    tt = h_ref.shape[0]
    n = sorted_ref.shape[1]
    par = tile & 1
    r_ids = lax.broadcasted_iota(jnp.int32, (n, tt), 0)
    perm = jnp.where(r_ids == slotT_ref[0:1, :], 1.0, 0.0)
    for k in range(1, TOP_K):
        perm = perm + jnp.where(r_ids == slotT_ref[k:k + 1, :], 1.0, 0.0)
    perm = perm.astype(BF16)
    sorted_ref[par] = jnp.dot(perm, h_ref[...].astype(BF16), preferred_element_type=F32)

    def copies_from(p):
        def make_copy(a, b, size):
            return pltpu.make_async_copy(sorted_ref.at[p, pl.ds(a, size)], rows_ref.at[pl.ds(b, size)], sem.at[p])
        return make_copy

    _segment_copies(tile, cnt_ref, seg_ref, off_ref, copies_from(par), True)

    @pl.when(tile > 0)
    def _():
        _segment_copies(tile - 1, cnt_ref, seg_ref, off_ref, copies_from(1 - par), False)

    last = pl.num_programs(0) - 1

    @pl.when(tile == last)
    def _():
        _segment_copies(tile, cnt_ref, seg_ref, off_ref, copies_from(par), False)
        zero_ref[...] = jnp.zeros(zero_ref.shape, F32)

        def make_zero_copy(a, b, size):
            return pltpu.make_async_copy(zero_ref.at[pl.ds(a, size)], rows_ref.at[pl.ds(b, size)], sem.at[0])

        _segment_copies(last + 1, cnt_ref, seg_ref, off_ref, make_zero_copy, True)
        _segment_copies(last + 1, cnt_ref, seg_ref, off_ref, make_zero_copy, False)

        def block_copy(b):
            return pltpu.make_async_copy(zero_ref, rows_ref.at[pl.ds(pl.multiple_of(b * MOE_BLK, MOE_BLK), MOE_BLK)],
                                         sem.at[0])

        def issue(b, carry):
            block_copy(b).start()
            return carry

        def drain(b, carry):
            block_copy(b).wait()
            return carry

        n_blocks = rows_ref.shape[0] // MOE_BLK
        lax.fori_loop(nused_ref[0], n_blocks, issue, 0)
        lax.fori_loop(nused_ref[0], n_blocks, drain, 0)


def _dispatch(h, slotT, tile_cnt, tile_seg, tile_off, n_used, n_rows):
    s, d = h.shape
    tt = MOE_TILE
    grid_spec = pltpu.PrefetchScalarGridSpec(
        num_scalar_prefetch=4,
        grid=(s // tt,),
        in_specs=[pl.BlockSpec((tt, d), lambda i, c, sg, of, nu: (i, 0)),
                  pl.BlockSpec((TOP_K, tt), lambda i, c, sg, of, nu: (0, i))],
        out_specs=pl.BlockSpec(memory_space=pl.ANY),
        scratch_shapes=[pltpu.VMEM((2, MOE_SORT_ROWS, d), F32), pltpu.VMEM((MOE_BLK, d), F32),
                        pltpu.SemaphoreType.DMA((2,))],
    )
    return pl.pallas_call(
        _dispatch_kernel,
        grid_spec=grid_spec,
        out_shape=jax.ShapeDtypeStruct((n_rows, d), F32),
        compiler_params=_cparams(("arbitrary",)),
    )(tile_cnt, tile_seg, tile_off, n_used, h, slotT)


def _ffn_kernel(be_ref, first_ref, next_ref, slot_ref, nused_ref, x_ref, wu_hbm, bu_ref, wd_hbm, bd_ref, y_ref,
                wu_f32, wd_f32, wu_bf, wd_bf, sem):
    b = pl.program_id(0)

    def weight_copies(expert, slot):
        return (pltpu.make_async_copy(wu_hbm.at[expert], wu_f32.at[slot], sem.at[0, slot]),
                pltpu.make_async_copy(wd_hbm.at[expert], wd_f32.at[slot], sem.at[1, slot]))

    @pl.when(b == 0)
    def _():
        for cp in weight_copies(be_ref[0], 0):
            cp.start()

    @pl.when(first_ref[b] == 1)
    def _():
        slot = slot_ref[b]
        for cp in weight_copies(be_ref[b], slot):
            cp.wait()

        @pl.when(next_ref[b] >= 0)
        def _():
            for cp in weight_copies(next_ref[b], 1 - slot):
                cp.start()

        wu_bf[...] = wu_f32[slot].astype(BF16)
        wd_bf[...] = wd_f32[slot].astype(BF16)

    @pl.when(b < nused_ref[0])
    def _():
        hu = jnp.dot(x_ref[...].astype(BF16), wu_bf[...], preferred_element_type=F32) + bu_ref[0]
        g = jnp.minimum(hu[:, :D_FF], SWIGLU_LIMIT)
        lin = jnp.clip(hu[:, D_FF:], -SWIGLU_LIMIT, SWIGLU_LIMIT)
        act = (lin + 1.0) * (g * _sigmoid(SWIGLU_ALPHA * g))
        y_ref[...] = jnp.dot(act.astype(BF16), wd_bf[...], preferred_element_type=F32) + bd_ref[0]

    @pl.when(b >= nused_ref[0])
    def _():
        y_ref[...] = jnp.zeros(y_ref.shape, F32)


def _ffn(x_rows, blk_expert, first, n_used, w_up, b_up, w_down, b_down):
    n_rows, d = x_rows.shape
    blk = MOE_BLK
    n_blocks = n_rows // blk
    n_exp, _, f2 = w_up.shape
    i32 = jnp.int32
    ids = jnp.arange(n_blocks, dtype=i32)
    later_start = (ids[None, :] > ids[:, None]) & (first[None, :] == 1)
    next_pos = jnp.min(jnp.where(later_start, ids[None, :], n_blocks), axis=1)
    next_expert = jnp.where(next_pos < n_blocks, blk_expert[jnp.minimum(next_pos, n_blocks - 1)], -1).astype(i32)
    slot = ((jnp.cumsum(first) - 1) & 1).astype(i32)
    idx = lambda b, be, fi, nx, sl, nu: (be[b], 0, 0)
    grid_spec = pltpu.PrefetchScalarGridSpec(
        num_scalar_prefetch=5,
        grid=(n_blocks,),
        in_specs=[pl.BlockSpec((blk, d), lambda b, be, fi, nx, sl, nu: (jnp.minimum(b, nu[0] - 1), 0)),
                  pl.BlockSpec(memory_space=pl.ANY),
                  pl.BlockSpec((1, 1, f2), idx),
                  pl.BlockSpec(memory_space=pl.ANY),
                  pl.BlockSpec((1, 1, d), idx)],
        out_specs=pl.BlockSpec((blk, d), lambda b, be, fi, nx, sl, nu: (b, 0)),
        scratch_shapes=[pltpu.VMEM((2, d, f2), F32), pltpu.VMEM((2, D_FF, d), F32),
                        pltpu.VMEM((d, f2), BF16), pltpu.VMEM((D_FF, d), BF16),
                        pltpu.SemaphoreType.DMA((2, 2))],
    )
    return pl.pallas_call(
        _ffn_kernel,
        grid_spec=grid_spec,
        out_shape=jax.ShapeDtypeStruct((n_rows, d), F32),
        compiler_params=_cparams(("arbitrary",)),
    )(blk_expert, first, next_expert, slot, n_used, x_rows, w_up, b_up.reshape(n_exp, 1, f2),
      w_down, b_down.reshape(n_exp, 1, d))


def _combine_kernel(cnt_ref, seg_ref, off_ref, x_ref, slot_ref, gate_ref, g2_ref, rows_ref, o_ref, buf_ref, sem):
    tile = pl.program_id(0)
    tt = x_ref.shape[0]
    n = buf_ref.shape[1]
    par = tile & 1

    def copies_into(p):
        def make_copy(a, b, size):
            return pltpu.make_async_copy(rows_ref.at[pl.ds(b, size)], buf_ref.at[p, pl.ds(a, size)], sem.at[p])
        return make_copy

    @pl.when(tile == 0)
    def _():
        buf_ref[...] = jnp.zeros(buf_ref.shape, F32)
        _segment_copies(tile, cnt_ref, seg_ref, off_ref, copies_into(par), True)

    @pl.when(tile + 1 < pl.num_programs(0))
    def _():
        _segment_copies(tile + 1, cnt_ref, seg_ref, off_ref, copies_into(1 - par), True)

    slots, gates = slot_ref[...], gate_ref[...]
    r_ids = lax.broadcasted_iota(jnp.int32, (tt, n), 1)
    sel = jnp.where(r_ids == slots[:, 0:1], gates[:, 0:1], 0.0)
    for k in range(1, TOP_K):
        sel = sel + jnp.where(r_ids == slots[:, k:k + 1], gates[:, k:k + 1], 0.0)
    sel_hi, sel_lo = _split_bf16(sel)
    _segment_copies(tile, cnt_ref, seg_ref, off_ref, copies_into(par), False)
    y_hi, y_lo = _split_bf16(buf_ref[par])
    both = jnp.dot(jnp.concatenate([sel_hi, sel_lo], axis=0), y_hi, preferred_element_type=F32)
    y = both[:tt] + both[tt:] + jnp.dot(sel_hi, y_lo, preferred_element_type=F32)
    o_ref[...] = x_ref[...] + g2_ref[...] * y


def _combine(x, slots, gates, g2, y_rows, tile_cnt, tile_seg, tile_off):
    s, d = x.shape
    tt = MOE_TILE
    row = lambda i, c, sg, of: (i, 0)
    grid_spec = pltpu.PrefetchScalarGridSpec(
        num_scalar_prefetch=3,
        grid=(s // tt,),
        in_specs=[pl.BlockSpec((tt, d), row), pl.BlockSpec((tt, LANE), row), pl.BlockSpec((tt, LANE), row),
                  pl.BlockSpec((1, d), lambda i, c, sg, of: (0, 0)),
                  pl.BlockSpec(memory_space=pl.ANY)],
        out_specs=pl.BlockSpec((tt, d), row),
        scratch_shapes=[pltpu.VMEM((2, MOE_SORT_ROWS, d), F32), pltpu.SemaphoreType.DMA((2,))],
    )
    return pl.pallas_call(
        _combine_kernel,
        grid_spec=grid_spec,
        out_shape=jax.ShapeDtypeStruct((s, d), F32),
        compiler_params=_cparams(("arbitrary",)),
    )(tile_cnt, tile_seg, tile_off, x, slots, gates, g2, y_rows)


def _moe(x, h, idx_pad, gates_pad, g2, w_up, b_up, w_down, b_down, layer_idx):
    s, d = x.shape
    n = s * TOP_K
    blk = MOE_BLK
    tt = MOE_TILE
    nt = s // tt
    n_blocks = -(-(n + nt * N_EXPERTS * SUBLANE) // blk) + N_EXPERTS
    i32 = jnp.int32
    idx = idx_pad[:, :TOP_K]
    onehot = jnp.sum((idx[:, :, None] == jnp.arange(N_EXPERTS, dtype=i32)).astype(i32), axis=1)
    onehot_t = onehot.reshape(nt, tt, N_EXPERTS)
    rank_in_tile = (jnp.cumsum(onehot_t, axis=1) - onehot_t).reshape(s, N_EXPERTS)
    tile_cnt = (jnp.sum(onehot_t, axis=1) + SUBLANE - 1) // SUBLANE * SUBLANE
    tile_base = jnp.cumsum(tile_cnt, axis=0) - tile_cnt
    tile_seg = jnp.cumsum(tile_cnt, axis=1) - tile_cnt
    counts = jnp.sum(tile_cnt, axis=0)
    padded = (counts + blk - 1) // blk * blk
    pad_ends = jnp.cumsum(padded)
    pad_starts = pad_ends - padded
    tile_off = pad_starts[None, :] + tile_base
    slot = (jnp.take_along_axis(jnp.repeat(tile_seg, tt, axis=0) + rank_in_tile, idx, axis=1)).astype(i32)
    slot_pad = jnp.pad(slot, ((0, 0), (0, LANE - TOP_K)))
    blk_start = jnp.arange(n_blocks, dtype=i32) * blk
    blk_expert = jnp.minimum(jnp.sum((pad_ends[None, :] <= blk_start[:, None]).astype(i32), axis=1),
                             N_EXPERTS - 1).astype(i32)
    first = jnp.concatenate([jnp.ones((1,), i32), (blk_expert[1:] != blk_expert[:-1]).astype(i32)])
    n_used = (pad_ends[-1:] // blk).astype(i32)
    tile_cnt = jnp.concatenate([tile_cnt, (padded - counts)[None, :]], axis=0)
    tile_seg = jnp.concatenate([tile_seg, jnp.zeros((1, N_EXPERTS), tile_seg.dtype)], axis=0)
    tile_off = jnp.concatenate([tile_off, (pad_starts + counts)[None, :]], axis=0)
    tile_cnt, tile_seg, tile_off = [t.astype(i32).reshape(-1) for t in (tile_cnt, tile_seg, tile_off)]

    x_rows = _dispatch(h, slot.T, tile_cnt, tile_seg, tile_off, n_used, n_blocks * blk)
    y_rows = _ffn(x_rows, blk_expert + layer_idx * N_EXPERTS, first, n_used, w_up, b_up, w_down, b_down)
    return _combine(x, slot_pad, gates_pad, g2, y_rows, tile_cnt, tile_seg, tile_off)


def _block_diag(w):
    n, d, _ = w.shape
    eye = jnp.eye(n, dtype=w.dtype)
    return (eye[:, None, :, None] * w[:, :, None, :]).reshape(n * d, n * d)


def kernel(x, c, ada_w, ada_b, norm1_g, norm2_g, w_in, w_out, a_q_gain, a_k_gain, a_lam_q1, a_lam_k1, a_lam_q2, a_lam_k2, a_out_gain, b_q_gain, b_k_gain, c_w_g2, c_b_g, c_out_gain, d_conv_w, d_conv_b, d_w_a, d_b_a, d_w_x, d_b_x, d_lambda, router_w, router_b, exp_w_up, exp_b_up, exp_w_down, exp_b_down):
    bsz, s, d = x.shape
    assert bsz == 1 and d == D_MODEL and s % ROW_TILE == 0 and s % (max(A_KEY_BLOCKS, B_KEY_BLOCKS) * ATT_TILE) == 0
    depth = ada_w.shape[0]
    mod = _modulation(c, ada_w, ada_b)
    xs = x.reshape(s, d)
    w_up_all = exp_w_up.reshape(depth * N_EXPERTS, d, 2 * D_FF)
    b_up_all = exp_b_up.reshape(depth * N_EXPERTS, 2 * D_FF)
    w_down_all = exp_w_down.reshape(depth * N_EXPERTS, D_FF, d)
    b_down_all = exp_b_down.reshape(depth * N_EXPERTS, d)
    for l in range(depth):
        sh1, sc1, g1, sh2, sc2, g2 = [mod[l, i * d:(i + 1) * d].reshape(1, d) for i in range(6)]
        w = w_in[l]
        w_cat = jnp.concatenate([w[:, :2048], w[:, 2064:2832], w[:, 2048:2064],
                                 jnp.zeros((d, W_IN_PAD - 2832), w.dtype)], axis=1).astype(BF16)
        wg2p = jnp.zeros((LANE, LANE), F32).at[:GLA_RANK, :].set(c_w_g2[l])
        (aqT, ak, avT, bqT, bk, bvT, bkm, cq, ck, cv, cr, cg, dx, dgate) = _inproj(
            xs, norm1_g[l].reshape(1, d), sc1, sh1, w_cat,
            jnp.tile(a_q_gain[l], 2 * A_HEADS).reshape(1, -1), jnp.tile(a_k_gain[l], 2 * A_HEADS).reshape(1, -1),
            jnp.tile(b_q_gain[l], B_HEADS).reshape(1, -1), jnp.tile(b_k_gain[l], B_HEADS).reshape(1, -1),
            wg2p, c_b_g[l].reshape(1, -1))
        o_aT = _diff_attention(aqT, ak, avT, a_lam_q1[l].reshape(1, -1), a_lam_k1[l].reshape(1, -1),
                               a_lam_q2[l].reshape(1, -1), a_lam_k2[l].reshape(1, -1),
                               a_out_gain[l].reshape(-1, 1), l)
        km = bkm.reshape(s // MOBA_BLOCK, B_HEADS, B_DIM).transpose(1, 0, 2)
        o_bT = _moba(bqT, bk, bvT, km)
        o_c = _gla(cq, ck, cg, cv, cr, jnp.tile(c_out_gain[l], C_HEADS).reshape(1, -1))
        o_d = _rglru(dx, dgate, d_conv_w[l], d_conv_b[l].reshape(1, -1),
                     _block_diag(d_w_a[l]).astype(BF16), d_b_a[l].reshape(1, -1),
                     _block_diag(d_w_x[l]).astype(BF16), d_b_x[l].reshape(1, -1), d_lambda[l].reshape(1, -1))
        w4 = w_out[l].astype(BF16).reshape(4, N_GROUP_W, d)
        xs, h2, idx_pad, gates_pad = _outproj(xs, o_aT, o_bT, o_c, o_d, w4, g1, norm2_g[l].reshape(1, d),
                                              sc2, sh2, router_w[l], router_b[l].reshape(1, -1))
        xs = _moe(xs, h2, idx_pad, gates_pad, g2, w_up_all, b_up_all, w_down_all, b_down_all, l)
    return xs.reshape(bsz, s, d)
```

```python
import functools
import math

import jax
import jax.numpy as jnp
from jax import lax
from jax.experimental import pallas as pl
from jax.experimental.pallas import tpu as pltpu

F32 = jnp.float32
BF16 = jnp.bfloat16
HI = lax.Precision.HIGHEST

EPS = 1e-6
NEG = -1e30
M_INIT = -1e30
MASKED = -2e30
LOG2E = 1.4426950408889634

D_MODEL = 1024
N_GROUP_W = 256
A_HEADS, A_QK, A_V = 4, 32, 64
B_HEADS, B_DIM = 4, 64
MOBA_BLOCK, MOBA_TOPK = 256, 3
C_HEADS, C_K, C_V = 4, 32, 64
GLA_RANK, GLA_TAU = 16, 16.0
GLA_SUB = 16
D_WIDTH, D_BLOCKS, CONV_W, LRU_C = 256, 4, 4, 8.0
N_EXPERTS, TOP_K, D_FF = 32, 4, 1024
SWIGLU_ALPHA, SWIGLU_LIMIT = 1.702, 7.0

LANE = 128
SUBLANE = 8
W_IN_PAD = 2944

ROW_TILE = 512
ATT_TILE = 256
A_KEY_BLOCKS = 2
B_KEY_BLOCKS = 4
SUM_ROWS = 16
MOE_BLK = 256
MOE_TILE = 256
MOE_SORT_ROWS = MOE_TILE * TOP_K + N_EXPERTS * SUBLANE
VMEM_LIMIT = 56 * 1024 * 1024


def _cparams(sem):
    return pltpu.CompilerParams(dimension_semantics=sem, vmem_limit_bytes=VMEM_LIMIT)


def _log_sigmoid(z):
    return jnp.minimum(z, 0.0) - jnp.log1p(jnp.exp(-jnp.abs(z)))


def _sigmoid(z):
    return 1.0 / (1.0 + jnp.exp(-z))


def _split_bf16(t):
    hi = t.astype(BF16)
    return hi, (t - hi.astype(F32)).astype(BF16)


def _dot_exact_rhs(t, w_bf16, parts=2):
    out, rest = None, t
    for _ in range(parts):
        piece = rest.astype(BF16)
        term = jnp.dot(piece, w_bf16, preferred_element_type=F32)
        out = term if out is None else out + term
        rest = rest - piece.astype(F32)
    return out


def _dot_exact_lhs(w_bf16, t, parts=3):
    out, rest = None, t
    for _ in range(parts):
        piece = rest.astype(BF16)
        term = jnp.dot(w_bf16, piece, preferred_element_type=F32)
        out = term if out is None else out + term
        rest = rest - piece.astype(F32)
    return out


def _group_rms(t, group):
    n = t.shape[-1]
    shift = int(math.log2(group))
    r = lax.broadcasted_iota(jnp.int32, (n, n), 0) >> shift
    c = lax.broadcasted_iota(jnp.int32, (n, n), 1) >> shift
    bd = jnp.where(r == c, 1.0, 0.0).astype(BF16)
    ss = _dot_exact_rhs(t * t, bd)
    return t * lax.rsqrt(ss * (1.0 / group) + EPS)


def _mod_kernel(c_ref, w_ref, b_ref, o_ref):
    c = c_ref[...]
    cond = c * _sigmoid(c)
    cond8 = jnp.broadcast_to(cond, (8, cond.shape[-1]))
    o_ref[0] = jnp.dot(cond8, w_ref[0], precision=HI, preferred_element_type=F32) + b_ref[0]


def _modulation(c, ada_w, ada_b):
    depth, d, n = ada_w.shape
    tn = 1536
    out = pl.pallas_call(
        _mod_kernel,
        grid=(depth, n // tn),
        in_specs=[pl.BlockSpec((1, d), lambda l, j: (0, 0)),
                  pl.BlockSpec((1, d, tn), lambda l, j: (l, 0, j)),
                  pl.BlockSpec((1, 1, tn), lambda l, j: (l, 0, j))],
        out_specs=pl.BlockSpec((1, 8, tn), lambda l, j: (l, 0, j)),
        out_shape=jax.ShapeDtypeStruct((depth, 8, n), F32),
        compiler_params=_cparams(("arbitrary", "arbitrary")),
    )(c, ada_w, ada_b.reshape(depth, 1, n))
    return out[:, 0, :]


def _inproj_kernel(x_ref, ng_ref, sc_ref, sh_ref, w_ref, aqg_ref, akg_ref, bqg_ref, bkg_ref, wg2_ref, bg_ref,
                   aqT_ref, ak_ref, avT_ref, bqT_ref, bk_ref, bvT_ref, bkm_ref,
                   cq_ref, ck_ref, cv_ref, cr_ref, cg_ref, dx_ref, dg_ref):
    x = x_ref[...]
    tm = x.shape[0]
    h = x * lax.rsqrt(jnp.mean(x * x, axis=-1, keepdims=True) + EPS) * ng_ref[...]
    h = h * (1.0 + sc_ref[...]) + sh_ref[...]
    proj = jnp.dot(h.astype(BF16), w_ref[...], preferred_element_type=F32)

    aq = _group_rms(proj[:, 0:256], A_QK) * aqg_ref[...] * (A_QK ** -0.5 * LOG2E)
    aqT_ref[...] = aq.T.astype(BF16)
    ak = _group_rms(proj[:, 256:512], A_QK) * akg_ref[...]
    ak_ref[...] = ak.astype(BF16)
    avT_ref[...] = proj[:, 512:768].T.astype(BF16)

    bq = _group_rms(proj[:, 768:1024], B_DIM) * bqg_ref[...] * (B_DIM ** -0.5 * LOG2E)
    bqT_ref[...] = bq.T.astype(BF16)
    bk = _group_rms(proj[:, 1024:1280], B_DIM) * bkg_ref[...]
    bk_ref[...] = bk.astype(BF16)
    bvT_ref[...] = proj[:, 1280:1536].T.astype(BF16)
    bkm_ref[0] = jnp.mean(bk.reshape(tm // MOBA_BLOCK, MOBA_BLOCK, N_GROUP_W), axis=1)

    cq_ref[...] = proj[:, 1536:1664] * (C_K ** -0.5)
    ck_ref[...] = proj[:, 1664:1792]
    cv_ref[...] = proj[:, 1792:2048]
    cr_ref[...] = proj[:, 2048:2304]
    gl_hi, gl_lo = _split_bf16(proj[:, 2816:2944])
    w2_hi, w2_lo = _split_bf16(wg2_ref[...])
    z = (jnp.dot(gl_hi, w2_hi, preferred_element_type=F32) + jnp.dot(gl_lo, w2_hi, preferred_element_type=F32)
         + jnp.dot(gl_hi, w2_lo, preferred_element_type=F32)) + bg_ref[...]
    cg_ref[...] = _log_sigmoid(z) * (1.0 / GLA_TAU)

    dx_ref[...] = proj[:, 2304:2560]
    dg_ref[...] = proj[:, 2560:2816]


def _inproj(x, ng, sc, sh, w_cat, aqg, akg, bqg, bkg, wg2p, bg):
    s, d = x.shape
    tm = ROW_TILE
    nt = s // tm
    row = lambda i: (i, 0)
    const2 = lambda i: (0, 0)
    vec = pl.BlockSpec((1, d), const2)
    gvec = pl.BlockSpec((1, N_GROUP_W), const2)
    out_shape = (
        jax.ShapeDtypeStruct((N_GROUP_W, s), BF16),
        jax.ShapeDtypeStruct((s, N_GROUP_W), BF16),
        jax.ShapeDtypeStruct((N_GROUP_W, s), BF16),
        jax.ShapeDtypeStruct((N_GROUP_W, s), BF16),
        jax.ShapeDtypeStruct((s, N_GROUP_W), BF16),
        jax.ShapeDtypeStruct((N_GROUP_W, s), BF16),
        jax.ShapeDtypeStruct((nt, tm // MOBA_BLOCK, N_GROUP_W), F32),
        jax.ShapeDtypeStruct((s, 128), F32),
        jax.ShapeDtypeStruct((s, 128), F32),
        jax.ShapeDtypeStruct((s, 256), F32),
        jax.ShapeDtypeStruct((s, 256), F32),
        jax.ShapeDtypeStruct((s, 128), F32),
        jax.ShapeDtypeStruct((s, 256), F32),
        jax.ShapeDtypeStruct((s, 256), F32),
    )
    out_specs = (
        pl.BlockSpec((N_GROUP_W, tm), lambda i: (0, i)),
        pl.BlockSpec((tm, N_GROUP_W), row),
        pl.BlockSpec((N_GROUP_W, tm), lambda i: (0, i)),
        pl.BlockSpec((N_GROUP_W, tm), lambda i: (0, i)),
        pl.BlockSpec((tm, N_GROUP_W), row),
        pl.BlockSpec((N_GROUP_W, tm), lambda i: (0, i)),
        pl.BlockSpec((1, tm // MOBA_BLOCK, N_GROUP_W), lambda i: (i, 0, 0)),
        pl.BlockSpec((tm, 128), row),
        pl.BlockSpec((tm, 128), row),
        pl.BlockSpec((tm, 256), row),
        pl.BlockSpec((tm, 256), row),
        pl.BlockSpec((tm, 128), row),
        pl.BlockSpec((tm, 256), row),
        pl.BlockSpec((tm, 256), row),
    )
    return pl.pallas_call(
        _inproj_kernel,
        grid=(nt,),
        in_specs=[pl.BlockSpec((tm, d), row), vec, vec, vec,
                  pl.BlockSpec((d, W_IN_PAD), const2),
                  gvec, gvec, gvec, gvec,
                  pl.BlockSpec((LANE, LANE), const2),
                  pl.BlockSpec((1, LANE), const2)],
        out_specs=out_specs,
        out_shape=out_shape,
        compiler_params=_cparams(("arbitrary",)),
    )(x, ng, sc, sh, w_cat, aqg, akg, bqg, bkg, wg2p, bg)


def _pad_query_rows(qT_ref, qpad_ref, width):
    per = LANE // width
    rows = lax.broadcasted_iota(jnp.int32, qpad_ref.shape[1:], 0)
    for g in range(qpad_ref.shape[0]):
        blk = qT_ref[LANE * (g // per):LANE * (g // per + 1), :]
        lo = width * (g % per)
        qpad_ref[g] = jnp.where((rows >= lo) & (rows < lo + width), blk, jnp.zeros_like(blk))


def _with_sum_rows(vT):
    return jnp.concatenate([vT, jnp.ones((SUM_ROWS, vT.shape[1]), vT.dtype)], axis=0)


def _softmax_tiles(k_slabs, qpad_ref, v_tiles, adjust, m_ref, acc_ref):
    n = len(k_slabs)
    scores = [jnp.dot(k_slabs[g], qpad_ref[g], preferred_element_type=F32) for g in range(n)]
    probs, alphas = [], []
    for g in range(n):
        sT = adjust(g, scores[g])
        m_old = m_ref[g]
        m_new = jnp.maximum(m_old, jnp.max(sT, axis=0, keepdims=True))
        alphas.append(jnp.exp2(m_old - m_new))
        probs.append(jnp.exp2(sT - m_new).astype(BF16))
        m_ref[g] = m_new
    pv = [jnp.dot(v_tiles[g], probs[g], preferred_element_type=F32) for g in range(n)]
    for g in range(n):
        acc_ref[g] = alphas[g] * acc_ref[g] + pv[g]


def _diffattn_kernel(qT_ref, k_ref, vT_ref, lq1_ref, lk1_ref, lq2_ref, lk2_ref, og_ref, o_ref,
                     m_ref, acc_ref, qpad_ref, *, lam_init):
    KEY_BLOCKS = A_KEY_BLOCKS
    t, tk = ATT_TILE, KEY_BLOCKS * ATT_TILE
    n_maps = 2 * A_HEADS
    iq = pl.program_id(0)
    lam = (jnp.exp(jnp.sum(lq1_ref[...] * lk1_ref[...], axis=-1, keepdims=True))
           - jnp.exp(jnp.sum(lq2_ref[...] * lk2_ref[...], axis=-1, keepdims=True)) + lam_init)
    m_ref[...] = jnp.full(m_ref.shape, M_INIT, F32)
    acc_ref[...] = jnp.zeros(acc_ref.shape, F32)
    _pad_query_rows(qT_ref, qpad_ref, A_QK)

    def tile(ik, adjust):
        keys = pl.ds(pl.multiple_of(ik * tk, tk), tk)
        k_slabs = [k_ref[keys, LANE * (g // 4):LANE * (g // 4 + 1)] for g in range(n_maps)]
        v_heads = [_with_sum_rows(vT_ref[A_V * hd:A_V * (hd + 1), keys]) for hd in range(A_HEADS)]
        _softmax_tiles(k_slabs, qpad_ref, [v_heads[g // 2] for g in range(n_maps)], adjust, m_ref, acc_ref)

    def body(ik, carry):
        tile(ik, lambda g, sT: sT)
        return carry

    n_full = iq // KEY_BLOCKS
    lax.fori_loop(0, n_full, body, 0)
    causal = (lax.broadcasted_iota(jnp.int32, (tk, t), 0) - lax.broadcasted_iota(jnp.int32, (tk, t), 1)
              <= (iq - KEY_BLOCKS * n_full) * t)
    tile(n_full, lambda g, sT: jnp.where(causal, sT, MASKED))

    for hd in range(A_HEADS):
        a1, a2 = acc_ref[2 * hd], acc_ref[2 * hd + 1]
        o = a1[:A_V] / a1[A_V:A_V + 1] - lam * (a2[:A_V] / a2[A_V:A_V + 1])
        o = o * lax.rsqrt(jnp.mean(o * o, axis=0, keepdims=True) + EPS) * og_ref[...]
        o_ref[A_V * hd:A_V * (hd + 1), :] = o * (1.0 - lam_init)


def _diff_attention(aqT, ak, avT, lq1, lk1, lq2, lk2, out_gain_col, layer_idx):
    s = aqT.shape[1]
    t = ATT_TILE
    n_maps = 2 * A_HEADS
    lam_init = 0.8 - 0.6 * math.exp(-0.3 * layer_idx)
    vec = pl.BlockSpec((1, A_QK), lambda i: (0, 0))
    return pl.pallas_call(
        functools.partial(_diffattn_kernel, lam_init=lam_init),
        grid=(s // t,),
        in_specs=[pl.BlockSpec((N_GROUP_W, t), lambda i: (0, i)),
                  pl.BlockSpec((s, N_GROUP_W), lambda i: (0, 0)),
                  pl.BlockSpec((N_GROUP_W, s), lambda i: (0, 0)),
                  vec, vec, vec, vec,
                  pl.BlockSpec((A_V, 1), lambda i: (0, 0))],
        out_specs=pl.BlockSpec((N_GROUP_W, t), lambda i: (0, i)),
        out_shape=jax.ShapeDtypeStruct((N_GROUP_W, s), F32),
        scratch_shapes=[pltpu.VMEM((n_maps, 1, t), F32), pltpu.VMEM((n_maps, A_V + SUM_ROWS, t), F32),
                        pltpu.VMEM((n_maps, LANE, t), BF16)],
        compiler_params=_cparams(("arbitrary",)),
    )(aqT, ak, avT, lq1, lk1, lq2, lk2, out_gain_col)


def _moba_kernel(qT_ref, k_ref, vT_ref, km_ref, o_ref, m_ref, acc_ref, bias_ref, qpad_ref):
    KEY_BLOCKS = B_KEY_BLOCKS
    t, tk = ATT_TILE, KEY_BLOCKS * ATT_TILE
    own = pl.program_id(0)
    nb = km_ref.shape[1]
    m_ref[...] = jnp.full(m_ref.shape, M_INIT, F32)
    acc_ref[...] = jnp.zeros(acc_ref.shape, F32)
    _pad_query_rows(qT_ref, qpad_ref, B_DIM)

    blk = lax.broadcasted_iota(jnp.int32, (nb, t), 0)
    earlier = blk < own
    for hd in range(B_HEADS):
        qT = qT_ref[B_DIM * hd:B_DIM * (hd + 1), :].astype(F32)
        gate = jnp.dot(km_ref[hd], qT, precision=HI, preferred_element_type=F32)
        gate = jnp.where(earlier, gate, NEG)
        sel_bias = jnp.full((nb, t), MASKED, F32)
        for _ in range(MOBA_TOPK):
            top = jnp.max(gate, axis=0, keepdims=True)
            pick = blk == jnp.min(jnp.where(gate == top, blk, nb), axis=0, keepdims=True)
            sel_bias = jnp.where(pick, jnp.where(earlier, 0.0, MASKED), sel_bias)
            gate = jnp.where(pick, -jnp.inf, gate)
        bias_ref[hd] = sel_bias

    def tile(j, adjust):
        keys = pl.ds(pl.multiple_of(j * tk, tk), tk)
        k_slabs = [k_ref[keys, LANE * (hd // 2):LANE * (hd // 2 + 1)] for hd in range(B_HEADS)]
        v_tiles = [_with_sum_rows(vT_ref[B_DIM * hd:B_DIM * (hd + 1), keys]) for hd in range(B_HEADS)]
        _softmax_tiles(k_slabs, qpad_ref, v_tiles, adjust, m_ref, acc_ref)

    def block_bias(hd, b):
        return bias_ref[hd, pl.ds(b, 1), :]

    def body(j, carry):
        tile(j, lambda hd, sT: jnp.concatenate(
            [sT[t * part:t * (part + 1)] + block_bias(hd, KEY_BLOCKS * j + part) for part in range(KEY_BLOCKS)],
            axis=0))
        return carry

    n_full = own // KEY_BLOCKS
    lax.fori_loop(0, n_full, body, 0)

    causal = (lax.broadcasted_iota(jnp.int32, (t, t), 0) <= lax.broadcasted_iota(jnp.int32, (t, t), 1))

    def last(hd, sT):
        parts = []
        for part in range(KEY_BLOCKS):
            b = KEY_BLOCKS * n_full + part
            sp = sT[t * part:t * (part + 1)]
            other = sp + jnp.where(b < own, block_bias(hd, b), MASKED)
            parts.append(jnp.where(b == own, jnp.where(causal, sp, MASKED), other))
        return jnp.concatenate(parts, axis=0)

    tile(n_full, last)
    for hd in range(B_HEADS):
        a = acc_ref[hd]
        o_ref[B_DIM * hd:B_DIM * (hd + 1), :] = a[:B_DIM] / a[B_DIM:B_DIM + 1]


def _moba(bqT, bk, bvT, km):
    s = bqT.shape[1]
    t = ATT_TILE
    nb = km.shape[1]
    return pl.pallas_call(
        _moba_kernel,
        grid=(s // t,),
        in_specs=[pl.BlockSpec((N_GROUP_W, t), lambda i: (0, i)),
                  pl.BlockSpec((s, N_GROUP_W), lambda i: (0, 0)),
                  pl.BlockSpec((N_GROUP_W, s), lambda i: (0, 0)),
                  pl.BlockSpec((B_HEADS, nb, B_DIM), lambda i: (0, 0, 0))],
        out_specs=pl.BlockSpec((N_GROUP_W, t), lambda i: (0, i)),
        out_shape=jax.ShapeDtypeStruct((N_GROUP_W, s), F32),
        scratch_shapes=[pltpu.VMEM((B_HEADS, 1, t), F32),
                        pltpu.VMEM((B_HEADS, B_DIM + SUM_ROWS, t), F32), pltpu.VMEM((B_HEADS, nb, t), F32),
                        pltpu.VMEM((B_HEADS, LANE, t), BF16)],
        compiler_params=_cparams(("arbitrary",)),
    )(bqT, bk, bvT, km)


def _gla_kernel(q_ref, k_ref, g_ref, v_ref, r_ref, og_ref, o_ref, stateT_ref, kbuf_ref, bbuf_ref, vbuf_ref):
    tm = q_ref.shape[0]
    c = GLA_SUB
    shift = int(math.log2(c))

    @pl.when(pl.program_id(0) == 0)
    def _():
        stateT_ref[...] = jnp.zeros(stateT_ref.shape, F32)

    q, k, v = q_ref[...], k_ref[...], v_ref[...]
    ri = lax.broadcasted_iota(jnp.int32, (tm, tm), 0)
    ci = lax.broadcasted_iota(jnp.int32, (tm, tm), 1)
    same = (ri >> shift) == (ci >> shift)
    g = g_ref[...]
    b = _dot_exact_lhs(jnp.where(same & (ri >= ci), 1.0, 0.0).astype(BF16), g)
    b_last = _dot_exact_lhs(jnp.where(same, 1.0, 0.0).astype(BF16), g)

    k_shift, v_shift = int(math.log2(C_K)), int(math.log2(C_V))
    head_sum = jnp.where((lax.broadcasted_iota(jnp.int32, (C_HEADS * C_K, C_HEADS * C_V), 0) >> k_shift)
                         == (lax.broadcasted_iota(jnp.int32, (C_HEADS * C_K, C_HEADS * C_V), 1) >> v_shift),
                         1.0, 0.0).astype(BF16)

    for buf, val in ((kbuf_ref, k), (bbuf_ref, b), (vbuf_ref, v)):
        buf[0:c, :] = jnp.zeros((c, val.shape[1]), F32)
        buf[c:c + tm, :] = val
    pos = lax.broadcasted_iota(jnp.int32, (tm, C_HEADS * C_K), 0) & (c - 1)
    o = jnp.zeros((tm, C_HEADS * C_V), F32)
    for delta in range(c):
        k_d = kbuf_ref[c - delta:c - delta + tm, :]
        b_d = bbuf_ref[c - delta:c - delta + tm, :]
        v_d = vbuf_ref[c - delta:c - delta + tm, :]
        w = q * k_d * jnp.exp(jnp.where(pos >= delta, b - b_d, NEG))
        w_hi, w_lo = _split_bf16(w)
        att = (jnp.dot(w_hi, head_sum, preferred_element_type=F32)
               + jnp.dot(w_lo, head_sum, preferred_element_type=F32))
        o = o + att * v_d

    q_dec = (q * jnp.exp(b)).astype(BF16)
    k_dec = (k * jnp.exp(b_last - b)).astype(BF16)
    decay = jnp.exp(b_last)
    v_bf = v.astype(BF16)
    own_head = ((lax.broadcasted_iota(jnp.int32, (C_HEADS * C_V, C_HEADS * C_K), 0) >> v_shift)
                == (lax.broadcasted_iota(jnp.int32, (C_HEADS * C_V, C_HEADS * C_K), 1) >> k_shift))
    nt_dims = (((1,), (1,)), ((), ()))
    tn_dims = (((0,), (0,)), ((), ()))
    updates = [lax.dot_general(v_bf[c * i:c * (i + 1)], k_dec[c * i:c * (i + 1)], tn_dims,
                               preferred_element_type=F32) for i in range(tm // c)]
    stateT = stateT_ref[...]
    far = []
    for i in range(tm // c):
        far.append(lax.dot_general(q_dec[c * i:c * (i + 1)], stateT.astype(BF16), nt_dims,
                                   preferred_element_type=F32))
        stateT = stateT * decay[c * i:c * i + 1, :] + jnp.where(own_head, updates[i], 0.0)
    stateT_ref[...] = stateT
    o = o + jnp.concatenate(far, axis=0)

    r = r_ref[...]
    o_ref[...] = _group_rms(o, C_V) * og_ref[...] * (r * _sigmoid(r))


def _gla(cq, ck, cg, cv, cr, out_gain):
    s = cq.shape[0]
    tm = ROW_TILE
    row = lambda i: (i, 0)
    return pl.pallas_call(
        _gla_kernel,
        grid=(s // tm,),
        in_specs=[pl.BlockSpec((tm, 128), row), pl.BlockSpec((tm, 128), row), pl.BlockSpec((tm, 128), row),
                  pl.BlockSpec((tm, 256), row), pl.BlockSpec((tm, 256), row),
                  pl.BlockSpec((1, C_HEADS * C_V), lambda i: (0, 0))],
        out_specs=pl.BlockSpec((tm, 256), row),
        out_shape=jax.ShapeDtypeStruct((s, 256), F32),
        scratch_shapes=[pltpu.VMEM((C_HEADS * C_V, C_HEADS * C_K), F32),
                        pltpu.VMEM((tm + GLA_SUB, C_HEADS * C_K), F32),
                        pltpu.VMEM((tm + GLA_SUB, C_HEADS * C_K), F32),
                        pltpu.VMEM((tm + GLA_SUB, C_HEADS * C_V), F32)],
        compiler_params=_cparams(("arbitrary",)),
    )(cq, ck, cg, cv, cr, out_gain)


def _rglru_kernel(x_ref, gate_ref, cw_ref, cb_ref, wa_ref, ba_ref, wx_ref, bx_ref, lam_ref, o_ref,
                  xbuf_ref, h_ref):
    tm = x_ref.shape[0]
    pad = 8

    @pl.when(pl.program_id(0) == 0)
    def _():
        xbuf_ref[0:pad, :] = jnp.zeros((pad, D_WIDTH), F32)
        h_ref[...] = jnp.zeros(h_ref.shape, F32)

    xbuf_ref[pad:pad + tm, :] = x_ref[...]
    xc = cb_ref[...] + jnp.zeros((tm, D_WIDTH), F32)
    for w in range(CONV_W):
        lag = CONV_W - 1 - w
        xc = xc + cw_ref[w:w + 1, :] * xbuf_ref[pad - lag:pad - lag + tm, :]
    xbuf_ref[0:pad, :] = xbuf_ref[tm:tm + pad, :]

    xcb = xc.astype(BF16)
    r_gate = _sigmoid(jnp.dot(xcb, wa_ref[...], preferred_element_type=F32) + ba_ref[...])
    i_gate = _sigmoid(jnp.dot(xcb, wx_ref[...], preferred_element_type=F32) + bx_ref[...])
    log_a = LRU_C * r_gate * _log_sigmoid(lam_ref[...])
    a = jnp.exp(log_a)
    b = jnp.sqrt(1.0 - jnp.exp(2.0 * log_a)) * (i_gate * xc)

    rows = lax.broadcasted_iota(jnp.int32, (tm, D_WIDTH), 0)
    step = 1
    while step < tm:
        keep = rows >= step
        a_prev = jnp.where(keep, pltpu.roll(a, step, 0), 1.0)
        b_prev = jnp.where(keep, pltpu.roll(b, step, 0), 0.0)
        b = a * b_prev + b
        a = a * a_prev
        step *= 2
    h = a * h_ref[0:1, :] + b
    h_ref[...] = jnp.broadcast_to(h[tm - 1:tm, :], h_ref.shape)

    g = gate_ref[...]
    gelu = g * (0.5 * (1.0 + jnp.tanh(math.sqrt(2.0 / math.pi) * (g + 0.044715 * (g * g * g)))))
    o_ref[...] = h * gelu


def _rglru(dx, dgate, conv_w, conv_b, wa_bd, b_a, wx_bd, b_x, lam):
    s = dx.shape[0]
    tm = ROW_TILE
    row = lambda i: (i, 0)
    const2 = lambda i: (0, 0)
    vec = pl.BlockSpec((1, D_WIDTH), const2)
    mat = pl.BlockSpec((D_WIDTH, D_WIDTH), const2)
    return pl.pallas_call(
        _rglru_kernel,
        grid=(s // tm,),
        in_specs=[pl.BlockSpec((tm, D_WIDTH), row), pl.BlockSpec((tm, D_WIDTH), row),
                  pl.BlockSpec((CONV_W, D_WIDTH), const2), vec, mat, vec, mat, vec, vec],
        out_specs=pl.BlockSpec((tm, D_WIDTH), row),
        out_shape=jax.ShapeDtypeStruct((s, D_WIDTH), F32),
        scratch_shapes=[pltpu.VMEM((tm + 8, D_WIDTH), F32), pltpu.VMEM((8, D_WIDTH), F32)],
        compiler_params=_cparams(("arbitrary",)),
    )(dx, dgate, conv_w, conv_b, wa_bd, b_a, wx_bd, b_x, lam)


def _outproj_kernel(x_ref, oaT_ref, obT_ref, oc_ref, od_ref, w_ref, g1_ref, ng_ref, sc_ref, sh_ref,
                    rw_ref, rb_ref, xo_ref, h_ref, idx_ref, gate_ref):
    tm = x_ref.shape[0]
    y = jnp.dot(oaT_ref[...].T.astype(BF16), w_ref[0], preferred_element_type=F32)
    y = y + jnp.dot(obT_ref[...].T.astype(BF16), w_ref[1], preferred_element_type=F32)
    y = y + jnp.dot(oc_ref[...].astype(BF16), w_ref[2], preferred_element_type=F32)
    y = y + jnp.dot(od_ref[...].astype(BF16), w_ref[3], preferred_element_type=F32)
    x = x_ref[...] + g1_ref[...] * y
    xo_ref[...] = x
    h = x * lax.rsqrt(jnp.mean(x * x, axis=-1, keepdims=True) + EPS) * ng_ref[...]
    h = h * (1.0 + sc_ref[...]) + sh_ref[...]
    h_ref[...] = h

    h_hi, h_lo = _split_bf16(h)
    w_hi, w_lo = _split_bf16(rw_ref[...])
    logits = (jnp.dot(h_hi, w_hi, preferred_element_type=F32) + jnp.dot(h_lo, w_hi, preferred_element_type=F32)
              + jnp.dot(h_hi, w_lo, preferred_element_type=F32)) + rb_ref[...]
    e_ids = lax.broadcasted_iota(jnp.int32, (tm, N_EXPERTS), 1)
    lane = lax.broadcasted_iota(jnp.int32, (tm, LANE), 1)
    idx_out = jnp.zeros((tm, LANE), jnp.int32)
    val_out = jnp.zeros((tm, LANE), F32)
    top0 = None
    denom = jnp.zeros((tm, 1), F32)
    for k in range(TOP_K):
        top = jnp.max(logits, axis=-1, keepdims=True)
        arg = jnp.min(jnp.where(logits == top, e_ids, N_EXPERTS), axis=-1, keepdims=True)
        if k == 0:
            top0 = top
        w = jnp.exp(top - top0)
        denom = denom + w
        idx_out = jnp.where(lane == k, arg, idx_out)
        val_out = jnp.where(lane == k, w, val_out)
        logits = jnp.where(e_ids == arg, -jnp.inf, logits)
    idx_ref[...] = idx_out
    gate_ref[...] = val_out / denom


def _outproj(x, oaT, obT, oc, od, w4, g1, ng, sc, sh, rw, rb):
    s, d = x.shape
    tm = ROW_TILE
    row = lambda i: (i, 0)
    col = lambda i: (0, i)
    const2 = lambda i: (0, 0)
    vec = pl.BlockSpec((1, d), const2)
    return pl.pallas_call(
        _outproj_kernel,
        grid=(s // tm,),
        in_specs=[pl.BlockSpec((tm, d), row),
                  pl.BlockSpec((N_GROUP_W, tm), col), pl.BlockSpec((N_GROUP_W, tm), col),
                  pl.BlockSpec((tm, 256), row), pl.BlockSpec((tm, 256), row),
                  pl.BlockSpec((4, N_GROUP_W, d), lambda i: (0, 0, 0)),
                  vec, vec, vec, vec,
                  pl.BlockSpec((d, N_EXPERTS), const2), pl.BlockSpec((1, N_EXPERTS), const2)],
        out_specs=(pl.BlockSpec((tm, d), row), pl.BlockSpec((tm, d), row),
                   pl.BlockSpec((tm, LANE), row), pl.BlockSpec((tm, LANE), row)),
        out_shape=(jax.ShapeDtypeStruct((s, d), F32), jax.ShapeDtypeStruct((s, d), F32),
                   jax.ShapeDtypeStruct((s, LANE), jnp.int32), jax.ShapeDtypeStruct((s, LANE), F32)),
        compiler_params=_cparams(("arbitrary",)),
    )(x, oaT, obT, oc, od, w4, g1, ng, sc, sh, rw, rb)


def _segment_copies(tile, cnt_ref, seg_ref, off_ref, make_copy, start):
    def per_expert(e, carry):
        j = tile * N_EXPERTS + e
        cnt, a, b = cnt_ref[j], seg_ref[j], off_ref[j]
        size = MOE_TILE
        while size >= SUBLANE:
            bit = cnt & size

            @pl.when(bit != 0)
            def _():
                cp = make_copy(pl.multiple_of(a, SUBLANE), pl.multiple_of(b, SUBLANE), size)
                if start:
                    cp.start()
                else:
                    cp.wait()

            a, b = a + bit, b + bit
            size //= 2
        return carry

    lax.fori_loop(0, N_EXPERTS, per_expert, 0)


def _dispatch_kernel(cnt_ref, seg_ref, off_ref, nused_ref, h_ref, slotT_ref, rows_ref, sorted_ref, zero_ref, sem):
    tile = pl.program_id(0)
    tt = h_ref.shape[0]
    n = sorted_ref.shape[1]
    par = tile & 1
    r_ids = lax.broadcasted_iota(jnp.int32, (n, tt), 0)
    perm = jnp.where(r_ids == slotT_ref[0:1, :], 1.0, 0.0)
    for k in range(1, TOP_K):
        perm = perm + jnp.where(r_ids == slotT_ref[k:k + 1, :], 1.0, 0.0)
    perm = perm.astype(BF16)
    sorted_ref[par] = jnp.dot(perm, h_ref[...].astype(BF16), preferred_element_type=F32)

    def copies_from(p):
        def make_copy(a, b, size):
            return pltpu.make_async_copy(sorted_ref.at[p, pl.ds(a, size)], rows_ref.at[pl.ds(b, size)], sem.at[p])
        return make_copy

    _segment_copies(tile, cnt_ref, seg_ref, off_ref, copies_from(par), True)

    @pl.when(tile > 0)
    def _():
        _segment_copies(tile - 1, cnt_ref, seg_ref, off_ref, copies_from(1 - par), False)

    last = pl.num_programs(0) - 1

    @pl.when(tile == last)
    def _():
        _segment_copies(tile, cnt_ref, seg_ref, off_ref, copies_from(par), False)
        zero_ref[...] = jnp.zeros(zero_ref.shape, F32)

        def make_zero_copy(a, b, size):
            return pltpu.make_async_copy(zero_ref.at[pl.ds(a, size)], rows_ref.at[pl.ds(b, size)], sem.at[0])

        _segment_copies(last + 1, cnt_ref, seg_ref, off_ref, make_zero_copy, True)
        _segment_copies(last + 1, cnt_ref, seg_ref, off_ref, make_zero_copy, False)

        def block_copy(b):
            return pltpu.make_async_copy(zero_ref, rows_ref.at[pl.ds(pl.multiple_of(b * MOE_BLK, MOE_BLK), MOE_BLK)],
                                         sem.at[0])

        def issue(b, carry):
            block_copy(b).start()
            return carry

        def drain(b, carry):
            block_copy(b).wait()
            return carry

        n_blocks = rows_ref.shape[0] // MOE_BLK
        lax.fori_loop(nused_ref[0], n_blocks, issue, 0)
        lax.fori_loop(nused_ref[0], n_blocks, drain, 0)


def _dispatch(h, slotT, tile_cnt, tile_seg, tile_off, n_used, n_rows):
    s, d = h.shape
    tt = MOE_TILE
    grid_spec = pltpu.PrefetchScalarGridSpec(
        num_scalar_prefetch=4,
        grid=(s // tt,),
        in_specs=[pl.BlockSpec((tt, d), lambda i, c, sg, of, nu: (i, 0)),
                  pl.BlockSpec((TOP_K, tt), lambda i, c, sg, of, nu: (0, i))],
        out_specs=pl.BlockSpec(memory_space=pl.ANY),
        scratch_shapes=[pltpu.VMEM((2, MOE_SORT_ROWS, d), F32), pltpu.VMEM((MOE_BLK, d), F32),
                        pltpu.SemaphoreType.DMA((2,))],
    )
    return pl.pallas_call(
        _dispatch_kernel,
        grid_spec=grid_spec,
        out_shape=jax.ShapeDtypeStruct((n_rows, d), F32),
        compiler_params=_cparams(("arbitrary",)),
    )(tile_cnt, tile_seg, tile_off, n_used, h, slotT)


def _ffn_kernel(be_ref, first_ref, next_ref, slot_ref, nused_ref, x_ref, wu_hbm, bu_ref, wd_hbm, bd_ref, y_ref,
                wu_f32, wd_f32, wu_bf, wd_bf, sem):
    b = pl.program_id(0)

    def weight_copies(expert, slot):
        return (pltpu.make_async_copy(wu_hbm.at[expert], wu_f32.at[slot], sem.at[0, slot]),
                pltpu.make_async_copy(wd_hbm.at[expert], wd_f32.at[slot], sem.at[1, slot]))

    @pl.when(b == 0)
    def _():
        for cp in weight_copies(be_ref[0], 0):
            cp.start()

    @pl.when(first_ref[b] == 1)
    def _():
        slot = slot_ref[b]
        for cp in weight_copies(be_ref[b], slot):
            cp.wait()

        @pl.when(next_ref[b] >= 0)
        def _():
            for cp in weight_copies(next_ref[b], 1 - slot):
                cp.start()

        wu_bf[...] = wu_f32[slot].astype(BF16)
        wd_bf[...] = wd_f32[slot].astype(BF16)

    @pl.when(b < nused_ref[0])
    def _():
        hu = jnp.dot(x_ref[...].astype(BF16), wu_bf[...], preferred_element_type=F32) + bu_ref[0]
        g = jnp.minimum(hu[:, :D_FF], SWIGLU_LIMIT)
        lin = jnp.clip(hu[:, D_FF:], -SWIGLU_LIMIT, SWIGLU_LIMIT)
        act = (lin + 1.0) * (g * _sigmoid(SWIGLU_ALPHA * g))
        y_ref[...] = jnp.dot(act.astype(BF16), wd_bf[...], preferred_element_type=F32) + bd_ref[0]

    @pl.when(b >= nused_ref[0])
    def _():
        y_ref[...] = jnp.zeros(y_ref.shape, F32)


def _ffn(x_rows, blk_expert, first, n_used, w_up, b_up, w_down, b_down):
    n_rows, d = x_rows.shape
    blk = MOE_BLK
    n_blocks = n_rows // blk
    n_exp, _, f2 = w_up.shape
    i32 = jnp.int32
    ids = jnp.arange(n_blocks, dtype=i32)
    later_start = (ids[None, :] > ids[:, None]) & (first[None, :] == 1)
    next_pos = jnp.min(jnp.where(later_start, ids[None, :], n_blocks), axis=1)
    next_expert = jnp.where(next_pos < n_blocks, blk_expert[jnp.minimum(next_pos, n_blocks - 1)], -1).astype(i32)
    slot = ((jnp.cumsum(first) - 1) & 1).astype(i32)
    idx = lambda b, be, fi, nx, sl, nu: (be[b], 0, 0)
    grid_spec = pltpu.PrefetchScalarGridSpec(
        num_scalar_prefetch=5,
        grid=(n_blocks,),
        in_specs=[pl.BlockSpec((blk, d), lambda b, be, fi, nx, sl, nu: (jnp.minimum(b, nu[0] - 1), 0)),
                  pl.BlockSpec(memory_space=pl.ANY),
                  pl.BlockSpec((1, 1, f2), idx),
                  pl.BlockSpec(memory_space=pl.ANY),
                  pl.BlockSpec((1, 1, d), idx)],
        out_specs=pl.BlockSpec((blk, d), lambda b, be, fi, nx, sl, nu: (b, 0)),
        scratch_shapes=[pltpu.VMEM((2, d, f2), F32), pltpu.VMEM((2, D_FF, d), F32),
                        pltpu.VMEM((d, f2), BF16), pltpu.VMEM((D_FF, d), BF16),
                        pltpu.SemaphoreType.DMA((2, 2))],
    )
    return pl.pallas_call(
        _ffn_kernel,
        grid_spec=grid_spec,
        out_shape=jax.ShapeDtypeStruct((n_rows, d), F32),
        compiler_params=_cparams(("arbitrary",)),
    )(blk_expert, first, next_expert, slot, n_used, x_rows, w_up, b_up.reshape(n_exp, 1, f2),
      w_down, b_down.reshape(n_exp, 1, d))


def _combine_kernel(cnt_ref, seg_ref, off_ref, x_ref, slot_ref, gate_ref, g2_ref, rows_ref, o_ref, buf_ref, sem):
    tile = pl.program_id(0)
    tt = x_ref.shape[0]
    n = buf_ref.shape[1]
    par = tile & 1

    def copies_into(p):
        def make_copy(a, b, size):
            return pltpu.make_async_copy(rows_ref.at[pl.ds(b, size)], buf_ref.at[p, pl.ds(a, size)], sem.at[p])
        return make_copy

    @pl.when(tile == 0)
    def _():
        buf_ref[...] = jnp.zeros(buf_ref.shape, F32)
        _segment_copies(tile, cnt_ref, seg_ref, off_ref, copies_into(par), True)

    @pl.when(tile + 1 < pl.num_programs(0))
    def _():
        _segment_copies(tile + 1, cnt_ref, seg_ref, off_ref, copies_into(1 - par), True)

    slots, gates = slot_ref[...], gate_ref[...]
    r_ids = lax.broadcasted_iota(jnp.int32, (tt, n), 1)
    sel = jnp.where(r_ids == slots[:, 0:1], gates[:, 0:1], 0.0)
    for k in range(1, TOP_K):
        sel = sel + jnp.where(r_ids == slots[:, k:k + 1], gates[:, k:k + 1], 0.0)
    sel_hi, sel_lo = _split_bf16(sel)
    _segment_copies(tile, cnt_ref, seg_ref, off_ref, copies_into(par), False)
    y_hi, y_lo = _split_bf16(buf_ref[par])
    both = jnp.dot(jnp.concatenate([sel_hi, sel_lo], axis=0), y_hi, preferred_element_type=F32)
    y = both[:tt] + both[tt:] + jnp.dot(sel_hi, y_lo, preferred_element_type=F32)
    o_ref[...] = x_ref[...] + g2_ref[...] * y


def _combine(x, slots, gates, g2, y_rows, tile_cnt, tile_seg, tile_off):
    s, d = x.shape
    tt = MOE_TILE
    row = lambda i, c, sg, of: (i, 0)
    grid_spec = pltpu.PrefetchScalarGridSpec(
        num_scalar_prefetch=3,
        grid=(s // tt,),
        in_specs=[pl.BlockSpec((tt, d), row), pl.BlockSpec((tt, LANE), row), pl.BlockSpec((tt, LANE), row),
                  pl.BlockSpec((1, d), lambda i, c, sg, of: (0, 0)),
                  pl.BlockSpec(memory_space=pl.ANY)],
        out_specs=pl.BlockSpec((tt, d), row),
        scratch_shapes=[pltpu.VMEM((2, MOE_SORT_ROWS, d), F32), pltpu.SemaphoreType.DMA((2,))],
    )
    return pl.pallas_call(
        _combine_kernel,
        grid_spec=grid_spec,
        out_shape=jax.ShapeDtypeStruct((s, d), F32),
        compiler_params=_cparams(("arbitrary",)),
    )(tile_cnt, tile_seg, tile_off, x, slots, gates, g2, y_rows)


def _moe(x, h, idx_pad, gates_pad, g2, w_up, b_up, w_down, b_down, layer_idx):
    s, d = x.shape
    n = s * TOP_K
    blk = MOE_BLK
    tt = MOE_TILE
    nt = s // tt
    n_blocks = -(-(n + nt * N_EXPERTS * SUBLANE) // blk) + N_EXPERTS
    i32 = jnp.int32
    idx = idx_pad[:, :TOP_K]
    hot = (idx[:, :, None] == jnp.arange(N_EXPERTS, dtype=i32)).astype(F32)
    onehot_t = jnp.sum(hot, axis=1).reshape(nt, tt, N_EXPERTS)
    earlier = jnp.tril(jnp.ones((tt, tt), F32), -1)
    rank_in_tile = jnp.einsum('ij,tjk->tik', earlier, onehot_t, precision=HI).reshape(s, N_EXPERTS)
    tile_cnt = (jnp.sum(onehot_t, axis=1).astype(i32) + SUBLANE - 1) // SUBLANE * SUBLANE
    tile_base = jnp.cumsum(tile_cnt, axis=0) - tile_cnt
    tile_seg = jnp.cumsum(tile_cnt, axis=1) - tile_cnt
    counts = jnp.sum(tile_cnt, axis=0)
    padded = (counts + blk - 1) // blk * blk
    pad_ends = jnp.cumsum(padded)
    pad_starts = pad_ends - padded
    tile_off = pad_starts[None, :] + tile_base
    slot_by_expert = jnp.repeat(tile_seg, tt, axis=0).astype(F32) + rank_in_tile
    slot = jnp.sum(hot * slot_by_expert[:, None, :], axis=2).astype(i32)
    slot_pad = jnp.pad(slot, ((0, 0), (0, LANE - TOP_K)))
    blk_start = jnp.arange(n_blocks, dtype=i32) * blk
    blk_expert = jnp.minimum(jnp.sum((pad_ends[None, :] <= blk_start[:, None]).astype(i32), axis=1),
                             N_EXPERTS - 1).astype(i32)
    first = jnp.concatenate([jnp.ones((1,), i32), (blk_expert[1:] != blk_expert[:-1]).astype(i32)])
    n_used = (pad_ends[-1:] // blk).astype(i32)
    tile_cnt = jnp.concatenate([tile_cnt, (padded - counts)[None, :]], axis=0)
    tile_seg = jnp.concatenate([tile_seg, jnp.zeros((1, N_EXPERTS), tile_seg.dtype)], axis=0)
    tile_off = jnp.concatenate([tile_off, (pad_starts + counts)[None, :]], axis=0)
    tile_cnt, tile_seg, tile_off = [t.astype(i32).reshape(-1) for t in (tile_cnt, tile_seg, tile_off)]

    x_rows = _dispatch(h, slot.T, tile_cnt, tile_seg, tile_off, n_used, n_blocks * blk)
    y_rows = _ffn(x_rows, blk_expert + layer_idx * N_EXPERTS, first, n_used, w_up, b_up, w_down, b_down)
    return _combine(x, slot_pad, gates_pad, g2, y_rows, tile_cnt, tile_seg, tile_off)


def _block_diag(w):
    n, d, _ = w.shape
    eye = jnp.eye(n, dtype=w.dtype)
    return (eye[:, None, :, None] * w[:, :, None, :]).reshape(n * d, n * d)


def kernel(x, c, ada_w, ada_b, norm1_g, norm2_g, w_in, w_out, a_q_gain, a_k_gain, a_lam_q1, a_lam_k1, a_lam_q2, a_lam_k2, a_out_gain, b_q_gain, b_k_gain, c_w_g2, c_b_g, c_out_gain, d_conv_w, d_conv_b, d_w_a, d_b_a, d_w_x, d_b_x, d_lambda, router_w, router_b, exp_w_up, exp_b_up, exp_w_down, exp_b_down):
    bsz, s, d = x.shape
    assert bsz == 1 and d == D_MODEL and s % ROW_TILE == 0 and s % (max(A_KEY_BLOCKS, B_KEY_BLOCKS) * ATT_TILE) == 0
    depth = ada_w.shape[0]
    mod = _modulation(c, ada_w, ada_b)
    xs = x.reshape(s, d)
    w_up_all = exp_w_up.reshape(depth * N_EXPERTS, d, 2 * D_FF)
    b_up_all = exp_b_up.reshape(depth * N_EXPERTS, 2 * D_FF)
    w_down_all = exp_w_down.reshape(depth * N_EXPERTS, D_FF, d)
    b_down_all = exp_b_down.reshape(depth * N_EXPERTS, d)
    for l in range(depth):
        sh1, sc1, g1, sh2, sc2, g2 = [mod[l, i * d:(i + 1) * d].reshape(1, d) for i in range(6)]
        w = w_in[l]
        w_cat = jnp.concatenate([w[:, :2048], w[:, 2064:2832], w[:, 2048:2064],
                                 jnp.zeros((d, W_IN_PAD - 2832), w.dtype)], axis=1).astype(BF16)
        wg2p = jnp.zeros((LANE, LANE), F32).at[:GLA_RANK, :].set(c_w_g2[l])
        (aqT, ak, avT, bqT, bk, bvT, bkm, cq, ck, cv, cr, cg, dx, dgate) = _inproj(
            xs, norm1_g[l].reshape(1, d), sc1, sh1, w_cat,
            jnp.tile(a_q_gain[l], 2 * A_HEADS).reshape(1, -1), jnp.tile(a_k_gain[l], 2 * A_HEADS).reshape(1, -1),
            jnp.tile(b_q_gain[l], B_HEADS).reshape(1, -1), jnp.tile(b_k_gain[l], B_HEADS).reshape(1, -1),
            wg2p, c_b_g[l].reshape(1, -1))
        o_aT = _diff_attention(aqT, ak, avT, a_lam_q1[l].reshape(1, -1), a_lam_k1[l].reshape(1, -1),
                               a_lam_q2[l].reshape(1, -1), a_lam_k2[l].reshape(1, -1),
                               a_out_gain[l].reshape(-1, 1), l)
        km = bkm.reshape(s // MOBA_BLOCK, B_HEADS, B_DIM).transpose(1, 0, 2)
        o_bT = _moba(bqT, bk, bvT, km)
        o_c = _gla(cq, ck, cg, cv, cr, jnp.tile(c_out_gain[l], C_HEADS).reshape(1, -1))
        o_d = _rglru(dx, dgate, d_conv_w[l], d_conv_b[l].reshape(1, -1),
                     _block_diag(d_w_a[l]).astype(BF16), d_b_a[l].reshape(1, -1),
                     _block_diag(d_w_x[l]).astype(BF16), d_b_x[l].reshape(1, -1), d_lambda[l].reshape(1, -1))
        w4 = w_out[l].astype(BF16).reshape(4, N_GROUP_W, d)
        xs, h2, idx_pad, gates_pad = _outproj(xs, o_aT, o_bT, o_c, o_d, w4, g1, norm2_g[l].reshape(1, d),
                                              sc2, sh2, router_w[l], router_b[l].reshape(1, -1))
        xs = _moe(xs, h2, idx_pad, gates_pad, g2, w_up_all, b_up_all, w_down_all, b_down_all, l)
    return xs.reshape(bsz, s, d)
```

```python
import functools
import math

import jax
import jax.numpy as jnp
from jax import lax
from jax.experimental import pallas as pl
from jax.experimental.pallas import tpu as pltpu

F32 = jnp.float32
BF16 = jnp.bfloat16
HI = lax.Precision.HIGHEST

EPS = 1e-6
NEG = -1e30
M_INIT = -1e30
MASKED = -2e30
LOG2E = 1.4426950408889634

D_MODEL = 1024
N_GROUP_W = 256
A_HEADS, A_QK, A_V = 4, 32, 64
B_HEADS, B_DIM = 4, 64
MOBA_BLOCK, MOBA_TOPK = 256, 3
C_HEADS, C_K, C_V = 4, 32, 64
GLA_RANK, GLA_TAU = 16, 16.0
GLA_SUB = 16
D_WIDTH, D_BLOCKS, CONV_W, LRU_C = 256, 4, 4, 8.0
N_EXPERTS, TOP_K, D_FF = 32, 4, 1024
SWIGLU_ALPHA, SWIGLU_LIMIT = 1.702, 7.0

LANE = 128
SUBLANE = 8
W_IN_PAD = 2944

ROW_TILE = 512
ATT_TILE = 256
A_KEY_BLOCKS = 2
B_KEY_BLOCKS = 4
SUM_ROWS = 16
MOE_BLK = 256
MOE_TILE = 256
MOE_SORT_ROWS = MOE_TILE * TOP_K + N_EXPERTS * SUBLANE
VMEM_LIMIT = 56 * 1024 * 1024


def _cparams(sem):
    return pltpu.CompilerParams(dimension_semantics=sem, vmem_limit_bytes=VMEM_LIMIT)


def _log_sigmoid(z):
    return jnp.minimum(z, 0.0) - jnp.log1p(jnp.exp(-jnp.abs(z)))


def _sigmoid(z):
    return 1.0 / (1.0 + jnp.exp(-z))


def _split_bf16(t):
    hi = t.astype(BF16)
    return hi, (t - hi.astype(F32)).astype(BF16)


def _dot_exact_rhs(t, w_bf16, parts=2):
    out, rest = None, t
    for _ in range(parts):
        piece = rest.astype(BF16)
        term = jnp.dot(piece, w_bf16, preferred_element_type=F32)
        out = term if out is None else out + term
        rest = rest - piece.astype(F32)
    return out


def _dot_exact_lhs(w_bf16, t, parts=3):
    out, rest = None, t
    for _ in range(parts):
        piece = rest.astype(BF16)
        term = jnp.dot(w_bf16, piece, preferred_element_type=F32)
        out = term if out is None else out + term
        rest = rest - piece.astype(F32)
    return out


def _group_rms(t, group):
    n = t.shape[-1]
    shift = int(math.log2(group))
    r = lax.broadcasted_iota(jnp.int32, (n, n), 0) >> shift
    c = lax.broadcasted_iota(jnp.int32, (n, n), 1) >> shift
    bd = jnp.where(r == c, 1.0, 0.0).astype(BF16)
    ss = _dot_exact_rhs(t * t, bd)
    return t * lax.rsqrt(ss * (1.0 / group) + EPS)


def _mod_kernel(c_ref, w_ref, b_ref, o_ref):
    c = c_ref[...]
    cond = c * _sigmoid(c)
    cond8 = jnp.broadcast_to(cond, (8, cond.shape[-1]))
    o_ref[0] = jnp.dot(cond8, w_ref[0], precision=HI, preferred_element_type=F32) + b_ref[0]


def _modulation(c, ada_w, ada_b):
    depth, d, n = ada_w.shape
    tn = 1536
    out = pl.pallas_call(
        _mod_kernel,
        grid=(depth, n // tn),
        in_specs=[pl.BlockSpec((1, d), lambda l, j: (0, 0)),
                  pl.BlockSpec((1, d, tn), lambda l, j: (l, 0, j)),
                  pl.BlockSpec((1, 1, tn), lambda l, j: (l, 0, j))],
        out_specs=pl.BlockSpec((1, 8, tn), lambda l, j: (l, 0, j)),
        out_shape=jax.ShapeDtypeStruct((depth, 8, n), F32),
        compiler_params=_cparams(("arbitrary", "arbitrary")),
    )(c, ada_w, ada_b.reshape(depth, 1, n))
    return out[:, 0, :]


def _inproj_kernel(x_ref, ng_ref, sc_ref, sh_ref, w_ref, aqg_ref, akg_ref, bqg_ref, bkg_ref, wg2_ref, bg_ref,
                   aqT_ref, ak_ref, avT_ref, bqT_ref, bk_ref, bvT_ref, bkm_ref,
                   cq_ref, ck_ref, cv_ref, cr_ref, cg_ref, dx_ref, dg_ref):
    x = x_ref[...]
    tm = x.shape[0]
    h = x * lax.rsqrt(jnp.mean(x * x, axis=-1, keepdims=True) + EPS) * ng_ref[...]
    h = h * (1.0 + sc_ref[...]) + sh_ref[...]
    proj = jnp.dot(h.astype(BF16), w_ref[...], preferred_element_type=F32)

    aq = _group_rms(proj[:, 0:256], A_QK) * aqg_ref[...] * (A_QK ** -0.5 * LOG2E)
    aqT_ref[...] = aq.T.astype(BF16)
    ak = _group_rms(proj[:, 256:512], A_QK) * akg_ref[...]
    ak_ref[...] = ak.astype(BF16)
    avT_ref[...] = proj[:, 512:768].T.astype(BF16)

    bq = _group_rms(proj[:, 768:1024], B_DIM) * bqg_ref[...] * (B_DIM ** -0.5 * LOG2E)
    bqT_ref[...] = bq.T.astype(BF16)
    bk = _group_rms(proj[:, 1024:1280], B_DIM) * bkg_ref[...]
    bk_ref[...] = bk.astype(BF16)
    bvT_ref[...] = proj[:, 1280:1536].T.astype(BF16)
    bkm_ref[0] = jnp.mean(bk.reshape(tm // MOBA_BLOCK, MOBA_BLOCK, N_GROUP_W), axis=1)

    cq_ref[...] = proj[:, 1536:1664] * (C_K ** -0.5)
    ck_ref[...] = proj[:, 1664:1792]
    cv_ref[...] = proj[:, 1792:2048]
    cr_ref[...] = proj[:, 2048:2304]
    gl_hi, gl_lo = _split_bf16(proj[:, 2816:2944])
    w2_hi, w2_lo = _split_bf16(wg2_ref[...])
    z = (jnp.dot(gl_hi, w2_hi, preferred_element_type=F32) + jnp.dot(gl_lo, w2_hi, preferred_element_type=F32)
         + jnp.dot(gl_hi, w2_lo, preferred_element_type=F32)) + bg_ref[...]
    cg_ref[...] = _log_sigmoid(z) * (1.0 / GLA_TAU)

    dx_ref[...] = proj[:, 2304:2560]
    dg_ref[...] = proj[:, 2560:2816]


def _inproj(x, ng, sc, sh, w_cat, aqg, akg, bqg, bkg, wg2p, bg):
    s, d = x.shape
    tm = ROW_TILE
    nt = s // tm
    row = lambda i: (i, 0)
    const2 = lambda i: (0, 0)
    vec = pl.BlockSpec((1, d), const2)
    gvec = pl.BlockSpec((1, N_GROUP_W), const2)
    out_shape = (
        jax.ShapeDtypeStruct((N_GROUP_W, s), BF16),
        jax.ShapeDtypeStruct((s, N_GROUP_W), BF16),
        jax.ShapeDtypeStruct((N_GROUP_W, s), BF16),
        jax.ShapeDtypeStruct((N_GROUP_W, s), BF16),
        jax.ShapeDtypeStruct((s, N_GROUP_W), BF16),
        jax.ShapeDtypeStruct((N_GROUP_W, s), BF16),
        jax.ShapeDtypeStruct((nt, tm // MOBA_BLOCK, N_GROUP_W), F32),
        jax.ShapeDtypeStruct((s, 128), F32),
        jax.ShapeDtypeStruct((s, 128), F32),
        jax.ShapeDtypeStruct((s, 256), F32),
        jax.ShapeDtypeStruct((s, 256), F32),
        jax.ShapeDtypeStruct((s, 128), F32),
        jax.ShapeDtypeStruct((s, 256), F32),
        jax.ShapeDtypeStruct((s, 256), F32),
    )
    out_specs = (
        pl.BlockSpec((N_GROUP_W, tm), lambda i: (0, i)),
        pl.BlockSpec((tm, N_GROUP_W), row),
        pl.BlockSpec((N_GROUP_W, tm), lambda i: (0, i)),
        pl.BlockSpec((N_GROUP_W, tm), lambda i: (0, i)),
        pl.BlockSpec((tm, N_GROUP_W), row),
        pl.BlockSpec((N_GROUP_W, tm), lambda i: (0, i)),
        pl.BlockSpec((1, tm // MOBA_BLOCK, N_GROUP_W), lambda i: (i, 0, 0)),
        pl.BlockSpec((tm, 128), row),
        pl.BlockSpec((tm, 128), row),
        pl.BlockSpec((tm, 256), row),
        pl.BlockSpec((tm, 256), row),
        pl.BlockSpec((tm, 128), row),
        pl.BlockSpec((tm, 256), row),
        pl.BlockSpec((tm, 256), row),
    )
    return pl.pallas_call(
        _inproj_kernel,
        grid=(nt,),
        in_specs=[pl.BlockSpec((tm, d), row), vec, vec, vec,
                  pl.BlockSpec((d, W_IN_PAD), const2),
                  gvec, gvec, gvec, gvec,
                  pl.BlockSpec((LANE, LANE), const2),
                  pl.BlockSpec((1, LANE), const2)],
        out_specs=out_specs,
        out_shape=out_shape,
        compiler_params=_cparams(("arbitrary",)),
    )(x, ng, sc, sh, w_cat, aqg, akg, bqg, bkg, wg2p, bg)


def _pad_query_rows(qT_ref, qpad_ref, width):
    per = LANE // width
    rows = lax.broadcasted_iota(jnp.int32, qpad_ref.shape[1:], 0)
    for g in range(qpad_ref.shape[0]):
        blk = qT_ref[LANE * (g // per):LANE * (g // per + 1), :]
        lo = width * (g % per)
        qpad_ref[g] = jnp.where((rows >= lo) & (rows < lo + width), blk, jnp.zeros_like(blk))


def _with_sum_rows(vT):
    return jnp.concatenate([vT, jnp.ones((SUM_ROWS, vT.shape[1]), vT.dtype)], axis=0)


def _softmax_tiles(k_slabs, qpad_ref, v_tiles, adjust, m_ref, acc_ref):
    n = len(k_slabs)
    scores = [jnp.dot(k_slabs[g], qpad_ref[g], preferred_element_type=F32) for g in range(n)]
    probs, alphas = [], []
    for g in range(n):
        sT = adjust(g, scores[g])
        m_old = m_ref[g]
        m_new = jnp.maximum(m_old, jnp.max(sT, axis=0, keepdims=True))
        alphas.append(jnp.exp2(m_old - m_new))
        probs.append(jnp.exp2(sT - m_new).astype(BF16))
        m_ref[g] = m_new
    pv = [jnp.dot(v_tiles[g], probs[g], preferred_element_type=F32) for g in range(n)]
    for g in range(n):
        acc_ref[g] = alphas[g] * acc_ref[g] + pv[g]


def _diffattn_kernel(qT_ref, k_ref, vT_ref, lq1_ref, lk1_ref, lq2_ref, lk2_ref, og_ref, o_ref,
                     m_ref, acc_ref, qpad_ref, *, lam_init):
    KEY_BLOCKS = A_KEY_BLOCKS
    t, tk = ATT_TILE, KEY_BLOCKS * ATT_TILE
    n_maps = 2 * A_HEADS
    iq = pl.program_id(0)
    lam = (jnp.exp(jnp.sum(lq1_ref[...] * lk1_ref[...], axis=-1, keepdims=True))
           - jnp.exp(jnp.sum(lq2_ref[...] * lk2_ref[...], axis=-1, keepdims=True)) + lam_init)
    m_ref[...] = jnp.full(m_ref.shape, M_INIT, F32)
    acc_ref[...] = jnp.zeros(acc_ref.shape, F32)
    _pad_query_rows(qT_ref, qpad_ref, A_QK)

    def tile(ik, adjust):
        keys = pl.ds(pl.multiple_of(ik * tk, tk), tk)
        k_slabs = [k_ref[keys, LANE * (g // 4):LANE * (g // 4 + 1)] for g in range(n_maps)]
        v_heads = [_with_sum_rows(vT_ref[A_V * hd:A_V * (hd + 1), keys]) for hd in range(A_HEADS)]
        _softmax_tiles(k_slabs, qpad_ref, [v_heads[g // 2] for g in range(n_maps)], adjust, m_ref, acc_ref)

    def body(ik, carry):
        tile(ik, lambda g, sT: sT)
        return carry

    n_full = iq // KEY_BLOCKS
    lax.fori_loop(0, n_full, body, 0)
    causal = (lax.broadcasted_iota(jnp.int32, (tk, t), 0) - lax.broadcasted_iota(jnp.int32, (tk, t), 1)
              <= (iq - KEY_BLOCKS * n_full) * t)
    tile(n_full, lambda g, sT: jnp.where(causal, sT, MASKED))

    for hd in range(A_HEADS):
        a1, a2 = acc_ref[2 * hd], acc_ref[2 * hd + 1]
        o = a1[:A_V] / a1[A_V:A_V + 1] - lam * (a2[:A_V] / a2[A_V:A_V + 1])
        o = o * lax.rsqrt(jnp.mean(o * o, axis=0, keepdims=True) + EPS) * og_ref[...]
        o_ref[A_V * hd:A_V * (hd + 1), :] = o * (1.0 - lam_init)


def _diff_attention(aqT, ak, avT, lq1, lk1, lq2, lk2, out_gain_col, layer_idx):
    s = aqT.shape[1]
    t = ATT_TILE
    n_maps = 2 * A_HEADS
    lam_init = 0.8 - 0.6 * math.exp(-0.3 * layer_idx)
    vec = pl.BlockSpec((1, A_QK), lambda i: (0, 0))
    return pl.pallas_call(
        functools.partial(_diffattn_kernel, lam_init=lam_init),
        grid=(s // t,),
        in_specs=[pl.BlockSpec((N_GROUP_W, t), lambda i: (0, i)),
                  pl.BlockSpec((s, N_GROUP_W), lambda i: (0, 0)),
                  pl.BlockSpec((N_GROUP_W, s), lambda i: (0, 0)),
                  vec, vec, vec, vec,
                  pl.BlockSpec((A_V, 1), lambda i: (0, 0))],
        out_specs=pl.BlockSpec((N_GROUP_W, t), lambda i: (0, i)),
        out_shape=jax.ShapeDtypeStruct((N_GROUP_W, s), F32),
        scratch_shapes=[pltpu.VMEM((n_maps, 1, t), F32), pltpu.VMEM((n_maps, A_V + SUM_ROWS, t), F32),
                        pltpu.VMEM((n_maps, LANE, t), BF16)],
        compiler_params=_cparams(("arbitrary",)),
    )(aqT, ak, avT, lq1, lk1, lq2, lk2, out_gain_col)


def _moba_kernel(qT_ref, k_ref, vT_ref, km_ref, o_ref, m_ref, acc_ref, bias_ref, qpad_ref):
    KEY_BLOCKS = B_KEY_BLOCKS
    t, tk = ATT_TILE, KEY_BLOCKS * ATT_TILE
    own = pl.program_id(0)
    nb = km_ref.shape[1]
    m_ref[...] = jnp.full(m_ref.shape, M_INIT, F32)
    acc_ref[...] = jnp.zeros(acc_ref.shape, F32)
    _pad_query_rows(qT_ref, qpad_ref, B_DIM)

    blk = lax.broadcasted_iota(jnp.int32, (nb, t), 0)
    earlier = blk < own
    for hd in range(B_HEADS):
        qT = qT_ref[B_DIM * hd:B_DIM * (hd + 1), :].astype(F32)
        gate = jnp.dot(km_ref[hd], qT, precision=HI, preferred_element_type=F32)
        gate = jnp.where(earlier, gate, NEG)
        sel_bias = jnp.full((nb, t), MASKED, F32)
        for _ in range(MOBA_TOPK):
            top = jnp.max(gate, axis=0, keepdims=True)
            pick = blk == jnp.min(jnp.where(gate == top, blk, nb), axis=0, keepdims=True)
            sel_bias = jnp.where(pick, jnp.where(earlier, 0.0, MASKED), sel_bias)
            gate = jnp.where(pick, -jnp.inf, gate)
        bias_ref[hd] = sel_bias

    def tile(j, adjust):
        keys = pl.ds(pl.multiple_of(j * tk, tk), tk)
        k_slabs = [k_ref[keys, LANE * (hd // 2):LANE * (hd // 2 + 1)] for hd in range(B_HEADS)]
        v_tiles = [_with_sum_rows(vT_ref[B_DIM * hd:B_DIM * (hd + 1), keys]) for hd in range(B_HEADS)]
        _softmax_tiles(k_slabs, qpad_ref, v_tiles, adjust, m_ref, acc_ref)

    def block_bias(hd, b):
        return bias_ref[hd, pl.ds(b, 1), :]

    def body(j, carry):
        tile(j, lambda hd, sT: jnp.concatenate(
            [sT[t * part:t * (part + 1)] + block_bias(hd, KEY_BLOCKS * j + part) for part in range(KEY_BLOCKS)],
            axis=0))
        return carry

    n_full = own // KEY_BLOCKS
    lax.fori_loop(0, n_full, body, 0)

    causal = (lax.broadcasted_iota(jnp.int32, (t, t), 0) <= lax.broadcasted_iota(jnp.int32, (t, t), 1))

    def last(hd, sT):
        parts = []
        for part in range(KEY_BLOCKS):
            b = KEY_BLOCKS * n_full + part
            sp = sT[t * part:t * (part + 1)]
            other = sp + jnp.where(b < own, block_bias(hd, b), MASKED)
            parts.append(jnp.where(b == own, jnp.where(causal, sp, MASKED), other))
        return jnp.concatenate(parts, axis=0)

    tile(n_full, last)
    for hd in range(B_HEADS):
        a = acc_ref[hd]
        o_ref[B_DIM * hd:B_DIM * (hd + 1), :] = a[:B_DIM] / a[B_DIM:B_DIM + 1]


def _moba(bqT, bk, bvT, km):
    s = bqT.shape[1]
    t = ATT_TILE
    nb = km.shape[1]
    return pl.pallas_call(
        _moba_kernel,
        grid=(s // t,),
        in_specs=[pl.BlockSpec((N_GROUP_W, t), lambda i: (0, i)),
                  pl.BlockSpec((s, N_GROUP_W), lambda i: (0, 0)),
                  pl.BlockSpec((N_GROUP_W, s), lambda i: (0, 0)),
                  pl.BlockSpec((B_HEADS, nb, B_DIM), lambda i: (0, 0, 0))],
        out_specs=pl.BlockSpec((N_GROUP_W, t), lambda i: (0, i)),
        out_shape=jax.ShapeDtypeStruct((N_GROUP_W, s), F32),
        scratch_shapes=[pltpu.VMEM((B_HEADS, 1, t), F32),
                        pltpu.VMEM((B_HEADS, B_DIM + SUM_ROWS, t), F32), pltpu.VMEM((B_HEADS, nb, t), F32),
                        pltpu.VMEM((B_HEADS, LANE, t), BF16)],
        compiler_params=_cparams(("arbitrary",)),
    )(bqT, bk, bvT, km)


def _gla_kernel(q_ref, k_ref, g_ref, v_ref, r_ref, og_ref, o_ref, stateT_ref, kbuf_ref, bbuf_ref, vbuf_ref):
    tm = q_ref.shape[0]
    c = GLA_SUB
    shift = int(math.log2(c))

    @pl.when(pl.program_id(0) == 0)
    def _():
        stateT_ref[...] = jnp.zeros(stateT_ref.shape, F32)

    q, k, v = q_ref[...], k_ref[...], v_ref[...]
    ri = lax.broadcasted_iota(jnp.int32, (tm, tm), 0)
    ci = lax.broadcasted_iota(jnp.int32, (tm, tm), 1)
    same = (ri >> shift) == (ci >> shift)
    g = g_ref[...]
    b = _dot_exact_lhs(jnp.where(same & (ri >= ci), 1.0, 0.0).astype(BF16), g)
    b_last = _dot_exact_lhs(jnp.where(same, 1.0, 0.0).astype(BF16), g)

    k_shift, v_shift = int(math.log2(C_K)), int(math.log2(C_V))
    head_sum = jnp.where((lax.broadcasted_iota(jnp.int32, (C_HEADS * C_K, C_HEADS * C_V), 0) >> k_shift)
                         == (lax.broadcasted_iota(jnp.int32, (C_HEADS * C_K, C_HEADS * C_V), 1) >> v_shift),
                         1.0, 0.0).astype(BF16)

    for buf, val in ((kbuf_ref, k), (bbuf_ref, b), (vbuf_ref, v)):
        buf[0:c, :] = jnp.zeros((c, val.shape[1]), F32)
        buf[c:c + tm, :] = val
    pos = lax.broadcasted_iota(jnp.int32, (tm, C_HEADS * C_K), 0) & (c - 1)
    o = jnp.zeros((tm, C_HEADS * C_V), F32)
    for delta in range(c):
        k_d = kbuf_ref[c - delta:c - delta + tm, :]
        b_d = bbuf_ref[c - delta:c - delta + tm, :]
        v_d = vbuf_ref[c - delta:c - delta + tm, :]
        w = q * k_d * jnp.exp(jnp.where(pos >= delta, b - b_d, NEG))
        w_hi, w_lo = _split_bf16(w)
        att = (jnp.dot(w_hi, head_sum, preferred_element_type=F32)
               + jnp.dot(w_lo, head_sum, preferred_element_type=F32))
        o = o + att * v_d

    q_dec = (q * jnp.exp(b)).astype(BF16)
    k_dec = (k * jnp.exp(b_last - b)).astype(BF16)
    decay = jnp.exp(b_last)
    v_bf = v.astype(BF16)
    own_head = ((lax.broadcasted_iota(jnp.int32, (C_HEADS * C_V, C_HEADS * C_K), 0) >> v_shift)
                == (lax.broadcasted_iota(jnp.int32, (C_HEADS * C_V, C_HEADS * C_K), 1) >> k_shift))
    nt_dims = (((1,), (1,)), ((), ()))
    tn_dims = (((0,), (0,)), ((), ()))
    updates = [lax.dot_general(v_bf[c * i:c * (i + 1)], k_dec[c * i:c * (i + 1)], tn_dims,
                               preferred_element_type=F32) for i in range(tm // c)]
    stateT = stateT_ref[...]
    far = []
    for i in range(tm // c):
        far.append(lax.dot_general(q_dec[c * i:c * (i + 1)], stateT.astype(BF16), nt_dims,
                                   preferred_element_type=F32))
        stateT = stateT * decay[c * i:c * i + 1, :] + jnp.where(own_head, updates[i], 0.0)
    stateT_ref[...] = stateT
    o = o + jnp.concatenate(far, axis=0)

    r = r_ref[...]
    o_ref[...] = _group_rms(o, C_V) * og_ref[...] * (r * _sigmoid(r))


def _gla(cq, ck, cg, cv, cr, out_gain):
    s = cq.shape[0]
    tm = ROW_TILE
    row = lambda i: (i, 0)
    return pl.pallas_call(
        _gla_kernel,
        grid=(s // tm,),
        in_specs=[pl.BlockSpec((tm, 128), row), pl.BlockSpec((tm, 128), row), pl.BlockSpec((tm, 128), row),
                  pl.BlockSpec((tm, 256), row), pl.BlockSpec((tm, 256), row),
                  pl.BlockSpec((1, C_HEADS * C_V), lambda i: (0, 0))],
        out_specs=pl.BlockSpec((tm, 256), row),
        out_shape=jax.ShapeDtypeStruct((s, 256), F32),
        scratch_shapes=[pltpu.VMEM((C_HEADS * C_V, C_HEADS * C_K), F32),
                        pltpu.VMEM((tm + GLA_SUB, C_HEADS * C_K), F32),
                        pltpu.VMEM((tm + GLA_SUB, C_HEADS * C_K), F32),
                        pltpu.VMEM((tm + GLA_SUB, C_HEADS * C_V), F32)],
        compiler_params=_cparams(("arbitrary",)),
    )(cq, ck, cg, cv, cr, out_gain)


def _rglru_kernel(x_ref, gate_ref, cw_ref, cb_ref, wa_ref, ba_ref, wx_ref, bx_ref, lam_ref, o_ref,
                  xbuf_ref, h_ref):
    tm = x_ref.shape[0]
    pad = 8

    @pl.when(pl.program_id(0) == 0)
    def _():
        xbuf_ref[0:pad, :] = jnp.zeros((pad, D_WIDTH), F32)
        h_ref[...] = jnp.zeros(h_ref.shape, F32)

    xbuf_ref[pad:pad + tm, :] = x_ref[...]
    xc = cb_ref[...] + jnp.zeros((tm, D_WIDTH), F32)
    for w in range(CONV_W):
        lag = CONV_W - 1 - w
        xc = xc + cw_ref[w:w + 1, :] * xbuf_ref[pad - lag:pad - lag + tm, :]
    xbuf_ref[0:pad, :] = xbuf_ref[tm:tm + pad, :]

    xcb = xc.astype(BF16)
    r_gate = _sigmoid(jnp.dot(xcb, wa_ref[...], preferred_element_type=F32) + ba_ref[...])
    i_gate = _sigmoid(jnp.dot(xcb, wx_ref[...], preferred_element_type=F32) + bx_ref[...])
    log_a = LRU_C * r_gate * _log_sigmoid(lam_ref[...])
    a = jnp.exp(log_a)
    b = jnp.sqrt(1.0 - jnp.exp(2.0 * log_a)) * (i_gate * xc)

    rows = lax.broadcasted_iota(jnp.int32, (tm, D_WIDTH), 0)
    step = 1
    while step < tm:
        keep = rows >= step
        a_prev = jnp.where(keep, pltpu.roll(a, step, 0), 1.0)
        b_prev = jnp.where(keep, pltpu.roll(b, step, 0), 0.0)
        b = a * b_prev + b
        a = a * a_prev
        step *= 2
    h = a * h_ref[0:1, :] + b
    h_ref[...] = jnp.broadcast_to(h[tm - 1:tm, :], h_ref.shape)

    g = gate_ref[...]
    gelu = g * (0.5 * (1.0 + jnp.tanh(math.sqrt(2.0 / math.pi) * (g + 0.044715 * (g * g * g)))))
    o_ref[...] = h * gelu


def _rglru(dx, dgate, conv_w, conv_b, wa_bd, b_a, wx_bd, b_x, lam):
    s = dx.shape[0]
    tm = ROW_TILE
    row = lambda i: (i, 0)
    const2 = lambda i: (0, 0)
    vec = pl.BlockSpec((1, D_WIDTH), const2)
    mat = pl.BlockSpec((D_WIDTH, D_WIDTH), const2)
    return pl.pallas_call(
        _rglru_kernel,
        grid=(s // tm,),
        in_specs=[pl.BlockSpec((tm, D_WIDTH), row), pl.BlockSpec((tm, D_WIDTH), row),
                  pl.BlockSpec((CONV_W, D_WIDTH), const2), vec, mat, vec, mat, vec, vec],
        out_specs=pl.BlockSpec((tm, D_WIDTH), row),
        out_shape=jax.ShapeDtypeStruct((s, D_WIDTH), F32),
        scratch_shapes=[pltpu.VMEM((tm + 8, D_WIDTH), F32), pltpu.VMEM((8, D_WIDTH), F32)],
        compiler_params=_cparams(("arbitrary",)),
    )(dx, dgate, conv_w, conv_b, wa_bd, b_a, wx_bd, b_x, lam)


def _outproj_kernel(x_ref, oaT_ref, obT_ref, oc_ref, od_ref, w_ref, g1_ref, ng_ref, sc_ref, sh_ref,
                    rw_ref, rb_ref, xo_ref, h_ref, idx_ref, gate_ref):
    tm = x_ref.shape[0]
    y = jnp.dot(oaT_ref[...].T.astype(BF16), w_ref[0], preferred_element_type=F32)
    y = y + jnp.dot(obT_ref[...].T.astype(BF16), w_ref[1], preferred_element_type=F32)
    y = y + jnp.dot(oc_ref[...].astype(BF16), w_ref[2], preferred_element_type=F32)
    y = y + jnp.dot(od_ref[...].astype(BF16), w_ref[3], preferred_element_type=F32)
    x = x_ref[...] + g1_ref[...] * y
    xo_ref[...] = x
    h = x * lax.rsqrt(jnp.mean(x * x, axis=-1, keepdims=True) + EPS) * ng_ref[...]
    h = h * (1.0 + sc_ref[...]) + sh_ref[...]
    h_ref[...] = h

    h_hi, h_lo = _split_bf16(h)
    w_hi, w_lo = _split_bf16(rw_ref[...])
    logits = (jnp.dot(h_hi, w_hi, preferred_element_type=F32) + jnp.dot(h_lo, w_hi, preferred_element_type=F32)
              + jnp.dot(h_hi, w_lo, preferred_element_type=F32)) + rb_ref[...]
    e_ids = lax.broadcasted_iota(jnp.int32, (tm, N_EXPERTS), 1)
    lane = lax.broadcasted_iota(jnp.int32, (tm, LANE), 1)
    idx_out = jnp.zeros((tm, LANE), jnp.int32)
    val_out = jnp.zeros((tm, LANE), F32)
    top0 = None
    denom = jnp.zeros((tm, 1), F32)
    for k in range(TOP_K):
        top = jnp.max(logits, axis=-1, keepdims=True)
        arg = jnp.min(jnp.where(logits == top, e_ids, N_EXPERTS), axis=-1, keepdims=True)
        if k == 0:
            top0 = top
        w = jnp.exp(top - top0)
        denom = denom + w
        idx_out = jnp.where(lane == k, arg, idx_out)
        val_out = jnp.where(lane == k, w, val_out)
        logits = jnp.where(e_ids == arg, -jnp.inf, logits)
    idx_ref[...] = idx_out
    gate_ref[...] = val_out / denom


def _outproj(x, oaT, obT, oc, od, w4, g1, ng, sc, sh, rw, rb):
    s, d = x.shape
    tm = ROW_TILE
    row = lambda i: (i, 0)
    col = lambda i: (0, i)
    const2 = lambda i: (0, 0)
    vec = pl.BlockSpec((1, d), const2)
    return pl.pallas_call(
        _outproj_kernel,
        grid=(s // tm,),
        in_specs=[pl.BlockSpec((tm, d), row),
                  pl.BlockSpec((N_GROUP_W, tm), col), pl.BlockSpec((N_GROUP_W, tm), col),
                  pl.BlockSpec((tm, 256), row), pl.BlockSpec((tm, 256), row),
                  pl.BlockSpec((4, N_GROUP_W, d), lambda i: (0, 0, 0)),
                  vec, vec, vec, vec,
                  pl.BlockSpec((d, N_EXPERTS), const2), pl.BlockSpec((1, N_EXPERTS), const2)],
        out_specs=(pl.BlockSpec((tm, d), row), pl.BlockSpec((tm, d), row),
                   pl.BlockSpec((tm, LANE), row), pl.BlockSpec((tm, LANE), row)),
        out_shape=(jax.ShapeDtypeStruct((s, d), F32), jax.ShapeDtypeStruct((s, d), F32),
                   jax.ShapeDtypeStruct((s, LANE), jnp.int32), jax.ShapeDtypeStruct((s, LANE), F32)),
        compiler_params=_cparams(("arbitrary",)),
    )(x, oaT, obT, oc, od, w4, g1, ng, sc, sh, rw, rb)


def _segment_copies(tile, cnt_ref, seg_ref, off_ref, make_copy, start):
    def per_expert(e, carry):
        j = tile * N_EXPERTS + e
        cnt, a, b = cnt_ref[j], seg_ref[j], off_ref[j]
        size = MOE_TILE
        while size >= SUBLANE:
            bit = cnt & size

            @pl.when(bit != 0)
            def _():
                cp = make_copy(pl.multiple_of(a, SUBLANE), pl.multiple_of(b, SUBLANE), size)
                if start:
                    cp.start()
                else:
                    cp.wait()

            a, b = a + bit, b + bit
            size //= 2
        return carry

    lax.fori_loop(0, N_EXPERTS, per_expert, 0)


def _segment_wait(tile, cnt_ref, seg_ref, make_copy):
    j = tile * N_EXPERTS + N_EXPERTS - 1
    total = seg_ref[j] + cnt_ref[j]
    size = 1 << (MOE_SORT_ROWS.bit_length() - 1)
    while size >= SUBLANE:
        @pl.when((total & size) != 0)
        def _():
            make_copy(0, 0, size).wait()

        size //= 2


def _dispatch_kernel(cnt_ref, seg_ref, off_ref, nused_ref, h_ref, slotT_ref, rows_ref, sorted_ref, zero_ref, sem):
    tile = pl.program_id(0)
    tt = h_ref.shape[0]
    n = sorted_ref.shape[1]
    par = tile & 1
    r_ids = lax.broadcasted_iota(jnp.int32, (n, tt), 0)
    perm = jnp.where(r_ids == slotT_ref[0:1, :], 1.0, 0.0)
    for k in range(1, TOP_K):
        perm = perm + jnp.where(r_ids == slotT_ref[k:k + 1, :], 1.0, 0.0)
    perm = perm.astype(BF16)
    sorted_ref[par] = jnp.dot(perm, h_ref[...].astype(BF16), preferred_element_type=F32)

    def copies_from(p):
        def make_copy(a, b, size):
            return pltpu.make_async_copy(sorted_ref.at[p, pl.ds(a, size)], rows_ref.at[pl.ds(b, size)], sem.at[p])
        return make_copy

    _segment_copies(tile, cnt_ref, seg_ref, off_ref, copies_from(par), True)

    @pl.when(tile > 0)
    def _():
        _segment_wait(tile - 1, cnt_ref, seg_ref, copies_from(1 - par))

    last = pl.num_programs(0) - 1

    @pl.when(tile == last)
    def _():
        _segment_wait(tile, cnt_ref, seg_ref, copies_from(par))
        zero_ref[...] = jnp.zeros(zero_ref.shape, F32)

        def make_zero_copy(a, b, size):
            return pltpu.make_async_copy(zero_ref.at[pl.ds(a, size)], rows_ref.at[pl.ds(b, size)], sem.at[0])

        _segment_copies(last + 1, cnt_ref, seg_ref, off_ref, make_zero_copy, True)
        _segment_copies(last + 1, cnt_ref, seg_ref, off_ref, make_zero_copy, False)

        def block_copy(b):
            return pltpu.make_async_copy(zero_ref, rows_ref.at[pl.ds(pl.multiple_of(b * MOE_BLK, MOE_BLK), MOE_BLK)],
                                         sem.at[0])

        def issue(b, carry):
            block_copy(b).start()
            return carry

        def drain(b, carry):
            block_copy(b).wait()
            return carry

        n_blocks = rows_ref.shape[0] // MOE_BLK
        lax.fori_loop(nused_ref[0], n_blocks, issue, 0)
        lax.fori_loop(nused_ref[0], n_blocks, drain, 0)


def _dispatch(h, slotT, tile_cnt, tile_seg, tile_off, n_used, n_rows):
    s, d = h.shape
    tt = MOE_TILE
    grid_spec = pltpu.PrefetchScalarGridSpec(
        num_scalar_prefetch=4,
        grid=(s // tt,),
        in_specs=[pl.BlockSpec((tt, d), lambda i, c, sg, of, nu: (i, 0)),
                  pl.BlockSpec((TOP_K, tt), lambda i, c, sg, of, nu: (0, i))],
        out_specs=pl.BlockSpec(memory_space=pl.ANY),
        scratch_shapes=[pltpu.VMEM((2, MOE_SORT_ROWS, d), F32), pltpu.VMEM((MOE_BLK, d), F32),
                        pltpu.SemaphoreType.DMA((2,))],
    )
    return pl.pallas_call(
        _dispatch_kernel,
        grid_spec=grid_spec,
        out_shape=jax.ShapeDtypeStruct((n_rows, d), F32),
        compiler_params=_cparams(("arbitrary",)),
    )(tile_cnt, tile_seg, tile_off, n_used, h, slotT)


def _ffn_kernel(be_ref, first_ref, next_ref, slot_ref, nused_ref, x_ref, wu_hbm, bu_ref, wd_hbm, bd_ref, y_ref,
                wu_f32, wd_f32, wu_bf, wd_bf, sem):
    b = pl.program_id(0)

    def weight_copies(expert, slot):
        return (pltpu.make_async_copy(wu_hbm.at[expert], wu_f32.at[slot], sem.at[0, slot]),
                pltpu.make_async_copy(wd_hbm.at[expert], wd_f32.at[slot], sem.at[1, slot]))

    @pl.when(b == 0)
    def _():
        for cp in weight_copies(be_ref[0], 0):
            cp.start()

    @pl.when(first_ref[b] == 1)
    def _():
        slot = slot_ref[b]
        for cp in weight_copies(be_ref[b], slot):
            cp.wait()

        @pl.when(next_ref[b] >= 0)
        def _():
            for cp in weight_copies(next_ref[b], 1 - slot):
                cp.start()

        wu_bf[...] = wu_f32[slot].astype(BF16)
        wd_bf[...] = wd_f32[slot].astype(BF16)

    @pl.when(b < nused_ref[0])
    def _():
        hu = jnp.dot(x_ref[...].astype(BF16), wu_bf[...], preferred_element_type=F32) + bu_ref[0]
        g = jnp.minimum(hu[:, :D_FF], SWIGLU_LIMIT)
        lin = jnp.clip(hu[:, D_FF:], -SWIGLU_LIMIT, SWIGLU_LIMIT)
        act = (lin + 1.0) * (g * _sigmoid(SWIGLU_ALPHA * g))
        y_ref[...] = jnp.dot(act.astype(BF16), wd_bf[...], preferred_element_type=F32) + bd_ref[0]

    @pl.when(b >= nused_ref[0])
    def _():
        y_ref[...] = jnp.zeros(y_ref.shape, F32)


def _ffn(x_rows, blk_expert, first, n_used, w_up, b_up, w_down, b_down):
    n_rows, d = x_rows.shape
    blk = MOE_BLK
    n_blocks = n_rows // blk
    n_exp, _, f2 = w_up.shape
    i32 = jnp.int32
    ids = jnp.arange(n_blocks, dtype=i32)
    later_start = (ids[None, :] > ids[:, None]) & (first[None, :] == 1)
    next_pos = jnp.min(jnp.where(later_start, ids[None, :], n_blocks), axis=1)
    next_expert = jnp.where(next_pos < n_blocks, blk_expert[jnp.minimum(next_pos, n_blocks - 1)], -1).astype(i32)
    slot = ((jnp.cumsum(first) - 1) & 1).astype(i32)
    idx = lambda b, be, fi, nx, sl, nu: (be[b], 0, 0)
    grid_spec = pltpu.PrefetchScalarGridSpec(
        num_scalar_prefetch=5,
        grid=(n_blocks,),
        in_specs=[pl.BlockSpec((blk, d), lambda b, be, fi, nx, sl, nu: (jnp.minimum(b, nu[0] - 1), 0)),
                  pl.BlockSpec(memory_space=pl.ANY),
                  pl.BlockSpec((1, 1, f2), idx),
                  pl.BlockSpec(memory_space=pl.ANY),
                  pl.BlockSpec((1, 1, d), idx)],
        out_specs=pl.BlockSpec((blk, d), lambda b, be, fi, nx, sl, nu: (b, 0)),
        scratch_shapes=[pltpu.VMEM((2, d, f2), F32), pltpu.VMEM((2, D_FF, d), F32),
                        pltpu.VMEM((d, f2), BF16), pltpu.VMEM((D_FF, d), BF16),
                        pltpu.SemaphoreType.DMA((2, 2))],
    )
    return pl.pallas_call(
        _ffn_kernel,
        grid_spec=grid_spec,
        out_shape=jax.ShapeDtypeStruct((n_rows, d), F32),
        compiler_params=_cparams(("arbitrary",)),
    )(blk_expert, first, next_expert, slot, n_used, x_rows, w_up, b_up.reshape(n_exp, 1, f2),
      w_down, b_down.reshape(n_exp, 1, d))


def _combine_kernel(cnt_ref, seg_ref, off_ref, x_ref, slot_ref, gate_ref, g2_ref, rows_ref, o_ref, buf_ref, sem):
    tile = pl.program_id(0)
    tt = x_ref.shape[0]
    n = buf_ref.shape[1]
    par = tile & 1

    def copies_into(p):
        def make_copy(a, b, size):
            return pltpu.make_async_copy(rows_ref.at[pl.ds(b, size)], buf_ref.at[p, pl.ds(a, size)], sem.at[p])
        return make_copy

    @pl.when(tile == 0)
    def _():
        buf_ref[...] = jnp.zeros(buf_ref.shape, F32)
        _segment_copies(tile, cnt_ref, seg_ref, off_ref, copies_into(par), True)

    @pl.when(tile + 1 < pl.num_programs(0))
    def _():
        _segment_copies(tile + 1, cnt_ref, seg_ref, off_ref, copies_into(1 - par), True)

    slots, gates = slot_ref[...], gate_ref[...]
    r_ids = lax.broadcasted_iota(jnp.int32, (tt, n), 1)
    sel = jnp.where(r_ids == slots[:, 0:1], gates[:, 0:1], 0.0)
    for k in range(1, TOP_K):
        sel = sel + jnp.where(r_ids == slots[:, k:k + 1], gates[:, k:k + 1], 0.0)
    sel_hi, sel_lo = _split_bf16(sel)
    _segment_wait(tile, cnt_ref, seg_ref, copies_into(par))
    y_hi, y_lo = _split_bf16(buf_ref[par])
    both = jnp.dot(jnp.concatenate([sel_hi, sel_lo], axis=0), y_hi, preferred_element_type=F32)
    y = both[:tt] + both[tt:] + jnp.dot(sel_hi, y_lo, preferred_element_type=F32)
    o_ref[...] = x_ref[...] + g2_ref[...] * y


def _combine(x, slots, gates, g2, y_rows, tile_cnt, tile_seg, tile_off):
    s, d = x.shape
    tt = MOE_TILE
    row = lambda i, c, sg, of: (i, 0)
    grid_spec = pltpu.PrefetchScalarGridSpec(
        num_scalar_prefetch=3,
        grid=(s // tt,),
        in_specs=[pl.BlockSpec((tt, d), row), pl.BlockSpec((tt, LANE), row), pl.BlockSpec((tt, LANE), row),
                  pl.BlockSpec((1, d), lambda i, c, sg, of: (0, 0)),
                  pl.BlockSpec(memory_space=pl.ANY)],
        out_specs=pl.BlockSpec((tt, d), row),
        scratch_shapes=[pltpu.VMEM((2, MOE_SORT_ROWS, d), F32), pltpu.SemaphoreType.DMA((2,))],
    )
    return pl.pallas_call(
        _combine_kernel,
        grid_spec=grid_spec,
        out_shape=jax.ShapeDtypeStruct((s, d), F32),
        compiler_params=_cparams(("arbitrary",)),
    )(tile_cnt, tile_seg, tile_off, x, slots, gates, g2, y_rows)


def _moe(x, h, idx_pad, gates_pad, g2, w_up, b_up, w_down, b_down, layer_idx):
    s, d = x.shape
    n = s * TOP_K
    blk = MOE_BLK
    tt = MOE_TILE
    nt = s // tt
    n_blocks = -(-(n + nt * N_EXPERTS * SUBLANE) // blk) + N_EXPERTS
    i32 = jnp.int32
    idx = idx_pad[:, :TOP_K]
    hot = (idx[:, :, None] == jnp.arange(N_EXPERTS, dtype=i32)).astype(F32)
    onehot_t = jnp.sum(hot, axis=1).reshape(nt, tt, N_EXPERTS)
    earlier = jnp.tril(jnp.ones((tt, tt), F32), -1)
    rank_in_tile = jnp.einsum('ij,tjk->tik', earlier, onehot_t, precision=HI).reshape(s, N_EXPERTS)
    tile_cnt = (jnp.sum(onehot_t, axis=1).astype(i32) + SUBLANE - 1) // SUBLANE * SUBLANE
    tile_base = jnp.cumsum(tile_cnt, axis=0) - tile_cnt
    tile_seg = jnp.cumsum(tile_cnt, axis=1) - tile_cnt
    counts = jnp.sum(tile_cnt, axis=0)
    padded = (counts + blk - 1) // blk * blk
    pad_ends = jnp.cumsum(padded)
    pad_starts = pad_ends - padded
    tile_off = pad_starts[None, :] + tile_base
    slot_by_expert = jnp.repeat(tile_seg, tt, axis=0).astype(F32) + rank_in_tile
    slot = jnp.sum(hot * slot_by_expert[:, None, :], axis=2).astype(i32)
    slot_pad = jnp.pad(slot, ((0, 0), (0, LANE - TOP_K)))
    blk_start = jnp.arange(n_blocks, dtype=i32) * blk
    blk_expert = jnp.minimum(jnp.sum((pad_ends[None, :] <= blk_start[:, None]).astype(i32), axis=1),
                             N_EXPERTS - 1).astype(i32)
    first = jnp.concatenate([jnp.ones((1,), i32), (blk_expert[1:] != blk_expert[:-1]).astype(i32)])
    n_used = (pad_ends[-1:] // blk).astype(i32)
    tile_cnt = jnp.concatenate([tile_cnt, (padded - counts)[None, :]], axis=0)
    tile_seg = jnp.concatenate([tile_seg, jnp.zeros((1, N_EXPERTS), tile_seg.dtype)], axis=0)
    tile_off = jnp.concatenate([tile_off, (pad_starts + counts)[None, :]], axis=0)
    tile_cnt, tile_seg, tile_off = [t.astype(i32).reshape(-1) for t in (tile_cnt, tile_seg, tile_off)]

    x_rows = _dispatch(h, slot.T, tile_cnt, tile_seg, tile_off, n_used, n_blocks * blk)
    y_rows = _ffn(x_rows, blk_expert + layer_idx * N_EXPERTS, first, n_used, w_up, b_up, w_down, b_down)
    return _combine(x, slot_pad, gates_pad, g2, y_rows, tile_cnt, tile_seg, tile_off)


def _block_diag(w):
    n, d, _ = w.shape
    eye = jnp.eye(n, dtype=w.dtype)
    return (eye[:, None, :, None] * w[:, :, None, :]).reshape(n * d, n * d)


def kernel(x, c, ada_w, ada_b, norm1_g, norm2_g, w_in, w_out, a_q_gain, a_k_gain, a_lam_q1, a_lam_k1, a_lam_q2, a_lam_k2, a_out_gain, b_q_gain, b_k_gain, c_w_g2, c_b_g, c_out_gain, d_conv_w, d_conv_b, d_w_a, d_b_a, d_w_x, d_b_x, d_lambda, router_w, router_b, exp_w_up, exp_b_up, exp_w_down, exp_b_down):
    bsz, s, d = x.shape
    assert bsz == 1 and d == D_MODEL and s % ROW_TILE == 0 and s % (max(A_KEY_BLOCKS, B_KEY_BLOCKS) * ATT_TILE) == 0
    depth = ada_w.shape[0]
    mod = _modulation(c, ada_w, ada_b)
    xs = x.reshape(s, d)
    w_up_all = exp_w_up.reshape(depth * N_EXPERTS, d, 2 * D_FF)
    b_up_all = exp_b_up.reshape(depth * N_EXPERTS, 2 * D_FF)
    w_down_all = exp_w_down.reshape(depth * N_EXPERTS, D_FF, d)
    b_down_all = exp_b_down.reshape(depth * N_EXPERTS, d)
    for l in range(depth):
        sh1, sc1, g1, sh2, sc2, g2 = [mod[l, i * d:(i + 1) * d].reshape(1, d) for i in range(6)]
        w = w_in[l]
        w_cat = jnp.concatenate([w[:, :2048], w[:, 2064:2832], w[:, 2048:2064],
                                 jnp.zeros((d, W_IN_PAD - 2832), w.dtype)], axis=1).astype(BF16)
        wg2p = jnp.zeros((LANE, LANE), F32).at[:GLA_RANK, :].set(c_w_g2[l])
        (aqT, ak, avT, bqT, bk, bvT, bkm, cq, ck, cv, cr, cg, dx, dgate) = _inproj(
            xs, norm1_g[l].reshape(1, d), sc1, sh1, w_cat,
            jnp.tile(a_q_gain[l], 2 * A_HEADS).reshape(1, -1), jnp.tile(a_k_gain[l], 2 * A_HEADS).reshape(1, -1),
            jnp.tile(b_q_gain[l], B_HEADS).reshape(1, -1), jnp.tile(b_k_gain[l], B_HEADS).reshape(1, -1),
            wg2p, c_b_g[l].reshape(1, -1))
        o_aT = _diff_attention(aqT, ak, avT, a_lam_q1[l].reshape(1, -1), a_lam_k1[l].reshape(1, -1),
                               a_lam_q2[l].reshape(1, -1), a_lam_k2[l].reshape(1, -1),
                               a_out_gain[l].reshape(-1, 1), l)
        km = bkm.reshape(s // MOBA_BLOCK, B_HEADS, B_DIM).transpose(1, 0, 2)
        o_bT = _moba(bqT, bk, bvT, km)
        o_c = _gla(cq, ck, cg, cv, cr, jnp.tile(c_out_gain[l], C_HEADS).reshape(1, -1))
        o_d = _rglru(dx, dgate, d_conv_w[l], d_conv_b[l].reshape(1, -1),
                     _block_diag(d_w_a[l]).astype(BF16), d_b_a[l].reshape(1, -1),
                     _block_diag(d_w_x[l]).astype(BF16), d_b_x[l].reshape(1, -1), d_lambda[l].reshape(1, -1))
        w4 = w_out[l].astype(BF16).reshape(4, N_GROUP_W, d)
        xs, h2, idx_pad, gates_pad = _outproj(xs, o_aT, o_bT, o_c, o_d, w4, g1, norm2_g[l].reshape(1, d),
                                              sc2, sh2, router_w[l], router_b[l].reshape(1, -1))
        xs = _moe(xs, h2, idx_pad, gates_pad, g2, w_up_all, b_up_all, w_down_all, b_down_all, l)
    return xs.reshape(bsz, s, d)
```

```python
import functools
import math

import jax
import jax.numpy as jnp
from jax import lax
from jax.experimental import pallas as pl
from jax.experimental.pallas import tpu as pltpu

F32 = jnp.float32
BF16 = jnp.bfloat16
HI = lax.Precision.HIGHEST

EPS = 1e-6
NEG = -1e30
M_INIT = -1e30
MASKED = -2e30
LOG2E = 1.4426950408889634

D_MODEL = 1024
N_GROUP_W = 256
A_HEADS, A_QK, A_V = 4, 32, 64
B_HEADS, B_DIM = 4, 64
MOBA_BLOCK, MOBA_TOPK = 256, 3
C_HEADS, C_K, C_V = 4, 32, 64
GLA_RANK, GLA_TAU = 16, 16.0
GLA_SUB = 16
D_WIDTH, D_BLOCKS, CONV_W, LRU_C = 256, 4, 4, 8.0
N_EXPERTS, TOP_K, D_FF = 32, 4, 1024
SWIGLU_ALPHA, SWIGLU_LIMIT = 1.702, 7.0

LANE = 128
SUBLANE = 8
W_IN_PAD = 2944

ROW_TILE = 512
ATT_TILE = 256
A_Q_TILES = 2
B_KEY_BLOCKS = 4
SUM_ROWS = 16
MOE_BLK = 256
MOE_TILE = 256
MOE_SORT_ROWS = MOE_TILE * TOP_K + N_EXPERTS * SUBLANE
VMEM_LIMIT = 56 * 1024 * 1024


def _cparams(sem):
    return pltpu.CompilerParams(dimension_semantics=sem, vmem_limit_bytes=VMEM_LIMIT)


def _log_sigmoid(z):
    return jnp.minimum(z, 0.0) - jnp.log1p(jnp.exp(-jnp.abs(z)))


def _sigmoid(z):
    return 1.0 / (1.0 + jnp.exp(-z))


def _split_bf16(t):
    hi = t.astype(BF16)
    return hi, (t - hi.astype(F32)).astype(BF16)


def _dot_exact_rhs(t, w_bf16, parts=2):
    out, rest = None, t
    for _ in range(parts):
        piece = rest.astype(BF16)
        term = jnp.dot(piece, w_bf16, preferred_element_type=F32)
        out = term if out is None else out + term
        rest = rest - piece.astype(F32)
    return out


def _dot_exact_lhs(w_bf16, t, parts=3):
    out, rest = None, t
    for _ in range(parts):
        piece = rest.astype(BF16)
        term = jnp.dot(w_bf16, piece, preferred_element_type=F32)
        out = term if out is None else out + term
        rest = rest - piece.astype(F32)
    return out


def _group_rms(t, group):
    n = t.shape[-1]
    shift = int(math.log2(group))
    r = lax.broadcasted_iota(jnp.int32, (n, n), 0) >> shift
    c = lax.broadcasted_iota(jnp.int32, (n, n), 1) >> shift
    bd = jnp.where(r == c, 1.0, 0.0).astype(BF16)
    ss = _dot_exact_rhs(t * t, bd)
    return t * lax.rsqrt(ss * (1.0 / group) + EPS)


def _mod_kernel(c_ref, w_ref, b_ref, o_ref):
    c = c_ref[...]
    cond = c * _sigmoid(c)
    cond8 = jnp.broadcast_to(cond, (8, cond.shape[-1]))
    o_ref[0] = jnp.dot(cond8, w_ref[0], precision=HI, preferred_element_type=F32) + b_ref[0]


def _modulation(c, ada_w, ada_b):
    depth, d, n = ada_w.shape
    tn = 1536
    out = pl.pallas_call(
        _mod_kernel,
        grid=(depth, n // tn),
        in_specs=[pl.BlockSpec((1, d), lambda l, j: (0, 0)),
                  pl.BlockSpec((1, d, tn), lambda l, j: (l, 0, j)),
                  pl.BlockSpec((1, 1, tn), lambda l, j: (l, 0, j))],
        out_specs=pl.BlockSpec((1, 8, tn), lambda l, j: (l, 0, j)),
        out_shape=jax.ShapeDtypeStruct((depth, 8, n), F32),
        compiler_params=_cparams(("arbitrary", "arbitrary")),
    )(c, ada_w, ada_b.reshape(depth, 1, n))
    return out[:, 0, :]


def _inproj_kernel(x_ref, ng_ref, sc_ref, sh_ref, w_ref, aqg_ref, akg_ref, bqg_ref, bkg_ref, wg2_ref, bg_ref,
                   aqT_ref, ak_ref, avT_ref, bqT_ref, bk_ref, bvT_ref, bkm_ref,
                   cq_ref, ck_ref, cv_ref, cr_ref, cg_ref, dx_ref, dg_ref):
    x = x_ref[...]
    tm = x.shape[0]
    h = x * lax.rsqrt(jnp.mean(x * x, axis=-1, keepdims=True) + EPS) * ng_ref[...]
    h = h * (1.0 + sc_ref[...]) + sh_ref[...]
    proj = jnp.dot(h.astype(BF16), w_ref[...], preferred_element_type=F32)

    aq = _group_rms(proj[:, 0:256], A_QK) * aqg_ref[...] * (A_QK ** -0.5 * LOG2E)
    aqT_ref[...] = aq.T.astype(BF16)
    ak = _group_rms(proj[:, 256:512], A_QK) * akg_ref[...]
    ak_ref[...] = ak.astype(BF16)
    avT_ref[...] = proj[:, 512:768].T.astype(BF16)

    bq = _group_rms(proj[:, 768:1024], B_DIM) * bqg_ref[...] * (B_DIM ** -0.5 * LOG2E)
    bqT_ref[...] = bq.T.astype(BF16)
    bk = _group_rms(proj[:, 1024:1280], B_DIM) * bkg_ref[...]
    bk_ref[...] = bk.astype(BF16)
    bvT_ref[...] = proj[:, 1280:1536].T.astype(BF16)
    bkm_ref[0] = jnp.mean(bk.reshape(tm // MOBA_BLOCK, MOBA_BLOCK, N_GROUP_W), axis=1)

    cq_ref[...] = proj[:, 1536:1664] * (C_K ** -0.5)
    ck_ref[...] = proj[:, 1664:1792]
    cv_ref[...] = proj[:, 1792:2048]
    cr_ref[...] = proj[:, 2048:2304]
    gl_hi, gl_lo = _split_bf16(proj[:, 2816:2944])
    w2_hi, w2_lo = _split_bf16(wg2_ref[...])
    z = (jnp.dot(gl_hi, w2_hi, preferred_element_type=F32) + jnp.dot(gl_lo, w2_hi, preferred_element_type=F32)
         + jnp.dot(gl_hi, w2_lo, preferred_element_type=F32)) + bg_ref[...]
    cg_ref[...] = _log_sigmoid(z) * (1.0 / GLA_TAU)

    dx_ref[...] = proj[:, 2304:2560]
    dg_ref[...] = proj[:, 2560:2816]


def _inproj(x, ng, sc, sh, w_cat, aqg, akg, bqg, bkg, wg2p, bg):
    s, d = x.shape
    tm = ROW_TILE
    nt = s // tm
    row = lambda i: (i, 0)
    const2 = lambda i: (0, 0)
    vec = pl.BlockSpec((1, d), const2)
    gvec = pl.BlockSpec((1, N_GROUP_W), const2)
    out_shape = (
        jax.ShapeDtypeStruct((N_GROUP_W, s), BF16),
        jax.ShapeDtypeStruct((s, N_GROUP_W), BF16),
        jax.ShapeDtypeStruct((N_GROUP_W, s), BF16),
        jax.ShapeDtypeStruct((N_GROUP_W, s), BF16),
        jax.ShapeDtypeStruct((s, N_GROUP_W), BF16),
        jax.ShapeDtypeStruct((N_GROUP_W, s), BF16),
        jax.ShapeDtypeStruct((nt, tm // MOBA_BLOCK, N_GROUP_W), F32),
        jax.ShapeDtypeStruct((s, 128), F32),
        jax.ShapeDtypeStruct((s, 128), F32),
        jax.ShapeDtypeStruct((s, 256), F32),
        jax.ShapeDtypeStruct((s, 256), F32),
        jax.ShapeDtypeStruct((s, 128), F32),
        jax.ShapeDtypeStruct((s, 256), F32),
        jax.ShapeDtypeStruct((s, 256), F32),
    )
    out_specs = (
        pl.BlockSpec((N_GROUP_W, tm), lambda i: (0, i)),
        pl.BlockSpec((tm, N_GROUP_W), row),
        pl.BlockSpec((N_GROUP_W, tm), lambda i: (0, i)),
        pl.BlockSpec((N_GROUP_W, tm), lambda i: (0, i)),
        pl.BlockSpec((tm, N_GROUP_W), row),
        pl.BlockSpec((N_GROUP_W, tm), lambda i: (0, i)),
        pl.BlockSpec((1, tm // MOBA_BLOCK, N_GROUP_W), lambda i: (i, 0, 0)),
        pl.BlockSpec((tm, 128), row),
        pl.BlockSpec((tm, 128), row),
        pl.BlockSpec((tm, 256), row),
        pl.BlockSpec((tm, 256), row),
        pl.BlockSpec((tm, 128), row),
        pl.BlockSpec((tm, 256), row),
        pl.BlockSpec((tm, 256), row),
    )
    return pl.pallas_call(
        _inproj_kernel,
        grid=(nt,),
        in_specs=[pl.BlockSpec((tm, d), row), vec, vec, vec,
                  pl.BlockSpec((d, W_IN_PAD), const2),
                  gvec, gvec, gvec, gvec,
                  pl.BlockSpec((LANE, LANE), const2),
                  pl.BlockSpec((1, LANE), const2)],
        out_specs=out_specs,
        out_shape=out_shape,
        compiler_params=_cparams(("arbitrary",)),
    )(x, ng, sc, sh, w_cat, aqg, akg, bqg, bkg, wg2p, bg)


def _pad_query_rows(qT_ref, qpad_ref, width):
    per = LANE // width
    n_groups = qT_ref.shape[0] // width
    t = qpad_ref.shape[2]
    rows = lax.broadcasted_iota(jnp.int32, qpad_ref.shape[1:], 0)
    for c in range(qpad_ref.shape[0]):
        g, cols = c % n_groups, slice(t * (c // n_groups), t * (c // n_groups + 1))
        blk = qT_ref[LANE * (g // per):LANE * (g // per + 1), cols]
        lo = width * (g % per)
        qpad_ref[c] = jnp.where((rows >= lo) & (rows < lo + width), blk, jnp.zeros_like(blk))


def _with_sum_rows(vT):
    return jnp.concatenate([vT, jnp.ones((SUM_ROWS, vT.shape[1]), vT.dtype)], axis=0)


def _softmax_tiles(k_slabs, qpad_ref, v_tiles, adjust, m_ref, acc_ref):
    n = len(k_slabs)
    scores = [jnp.dot(k_slabs[g], qpad_ref[g], preferred_element_type=F32) for g in range(n)]
    probs, alphas = [], []
    for g in range(n):
        sT = adjust(g, scores[g])
        m_old = m_ref[g]
        m_new = jnp.maximum(m_old, jnp.max(sT, axis=0, keepdims=True))
        alphas.append(jnp.exp2(m_old - m_new))
        probs.append(jnp.exp2(sT - m_new).astype(BF16))
        m_ref[g] = m_new
    pv = [jnp.dot(v_tiles[g], probs[g], preferred_element_type=F32) for g in range(n)]
    for g in range(n):
        acc_ref[g] = alphas[g] * acc_ref[g] + pv[g]


def _diffattn_kernel(qT_ref, k_ref, vT_ref, lq1_ref, lk1_ref, lq2_ref, lk2_ref, og_ref, o_ref,
                     m_ref, acc_ref, qpad_ref, *, lam_init):
    t, tk = ATT_TILE, A_Q_TILES * ATT_TILE
    n_maps = 2 * A_HEADS
    n_chains = A_Q_TILES * n_maps
    step = pl.program_id(0)
    lam = (jnp.exp(jnp.sum(lq1_ref[...] * lk1_ref[...], axis=-1, keepdims=True))
           - jnp.exp(jnp.sum(lq2_ref[...] * lk2_ref[...], axis=-1, keepdims=True)) + lam_init)
    m_ref[...] = jnp.full(m_ref.shape, M_INIT, F32)
    acc_ref[...] = jnp.zeros(acc_ref.shape, F32)
    _pad_query_rows(qT_ref, qpad_ref, A_QK)

    def tile(ik, adjust):
        keys = pl.ds(pl.multiple_of(ik * tk, tk), tk)
        slabs = [k_ref[keys, LANE * j:LANE * (j + 1)] for j in range(N_GROUP_W // LANE)]
        v_heads = [_with_sum_rows(vT_ref[A_V * hd:A_V * (hd + 1), keys]) for hd in range(A_HEADS)]
        _softmax_tiles([slabs[(c % n_maps) // 4] for c in range(n_chains)], qpad_ref,
                       [v_heads[(c % n_maps) // 2] for c in range(n_chains)], adjust, m_ref, acc_ref)

    def body(ik, carry):
        tile(ik, lambda c, sT: sT)
        return carry

    lax.fori_loop(0, step, body, 0)
    diff = lax.broadcasted_iota(jnp.int32, (tk, t), 0) - lax.broadcasted_iota(jnp.int32, (tk, t), 1)
    tile(step, lambda c, sT: jnp.where(diff <= (c // n_maps) * t, sT, MASKED))

    for sub in range(A_Q_TILES):
        for hd in range(A_HEADS):
            a1, a2 = acc_ref[sub * n_maps + 2 * hd], acc_ref[sub * n_maps + 2 * hd + 1]
            o = a1[:A_V] / a1[A_V:A_V + 1] - lam * (a2[:A_V] / a2[A_V:A_V + 1])
            o = o * lax.rsqrt(jnp.mean(o * o, axis=0, keepdims=True) + EPS) * og_ref[...]
            o_ref[A_V * hd:A_V * (hd + 1), t * sub:t * (sub + 1)] = o * (1.0 - lam_init)


def _diff_attention(aqT, ak, avT, lq1, lk1, lq2, lk2, out_gain_col, layer_idx):
    s = aqT.shape[1]
    t = ATT_TILE
    tq = A_Q_TILES * ATT_TILE
    n_maps = A_Q_TILES * 2 * A_HEADS
    lam_init = 0.8 - 0.6 * math.exp(-0.3 * layer_idx)
    vec = pl.BlockSpec((1, A_QK), lambda i: (0, 0))
    return pl.pallas_call(
        functools.partial(_diffattn_kernel, lam_init=lam_init),
        grid=(s // tq,),
        in_specs=[pl.BlockSpec((N_GROUP_W, tq), lambda i: (0, i)),
                  pl.BlockSpec((s, N_GROUP_W), lambda i: (0, 0)),
                  pl.BlockSpec((N_GROUP_W, s), lambda i: (0, 0)),
                  vec, vec, vec, vec,
                  pl.BlockSpec((A_V, 1), lambda i: (0, 0))],
        out_specs=pl.BlockSpec((N_GROUP_W, tq), lambda i: (0, i)),
        out_shape=jax.ShapeDtypeStruct((N_GROUP_W, s), F32),
        scratch_shapes=[pltpu.VMEM((n_maps, 1, t), F32), pltpu.VMEM((n_maps, A_V + SUM_ROWS, t), F32),
                        pltpu.VMEM((n_maps, LANE, t), BF16)],
        compiler_params=_cparams(("arbitrary",)),
    )(aqT, ak, avT, lq1, lk1, lq2, lk2, out_gain_col)


def _moba_kernel(qT_ref, k_ref, vT_ref, km_ref, o_ref, m_ref, acc_ref, bias_ref, qpad_ref):
    KEY_BLOCKS = B_KEY_BLOCKS
    t, tk = ATT_TILE, KEY_BLOCKS * ATT_TILE
    own = pl.program_id(0)
    nb = km_ref.shape[1]
    m_ref[...] = jnp.full(m_ref.shape, M_INIT, F32)
    acc_ref[...] = jnp.zeros(acc_ref.shape, F32)
    _pad_query_rows(qT_ref, qpad_ref, B_DIM)

    blk = lax.broadcasted_iota(jnp.int32, (nb, t), 0)
    earlier = blk < own
    for hd in range(B_HEADS):
        qT = qT_ref[B_DIM * hd:B_DIM * (hd + 1), :].astype(F32)
        gate = jnp.dot(km_ref[hd], qT, precision=HI, preferred_element_type=F32)
        gate = jnp.where(earlier, gate, NEG)
        sel_bias = jnp.full((nb, t), MASKED, F32)
        for _ in range(MOBA_TOPK):
            top = jnp.max(gate, axis=0, keepdims=True)
            pick = blk == jnp.min(jnp.where(gate == top, blk, nb), axis=0, keepdims=True)
            sel_bias = jnp.where(pick, jnp.where(earlier, 0.0, MASKED), sel_bias)
            gate = jnp.where(pick, -jnp.inf, gate)
        bias_ref[hd] = sel_bias

    def tile(j, adjust):
        keys = pl.ds(pl.multiple_of(j * tk, tk), tk)
        k_slabs = [k_ref[keys, LANE * (hd // 2):LANE * (hd // 2 + 1)] for hd in range(B_HEADS)]
        v_tiles = [_with_sum_rows(vT_ref[B_DIM * hd:B_DIM * (hd + 1), keys]) for hd in range(B_HEADS)]
        _softmax_tiles(k_slabs, qpad_ref, v_tiles, adjust, m_ref, acc_ref)

    def block_bias(hd, b):
        return bias_ref[hd, pl.ds(b, 1), :]

    def body(j, carry):
        tile(j, lambda hd, sT: jnp.concatenate(
            [sT[t * part:t * (part + 1)] + block_bias(hd, KEY_BLOCKS * j + part) for part in range(KEY_BLOCKS)],
            axis=0))
        return carry

    n_full = own // KEY_BLOCKS
    lax.fori_loop(0, n_full, body, 0)

    causal = (lax.broadcasted_iota(jnp.int32, (t, t), 0) <= lax.broadcasted_iota(jnp.int32, (t, t), 1))

    def last(hd, sT):
        parts = []
        for part in range(KEY_BLOCKS):
            b = KEY_BLOCKS * n_full + part
            sp = sT[t * part:t * (part + 1)]
            other = sp + jnp.where(b < own, block_bias(hd, b), MASKED)
            parts.append(jnp.where(b == own, jnp.where(causal, sp, MASKED), other))
        return jnp.concatenate(parts, axis=0)

    tile(n_full, last)
    for hd in range(B_HEADS):
        a = acc_ref[hd]
        o_ref[B_DIM * hd:B_DIM * (hd + 1), :] = a[:B_DIM] / a[B_DIM:B_DIM + 1]


def _moba(bqT, bk, bvT, km):
    s = bqT.shape[1]
    t = ATT_TILE
    nb = km.shape[1]
    return pl.pallas_call(
        _moba_kernel,
        grid=(s // t,),
        in_specs=[pl.BlockSpec((N_GROUP_W, t), lambda i: (0, i)),
                  pl.BlockSpec((s, N_GROUP_W), lambda i: (0, 0)),
                  pl.BlockSpec((N_GROUP_W, s), lambda i: (0, 0)),
                  pl.BlockSpec((B_HEADS, nb, B_DIM), lambda i: (0, 0, 0))],
        out_specs=pl.BlockSpec((N_GROUP_W, t), lambda i: (0, i)),
        out_shape=jax.ShapeDtypeStruct((N_GROUP_W, s), F32),
        scratch_shapes=[pltpu.VMEM((B_HEADS, 1, t), F32),
                        pltpu.VMEM((B_HEADS, B_DIM + SUM_ROWS, t), F32), pltpu.VMEM((B_HEADS, nb, t), F32),
                        pltpu.VMEM((B_HEADS, LANE, t), BF16)],
        compiler_params=_cparams(("arbitrary",)),
    )(bqT, bk, bvT, km)


def _gla_kernel(q_ref, k_ref, g_ref, v_ref, r_ref, og_ref, o_ref, stateT_ref, kbuf_ref, bbuf_ref, vbuf_ref):
    tm = q_ref.shape[0]
    c = GLA_SUB
    shift = int(math.log2(c))

    @pl.when(pl.program_id(0) == 0)
    def _():
        stateT_ref[...] = jnp.zeros(stateT_ref.shape, F32)

    q, k, v = q_ref[...], k_ref[...], v_ref[...]
    ri = lax.broadcasted_iota(jnp.int32, (tm, tm), 0)
    ci = lax.broadcasted_iota(jnp.int32, (tm, tm), 1)
    same = (ri >> shift) == (ci >> shift)
    g = g_ref[...]
    b = _dot_exact_lhs(jnp.where(same & (ri >= ci), 1.0, 0.0).astype(BF16), g)
    b_last = _dot_exact_lhs(jnp.where(same, 1.0, 0.0).astype(BF16), g)

    k_shift, v_shift = int(math.log2(C_K)), int(math.log2(C_V))
    head_sum = jnp.where((lax.broadcasted_iota(jnp.int32, (C_HEADS * C_K, C_HEADS * C_V), 0) >> k_shift)
                         == (lax.broadcasted_iota(jnp.int32, (C_HEADS * C_K, C_HEADS * C_V), 1) >> v_shift),
                         1.0, 0.0).astype(BF16)

    for buf, val in ((kbuf_ref, k), (bbuf_ref, b), (vbuf_ref, v)):
        buf[0:c, :] = jnp.zeros((c, val.shape[1]), F32)
        buf[c:c + tm, :] = val
    pos = lax.broadcasted_iota(jnp.int32, (tm, C_HEADS * C_K), 0) & (c - 1)
    o = jnp.zeros((tm, C_HEADS * C_V), F32)
    for delta in range(c):
        k_d = kbuf_ref[c - delta:c - delta + tm, :]
        b_d = bbuf_ref[c - delta:c - delta + tm, :]
        v_d = vbuf_ref[c - delta:c - delta + tm, :]
        w = q * k_d * jnp.exp(jnp.where(pos >= delta, b - b_d, NEG))
        w_hi, w_lo = _split_bf16(w)
        att = (jnp.dot(w_hi, head_sum, preferred_element_type=F32)
               + jnp.dot(w_lo, head_sum, preferred_element_type=F32))
        o = o + att * v_d

    q_dec = (q * jnp.exp(b)).astype(BF16)
    k_dec = (k * jnp.exp(b_last - b)).astype(BF16)
    decay = jnp.exp(b_last)
    v_bf = v.astype(BF16)
    own_head = ((lax.broadcasted_iota(jnp.int32, (C_HEADS * C_V, C_HEADS * C_K), 0) >> v_shift)
                == (lax.broadcasted_iota(jnp.int32, (C_HEADS * C_V, C_HEADS * C_K), 1) >> k_shift))
    nt_dims = (((1,), (1,)), ((), ()))
    tn_dims = (((0,), (0,)), ((), ()))
    updates = [lax.dot_general(v_bf[c * i:c * (i + 1)], k_dec[c * i:c * (i + 1)], tn_dims,
                               preferred_element_type=F32) for i in range(tm // c)]
    stateT = stateT_ref[...]
    far = []
    for i in range(tm // c):
        far.append(lax.dot_general(q_dec[c * i:c * (i + 1)], stateT.astype(BF16), nt_dims,
                                   preferred_element_type=F32))
        stateT = stateT * decay[c * i:c * i + 1, :] + jnp.where(own_head, updates[i], 0.0)
    stateT_ref[...] = stateT
    o = o + jnp.concatenate(far, axis=0)

    r = r_ref[...]
    o_ref[...] = _group_rms(o, C_V) * og_ref[...] * (r * _sigmoid(r))


def _gla(cq, ck, cg, cv, cr, out_gain):
    s = cq.shape[0]
    tm = ROW_TILE
    row = lambda i: (i, 0)
    return pl.pallas_call(
        _gla_kernel,
        grid=(s // tm,),
        in_specs=[pl.BlockSpec((tm, 128), row), pl.BlockSpec((tm, 128), row), pl.BlockSpec((tm, 128), row),
                  pl.BlockSpec((tm, 256), row), pl.BlockSpec((tm, 256), row),
                  pl.BlockSpec((1, C_HEADS * C_V), lambda i: (0, 0))],
        out_specs=pl.BlockSpec((tm, 256), row),
        out_shape=jax.ShapeDtypeStruct((s, 256), F32),
        scratch_shapes=[pltpu.VMEM((C_HEADS * C_V, C_HEADS * C_K), F32),
                        pltpu.VMEM((tm + GLA_SUB, C_HEADS * C_K), F32),
                        pltpu.VMEM((tm + GLA_SUB, C_HEADS * C_K), F32),
                        pltpu.VMEM((tm + GLA_SUB, C_HEADS * C_V), F32)],
        compiler_params=_cparams(("arbitrary",)),
    )(cq, ck, cg, cv, cr, out_gain)


def _rglru_kernel(x_ref, gate_ref, cw_ref, cb_ref, wa_ref, ba_ref, wx_ref, bx_ref, lam_ref, o_ref,
                  xbuf_ref, h_ref):
    tm = x_ref.shape[0]
    pad = 8

    @pl.when(pl.program_id(0) == 0)
    def _():
        xbuf_ref[0:pad, :] = jnp.zeros((pad, D_WIDTH), F32)
        h_ref[...] = jnp.zeros(h_ref.shape, F32)

    xbuf_ref[pad:pad + tm, :] = x_ref[...]
    xc = cb_ref[...] + jnp.zeros((tm, D_WIDTH), F32)
    for w in range(CONV_W):
        lag = CONV_W - 1 - w
        xc = xc + cw_ref[w:w + 1, :] * xbuf_ref[pad - lag:pad - lag + tm, :]
    xbuf_ref[0:pad, :] = xbuf_ref[tm:tm + pad, :]

    xcb = xc.astype(BF16)
    r_gate = _sigmoid(jnp.dot(xcb, wa_ref[...], preferred_element_type=F32) + ba_ref[...])
    i_gate = _sigmoid(jnp.dot(xcb, wx_ref[...], preferred_element_type=F32) + bx_ref[...])
    log_a = LRU_C * r_gate * _log_sigmoid(lam_ref[...])
    a = jnp.exp(log_a)
    b = jnp.sqrt(1.0 - jnp.exp(2.0 * log_a)) * (i_gate * xc)

    rows = lax.broadcasted_iota(jnp.int32, (tm, D_WIDTH), 0)
    step = 1
    while step < tm:
        keep = rows >= step
        a_prev = jnp.where(keep, pltpu.roll(a, step, 0), 1.0)
        b_prev = jnp.where(keep, pltpu.roll(b, step, 0), 0.0)
        b = a * b_prev + b
        a = a * a_prev
        step *= 2
    h = a * h_ref[0:1, :] + b
    h_ref[...] = jnp.broadcast_to(h[tm - 1:tm, :], h_ref.shape)

    g = gate_ref[...]
    gelu = g * (0.5 * (1.0 + jnp.tanh(math.sqrt(2.0 / math.pi) * (g + 0.044715 * (g * g * g)))))
    o_ref[...] = h * gelu


def _rglru(dx, dgate, conv_w, conv_b, wa_bd, b_a, wx_bd, b_x, lam):
    s = dx.shape[0]
    tm = ROW_TILE
    row = lambda i: (i, 0)
    const2 = lambda i: (0, 0)
    vec = pl.BlockSpec((1, D_WIDTH), const2)
    mat = pl.BlockSpec((D_WIDTH, D_WIDTH), const2)
    return pl.pallas_call(
        _rglru_kernel,
        grid=(s // tm,),
        in_specs=[pl.BlockSpec((tm, D_WIDTH), row), pl.BlockSpec((tm, D_WIDTH), row),
                  pl.BlockSpec((CONV_W, D_WIDTH), const2), vec, mat, vec, mat, vec, vec],
        out_specs=pl.BlockSpec((tm, D_WIDTH), row),
        out_shape=jax.ShapeDtypeStruct((s, D_WIDTH), F32),
        scratch_shapes=[pltpu.VMEM((tm + 8, D_WIDTH), F32), pltpu.VMEM((8, D_WIDTH), F32)],
        compiler_params=_cparams(("arbitrary",)),
    )(dx, dgate, conv_w, conv_b, wa_bd, b_a, wx_bd, b_x, lam)


def _outproj_kernel(x_ref, oaT_ref, obT_ref, oc_ref, od_ref, w_ref, g1_ref, ng_ref, sc_ref, sh_ref,
                    rw_ref, rb_ref, xo_ref, h_ref, idx_ref, gate_ref):
    tm = x_ref.shape[0]
    y = jnp.dot(oaT_ref[...].T.astype(BF16), w_ref[0], preferred_element_type=F32)
    y = y + jnp.dot(obT_ref[...].T.astype(BF16), w_ref[1], preferred_element_type=F32)
    y = y + jnp.dot(oc_ref[...].astype(BF16), w_ref[2], preferred_element_type=F32)
    y = y + jnp.dot(od_ref[...].astype(BF16), w_ref[3], preferred_element_type=F32)
    x = x_ref[...] + g1_ref[...] * y
    xo_ref[...] = x
    h = x * lax.rsqrt(jnp.mean(x * x, axis=-1, keepdims=True) + EPS) * ng_ref[...]
    h = h * (1.0 + sc_ref[...]) + sh_ref[...]
    h_ref[...] = h

    h_hi, h_lo = _split_bf16(h)
    w_hi, w_lo = _split_bf16(rw_ref[...])
    logits = (jnp.dot(h_hi, w_hi, preferred_element_type=F32) + jnp.dot(h_lo, w_hi, preferred_element_type=F32)
              + jnp.dot(h_hi, w_lo, preferred_element_type=F32)) + rb_ref[...]
    e_ids = lax.broadcasted_iota(jnp.int32, (tm, N_EXPERTS), 1)
    lane = lax.broadcasted_iota(jnp.int32, (tm, LANE), 1)
    idx_out = jnp.zeros((tm, LANE), jnp.int32)
    val_out = jnp.zeros((tm, LANE), F32)
    top0 = None
    denom = jnp.zeros((tm, 1), F32)
    for k in range(TOP_K):
        top = jnp.max(logits, axis=-1, keepdims=True)
        arg = jnp.min(jnp.where(logits == top, e_ids, N_EXPERTS), axis=-1, keepdims=True)
        if k == 0:
            top0 = top
        w = jnp.exp(top - top0)
        denom = denom + w
        idx_out = jnp.where(lane == k, arg, idx_out)
        val_out = jnp.where(lane == k, w, val_out)
        logits = jnp.where(e_ids == arg, -jnp.inf, logits)
    idx_ref[...] = idx_out
    gate_ref[...] = val_out / denom


def _outproj(x, oaT, obT, oc, od, w4, g1, ng, sc, sh, rw, rb):
    s, d = x.shape
    tm = ROW_TILE
    row = lambda i: (i, 0)
    col = lambda i: (0, i)
    const2 = lambda i: (0, 0)
    vec = pl.BlockSpec((1, d), const2)
    return pl.pallas_call(
        _outproj_kernel,
        grid=(s // tm,),
        in_specs=[pl.BlockSpec((tm, d), row),
                  pl.BlockSpec((N_GROUP_W, tm), col), pl.BlockSpec((N_GROUP_W, tm), col),
                  pl.BlockSpec((tm, 256), row), pl.BlockSpec((tm, 256), row),
                  pl.BlockSpec((4, N_GROUP_W, d), lambda i: (0, 0, 0)),
                  vec, vec, vec, vec,
                  pl.BlockSpec((d, N_EXPERTS), const2), pl.BlockSpec((1, N_EXPERTS), const2)],
        out_specs=(pl.BlockSpec((tm, d), row), pl.BlockSpec((tm, d), row),
                   pl.BlockSpec((tm, LANE), row), pl.BlockSpec((tm, LANE), row)),
        out_shape=(jax.ShapeDtypeStruct((s, d), F32), jax.ShapeDtypeStruct((s, d), F32),
                   jax.ShapeDtypeStruct((s, LANE), jnp.int32), jax.ShapeDtypeStruct((s, LANE), F32)),
        compiler_params=_cparams(("arbitrary",)),
    )(x, oaT, obT, oc, od, w4, g1, ng, sc, sh, rw, rb)


def _segment_copies(tile, cnt_ref, seg_ref, off_ref, make_copy, start):
    def per_expert(e, carry):
        j = tile * N_EXPERTS + e
        cnt, a, b = cnt_ref[j], seg_ref[j], off_ref[j]
        size = MOE_TILE
        while size >= SUBLANE:
            bit = cnt & size

            @pl.when(bit != 0)
            def _():
                cp = make_copy(pl.multiple_of(a, SUBLANE), pl.multiple_of(b, SUBLANE), size)
                if start:
                    cp.start()
                else:
                    cp.wait()

            a, b = a + bit, b + bit
            size //= 2
        return carry

    lax.fori_loop(0, N_EXPERTS, per_expert, 0)


def _segment_wait(tile, cnt_ref, seg_ref, make_copy):
    j = tile * N_EXPERTS + N_EXPERTS - 1
    total = seg_ref[j] + cnt_ref[j]
    size = 1 << (MOE_SORT_ROWS.bit_length() - 1)
    while size >= SUBLANE:
        @pl.when((total & size) != 0)
        def _():
            make_copy(0, 0, size).wait()

        size //= 2


def _dispatch_kernel(cnt_ref, seg_ref, off_ref, nused_ref, h_ref, slotT_ref, rows_ref, sorted_ref, zero_ref, sem):
    tile = pl.program_id(0)
    tt = h_ref.shape[0]
    n = sorted_ref.shape[1]
    par = tile & 1
    r_ids = lax.broadcasted_iota(jnp.int32, (n, tt), 0)
    perm = jnp.where(r_ids == slotT_ref[0:1, :], 1.0, 0.0)
    for k in range(1, TOP_K):
        perm = perm + jnp.where(r_ids == slotT_ref[k:k + 1, :], 1.0, 0.0)
    perm = perm.astype(BF16)
    sorted_ref[par] = jnp.dot(perm, h_ref[...].astype(BF16), preferred_element_type=F32)

    def copies_from(p):
        def make_copy(a, b, size):
            return pltpu.make_async_copy(sorted_ref.at[p, pl.ds(a, size)], rows_ref.at[pl.ds(b, size)], sem.at[p])
        return make_copy

    _segment_copies(tile, cnt_ref, seg_ref, off_ref, copies_from(par), True)

    @pl.when(tile > 0)
    def _():
        _segment_wait(tile - 1, cnt_ref, seg_ref, copies_from(1 - par))

    last = pl.num_programs(0) - 1

    @pl.when(tile == last)
    def _():
        _segment_wait(tile, cnt_ref, seg_ref, copies_from(par))
        zero_ref[...] = jnp.zeros(zero_ref.shape, F32)

        def make_zero_copy(a, b, size):
            return pltpu.make_async_copy(zero_ref.at[pl.ds(a, size)], rows_ref.at[pl.ds(b, size)], sem.at[0])

        _segment_copies(last + 1, cnt_ref, seg_ref, off_ref, make_zero_copy, True)
        _segment_copies(last + 1, cnt_ref, seg_ref, off_ref, make_zero_copy, False)

        def block_copy(b):
            return pltpu.make_async_copy(zero_ref, rows_ref.at[pl.ds(pl.multiple_of(b * MOE_BLK, MOE_BLK), MOE_BLK)],
                                         sem.at[0])

        def issue(b, carry):
            block_copy(b).start()
            return carry

        def drain(b, carry):
            block_copy(b).wait()
            return carry

        n_blocks = rows_ref.shape[0] // MOE_BLK
        lax.fori_loop(nused_ref[0], n_blocks, issue, 0)
        lax.fori_loop(nused_ref[0], n_blocks, drain, 0)


def _dispatch(h, slotT, tile_cnt, tile_seg, tile_off, n_used, n_rows):
    s, d = h.shape
    tt = MOE_TILE
    grid_spec = pltpu.PrefetchScalarGridSpec(
        num_scalar_prefetch=4,
        grid=(s // tt,),
        in_specs=[pl.BlockSpec((tt, d), lambda i, c, sg, of, nu: (i, 0)),
                  pl.BlockSpec((TOP_K, tt), lambda i, c, sg, of, nu: (0, i))],
        out_specs=pl.BlockSpec(memory_space=pl.ANY),
        scratch_shapes=[pltpu.VMEM((2, MOE_SORT_ROWS, d), F32), pltpu.VMEM((MOE_BLK, d), F32),
                        pltpu.SemaphoreType.DMA((2,))],
    )
    return pl.pallas_call(
        _dispatch_kernel,
        grid_spec=grid_spec,
        out_shape=jax.ShapeDtypeStruct((n_rows, d), F32),
        compiler_params=_cparams(("arbitrary",)),
    )(tile_cnt, tile_seg, tile_off, n_used, h, slotT)


def _ffn_kernel(be_ref, first_ref, next_ref, slot_ref, nused_ref, x_ref, wu_hbm, bu_ref, wd_hbm, bd_ref, y_ref,
                wu_f32, wd_f32, wu_bf, wd_bf, sem):
    b = pl.program_id(0)

    def weight_copies(expert, slot):
        return (pltpu.make_async_copy(wu_hbm.at[expert], wu_f32.at[slot], sem.at[0, slot]),
                pltpu.make_async_copy(wd_hbm.at[expert], wd_f32.at[slot], sem.at[1, slot]))

    @pl.when(b == 0)
    def _():
        for cp in weight_copies(be_ref[0], 0):
            cp.start()

    @pl.when(first_ref[b] == 1)
    def _():
        slot = slot_ref[b]
        for cp in weight_copies(be_ref[b], slot):
            cp.wait()

        @pl.when(next_ref[b] >= 0)
        def _():
            for cp in weight_copies(next_ref[b], 1 - slot):
                cp.start()

        wu_bf[...] = wu_f32[slot].astype(BF16)
        wd_bf[...] = wd_f32[slot].astype(BF16)

    @pl.when(b < nused_ref[0])
    def _():
        hu = jnp.dot(x_ref[...].astype(BF16), wu_bf[...], preferred_element_type=F32) + bu_ref[0]
        g = jnp.minimum(hu[:, :D_FF], SWIGLU_LIMIT)
        lin = jnp.clip(hu[:, D_FF:], -SWIGLU_LIMIT, SWIGLU_LIMIT)
        act = (lin + 1.0) * (g * _sigmoid(SWIGLU_ALPHA * g))
        y_ref[...] = jnp.dot(act.astype(BF16), wd_bf[...], preferred_element_type=F32) + bd_ref[0]

    @pl.when(b >= nused_ref[0])
    def _():
        y_ref[...] = jnp.zeros(y_ref.shape, F32)


def _ffn(x_rows, blk_expert, first, n_used, w_up, b_up, w_down, b_down):
    n_rows, d = x_rows.shape
    blk = MOE_BLK
    n_blocks = n_rows // blk
    n_exp, _, f2 = w_up.shape
    i32 = jnp.int32
    ids = jnp.arange(n_blocks, dtype=i32)
    later_start = (ids[None, :] > ids[:, None]) & (first[None, :] == 1)
    next_pos = jnp.min(jnp.where(later_start, ids[None, :], n_blocks), axis=1)
    next_expert = jnp.where(next_pos < n_blocks, blk_expert[jnp.minimum(next_pos, n_blocks - 1)], -1).astype(i32)
    slot = ((jnp.cumsum(first) - 1) & 1).astype(i32)
    idx = lambda b, be, fi, nx, sl, nu: (be[b], 0, 0)
    grid_spec = pltpu.PrefetchScalarGridSpec(
        num_scalar_prefetch=5,
        grid=(n_blocks,),
        in_specs=[pl.BlockSpec((blk, d), lambda b, be, fi, nx, sl, nu: (jnp.minimum(b, nu[0] - 1), 0)),
                  pl.BlockSpec(memory_space=pl.ANY),
                  pl.BlockSpec((1, 1, f2), idx),
                  pl.BlockSpec(memory_space=pl.ANY),
                  pl.BlockSpec((1, 1, d), idx)],
        out_specs=pl.BlockSpec((blk, d), lambda b, be, fi, nx, sl, nu: (b, 0)),
        scratch_shapes=[pltpu.VMEM((2, d, f2), F32), pltpu.VMEM((2, D_FF, d), F32),
                        pltpu.VMEM((d, f2), BF16), pltpu.VMEM((D_FF, d), BF16),
                        pltpu.SemaphoreType.DMA((2, 2))],
    )
    return pl.pallas_call(
        _ffn_kernel,
        grid_spec=grid_spec,
        out_shape=jax.ShapeDtypeStruct((n_rows, d), F32),
        compiler_params=_cparams(("arbitrary",)),
    )(blk_expert, first, next_expert, slot, n_used, x_rows, w_up, b_up.reshape(n_exp, 1, f2),
      w_down, b_down.reshape(n_exp, 1, d))


def _combine_kernel(cnt_ref, seg_ref, off_ref, x_ref, slot_ref, gate_ref, g2_ref, rows_ref, o_ref, buf_ref, sem):
    tile = pl.program_id(0)
    tt = x_ref.shape[0]
    n = buf_ref.shape[1]
    par = tile & 1

    def copies_into(p):
        def make_copy(a, b, size):
            return pltpu.make_async_copy(rows_ref.at[pl.ds(b, size)], buf_ref.at[p, pl.ds(a, size)], sem.at[p])
        return make_copy

    @pl.when(tile == 0)
    def _():
        buf_ref[...] = jnp.zeros(buf_ref.shape, F32)
        _segment_copies(tile, cnt_ref, seg_ref, off_ref, copies_into(par), True)

    @pl.when(tile + 1 < pl.num_programs(0))
    def _():
        _segment_copies(tile + 1, cnt_ref, seg_ref, off_ref, copies_into(1 - par), True)

    slots, gates = slot_ref[...], gate_ref[...]
    r_ids = lax.broadcasted_iota(jnp.int32, (tt, n), 1)
    sel = jnp.where(r_ids == slots[:, 0:1], gates[:, 0:1], 0.0)
    for k in range(1, TOP_K):
        sel = sel + jnp.where(r_ids == slots[:, k:k + 1], gates[:, k:k + 1], 0.0)
    sel_hi, sel_lo = _split_bf16(sel)
    _segment_wait(tile, cnt_ref, seg_ref, copies_into(par))
    y_hi, y_lo = _split_bf16(buf_ref[par])
    both = jnp.dot(jnp.concatenate([sel_hi, sel_lo], axis=0), y_hi, preferred_element_type=F32)
    y = both[:tt] + both[tt:] + jnp.dot(sel_hi, y_lo, preferred_element_type=F32)
    o_ref[...] = x_ref[...] + g2_ref[...] * y


def _combine(x, slots, gates, g2, y_rows, tile_cnt, tile_seg, tile_off):
    s, d = x.shape
    tt = MOE_TILE
    row = lambda i, c, sg, of: (i, 0)
    grid_spec = pltpu.PrefetchScalarGridSpec(
        num_scalar_prefetch=3,
        grid=(s // tt,),
        in_specs=[pl.BlockSpec((tt, d), row), pl.BlockSpec((tt, LANE), row), pl.BlockSpec((tt, LANE), row),
                  pl.BlockSpec((1, d), lambda i, c, sg, of: (0, 0)),
                  pl.BlockSpec(memory_space=pl.ANY)],
        out_specs=pl.BlockSpec((tt, d), row),
        scratch_shapes=[pltpu.VMEM((2, MOE_SORT_ROWS, d), F32), pltpu.SemaphoreType.DMA((2,))],
    )
    return pl.pallas_call(
        _combine_kernel,
        grid_spec=grid_spec,
        out_shape=jax.ShapeDtypeStruct((s, d), F32),
        compiler_params=_cparams(("arbitrary",)),
    )(tile_cnt, tile_seg, tile_off, x, slots, gates, g2, y_rows)


def _moe(x, h, idx_pad, gates_pad, g2, w_up, b_up, w_down, b_down, layer_idx):
    s, d = x.shape
    n = s * TOP_K
    blk = MOE_BLK
    tt = MOE_TILE
    nt = s // tt
    n_blocks = -(-(n + nt * N_EXPERTS * SUBLANE) // blk) + N_EXPERTS
    i32 = jnp.int32
    idx = idx_pad[:, :TOP_K]
    hot = (idx[:, :, None] == jnp.arange(N_EXPERTS, dtype=i32)).astype(F32)
    onehot_t = jnp.sum(hot, axis=1).reshape(nt, tt, N_EXPERTS)
    earlier = jnp.tril(jnp.ones((tt, tt), F32), -1)
    rank_in_tile = jnp.einsum('ij,tjk->tik', earlier, onehot_t, precision=HI).reshape(s, N_EXPERTS)
    tile_cnt = (jnp.sum(onehot_t, axis=1).astype(i32) + SUBLANE - 1) // SUBLANE * SUBLANE
    tile_base = jnp.cumsum(tile_cnt, axis=0) - tile_cnt
    tile_seg = jnp.cumsum(tile_cnt, axis=1) - tile_cnt
    counts = jnp.sum(tile_cnt, axis=0)
    padded = (counts + blk - 1) // blk * blk
    pad_ends = jnp.cumsum(padded)
    pad_starts = pad_ends - padded
    tile_off = pad_starts[None, :] + tile_base
    slot_by_expert = jnp.repeat(tile_seg, tt, axis=0).astype(F32) + rank_in_tile
    slot = jnp.sum(hot * slot_by_expert[:, None, :], axis=2).astype(i32)
    slot_pad = jnp.pad(slot, ((0, 0), (0, LANE - TOP_K)))
    blk_start = jnp.arange(n_blocks, dtype=i32) * blk
    blk_expert = jnp.minimum(jnp.sum((pad_ends[None, :] <= blk_start[:, None]).astype(i32), axis=1),
                             N_EXPERTS - 1).astype(i32)
    first = jnp.concatenate([jnp.ones((1,), i32), (blk_expert[1:] != blk_expert[:-1]).astype(i32)])
    n_used = (pad_ends[-1:] // blk).astype(i32)
    tile_cnt = jnp.concatenate([tile_cnt, (padded - counts)[None, :]], axis=0)
    tile_seg = jnp.concatenate([tile_seg, jnp.zeros((1, N_EXPERTS), tile_seg.dtype)], axis=0)
    tile_off = jnp.concatenate([tile_off, (pad_starts + counts)[None, :]], axis=0)
    tile_cnt, tile_seg, tile_off = [t.astype(i32).reshape(-1) for t in (tile_cnt, tile_seg, tile_off)]

    x_rows = _dispatch(h, slot.T, tile_cnt, tile_seg, tile_off, n_used, n_blocks * blk)
    y_rows = _ffn(x_rows, blk_expert + layer_idx * N_EXPERTS, first, n_used, w_up, b_up, w_down, b_down)
    return _combine(x, slot_pad, gates_pad, g2, y_rows, tile_cnt, tile_seg, tile_off)


def _block_diag(w):
    n, d, _ = w.shape
    eye = jnp.eye(n, dtype=w.dtype)
    return (eye[:, None, :, None] * w[:, :, None, :]).reshape(n * d, n * d)


def kernel(x, c, ada_w, ada_b, norm1_g, norm2_g, w_in, w_out, a_q_gain, a_k_gain, a_lam_q1, a_lam_k1, a_lam_q2, a_lam_k2, a_out_gain, b_q_gain, b_k_gain, c_w_g2, c_b_g, c_out_gain, d_conv_w, d_conv_b, d_w_a, d_b_a, d_w_x, d_b_x, d_lambda, router_w, router_b, exp_w_up, exp_b_up, exp_w_down, exp_b_down):
    bsz, s, d = x.shape
    assert bsz == 1 and d == D_MODEL and s % ROW_TILE == 0 and s % (max(A_Q_TILES, B_KEY_BLOCKS) * ATT_TILE) == 0
    depth = ada_w.shape[0]
    mod = _modulation(c, ada_w, ada_b)
    xs = x.reshape(s, d)
    w_up_all = exp_w_up.reshape(depth * N_EXPERTS, d, 2 * D_FF)
    b_up_all = exp_b_up.reshape(depth * N_EXPERTS, 2 * D_FF)
    w_down_all = exp_w_down.reshape(depth * N_EXPERTS, D_FF, d)
    b_down_all = exp_b_down.reshape(depth * N_EXPERTS, d)
    for l in range(depth):
        sh1, sc1, g1, sh2, sc2, g2 = [mod[l, i * d:(i + 1) * d].reshape(1, d) for i in range(6)]
        w = w_in[l]
        w_cat = jnp.concatenate([w[:, :2048], w[:, 2064:2832], w[:, 2048:2064],
                                 jnp.zeros((d, W_IN_PAD - 2832), w.dtype)], axis=1).astype(BF16)
        wg2p = jnp.zeros((LANE, LANE), F32).at[:GLA_RANK, :].set(c_w_g2[l])
        (aqT, ak, avT, bqT, bk, bvT, bkm, cq, ck, cv, cr, cg, dx, dgate) = _inproj(
            xs, norm1_g[l].reshape(1, d), sc1, sh1, w_cat,
            jnp.tile(a_q_gain[l], 2 * A_HEADS).reshape(1, -1), jnp.tile(a_k_gain[l], 2 * A_HEADS).reshape(1, -1),
            jnp.tile(b_q_gain[l], B_HEADS).reshape(1, -1), jnp.tile(b_k_gain[l], B_HEADS).reshape(1, -1),
            wg2p, c_b_g[l].reshape(1, -1))
        o_aT = _diff_attention(aqT, ak, avT, a_lam_q1[l].reshape(1, -1), a_lam_k1[l].reshape(1, -1),
                               a_lam_q2[l].reshape(1, -1), a_lam_k2[l].reshape(1, -1),
                               a_out_gain[l].reshape(-1, 1), l)
        km = bkm.reshape(s // MOBA_BLOCK, B_HEADS, B_DIM).transpose(1, 0, 2)
        o_bT = _moba(bqT, bk, bvT, km)
        o_c = _gla(cq, ck, cg, cv, cr, jnp.tile(c_out_gain[l], C_HEADS).reshape(1, -1))
        o_d = _rglru(dx, dgate, d_conv_w[l], d_conv_b[l].reshape(1, -1),
                     _block_diag(d_w_a[l]).astype(BF16), d_b_a[l].reshape(1, -1),
                     _block_diag(d_w_x[l]).astype(BF16), d_b_x[l].reshape(1, -1), d_lambda[l].reshape(1, -1))
        w4 = w_out[l].astype(BF16).reshape(4, N_GROUP_W, d)
        xs, h2, idx_pad, gates_pad = _outproj(xs, o_aT, o_bT, o_c, o_d, w4, g1, norm2_g[l].reshape(1, d),
                                              sc2, sh2, router_w[l], router_b[l].reshape(1, -1))
        xs = _moe(xs, h2, idx_pad, gates_pad, g2, w_up_all, b_up_all, w_down_all, b_down_all, l)
    return xs.reshape(bsz, s, d)
```

```python
import functools
import math

import jax
import jax.numpy as jnp
from jax import lax
from jax.experimental import pallas as pl
from jax.experimental.pallas import tpu as pltpu

F32 = jnp.float32
BF16 = jnp.bfloat16
HI = lax.Precision.HIGHEST

EPS = 1e-6
NEG = -1e30
M_INIT = -1e30
MASKED = -2e30
LOG2E = 1.4426950408889634

D_MODEL = 1024
N_GROUP_W = 256
A_HEADS, A_QK, A_V = 4, 32, 64
B_HEADS, B_DIM = 4, 64
MOBA_BLOCK, MOBA_TOPK = 256, 3
C_HEADS, C_K, C_V = 4, 32, 64
GLA_RANK, GLA_TAU = 16, 16.0
GLA_SUB = 16
D_WIDTH, D_BLOCKS, CONV_W, LRU_C = 256, 4, 4, 8.0
N_EXPERTS, TOP_K, D_FF = 32, 4, 1024
SWIGLU_ALPHA, SWIGLU_LIMIT = 1.702, 7.0

LANE = 128
SUBLANE = 8
W_IN_PAD = 2944

ROW_TILE = 512
ATT_TILE = 256
A_Q_TILES = 2
B_Q_TILES = 2
B_KEY_BLOCKS = 2
SUM_ROWS = 16
MOE_BLK = 256
MOE_TILE = 256
MOE_SORT_ROWS = MOE_TILE * TOP_K + N_EXPERTS * SUBLANE
VMEM_LIMIT = 56 * 1024 * 1024


def _cparams(sem):
    return pltpu.CompilerParams(dimension_semantics=sem, vmem_limit_bytes=VMEM_LIMIT)


def _log_sigmoid(z):
    return jnp.minimum(z, 0.0) - jnp.log1p(jnp.exp(-jnp.abs(z)))


def _sigmoid(z):
    return 1.0 / (1.0 + jnp.exp(-z))


def _split_bf16(t):
    hi = t.astype(BF16)
    return hi, (t - hi.astype(F32)).astype(BF16)


def _dot_exact_rhs(t, w_bf16, parts=2):
    out, rest = None, t
    for _ in range(parts):
        piece = rest.astype(BF16)
        term = jnp.dot(piece, w_bf16, preferred_element_type=F32)
        out = term if out is None else out + term
        rest = rest - piece.astype(F32)
    return out


def _dot_exact_lhs(w_bf16, t, parts=3):
    out, rest = None, t
    for _ in range(parts):
        piece = rest.astype(BF16)
        term = jnp.dot(w_bf16, piece, preferred_element_type=F32)
        out = term if out is None else out + term
        rest = rest - piece.astype(F32)
    return out


def _group_rms(t, group):
    n = t.shape[-1]
    shift = int(math.log2(group))
    r = lax.broadcasted_iota(jnp.int32, (n, n), 0) >> shift
    c = lax.broadcasted_iota(jnp.int32, (n, n), 1) >> shift
    bd = jnp.where(r == c, 1.0, 0.0).astype(BF16)
    ss = _dot_exact_rhs(t * t, bd)
    return t * lax.rsqrt(ss * (1.0 / group) + EPS)


def _mod_kernel(c_ref, w_ref, b_ref, o_ref):
    c = c_ref[...]
    cond = c * _sigmoid(c)
    cond8 = jnp.broadcast_to(cond, (8, cond.shape[-1]))
    o_ref[0] = jnp.dot(cond8, w_ref[0], precision=HI, preferred_element_type=F32) + b_ref[0]


def _modulation(c, ada_w, ada_b):
    depth, d, n = ada_w.shape
    tn = 1536
    out = pl.pallas_call(
        _mod_kernel,
        grid=(depth, n // tn),
        in_specs=[pl.BlockSpec((1, d), lambda l, j: (0, 0)),
                  pl.BlockSpec((1, d, tn), lambda l, j: (l, 0, j)),
                  pl.BlockSpec((1, 1, tn), lambda l, j: (l, 0, j))],
        out_specs=pl.BlockSpec((1, 8, tn), lambda l, j: (l, 0, j)),
        out_shape=jax.ShapeDtypeStruct((depth, 8, n), F32),
        compiler_params=_cparams(("arbitrary", "arbitrary")),
    )(c, ada_w, ada_b.reshape(depth, 1, n))
    return out[:, 0, :]


def _inproj_kernel(x_ref, ng_ref, sc_ref, sh_ref, w_ref, aqg_ref, akg_ref, bqg_ref, bkg_ref, wg2_ref, bg_ref,
                   aqT_ref, ak_ref, avT_ref, bqT_ref, bk_ref, bvT_ref, bkm_ref,
                   cq_ref, ck_ref, cv_ref, cr_ref, cg_ref, dx_ref, dg_ref):
    x = x_ref[...]
    tm = x.shape[0]
    h = x * lax.rsqrt(jnp.mean(x * x, axis=-1, keepdims=True) + EPS) * ng_ref[...]
    h = h * (1.0 + sc_ref[...]) + sh_ref[...]
    proj = jnp.dot(h.astype(BF16), w_ref[...], preferred_element_type=F32)

    aq = _group_rms(proj[:, 0:256], A_QK) * aqg_ref[...] * (A_QK ** -0.5 * LOG2E)
    aqT_ref[...] = aq.T.astype(BF16)
    ak = _group_rms(proj[:, 256:512], A_QK) * akg_ref[...]
    ak_ref[...] = ak.astype(BF16)
    avT_ref[...] = proj[:, 512:768].T.astype(BF16)

    bq = _group_rms(proj[:, 768:1024], B_DIM) * bqg_ref[...] * (B_DIM ** -0.5 * LOG2E)
    bqT_ref[...] = bq.T.astype(BF16)
    bk = _group_rms(proj[:, 1024:1280], B_DIM) * bkg_ref[...]
    bk_ref[...] = bk.astype(BF16)
    bvT_ref[...] = proj[:, 1280:1536].T.astype(BF16)
    bkm_ref[0] = jnp.mean(bk.reshape(tm // MOBA_BLOCK, MOBA_BLOCK, N_GROUP_W), axis=1)

    cq_ref[...] = proj[:, 1536:1664] * (C_K ** -0.5)
    ck_ref[...] = proj[:, 1664:1792]
    cv_ref[...] = proj[:, 1792:2048]
    cr_ref[...] = proj[:, 2048:2304]
    gl_hi, gl_lo = _split_bf16(proj[:, 2816:2944])
    w2_hi, w2_lo = _split_bf16(wg2_ref[...])
    z = (jnp.dot(gl_hi, w2_hi, preferred_element_type=F32) + jnp.dot(gl_lo, w2_hi, preferred_element_type=F32)
         + jnp.dot(gl_hi, w2_lo, preferred_element_type=F32)) + bg_ref[...]
    cg_ref[...] = _log_sigmoid(z) * (1.0 / GLA_TAU)

    dx_ref[...] = proj[:, 2304:2560]
    dg_ref[...] = proj[:, 2560:2816]


def _inproj(x, ng, sc, sh, w_cat, aqg, akg, bqg, bkg, wg2p, bg):
    s, d = x.shape
    tm = ROW_TILE
    nt = s // tm
    row = lambda i: (i, 0)
    const2 = lambda i: (0, 0)
    vec = pl.BlockSpec((1, d), const2)
    gvec = pl.BlockSpec((1, N_GROUP_W), const2)
    out_shape = (
        jax.ShapeDtypeStruct((N_GROUP_W, s), BF16),
        jax.ShapeDtypeStruct((s, N_GROUP_W), BF16),
        jax.ShapeDtypeStruct((N_GROUP_W, s), BF16),
        jax.ShapeDtypeStruct((N_GROUP_W, s), BF16),
        jax.ShapeDtypeStruct((s, N_GROUP_W), BF16),
        jax.ShapeDtypeStruct((N_GROUP_W, s), BF16),
        jax.ShapeDtypeStruct((nt, tm // MOBA_BLOCK, N_GROUP_W), F32),
        jax.ShapeDtypeStruct((s, 128), F32),
        jax.ShapeDtypeStruct((s, 128), F32),
        jax.ShapeDtypeStruct((s, 256), F32),
        jax.ShapeDtypeStruct((s, 256), F32),
        jax.ShapeDtypeStruct((s, 128), F32),
        jax.ShapeDtypeStruct((s, 256), F32),
        jax.ShapeDtypeStruct((s, 256), F32),
    )
    out_specs = (
        pl.BlockSpec((N_GROUP_W, tm), lambda i: (0, i)),
        pl.BlockSpec((tm, N_GROUP_W), row),
        pl.BlockSpec((N_GROUP_W, tm), lambda i: (0, i)),
        pl.BlockSpec((N_GROUP_W, tm), lambda i: (0, i)),
        pl.BlockSpec((tm, N_GROUP_W), row),
        pl.BlockSpec((N_GROUP_W, tm), lambda i: (0, i)),
        pl.BlockSpec((1, tm // MOBA_BLOCK, N_GROUP_W), lambda i: (i, 0, 0)),
        pl.BlockSpec((tm, 128), row),
        pl.BlockSpec((tm, 128), row),
        pl.BlockSpec((tm, 256), row),
        pl.BlockSpec((tm, 256), row),
        pl.BlockSpec((tm, 128), row),
        pl.BlockSpec((tm, 256), row),
        pl.BlockSpec((tm, 256), row),
    )
    return pl.pallas_call(
        _inproj_kernel,
        grid=(nt,),
        in_specs=[pl.BlockSpec((tm, d), row), vec, vec, vec,
                  pl.BlockSpec((d, W_IN_PAD), const2),
                  gvec, gvec, gvec, gvec,
                  pl.BlockSpec((LANE, LANE), const2),
                  pl.BlockSpec((1, LANE), const2)],
        out_specs=out_specs,
        out_shape=out_shape,
        compiler_params=_cparams(("arbitrary",)),
    )(x, ng, sc, sh, w_cat, aqg, akg, bqg, bkg, wg2p, bg)


def _pad_query_rows(qT_ref, qpad_ref, width):
    per = LANE // width
    n_groups = qT_ref.shape[0] // width
    t = qpad_ref.shape[2]
    rows = lax.broadcasted_iota(jnp.int32, qpad_ref.shape[1:], 0)
    for c in range(qpad_ref.shape[0]):
        g, cols = c % n_groups, slice(t * (c // n_groups), t * (c // n_groups + 1))
        blk = qT_ref[LANE * (g // per):LANE * (g // per + 1), cols]
        lo = width * (g % per)
        qpad_ref[c] = jnp.where((rows >= lo) & (rows < lo + width), blk, jnp.zeros_like(blk))


def _with_sum_rows(vT):
    return jnp.concatenate([vT, jnp.ones((SUM_ROWS, vT.shape[1]), vT.dtype)], axis=0)


def _softmax_tiles(k_slabs, qpad_ref, v_tiles, adjust, m_ref, acc_ref):
    n = len(k_slabs)
    scores = [jnp.dot(k_slabs[g], qpad_ref[g], preferred_element_type=F32) for g in range(n)]
    probs, alphas = [], []
    for g in range(n):
        sT = adjust(g, scores[g])
        m_old = m_ref[g]
        m_new = jnp.maximum(m_old, jnp.max(sT, axis=0, keepdims=True))
        alphas.append(jnp.exp2(m_old - m_new))
        probs.append(jnp.exp2(sT - m_new).astype(BF16))
        m_ref[g] = m_new
    pv = [jnp.dot(v_tiles[g], probs[g], preferred_element_type=F32) for g in range(n)]
    for g in range(n):
        acc_ref[g] = alphas[g] * acc_ref[g] + pv[g]


def _diffattn_kernel(qT_ref, k_ref, vT_ref, lq1_ref, lk1_ref, lq2_ref, lk2_ref, og_ref, o_ref,
                     m_ref, acc_ref, qpad_ref, *, lam_init):
    t, tk = ATT_TILE, A_Q_TILES * ATT_TILE
    n_maps = 2 * A_HEADS
    n_chains = A_Q_TILES * n_maps
    step = pl.program_id(0)
    lam = (jnp.exp(jnp.sum(lq1_ref[...] * lk1_ref[...], axis=-1, keepdims=True))
           - jnp.exp(jnp.sum(lq2_ref[...] * lk2_ref[...], axis=-1, keepdims=True)) + lam_init)
    m_ref[...] = jnp.full(m_ref.shape, M_INIT, F32)
    acc_ref[...] = jnp.zeros(acc_ref.shape, F32)
    _pad_query_rows(qT_ref, qpad_ref, A_QK)

    def tile(ik, adjust):
        keys = pl.ds(pl.multiple_of(ik * tk, tk), tk)
        slabs = [k_ref[keys, LANE * j:LANE * (j + 1)] for j in range(N_GROUP_W // LANE)]
        v_heads = [_with_sum_rows(vT_ref[A_V * hd:A_V * (hd + 1), keys]) for hd in range(A_HEADS)]
        _softmax_tiles([slabs[(c % n_maps) // 4] for c in range(n_chains)], qpad_ref,
                       [v_heads[(c % n_maps) // 2] for c in range(n_chains)], adjust, m_ref, acc_ref)

    def body(ik, carry):
        tile(ik, lambda c, sT: sT)
        return carry

    lax.fori_loop(0, step, body, 0)
    diff = lax.broadcasted_iota(jnp.int32, (tk, t), 0) - lax.broadcasted_iota(jnp.int32, (tk, t), 1)
    tile(step, lambda c, sT: jnp.where(diff <= (c // n_maps) * t, sT, MASKED))

    for sub in range(A_Q_TILES):
        for hd in range(A_HEADS):
            a1, a2 = acc_ref[sub * n_maps + 2 * hd], acc_ref[sub * n_maps + 2 * hd + 1]
            o = a1[:A_V] / a1[A_V:A_V + 1] - lam * (a2[:A_V] / a2[A_V:A_V + 1])
            o = o * lax.rsqrt(jnp.mean(o * o, axis=0, keepdims=True) + EPS) * og_ref[...]
            o_ref[A_V * hd:A_V * (hd + 1), t * sub:t * (sub + 1)] = o * (1.0 - lam_init)


def _diff_attention(aqT, ak, avT, lq1, lk1, lq2, lk2, out_gain_col, layer_idx):
    s = aqT.shape[1]
    t = ATT_TILE
    tq = A_Q_TILES * ATT_TILE
    n_maps = A_Q_TILES * 2 * A_HEADS
    lam_init = 0.8 - 0.6 * math.exp(-0.3 * layer_idx)
    vec = pl.BlockSpec((1, A_QK), lambda i: (0, 0))
    return pl.pallas_call(
        functools.partial(_diffattn_kernel, lam_init=lam_init),
        grid=(s // tq,),
        in_specs=[pl.BlockSpec((N_GROUP_W, tq), lambda i: (0, i)),
                  pl.BlockSpec((s, N_GROUP_W), lambda i: (0, 0)),
                  pl.BlockSpec((N_GROUP_W, s), lambda i: (0, 0)),
                  vec, vec, vec, vec,
                  pl.BlockSpec((A_V, 1), lambda i: (0, 0))],
        out_specs=pl.BlockSpec((N_GROUP_W, tq), lambda i: (0, i)),
        out_shape=jax.ShapeDtypeStruct((N_GROUP_W, s), F32),
        scratch_shapes=[pltpu.VMEM((n_maps, 1, t), F32), pltpu.VMEM((n_maps, A_V + SUM_ROWS, t), F32),
                        pltpu.VMEM((n_maps, LANE, t), BF16)],
        compiler_params=_cparams(("arbitrary",)),
    )(aqT, ak, avT, lq1, lk1, lq2, lk2, out_gain_col)


def _moba_kernel(qT_ref, k_ref, vT_ref, km_ref, o_ref, m_ref, acc_ref, bias_ref, qpad_ref):
    KEY_BLOCKS = B_KEY_BLOCKS
    t, tk = ATT_TILE, KEY_BLOCKS * ATT_TILE
    n_chains = B_Q_TILES * B_HEADS
    first_own = pl.program_id(0) * B_Q_TILES
    nb = km_ref.shape[1]
    m_ref[...] = jnp.full(m_ref.shape, M_INIT, F32)
    acc_ref[...] = jnp.zeros(acc_ref.shape, F32)
    _pad_query_rows(qT_ref, qpad_ref, B_DIM)

    blk = lax.broadcasted_iota(jnp.int32, (nb, t), 0)
    for c in range(n_chains):
        sub, hd = c // B_HEADS, c % B_HEADS
        earlier = blk < first_own + sub
        qT = qT_ref[B_DIM * hd:B_DIM * (hd + 1), t * sub:t * (sub + 1)].astype(F32)
        gate = jnp.dot(km_ref[hd], qT, precision=HI, preferred_element_type=F32)
        gate = jnp.where(earlier, gate, NEG)
        sel_bias = jnp.full((nb, t), MASKED, F32)
        for _ in range(MOBA_TOPK):
            top = jnp.max(gate, axis=0, keepdims=True)
            pick = blk == jnp.min(jnp.where(gate == top, blk, nb), axis=0, keepdims=True)
            sel_bias = jnp.where(pick, jnp.where(earlier, 0.0, MASKED), sel_bias)
            gate = jnp.where(pick, -jnp.inf, gate)
        bias_ref[c] = sel_bias

    def tile(j, adjust):
        keys = pl.ds(pl.multiple_of(j * tk, tk), tk)
        slabs = [k_ref[keys, LANE * i:LANE * (i + 1)] for i in range(N_GROUP_W // LANE)]
        v_heads = [_with_sum_rows(vT_ref[B_DIM * hd:B_DIM * (hd + 1), keys]) for hd in range(B_HEADS)]
        _softmax_tiles([slabs[(c % B_HEADS) // 2] for c in range(n_chains)], qpad_ref,
                       [v_heads[c % B_HEADS] for c in range(n_chains)], adjust, m_ref, acc_ref)

    def block_bias(c, b):
        return bias_ref[c, pl.ds(b, 1), :]

    def body(j, carry):
        tile(j, lambda c, sT: jnp.concatenate(
            [sT[t * part:t * (part + 1)] + block_bias(c, KEY_BLOCKS * j + part) for part in range(KEY_BLOCKS)],
            axis=0))
        return carry

    n_full = first_own // KEY_BLOCKS
    lax.fori_loop(0, n_full, body, 0)

    causal = (lax.broadcasted_iota(jnp.int32, (t, t), 0) <= lax.broadcasted_iota(jnp.int32, (t, t), 1))

    def last(c, sT):
        own = first_own + c // B_HEADS
        parts = []
        for part in range(KEY_BLOCKS):
            b = KEY_BLOCKS * n_full + part
            sp = sT[t * part:t * (part + 1)]
            other = sp + jnp.where(b < own, block_bias(c, b), MASKED)
            parts.append(jnp.where(b == own, jnp.where(causal, sp, MASKED), other))
        return jnp.concatenate(parts, axis=0)

    tile(n_full, last)
    for c in range(n_chains):
        sub, hd = c // B_HEADS, c % B_HEADS
        a = acc_ref[c]
        o_ref[B_DIM * hd:B_DIM * (hd + 1), t * sub:t * (sub + 1)] = a[:B_DIM] / a[B_DIM:B_DIM + 1]


def _moba(bqT, bk, bvT, km):
    assert B_KEY_BLOCKS % B_Q_TILES == 0
    s = bqT.shape[1]
    t = ATT_TILE
    tq = B_Q_TILES * ATT_TILE
    n_chains = B_Q_TILES * B_HEADS
    nb = km.shape[1]
    return pl.pallas_call(
        _moba_kernel,
        grid=(s // tq,),
        in_specs=[pl.BlockSpec((N_GROUP_W, tq), lambda i: (0, i)),
                  pl.BlockSpec((s, N_GROUP_W), lambda i: (0, 0)),
                  pl.BlockSpec((N_GROUP_W, s), lambda i: (0, 0)),
                  pl.BlockSpec((B_HEADS, nb, B_DIM), lambda i: (0, 0, 0))],
        out_specs=pl.BlockSpec((N_GROUP_W, tq), lambda i: (0, i)),
        out_shape=jax.ShapeDtypeStruct((N_GROUP_W, s), F32),
        scratch_shapes=[pltpu.VMEM((n_chains, 1, t), F32),
                        pltpu.VMEM((n_chains, B_DIM + SUM_ROWS, t), F32), pltpu.VMEM((n_chains, nb, t), F32),
                        pltpu.VMEM((n_chains, LANE, t), BF16)],
        compiler_params=_cparams(("arbitrary",)),
    )(bqT, bk, bvT, km)


def _gla_kernel(q_ref, k_ref, g_ref, v_ref, r_ref, og_ref, o_ref, stateT_ref, kbuf_ref, bbuf_ref, vbuf_ref):
    tm = q_ref.shape[0]
    c = GLA_SUB
    shift = int(math.log2(c))

    @pl.when(pl.program_id(0) == 0)
    def _():
        stateT_ref[...] = jnp.zeros(stateT_ref.shape, F32)

    q, k, v = q_ref[...], k_ref[...], v_ref[...]
    ri = lax.broadcasted_iota(jnp.int32, (tm, tm), 0)
    ci = lax.broadcasted_iota(jnp.int32, (tm, tm), 1)
    same = (ri >> shift) == (ci >> shift)
    g = g_ref[...]
    b = _dot_exact_lhs(jnp.where(same & (ri >= ci), 1.0, 0.0).astype(BF16), g)
    b_last = _dot_exact_lhs(jnp.where(same, 1.0, 0.0).astype(BF16), g)

    k_shift, v_shift = int(math.log2(C_K)), int(math.log2(C_V))
    head_sum = jnp.where((lax.broadcasted_iota(jnp.int32, (C_HEADS * C_K, C_HEADS * C_V), 0) >> k_shift)
                         == (lax.broadcasted_iota(jnp.int32, (C_HEADS * C_K, C_HEADS * C_V), 1) >> v_shift),
                         1.0, 0.0).astype(BF16)

    for buf, val in ((kbuf_ref, k), (bbuf_ref, b), (vbuf_ref, v)):
        buf[0:c, :] = jnp.zeros((c, val.shape[1]), F32)
        buf[c:c + tm, :] = val
    pos = lax.broadcasted_iota(jnp.int32, (tm, C_HEADS * C_K), 0) & (c - 1)
    o = jnp.zeros((tm, C_HEADS * C_V), F32)
    for delta in range(c):
        k_d = kbuf_ref[c - delta:c - delta + tm, :]
        b_d = bbuf_ref[c - delta:c - delta + tm, :]
        v_d = vbuf_ref[c - delta:c - delta + tm, :]
        w = q * k_d * jnp.exp(jnp.where(pos >= delta, b - b_d, NEG))
        w_hi, w_lo = _split_bf16(w)
        att = (jnp.dot(w_hi, head_sum, preferred_element_type=F32)
               + jnp.dot(w_lo, head_sum, preferred_element_type=F32))
        o = o + att * v_d

    q_dec = (q * jnp.exp(b)).astype(BF16)
    k_dec = (k * jnp.exp(b_last - b)).astype(BF16)
    decay = jnp.exp(b_last)
    v_bf = v.astype(BF16)
    own_head = ((lax.broadcasted_iota(jnp.int32, (C_HEADS * C_V, C_HEADS * C_K), 0) >> v_shift)
                == (lax.broadcasted_iota(jnp.int32, (C_HEADS * C_V, C_HEADS * C_K), 1) >> k_shift))
    nt_dims = (((1,), (1,)), ((), ()))
    tn_dims = (((0,), (0,)), ((), ()))
    updates = [lax.dot_general(v_bf[c * i:c * (i + 1)], k_dec[c * i:c * (i + 1)], tn_dims,
                               preferred_element_type=F32) for i in range(tm // c)]
    stateT = stateT_ref[...]
    far = []
    for i in range(tm // c):
        far.append(lax.dot_general(q_dec[c * i:c * (i + 1)], stateT.astype(BF16), nt_dims,
                                   preferred_element_type=F32))
        stateT = stateT * decay[c * i:c * i + 1, :] + jnp.where(own_head, updates[i], 0.0)
    stateT_ref[...] = stateT
    o = o + jnp.concatenate(far, axis=0)

    r = r_ref[...]
    o_ref[...] = _group_rms(o, C_V) * og_ref[...] * (r * _sigmoid(r))


def _gla(cq, ck, cg, cv, cr, out_gain):
    s = cq.shape[0]
    tm = ROW_TILE
    row = lambda i: (i, 0)
    return pl.pallas_call(
        _gla_kernel,
        grid=(s // tm,),
        in_specs=[pl.BlockSpec((tm, 128), row), pl.BlockSpec((tm, 128), row), pl.BlockSpec((tm, 128), row),
                  pl.BlockSpec((tm, 256), row), pl.BlockSpec((tm, 256), row),
                  pl.BlockSpec((1, C_HEADS * C_V), lambda i: (0, 0))],
        out_specs=pl.BlockSpec((tm, 256), row),
        out_shape=jax.ShapeDtypeStruct((s, 256), F32),
        scratch_shapes=[pltpu.VMEM((C_HEADS * C_V, C_HEADS * C_K), F32),
                        pltpu.VMEM((tm + GLA_SUB, C_HEADS * C_K), F32),
                        pltpu.VMEM((tm + GLA_SUB, C_HEADS * C_K), F32),
                        pltpu.VMEM((tm + GLA_SUB, C_HEADS * C_V), F32)],
        compiler_params=_cparams(("arbitrary",)),
    )(cq, ck, cg, cv, cr, out_gain)


def _rglru_kernel(x_ref, gate_ref, cw_ref, cb_ref, wa_ref, ba_ref, wx_ref, bx_ref, lam_ref, o_ref,
                  xbuf_ref, h_ref):
    tm = x_ref.shape[0]
    pad = 8

    @pl.when(pl.program_id(0) == 0)
    def _():
        xbuf_ref[0:pad, :] = jnp.zeros((pad, D_WIDTH), F32)
        h_ref[...] = jnp.zeros(h_ref.shape, F32)

    xbuf_ref[pad:pad + tm, :] = x_ref[...]
    xc = cb_ref[...] + jnp.zeros((tm, D_WIDTH), F32)
    for w in range(CONV_W):
        lag = CONV_W - 1 - w
        xc = xc + cw_ref[w:w + 1, :] * xbuf_ref[pad - lag:pad - lag + tm, :]
    xbuf_ref[0:pad, :] = xbuf_ref[tm:tm + pad, :]

    xcb = xc.astype(BF16)
    r_gate = _sigmoid(jnp.dot(xcb, wa_ref[...], preferred_element_type=F32) + ba_ref[...])
    i_gate = _sigmoid(jnp.dot(xcb, wx_ref[...], preferred_element_type=F32) + bx_ref[...])
    log_a = LRU_C * r_gate * _log_sigmoid(lam_ref[...])
    a = jnp.exp(log_a)
    b = jnp.sqrt(1.0 - jnp.exp(2.0 * log_a)) * (i_gate * xc)

    rows = lax.broadcasted_iota(jnp.int32, (tm, D_WIDTH), 0)
    step = 1
    while step < tm:
        keep = rows >= step
        a_prev = jnp.where(keep, pltpu.roll(a, step, 0), 1.0)
        b_prev = jnp.where(keep, pltpu.roll(b, step, 0), 0.0)
        b = a * b_prev + b
        a = a * a_prev
        step *= 2
    h = a * h_ref[0:1, :] + b
    h_ref[...] = jnp.broadcast_to(h[tm - 1:tm, :], h_ref.shape)

    g = gate_ref[...]
    gelu = g * (0.5 * (1.0 + jnp.tanh(math.sqrt(2.0 / math.pi) * (g + 0.044715 * (g * g * g)))))
    o_ref[...] = h * gelu


def _rglru(dx, dgate, conv_w, conv_b, wa_bd, b_a, wx_bd, b_x, lam):
    s = dx.shape[0]
    tm = ROW_TILE
    row = lambda i: (i, 0)
    const2 = lambda i: (0, 0)
    vec = pl.BlockSpec((1, D_WIDTH), const2)
    mat = pl.BlockSpec((D_WIDTH, D_WIDTH), const2)
    return pl.pallas_call(
        _rglru_kernel,
        grid=(s // tm,),
        in_specs=[pl.BlockSpec((tm, D_WIDTH), row), pl.BlockSpec((tm, D_WIDTH), row),
                  pl.BlockSpec((CONV_W, D_WIDTH), const2), vec, mat, vec, mat, vec, vec],
        out_specs=pl.BlockSpec((tm, D_WIDTH), row),
        out_shape=jax.ShapeDtypeStruct((s, D_WIDTH), F32),
        scratch_shapes=[pltpu.VMEM((tm + 8, D_WIDTH), F32), pltpu.VMEM((8, D_WIDTH), F32)],
        compiler_params=_cparams(("arbitrary",)),
    )(dx, dgate, conv_w, conv_b, wa_bd, b_a, wx_bd, b_x, lam)


def _outproj_kernel(x_ref, oaT_ref, obT_ref, oc_ref, od_ref, w_ref, g1_ref, ng_ref, sc_ref, sh_ref,
                    rw_ref, rb_ref, xo_ref, h_ref, idx_ref, gate_ref):
    tm = x_ref.shape[0]
    y = jnp.dot(oaT_ref[...].T.astype(BF16), w_ref[0], preferred_element_type=F32)
    y = y + jnp.dot(obT_ref[...].T.astype(BF16), w_ref[1], preferred_element_type=F32)
    y = y + jnp.dot(oc_ref[...].astype(BF16), w_ref[2], preferred_element_type=F32)
    y = y + jnp.dot(od_ref[...].astype(BF16), w_ref[3], preferred_element_type=F32)
    x = x_ref[...] + g1_ref[...] * y
    xo_ref[...] = x
    h = x * lax.rsqrt(jnp.mean(x * x, axis=-1, keepdims=True) + EPS) * ng_ref[...]
    h = h * (1.0 + sc_ref[...]) + sh_ref[...]
    h_ref[...] = h

    h_hi, h_lo = _split_bf16(h)
    w_hi, w_lo = _split_bf16(rw_ref[...])
    logits = (jnp.dot(h_hi, w_hi, preferred_element_type=F32) + jnp.dot(h_lo, w_hi, preferred_element_type=F32)
              + jnp.dot(h_hi, w_lo, preferred_element_type=F32)) + rb_ref[...]
    e_ids = lax.broadcasted_iota(jnp.int32, (tm, N_EXPERTS), 1)
    lane = lax.broadcasted_iota(jnp.int32, (tm, LANE), 1)
    idx_out = jnp.zeros((tm, LANE), jnp.int32)
    val_out = jnp.zeros((tm, LANE), F32)
    top0 = None
    denom = jnp.zeros((tm, 1), F32)
    for k in range(TOP_K):
        top = jnp.max(logits, axis=-1, keepdims=True)
        arg = jnp.min(jnp.where(logits == top, e_ids, N_EXPERTS), axis=-1, keepdims=True)
        if k == 0:
            top0 = top
        w = jnp.exp(top - top0)
        denom = denom + w
        idx_out = jnp.where(lane == k, arg, idx_out)
        val_out = jnp.where(lane == k, w, val_out)
        logits = jnp.where(e_ids == arg, -jnp.inf, logits)
    idx_ref[...] = idx_out
    gate_ref[...] = val_out / denom


def _outproj(x, oaT, obT, oc, od, w4, g1, ng, sc, sh, rw, rb):
    s, d = x.shape
    tm = ROW_TILE
    row = lambda i: (i, 0)
    col = lambda i: (0, i)
    const2 = lambda i: (0, 0)
    vec = pl.BlockSpec((1, d), const2)
    return pl.pallas_call(
        _outproj_kernel,
        grid=(s // tm,),
        in_specs=[pl.BlockSpec((tm, d), row),
                  pl.BlockSpec((N_GROUP_W, tm), col), pl.BlockSpec((N_GROUP_W, tm), col),
                  pl.BlockSpec((tm, 256), row), pl.BlockSpec((tm, 256), row),
                  pl.BlockSpec((4, N_GROUP_W, d), lambda i: (0, 0, 0)),
                  vec, vec, vec, vec,
                  pl.BlockSpec((d, N_EXPERTS), const2), pl.BlockSpec((1, N_EXPERTS), const2)],
        out_specs=(pl.BlockSpec((tm, d), row), pl.BlockSpec((tm, d), row),
                   pl.BlockSpec((tm, LANE), row), pl.BlockSpec((tm, LANE), row)),
        out_shape=(jax.ShapeDtypeStruct((s, d), F32), jax.ShapeDtypeStruct((s, d), F32),
                   jax.ShapeDtypeStruct((s, LANE), jnp.int32), jax.ShapeDtypeStruct((s, LANE), F32)),
        compiler_params=_cparams(("arbitrary",)),
    )(x, oaT, obT, oc, od, w4, g1, ng, sc, sh, rw, rb)


def _segment_copies(tile, cnt_ref, seg_ref, off_ref, make_copy, start):
    def per_expert(e, carry):
        j = tile * N_EXPERTS + e
        cnt, a, b = cnt_ref[j], seg_ref[j], off_ref[j]
        size = MOE_TILE
        while size >= SUBLANE:
            bit = cnt & size

            @pl.when(bit != 0)
            def _():
                cp = make_copy(pl.multiple_of(a, SUBLANE), pl.multiple_of(b, SUBLANE), size)
                if start:
                    cp.start()
                else:
                    cp.wait()

            a, b = a + bit, b + bit
            size //= 2
        return carry

    lax.fori_loop(0, N_EXPERTS, per_expert, 0)


def _segment_wait(tile, cnt_ref, seg_ref, make_copy):
    j = tile * N_EXPERTS + N_EXPERTS - 1
    total = seg_ref[j] + cnt_ref[j]
    size = 1 << (MOE_SORT_ROWS.bit_length() - 1)
    while size >= SUBLANE:
        @pl.when((total & size) != 0)
        def _():
            make_copy(0, 0, size).wait()

        size //= 2


def _dispatch_kernel(cnt_ref, seg_ref, off_ref, nused_ref, h_ref, slotT_ref, rows_ref, sorted_ref, zero_ref, sem):
    tile = pl.program_id(0)
    tt = h_ref.shape[0]
    n = sorted_ref.shape[1]
    par = tile & 1
    r_ids = lax.broadcasted_iota(jnp.int32, (n, tt), 0)
    perm = jnp.where(r_ids == slotT_ref[0:1, :], 1.0, 0.0)
    for k in range(1, TOP_K):
        perm = perm + jnp.where(r_ids == slotT_ref[k:k + 1, :], 1.0, 0.0)
    perm = perm.astype(BF16)
    sorted_ref[par] = jnp.dot(perm, h_ref[...].astype(BF16), preferred_element_type=F32)

    def copies_from(p):
        def make_copy(a, b, size):
            return pltpu.make_async_copy(sorted_ref.at[p, pl.ds(a, size)], rows_ref.at[pl.ds(b, size)], sem.at[p])
        return make_copy

    _segment_copies(tile, cnt_ref, seg_ref, off_ref, copies_from(par), True)

    @pl.when(tile > 0)
    def _():
        _segment_wait(tile - 1, cnt_ref, seg_ref, copies_from(1 - par))

    last = pl.num_programs(0) - 1

    @pl.when(tile == last)
    def _():
        _segment_wait(tile, cnt_ref, seg_ref, copies_from(par))
        zero_ref[...] = jnp.zeros(zero_ref.shape, F32)

        def make_zero_copy(a, b, size):
            return pltpu.make_async_copy(zero_ref.at[pl.ds(a, size)], rows_ref.at[pl.ds(b, size)], sem.at[0])

        _segment_copies(last + 1, cnt_ref, seg_ref, off_ref, make_zero_copy, True)
        _segment_copies(last + 1, cnt_ref, seg_ref, off_ref, make_zero_copy, False)

        def block_copy(b):
            return pltpu.make_async_copy(zero_ref, rows_ref.at[pl.ds(pl.multiple_of(b * MOE_BLK, MOE_BLK), MOE_BLK)],
                                         sem.at[0])

        def issue(b, carry):
            block_copy(b).start()
            return carry

        def drain(b, carry):
            block_copy(b).wait()
            return carry

        n_blocks = rows_ref.shape[0] // MOE_BLK
        lax.fori_loop(nused_ref[0], n_blocks, issue, 0)
        lax.fori_loop(nused_ref[0], n_blocks, drain, 0)


def _dispatch(h, slotT, tile_cnt, tile_seg, tile_off, n_used, n_rows):
    s, d = h.shape
    tt = MOE_TILE
    grid_spec = pltpu.PrefetchScalarGridSpec(
        num_scalar_prefetch=4,
        grid=(s // tt,),
        in_specs=[pl.BlockSpec((tt, d), lambda i, c, sg, of, nu: (i, 0)),
                  pl.BlockSpec((TOP_K, tt), lambda i, c, sg, of, nu: (0, i))],
        out_specs=pl.BlockSpec(memory_space=pl.ANY),
        scratch_shapes=[pltpu.VMEM((2, MOE_SORT_ROWS, d), F32), pltpu.VMEM((MOE_BLK, d), F32),
                        pltpu.SemaphoreType.DMA((2,))],
    )
    return pl.pallas_call(
        _dispatch_kernel,
        grid_spec=grid_spec,
        out_shape=jax.ShapeDtypeStruct((n_rows, d), F32),
        compiler_params=_cparams(("arbitrary",)),
    )(tile_cnt, tile_seg, tile_off, n_used, h, slotT)


def _ffn_kernel(be_ref, first_ref, next_ref, slot_ref, nused_ref, x_ref, wu_hbm, bu_ref, wd_hbm, bd_ref, y_ref,
                wu_f32, wd_f32, wu_bf, wd_bf, sem):
    b = pl.program_id(0)

    def weight_copies(expert, slot):
        return (pltpu.make_async_copy(wu_hbm.at[expert], wu_f32.at[slot], sem.at[0, slot]),
                pltpu.make_async_copy(wd_hbm.at[expert], wd_f32.at[slot], sem.at[1, slot]))

    @pl.when(b == 0)
    def _():
        for cp in weight_copies(be_ref[0], 0):
            cp.start()

    @pl.when(first_ref[b] == 1)
    def _():
        slot = slot_ref[b]
        for cp in weight_copies(be_ref[b], slot):
            cp.wait()

        @pl.when(next_ref[b] >= 0)
        def _():
            for cp in weight_copies(next_ref[b], 1 - slot):
                cp.start()

        wu_bf[...] = wu_f32[slot].astype(BF16)
        wd_bf[...] = wd_f32[slot].astype(BF16)

    @pl.when(b < nused_ref[0])
    def _():
        hu = jnp.dot(x_ref[...].astype(BF16), wu_bf[...], preferred_element_type=F32) + bu_ref[0]
        g = jnp.minimum(hu[:, :D_FF], SWIGLU_LIMIT)
        lin = jnp.clip(hu[:, D_FF:], -SWIGLU_LIMIT, SWIGLU_LIMIT)
        act = (lin + 1.0) * (g * _sigmoid(SWIGLU_ALPHA * g))
        y_ref[...] = jnp.dot(act.astype(BF16), wd_bf[...], preferred_element_type=F32) + bd_ref[0]

    @pl.when(b >= nused_ref[0])
    def _():
        y_ref[...] = jnp.zeros(y_ref.shape, F32)


def _ffn(x_rows, blk_expert, first, n_used, w_up, b_up, w_down, b_down):
    n_rows, d = x_rows.shape
    blk = MOE_BLK
    n_blocks = n_rows // blk
    n_exp, _, f2 = w_up.shape
    i32 = jnp.int32
    ids = jnp.arange(n_blocks, dtype=i32)
    later_start = (ids[None, :] > ids[:, None]) & (first[None, :] == 1)
    next_pos = jnp.min(jnp.where(later_start, ids[None, :], n_blocks), axis=1)
    next_expert = jnp.where(next_pos < n_blocks, blk_expert[jnp.minimum(next_pos, n_blocks - 1)], -1).astype(i32)
    slot = ((jnp.cumsum(first) - 1) & 1).astype(i32)
    idx = lambda b, be, fi, nx, sl, nu: (be[b], 0, 0)
    grid_spec = pltpu.PrefetchScalarGridSpec(
        num_scalar_prefetch=5,
        grid=(n_blocks,),
        in_specs=[pl.BlockSpec((blk, d), lambda b, be, fi, nx, sl, nu: (jnp.minimum(b, nu[0] - 1), 0)),
                  pl.BlockSpec(memory_space=pl.ANY),
                  pl.BlockSpec((1, 1, f2), idx),
                  pl.BlockSpec(memory_space=pl.ANY),
                  pl.BlockSpec((1, 1, d), idx)],
        out_specs=pl.BlockSpec((blk, d), lambda b, be, fi, nx, sl, nu: (b, 0)),
        scratch_shapes=[pltpu.VMEM((2, d, f2), F32), pltpu.VMEM((2, D_FF, d), F32),
                        pltpu.VMEM((d, f2), BF16), pltpu.VMEM((D_FF, d), BF16),
                        pltpu.SemaphoreType.DMA((2, 2))],
    )
    return pl.pallas_call(
        _ffn_kernel,
        grid_spec=grid_spec,
        out_shape=jax.ShapeDtypeStruct((n_rows, d), F32),
        compiler_params=_cparams(("arbitrary",)),
    )(blk_expert, first, next_expert, slot, n_used, x_rows, w_up, b_up.reshape(n_exp, 1, f2),
      w_down, b_down.reshape(n_exp, 1, d))


def _combine_kernel(cnt_ref, seg_ref, off_ref, x_ref, slot_ref, gate_ref, g2_ref, rows_ref, o_ref, buf_ref, sem):
    tile = pl.program_id(0)
    tt = x_ref.shape[0]
    n = buf_ref.shape[1]
    par = tile & 1

    def copies_into(p):
        def make_copy(a, b, size):
            return pltpu.make_async_copy(rows_ref.at[pl.ds(b, size)], buf_ref.at[p, pl.ds(a, size)], sem.at[p])
        return make_copy

    @pl.when(tile == 0)
    def _():
        buf_ref[...] = jnp.zeros(buf_ref.shape, F32)
        _segment_copies(tile, cnt_ref, seg_ref, off_ref, copies_into(par), True)

    @pl.when(tile + 1 < pl.num_programs(0))
    def _():
        _segment_copies(tile + 1, cnt_ref, seg_ref, off_ref, copies_into(1 - par), True)

    slots, gates = slot_ref[...], gate_ref[...]
    r_ids = lax.broadcasted_iota(jnp.int32, (tt, n), 1)
    sel = jnp.where(r_ids == slots[:, 0:1], gates[:, 0:1], 0.0)
    for k in range(1, TOP_K):
        sel = sel + jnp.where(r_ids == slots[:, k:k + 1], gates[:, k:k + 1], 0.0)
    sel_hi, sel_lo = _split_bf16(sel)
    _segment_wait(tile, cnt_ref, seg_ref, copies_into(par))
    y_hi, y_lo = _split_bf16(buf_ref[par])
    both = jnp.dot(jnp.concatenate([sel_hi, sel_lo], axis=0), y_hi, preferred_element_type=F32)
    y = both[:tt] + both[tt:] + jnp.dot(sel_hi, y_lo, preferred_element_type=F32)
    o_ref[...] = x_ref[...] + g2_ref[...] * y


def _combine(x, slots, gates, g2, y_rows, tile_cnt, tile_seg, tile_off):
    s, d = x.shape
    tt = MOE_TILE
    row = lambda i, c, sg, of: (i, 0)
    grid_spec = pltpu.PrefetchScalarGridSpec(
        num_scalar_prefetch=3,
        grid=(s // tt,),
        in_specs=[pl.BlockSpec((tt, d), row), pl.BlockSpec((tt, LANE), row), pl.BlockSpec((tt, LANE), row),
                  pl.BlockSpec((1, d), lambda i, c, sg, of: (0, 0)),
                  pl.BlockSpec(memory_space=pl.ANY)],
        out_specs=pl.BlockSpec((tt, d), row),
        scratch_shapes=[pltpu.VMEM((2, MOE_SORT_ROWS, d), F32), pltpu.SemaphoreType.DMA((2,))],
    )
    return pl.pallas_call(
        _combine_kernel,
        grid_spec=grid_spec,
        out_shape=jax.ShapeDtypeStruct((s, d), F32),
        compiler_params=_cparams(("arbitrary",)),
    )(tile_cnt, tile_seg, tile_off, x, slots, gates, g2, y_rows)


def _moe(x, h, idx_pad, gates_pad, g2, w_up, b_up, w_down, b_down, layer_idx):
    s, d = x.shape
    n = s * TOP_K
    blk = MOE_BLK
    tt = MOE_TILE
    nt = s // tt
    n_blocks = -(-(n + nt * N_EXPERTS * SUBLANE) // blk) + N_EXPERTS
    i32 = jnp.int32
    idx = idx_pad[:, :TOP_K]
    hot = (idx[:, :, None] == jnp.arange(N_EXPERTS, dtype=i32)).astype(F32)
    onehot_t = jnp.sum(hot, axis=1).reshape(nt, tt, N_EXPERTS)
    earlier = jnp.tril(jnp.ones((tt, tt), F32), -1)
    rank_in_tile = jnp.einsum('ij,tjk->tik', earlier, onehot_t, precision=HI).reshape(s, N_EXPERTS)
    tile_cnt = (jnp.sum(onehot_t, axis=1).astype(i32) + SUBLANE - 1) // SUBLANE * SUBLANE
    tile_base = jnp.cumsum(tile_cnt, axis=0) - tile_cnt
    tile_seg = jnp.cumsum(tile_cnt, axis=1) - tile_cnt
    counts = jnp.sum(tile_cnt, axis=0)
    padded = (counts + blk - 1) // blk * blk
    pad_ends = jnp.cumsum(padded)
    pad_starts = pad_ends - padded
    tile_off = pad_starts[None, :] + tile_base
    slot_by_expert = jnp.repeat(tile_seg, tt, axis=0).astype(F32) + rank_in_tile
    slot = jnp.sum(hot * slot_by_expert[:, None, :], axis=2).astype(i32)
    slot_pad = jnp.pad(slot, ((0, 0), (0, LANE - TOP_K)))
    blk_start = jnp.arange(n_blocks, dtype=i32) * blk
    blk_expert = jnp.minimum(jnp.sum((pad_ends[None, :] <= blk_start[:, None]).astype(i32), axis=1),
                             N_EXPERTS - 1).astype(i32)
    first = jnp.concatenate([jnp.ones((1,), i32), (blk_expert[1:] != blk_expert[:-1]).astype(i32)])
    n_used = (pad_ends[-1:] // blk).astype(i32)
    tile_cnt = jnp.concatenate([tile_cnt, (padded - counts)[None, :]], axis=0)
    tile_seg = jnp.concatenate([tile_seg, jnp.zeros((1, N_EXPERTS), tile_seg.dtype)], axis=0)
    tile_off = jnp.concatenate([tile_off, (pad_starts + counts)[None, :]], axis=0)
    tile_cnt, tile_seg, tile_off = [t.astype(i32).reshape(-1) for t in (tile_cnt, tile_seg, tile_off)]

    x_rows = _dispatch(h, slot.T, tile_cnt, tile_seg, tile_off, n_used, n_blocks * blk)
    y_rows = _ffn(x_rows, blk_expert + layer_idx * N_EXPERTS, first, n_used, w_up, b_up, w_down, b_down)
    return _combine(x, slot_pad, gates_pad, g2, y_rows, tile_cnt, tile_seg, tile_off)


def _block_diag(w):
    n, d, _ = w.shape
    eye = jnp.eye(n, dtype=w.dtype)
    return (eye[:, None, :, None] * w[:, :, None, :]).reshape(n * d, n * d)


def kernel(x, c, ada_w, ada_b, norm1_g, norm2_g, w_in, w_out, a_q_gain, a_k_gain, a_lam_q1, a_lam_k1, a_lam_q2, a_lam_k2, a_out_gain, b_q_gain, b_k_gain, c_w_g2, c_b_g, c_out_gain, d_conv_w, d_conv_b, d_w_a, d_b_a, d_w_x, d_b_x, d_lambda, router_w, router_b, exp_w_up, exp_b_up, exp_w_down, exp_b_down):
    bsz, s, d = x.shape
    assert bsz == 1 and d == D_MODEL and s % ROW_TILE == 0 and s % (max(A_Q_TILES, B_Q_TILES, B_KEY_BLOCKS) * ATT_TILE) == 0
    depth = ada_w.shape[0]
    mod = _modulation(c, ada_w, ada_b)
    xs = x.reshape(s, d)
    w_up_all = exp_w_up.reshape(depth * N_EXPERTS, d, 2 * D_FF)
    b_up_all = exp_b_up.reshape(depth * N_EXPERTS, 2 * D_FF)
    w_down_all = exp_w_down.reshape(depth * N_EXPERTS, D_FF, d)
    b_down_all = exp_b_down.reshape(depth * N_EXPERTS, d)
    for l in range(depth):
        sh1, sc1, g1, sh2, sc2, g2 = [mod[l, i * d:(i + 1) * d].reshape(1, d) for i in range(6)]
        w = w_in[l]
        w_cat = jnp.concatenate([w[:, :2048], w[:, 2064:2832], w[:, 2048:2064],
                                 jnp.zeros((d, W_IN_PAD - 2832), w.dtype)], axis=1).astype(BF16)
        wg2p = jnp.zeros((LANE, LANE), F32).at[:GLA_RANK, :].set(c_w_g2[l])
        (aqT, ak, avT, bqT, bk, bvT, bkm, cq, ck, cv, cr, cg, dx, dgate) = _inproj(
            xs, norm1_g[l].reshape(1, d), sc1, sh1, w_cat,
            jnp.tile(a_q_gain[l], 2 * A_HEADS).reshape(1, -1), jnp.tile(a_k_gain[l], 2 * A_HEADS).reshape(1, -1),
            jnp.tile(b_q_gain[l], B_HEADS).reshape(1, -1), jnp.tile(b_k_gain[l], B_HEADS).reshape(1, -1),
            wg2p, c_b_g[l].reshape(1, -1))
        o_aT = _diff_attention(aqT, ak, avT, a_lam_q1[l].reshape(1, -1), a_lam_k1[l].reshape(1, -1),
                               a_lam_q2[l].reshape(1, -1), a_lam_k2[l].reshape(1, -1),
                               a_out_gain[l].reshape(-1, 1), l)
        km = bkm.reshape(s // MOBA_BLOCK, B_HEADS, B_DIM).transpose(1, 0, 2)
        o_bT = _moba(bqT, bk, bvT, km)
        o_c = _gla(cq, ck, cg, cv, cr, jnp.tile(c_out_gain[l], C_HEADS).reshape(1, -1))
        o_d = _rglru(dx, dgate, d_conv_w[l], d_conv_b[l].reshape(1, -1),
                     _block_diag(d_w_a[l]).astype(BF16), d_b_a[l].reshape(1, -1),
                     _block_diag(d_w_x[l]).astype(BF16), d_b_x[l].reshape(1, -1), d_lambda[l].reshape(1, -1))
        w4 = w_out[l].astype(BF16).reshape(4, N_GROUP_W, d)
        xs, h2, idx_pad, gates_pad = _outproj(xs, o_aT, o_bT, o_c, o_d, w4, g1, norm2_g[l].reshape(1, d),
                                              sc2, sh2, router_w[l], router_b[l].reshape(1, -1))
        xs = _moe(xs, h2, idx_pad, gates_pad, g2, w_up_all, b_up_all, w_down_all, b_down_all, l)
    return xs.reshape(bsz, s, d)
```

```python
import functools
import math

import jax
import jax.numpy as jnp
from jax import lax
from jax.experimental import pallas as pl
from jax.experimental.pallas import tpu as pltpu

F32 = jnp.float32
BF16 = jnp.bfloat16
HI = lax.Precision.HIGHEST

EPS = 1e-6
NEG = -1e30
M_INIT = -1e30
MASKED = -2e30
LOG2E = 1.4426950408889634

D_MODEL = 1024
N_GROUP_W = 256
A_HEADS, A_QK, A_V = 4, 32, 64
B_HEADS, B_DIM = 4, 64
MOBA_BLOCK, MOBA_TOPK = 256, 3
C_HEADS, C_K, C_V = 4, 32, 64
GLA_RANK, GLA_TAU = 16, 16.0
GLA_SUB = 16
D_WIDTH, D_BLOCKS, CONV_W, LRU_C = 256, 4, 4, 8.0
N_EXPERTS, TOP_K, D_FF = 32, 4, 1024
SWIGLU_ALPHA, SWIGLU_LIMIT = 1.702, 7.0

LANE = 128
SUBLANE = 8
W_IN_PAD = 2944

ROW_TILE = 512
ATT_TILE = 256
A_Q_TILES = 2
B_Q_TILES = 2
B_KEY_BLOCKS = 2
SUM_ROWS = 16
MOE_BLK = 256
MOE_TILE = 256
MOE_SORT_ROWS = MOE_TILE * TOP_K + N_EXPERTS * SUBLANE
VMEM_LIMIT = 56 * 1024 * 1024


def _cparams(sem):
    return pltpu.CompilerParams(dimension_semantics=sem, vmem_limit_bytes=VMEM_LIMIT)


def _log_sigmoid(z):
    return jnp.minimum(z, 0.0) - jnp.log1p(jnp.exp(-jnp.abs(z)))


def _sigmoid(z):
    return 1.0 / (1.0 + jnp.exp(-z))


def _split_bf16(t):
    hi = t.astype(BF16)
    return hi, (t - hi.astype(F32)).astype(BF16)


def _dot_exact_rhs(t, w_bf16, parts=2):
    out, rest = None, t
    for _ in range(parts):
        piece = rest.astype(BF16)
        term = jnp.dot(piece, w_bf16, preferred_element_type=F32)
        out = term if out is None else out + term
        rest = rest - piece.astype(F32)
    return out


def _dot_exact_lhs(w_bf16, t, parts=3):
    out, rest = None, t
    for _ in range(parts):
        piece = rest.astype(BF16)
        term = jnp.dot(w_bf16, piece, preferred_element_type=F32)
        out = term if out is None else out + term
        rest = rest - piece.astype(F32)
    return out


def _group_rms(t, group):
    n = t.shape[-1]
    shift = int(math.log2(group))
    r = lax.broadcasted_iota(jnp.int32, (n, n), 0) >> shift
    c = lax.broadcasted_iota(jnp.int32, (n, n), 1) >> shift
    bd = jnp.where(r == c, 1.0, 0.0).astype(BF16)
    ss = _dot_exact_rhs(t * t, bd)
    return t * lax.rsqrt(ss * (1.0 / group) + EPS)


def _mod_kernel(c_ref, w_ref, b_ref, o_ref):
    c = c_ref[...]
    cond = c * _sigmoid(c)
    cond8 = jnp.broadcast_to(cond, (8, cond.shape[-1]))
    o_ref[0] = jnp.dot(cond8, w_ref[0], precision=HI, preferred_element_type=F32) + b_ref[0]


def _modulation(c, ada_w, ada_b):
    depth, d, n = ada_w.shape
    tn = 1536
    out = pl.pallas_call(
        _mod_kernel,
        grid=(depth, n // tn),
        in_specs=[pl.BlockSpec((1, d), lambda l, j: (0, 0)),
                  pl.BlockSpec((1, d, tn), lambda l, j: (l, 0, j)),
                  pl.BlockSpec((1, 1, tn), lambda l, j: (l, 0, j))],
        out_specs=pl.BlockSpec((1, 8, tn), lambda l, j: (l, 0, j)),
        out_shape=jax.ShapeDtypeStruct((depth, 8, n), F32),
        compiler_params=_cparams(("arbitrary", "arbitrary")),
    )(c, ada_w, ada_b.reshape(depth, 1, n))
    return out[:, 0, :]


def _inproj_kernel(x_ref, ng_ref, sc_ref, sh_ref, w_ref, aqg_ref, akg_ref, bqg_ref, bkg_ref, wg2_ref, bg_ref,
                   aqT_ref, ak_ref, avT_ref, bqT_ref, bk_ref, bvT_ref, bkm_ref,
                   cq_ref, ck_ref, cv_ref, cr_ref, cg_ref, dx_ref, dg_ref):
    x = x_ref[...]
    tm = x.shape[0]
    h = x * lax.rsqrt(jnp.mean(x * x, axis=-1, keepdims=True) + EPS) * ng_ref[...]
    h = h * (1.0 + sc_ref[...]) + sh_ref[...]
    proj = jnp.dot(h.astype(BF16), w_ref[...], preferred_element_type=F32)

    aq = _group_rms(proj[:, 0:256], A_QK) * aqg_ref[...] * (A_QK ** -0.5 * LOG2E)
    aqT_ref[...] = aq.T.astype(BF16)
    ak = _group_rms(proj[:, 256:512], A_QK) * akg_ref[...]
    ak_ref[...] = ak.astype(BF16)
    avT_ref[...] = proj[:, 512:768].T.astype(BF16)

    bq = _group_rms(proj[:, 768:1024], B_DIM) * bqg_ref[...] * (B_DIM ** -0.5 * LOG2E)
    bqT_ref[...] = bq.T.astype(BF16)
    bk = _group_rms(proj[:, 1024:1280], B_DIM) * bkg_ref[...]
    bk_ref[...] = bk.astype(BF16)
    bvT_ref[...] = proj[:, 1280:1536].T.astype(BF16)
    bkm_ref[0] = jnp.mean(bk.reshape(tm // MOBA_BLOCK, MOBA_BLOCK, N_GROUP_W), axis=1)

    cq_ref[...] = proj[:, 1536:1664] * (C_K ** -0.5)
    ck_ref[...] = proj[:, 1664:1792]
    cv_ref[...] = proj[:, 1792:2048]
    cr_ref[...] = proj[:, 2048:2304]
    gl_hi, gl_lo = _split_bf16(proj[:, 2816:2944])
    w2_hi, w2_lo = _split_bf16(wg2_ref[...])
    z = (jnp.dot(gl_hi, w2_hi, preferred_element_type=F32) + jnp.dot(gl_lo, w2_hi, preferred_element_type=F32)
         + jnp.dot(gl_hi, w2_lo, preferred_element_type=F32)) + bg_ref[...]
    cg_ref[...] = _log_sigmoid(z) * (1.0 / GLA_TAU)

    dx_ref[...] = proj[:, 2304:2560]
    dg_ref[...] = proj[:, 2560:2816]


def _inproj(x, ng, sc, sh, w_cat, aqg, akg, bqg, bkg, wg2p, bg):
    s, d = x.shape
    tm = ROW_TILE
    nt = s // tm
    row = lambda i: (i, 0)
    const2 = lambda i: (0, 0)
    vec = pl.BlockSpec((1, d), const2)
    gvec = pl.BlockSpec((1, N_GROUP_W), const2)
    out_shape = (
        jax.ShapeDtypeStruct((N_GROUP_W, s), BF16),
        jax.ShapeDtypeStruct((s, N_GROUP_W), BF16),
        jax.ShapeDtypeStruct((N_GROUP_W, s), BF16),
        jax.ShapeDtypeStruct((N_GROUP_W, s), BF16),
        jax.ShapeDtypeStruct((s, N_GROUP_W), BF16),
        jax.ShapeDtypeStruct((N_GROUP_W, s), BF16),
        jax.ShapeDtypeStruct((nt, tm // MOBA_BLOCK, N_GROUP_W), F32),
        jax.ShapeDtypeStruct((s, 128), F32),
        jax.ShapeDtypeStruct((s, 128), F32),
        jax.ShapeDtypeStruct((s, 256), F32),
        jax.ShapeDtypeStruct((s, 256), F32),
        jax.ShapeDtypeStruct((s, 128), F32),
        jax.ShapeDtypeStruct((s, 256), F32),
        jax.ShapeDtypeStruct((s, 256), F32),
    )
    out_specs = (
        pl.BlockSpec((N_GROUP_W, tm), lambda i: (0, i)),
        pl.BlockSpec((tm, N_GROUP_W), row),
        pl.BlockSpec((N_GROUP_W, tm), lambda i: (0, i)),
        pl.BlockSpec((N_GROUP_W, tm), lambda i: (0, i)),
        pl.BlockSpec((tm, N_GROUP_W), row),
        pl.BlockSpec((N_GROUP_W, tm), lambda i: (0, i)),
        pl.BlockSpec((1, tm // MOBA_BLOCK, N_GROUP_W), lambda i: (i, 0, 0)),
        pl.BlockSpec((tm, 128), row),
        pl.BlockSpec((tm, 128), row),
        pl.BlockSpec((tm, 256), row),
        pl.BlockSpec((tm, 256), row),
        pl.BlockSpec((tm, 128), row),
        pl.BlockSpec((tm, 256), row),
        pl.BlockSpec((tm, 256), row),
    )
    return pl.pallas_call(
        _inproj_kernel,
        grid=(nt,),
        in_specs=[pl.BlockSpec((tm, d), row), vec, vec, vec,
                  pl.BlockSpec((d, W_IN_PAD), const2),
                  gvec, gvec, gvec, gvec,
                  pl.BlockSpec((LANE, LANE), const2),
                  pl.BlockSpec((1, LANE), const2)],
        out_specs=out_specs,
        out_shape=out_shape,
        compiler_params=_cparams(("arbitrary",)),
    )(x, ng, sc, sh, w_cat, aqg, akg, bqg, bkg, wg2p, bg)


def _pad_query_rows(qT_ref, qpad_ref, width):
    per = LANE // width
    n_groups = qT_ref.shape[0] // width
    t = qpad_ref.shape[2]
    rows = lax.broadcasted_iota(jnp.int32, qpad_ref.shape[1:], 0)
    for c in range(qpad_ref.shape[0]):
        g, cols = c % n_groups, slice(t * (c // n_groups), t * (c // n_groups + 1))
        blk = qT_ref[LANE * (g // per):LANE * (g // per + 1), cols]
        lo = width * (g % per)
        qpad_ref[c] = jnp.where((rows >= lo) & (rows < lo + width), blk, jnp.zeros_like(blk))


def _with_sum_rows(vT):
    return jnp.concatenate([vT, jnp.ones((SUM_ROWS, vT.shape[1]), vT.dtype)], axis=0)


def _softmax_tiles(k_slabs, qpad_ref, v_tiles, adjust, m_ref, acc_ref):
    n = len(k_slabs)
    scores = [jnp.dot(k_slabs[g], qpad_ref[g], preferred_element_type=F32) for g in range(n)]
    probs, alphas = [], []
    for g in range(n):
        sT = adjust(g, scores[g])
        m_old = m_ref[g]
        m_new = jnp.maximum(m_old, jnp.max(sT, axis=0, keepdims=True))
        alphas.append(jnp.exp2(m_old - m_new))
        probs.append(jnp.exp2(sT - m_new).astype(BF16))
        m_ref[g] = m_new
    pv = [jnp.dot(v_tiles[g], probs[g], preferred_element_type=F32) for g in range(n)]
    for g in range(n):
        acc_ref[g] = alphas[g] * acc_ref[g] + pv[g]


def _diffattn_kernel(qT_ref, k_ref, vT_ref, lq1_ref, lk1_ref, lq2_ref, lk2_ref, og_ref, o_ref,
                     m_ref, acc_ref, qpad_ref, *, lam_init):
    t, tk = ATT_TILE, A_Q_TILES * ATT_TILE
    n_maps = 2 * A_HEADS
    n_chains = A_Q_TILES * n_maps
    step = pl.program_id(0)
    lam = (jnp.exp(jnp.sum(lq1_ref[...] * lk1_ref[...], axis=-1, keepdims=True))
           - jnp.exp(jnp.sum(lq2_ref[...] * lk2_ref[...], axis=-1, keepdims=True)) + lam_init)
    m_ref[...] = jnp.full(m_ref.shape, M_INIT, F32)
    acc_ref[...] = jnp.zeros(acc_ref.shape, F32)
    _pad_query_rows(qT_ref, qpad_ref, A_QK)

    def tile(ik, adjust):
        keys = pl.ds(pl.multiple_of(ik * tk, tk), tk)
        slabs = [k_ref[keys, LANE * j:LANE * (j + 1)] for j in range(N_GROUP_W // LANE)]
        v_heads = [_with_sum_rows(vT_ref[A_V * hd:A_V * (hd + 1), keys]) for hd in range(A_HEADS)]
        _softmax_tiles([slabs[(c % n_maps) // 4] for c in range(n_chains)], qpad_ref,
                       [v_heads[(c % n_maps) // 2] for c in range(n_chains)], adjust, m_ref, acc_ref)

    def body(ik, carry):
        tile(ik, lambda c, sT: sT)
        return carry

    lax.fori_loop(0, step, body, 0)
    diff = lax.broadcasted_iota(jnp.int32, (tk, t), 0) - lax.broadcasted_iota(jnp.int32, (tk, t), 1)
    tile(step, lambda c, sT: jnp.where(diff <= (c // n_maps) * t, sT, MASKED))

    for sub in range(A_Q_TILES):
        for hd in range(A_HEADS):
            a1, a2 = acc_ref[sub * n_maps + 2 * hd], acc_ref[sub * n_maps + 2 * hd + 1]
            o = a1[:A_V] / a1[A_V:A_V + 1] - lam * (a2[:A_V] / a2[A_V:A_V + 1])
            o = o * lax.rsqrt(jnp.mean(o * o, axis=0, keepdims=True) + EPS) * og_ref[...]
            o_ref[A_V * hd:A_V * (hd + 1), t * sub:t * (sub + 1)] = o * (1.0 - lam_init)


def _diff_attention(aqT, ak, avT, lq1, lk1, lq2, lk2, out_gain_col, layer_idx):
    s = aqT.shape[1]
    t = ATT_TILE
    tq = A_Q_TILES * ATT_TILE
    n_maps = A_Q_TILES * 2 * A_HEADS
    lam_init = 0.8 - 0.6 * math.exp(-0.3 * layer_idx)
    vec = pl.BlockSpec((1, A_QK), lambda i: (0, 0))
    return pl.pallas_call(
        functools.partial(_diffattn_kernel, lam_init=lam_init),
        grid=(s // tq,),
        in_specs=[pl.BlockSpec((N_GROUP_W, tq), lambda i: (0, i)),
                  pl.BlockSpec((s, N_GROUP_W), lambda i: (0, 0)),
                  pl.BlockSpec((N_GROUP_W, s), lambda i: (0, 0)),
                  vec, vec, vec, vec,
                  pl.BlockSpec((A_V, 1), lambda i: (0, 0))],
        out_specs=pl.BlockSpec((N_GROUP_W, tq), lambda i: (0, i)),
        out_shape=jax.ShapeDtypeStruct((N_GROUP_W, s), F32),
        scratch_shapes=[pltpu.VMEM((n_maps, 1, t), F32), pltpu.VMEM((n_maps, A_V + SUM_ROWS, t), F32),
                        pltpu.VMEM((n_maps, LANE, t), BF16)],
        compiler_params=_cparams(("arbitrary",)),
    )(aqT, ak, avT, lq1, lk1, lq2, lk2, out_gain_col)


def _moba_kernel(qT_ref, k_ref, vT_ref, km_ref, o_ref, m_ref, acc_ref, bias_ref, qpad_ref):
    KEY_BLOCKS = B_KEY_BLOCKS
    t, tk = ATT_TILE, KEY_BLOCKS * ATT_TILE
    n_chains = B_Q_TILES * B_HEADS
    first_own = pl.program_id(0) * B_Q_TILES
    nb = km_ref.shape[1]
    m_ref[...] = jnp.full(m_ref.shape, M_INIT, F32)
    acc_ref[...] = jnp.zeros(acc_ref.shape, F32)
    _pad_query_rows(qT_ref, qpad_ref, B_DIM)

    blk = lax.broadcasted_iota(jnp.int32, (nb, t), 0)
    for c in range(n_chains):
        sub, hd = c // B_HEADS, c % B_HEADS
        earlier = blk < first_own + sub
        qT = qT_ref[B_DIM * hd:B_DIM * (hd + 1), t * sub:t * (sub + 1)].astype(F32)
        gate = jnp.dot(km_ref[hd], qT, precision=HI, preferred_element_type=F32)
        gate = jnp.where(earlier, gate, NEG)
        sel_bias = jnp.full((nb, t), MASKED, F32)
        for _ in range(MOBA_TOPK):
            top = jnp.max(gate, axis=0, keepdims=True)
            pick = blk == jnp.min(jnp.where(gate == top, blk, nb), axis=0, keepdims=True)
            sel_bias = jnp.where(pick, jnp.where(earlier, 0.0, MASKED), sel_bias)
            gate = jnp.where(pick, -jnp.inf, gate)
        bias_ref[c] = sel_bias

    def tile(j, adjust):
        keys = pl.ds(pl.multiple_of(j * tk, tk), tk)
        slabs = [k_ref[keys, LANE * i:LANE * (i + 1)] for i in range(N_GROUP_W // LANE)]
        v_heads = [_with_sum_rows(vT_ref[B_DIM * hd:B_DIM * (hd + 1), keys]) for hd in range(B_HEADS)]
        _softmax_tiles([slabs[(c % B_HEADS) // 2] for c in range(n_chains)], qpad_ref,
                       [v_heads[c % B_HEADS] for c in range(n_chains)], adjust, m_ref, acc_ref)

    def block_bias(c, b):
        return bias_ref[c, pl.ds(b, 1), :]

    def body(j, carry):
        tile(j, lambda c, sT: jnp.concatenate(
            [sT[t * part:t * (part + 1)] + block_bias(c, KEY_BLOCKS * j + part) for part in range(KEY_BLOCKS)],
            axis=0))
        return carry

    n_full = first_own // KEY_BLOCKS
    lax.fori_loop(0, n_full, body, 0)

    causal = (lax.broadcasted_iota(jnp.int32, (t, t), 0) <= lax.broadcasted_iota(jnp.int32, (t, t), 1))

    def last(c, sT):
        own = first_own + c // B_HEADS
        parts = []
        for part in range(KEY_BLOCKS):
            b = KEY_BLOCKS * n_full + part
            sp = sT[t * part:t * (part + 1)]
            other = sp + jnp.where(b < own, block_bias(c, b), MASKED)
            parts.append(jnp.where(b == own, jnp.where(causal, sp, MASKED), other))
        return jnp.concatenate(parts, axis=0)

    tile(n_full, last)
    for c in range(n_chains):
        sub, hd = c // B_HEADS, c % B_HEADS
        a = acc_ref[c]
        o_ref[B_DIM * hd:B_DIM * (hd + 1), t * sub:t * (sub + 1)] = a[:B_DIM] / a[B_DIM:B_DIM + 1]


def _moba(bqT, bk, bvT, km):
    assert B_KEY_BLOCKS % B_Q_TILES == 0
    s = bqT.shape[1]
    t = ATT_TILE
    tq = B_Q_TILES * ATT_TILE
    n_chains = B_Q_TILES * B_HEADS
    nb = km.shape[1]
    return pl.pallas_call(
        _moba_kernel,
        grid=(s // tq,),
        in_specs=[pl.BlockSpec((N_GROUP_W, tq), lambda i: (0, i)),
                  pl.BlockSpec((s, N_GROUP_W), lambda i: (0, 0)),
                  pl.BlockSpec((N_GROUP_W, s), lambda i: (0, 0)),
                  pl.BlockSpec((B_HEADS, nb, B_DIM), lambda i: (0, 0, 0))],
        out_specs=pl.BlockSpec((N_GROUP_W, tq), lambda i: (0, i)),
        out_shape=jax.ShapeDtypeStruct((N_GROUP_W, s), F32),
        scratch_shapes=[pltpu.VMEM((n_chains, 1, t), F32),
                        pltpu.VMEM((n_chains, B_DIM + SUM_ROWS, t), F32), pltpu.VMEM((n_chains, nb, t), F32),
                        pltpu.VMEM((n_chains, LANE, t), BF16)],
        compiler_params=_cparams(("arbitrary",)),
    )(bqT, bk, bvT, km)


def _gla_kernel(q_ref, k_ref, g_ref, v_ref, r_ref, og_ref, o_ref, stateT_ref, kbuf_ref, bbuf_ref, vbuf_ref):
    tm = q_ref.shape[0]
    c = GLA_SUB
    shift = int(math.log2(c))

    @pl.when(pl.program_id(0) == 0)
    def _():
        stateT_ref[...] = jnp.zeros(stateT_ref.shape, F32)

    q, k, v = q_ref[...], k_ref[...], v_ref[...]
    ri = lax.broadcasted_iota(jnp.int32, (tm, tm), 0)
    ci = lax.broadcasted_iota(jnp.int32, (tm, tm), 1)
    same = (ri >> shift) == (ci >> shift)
    g = g_ref[...]
    b = _dot_exact_lhs(jnp.where(same & (ri >= ci), 1.0, 0.0).astype(BF16), g)
    b_last = _dot_exact_lhs(jnp.where(same, 1.0, 0.0).astype(BF16), g)

    k_shift, v_shift = int(math.log2(C_K)), int(math.log2(C_V))
    head_sum = jnp.where((lax.broadcasted_iota(jnp.int32, (C_HEADS * C_K, C_HEADS * C_V), 0) >> k_shift)
                         == (lax.broadcasted_iota(jnp.int32, (C_HEADS * C_K, C_HEADS * C_V), 1) >> v_shift),
                         1.0, 0.0).astype(BF16)

    for buf, val in ((kbuf_ref, k), (bbuf_ref, b), (vbuf_ref, v)):
        buf[0:c, :] = jnp.zeros((c, val.shape[1]), F32)
        buf[c:c + tm, :] = val
    pos = lax.broadcasted_iota(jnp.int32, (tm, C_HEADS * C_K), 0) & (c - 1)
    o = jnp.zeros((tm, C_HEADS * C_V), F32)
    for delta in range(c):
        k_d = kbuf_ref[c - delta:c - delta + tm, :]
        b_d = bbuf_ref[c - delta:c - delta + tm, :]
        v_d = vbuf_ref[c - delta:c - delta + tm, :]
        w = q * k_d * jnp.exp(jnp.where(pos >= delta, b - b_d, NEG))
        att = jnp.dot(w.astype(BF16), head_sum, preferred_element_type=F32)
        o = o + att * v_d

    q_dec = (q * jnp.exp(b)).astype(BF16)
    k_dec = (k * jnp.exp(b_last - b)).astype(BF16)
    decay = jnp.exp(b_last)
    v_bf = v.astype(BF16)
    own_head = ((lax.broadcasted_iota(jnp.int32, (C_HEADS * C_V, C_HEADS * C_K), 0) >> v_shift)
                == (lax.broadcasted_iota(jnp.int32, (C_HEADS * C_V, C_HEADS * C_K), 1) >> k_shift))
    nt_dims = (((1,), (1,)), ((), ()))
    tn_dims = (((0,), (0,)), ((), ()))
    updates = [lax.dot_general(v_bf[c * i:c * (i + 1)], k_dec[c * i:c * (i + 1)], tn_dims,
                               preferred_element_type=F32) for i in range(tm // c)]
    stateT = stateT_ref[...]
    far = []
    for i in range(tm // c):
        far.append(lax.dot_general(q_dec[c * i:c * (i + 1)], stateT.astype(BF16), nt_dims,
                                   preferred_element_type=F32))
        stateT = stateT * decay[c * i:c * i + 1, :] + jnp.where(own_head, updates[i], 0.0)
    stateT_ref[...] = stateT
    o = o + jnp.concatenate(far, axis=0)

    r = r_ref[...]
    o_ref[...] = _group_rms(o, C_V) * og_ref[...] * (r * _sigmoid(r))


def _gla(cq, ck, cg, cv, cr, out_gain):
    s = cq.shape[0]
    tm = ROW_TILE
    row = lambda i: (i, 0)
    return pl.pallas_call(
        _gla_kernel,
        grid=(s // tm,),
        in_specs=[pl.BlockSpec((tm, 128), row), pl.BlockSpec((tm, 128), row), pl.BlockSpec((tm, 128), row),
                  pl.BlockSpec((tm, 256), row), pl.BlockSpec((tm, 256), row),
                  pl.BlockSpec((1, C_HEADS * C_V), lambda i: (0, 0))],
        out_specs=pl.BlockSpec((tm, 256), row),
        out_shape=jax.ShapeDtypeStruct((s, 256), F32),
        scratch_shapes=[pltpu.VMEM((C_HEADS * C_V, C_HEADS * C_K), F32),
                        pltpu.VMEM((tm + GLA_SUB, C_HEADS * C_K), F32),
                        pltpu.VMEM((tm + GLA_SUB, C_HEADS * C_K), F32),
                        pltpu.VMEM((tm + GLA_SUB, C_HEADS * C_V), F32)],
        compiler_params=_cparams(("arbitrary",)),
    )(cq, ck, cg, cv, cr, out_gain)


def _rglru_kernel(x_ref, gate_ref, cw_ref, cb_ref, wa_ref, ba_ref, wx_ref, bx_ref, lam_ref, o_ref,
                  xbuf_ref, h_ref):
    tm = x_ref.shape[0]
    pad = 8

    @pl.when(pl.program_id(0) == 0)
    def _():
        xbuf_ref[0:pad, :] = jnp.zeros((pad, D_WIDTH), F32)
        h_ref[...] = jnp.zeros(h_ref.shape, F32)

    xbuf_ref[pad:pad + tm, :] = x_ref[...]
    xc = cb_ref[...] + jnp.zeros((tm, D_WIDTH), F32)
    for w in range(CONV_W):
        lag = CONV_W - 1 - w
        xc = xc + cw_ref[w:w + 1, :] * xbuf_ref[pad - lag:pad - lag + tm, :]
    xbuf_ref[0:pad, :] = xbuf_ref[tm:tm + pad, :]

    xcb = xc.astype(BF16)
    r_gate = _sigmoid(jnp.dot(xcb, wa_ref[...], preferred_element_type=F32) + ba_ref[...])
    i_gate = _sigmoid(jnp.dot(xcb, wx_ref[...], preferred_element_type=F32) + bx_ref[...])
    log_a = LRU_C * r_gate * _log_sigmoid(lam_ref[...])
    a = jnp.exp(log_a)
    b = jnp.sqrt(1.0 - jnp.exp(2.0 * log_a)) * (i_gate * xc)

    rows = lax.broadcasted_iota(jnp.int32, (tm, D_WIDTH), 0)
    step = 1
    while step < tm:
        keep = rows >= step
        a_prev = jnp.where(keep, pltpu.roll(a, step, 0), 1.0)
        b_prev = jnp.where(keep, pltpu.roll(b, step, 0), 0.0)
        b = a * b_prev + b
        a = a * a_prev
        step *= 2
    h = a * h_ref[0:1, :] + b
    h_ref[...] = jnp.broadcast_to(h[tm - 1:tm, :], h_ref.shape)

    g = gate_ref[...]
    gelu = g * (0.5 * (1.0 + jnp.tanh(math.sqrt(2.0 / math.pi) * (g + 0.044715 * (g * g * g)))))
    o_ref[...] = h * gelu


def _rglru(dx, dgate, conv_w, conv_b, wa_bd, b_a, wx_bd, b_x, lam):
    s = dx.shape[0]
    tm = ROW_TILE
    row = lambda i: (i, 0)
    const2 = lambda i: (0, 0)
    vec = pl.BlockSpec((1, D_WIDTH), const2)
    mat = pl.BlockSpec((D_WIDTH, D_WIDTH), const2)
    return pl.pallas_call(
        _rglru_kernel,
        grid=(s // tm,),
        in_specs=[pl.BlockSpec((tm, D_WIDTH), row), pl.BlockSpec((tm, D_WIDTH), row),
                  pl.BlockSpec((CONV_W, D_WIDTH), const2), vec, mat, vec, mat, vec, vec],
        out_specs=pl.BlockSpec((tm, D_WIDTH), row),
        out_shape=jax.ShapeDtypeStruct((s, D_WIDTH), F32),
        scratch_shapes=[pltpu.VMEM((tm + 8, D_WIDTH), F32), pltpu.VMEM((8, D_WIDTH), F32)],
        compiler_params=_cparams(("arbitrary",)),
    )(dx, dgate, conv_w, conv_b, wa_bd, b_a, wx_bd, b_x, lam)


def _outproj_kernel(x_ref, oaT_ref, obT_ref, oc_ref, od_ref, w_ref, g1_ref, ng_ref, sc_ref, sh_ref,
                    rw_ref, rb_ref, xo_ref, h_ref, idx_ref, gate_ref):
    tm = x_ref.shape[0]
    y = jnp.dot(oaT_ref[...].T.astype(BF16), w_ref[0], preferred_element_type=F32)
    y = y + jnp.dot(obT_ref[...].T.astype(BF16), w_ref[1], preferred_element_type=F32)
    y = y + jnp.dot(oc_ref[...].astype(BF16), w_ref[2], preferred_element_type=F32)
    y = y + jnp.dot(od_ref[...].astype(BF16), w_ref[3], preferred_element_type=F32)
    x = x_ref[...] + g1_ref[...] * y
    xo_ref[...] = x
    h = x * lax.rsqrt(jnp.mean(x * x, axis=-1, keepdims=True) + EPS) * ng_ref[...]
    h = h * (1.0 + sc_ref[...]) + sh_ref[...]
    h_ref[...] = h

    h_hi, h_lo = _split_bf16(h)
    w_hi, w_lo = _split_bf16(rw_ref[...])
    logits = (jnp.dot(h_hi, w_hi, preferred_element_type=F32) + jnp.dot(h_lo, w_hi, preferred_element_type=F32)
              + jnp.dot(h_hi, w_lo, preferred_element_type=F32)) + rb_ref[...]
    e_ids = lax.broadcasted_iota(jnp.int32, (tm, N_EXPERTS), 1)
    lane = lax.broadcasted_iota(jnp.int32, (tm, LANE), 1)
    idx_out = jnp.zeros((tm, LANE), jnp.int32)
    val_out = jnp.zeros((tm, LANE), F32)
    top0 = None
    denom = jnp.zeros((tm, 1), F32)
    for k in range(TOP_K):
        top = jnp.max(logits, axis=-1, keepdims=True)
        arg = jnp.min(jnp.where(logits == top, e_ids, N_EXPERTS), axis=-1, keepdims=True)
        if k == 0:
            top0 = top
        w = jnp.exp(top - top0)
        denom = denom + w
        idx_out = jnp.where(lane == k, arg, idx_out)
        val_out = jnp.where(lane == k, w, val_out)
        logits = jnp.where(e_ids == arg, -jnp.inf, logits)
    idx_ref[...] = idx_out
    gate_ref[...] = val_out / denom


def _outproj(x, oaT, obT, oc, od, w4, g1, ng, sc, sh, rw, rb):
    s, d = x.shape
    tm = ROW_TILE
    row = lambda i: (i, 0)
    col = lambda i: (0, i)
    const2 = lambda i: (0, 0)
    vec = pl.BlockSpec((1, d), const2)
    return pl.pallas_call(
        _outproj_kernel,
        grid=(s // tm,),
        in_specs=[pl.BlockSpec((tm, d), row),
                  pl.BlockSpec((N_GROUP_W, tm), col), pl.BlockSpec((N_GROUP_W, tm), col),
                  pl.BlockSpec((tm, 256), row), pl.BlockSpec((tm, 256), row),
                  pl.BlockSpec((4, N_GROUP_W, d), lambda i: (0, 0, 0)),
                  vec, vec, vec, vec,
                  pl.BlockSpec((d, N_EXPERTS), const2), pl.BlockSpec((1, N_EXPERTS), const2)],
        out_specs=(pl.BlockSpec((tm, d), row), pl.BlockSpec((tm, d), row),
                   pl.BlockSpec((tm, LANE), row), pl.BlockSpec((tm, LANE), row)),
        out_shape=(jax.ShapeDtypeStruct((s, d), F32), jax.ShapeDtypeStruct((s, d), F32),
                   jax.ShapeDtypeStruct((s, LANE), jnp.int32), jax.ShapeDtypeStruct((s, LANE), F32)),
        compiler_params=_cparams(("arbitrary",)),
    )(x, oaT, obT, oc, od, w4, g1, ng, sc, sh, rw, rb)


def _segment_copies(tile, cnt_ref, seg_ref, off_ref, make_copy, start):
    def per_expert(e, carry):
        j = tile * N_EXPERTS + e
        cnt, a, b = cnt_ref[j], seg_ref[j], off_ref[j]
        size = MOE_TILE
        while size >= SUBLANE:
            bit = cnt & size

            @pl.when(bit != 0)
            def _():
                cp = make_copy(pl.multiple_of(a, SUBLANE), pl.multiple_of(b, SUBLANE), size)
                if start:
                    cp.start()
                else:
                    cp.wait()

            a, b = a + bit, b + bit
            size //= 2
        return carry

    lax.fori_loop(0, N_EXPERTS, per_expert, 0)


def _segment_wait(tile, cnt_ref, seg_ref, make_copy):
    j = tile * N_EXPERTS + N_EXPERTS - 1
    total = seg_ref[j] + cnt_ref[j]
    size = 1 << (MOE_SORT_ROWS.bit_length() - 1)
    while size >= SUBLANE:
        @pl.when((total & size) != 0)
        def _():
            make_copy(0, 0, size).wait()

        size //= 2


def _dispatch_kernel(cnt_ref, seg_ref, off_ref, nused_ref, h_ref, slotT_ref, rows_ref, sorted_ref, zero_ref, sem):
    tile = pl.program_id(0)
    tt = h_ref.shape[0]
    n = sorted_ref.shape[1]
    par = tile & 1
    r_ids = lax.broadcasted_iota(jnp.int32, (n, tt), 0)
    perm = jnp.where(r_ids == slotT_ref[0:1, :], 1.0, 0.0)
    for k in range(1, TOP_K):
        perm = perm + jnp.where(r_ids == slotT_ref[k:k + 1, :], 1.0, 0.0)
    perm = perm.astype(BF16)
    sorted_ref[par] = jnp.dot(perm, h_ref[...].astype(BF16), preferred_element_type=F32)

    def copies_from(p):
        def make_copy(a, b, size):
            return pltpu.make_async_copy(sorted_ref.at[p, pl.ds(a, size)], rows_ref.at[pl.ds(b, size)], sem.at[p])
        return make_copy

    _segment_copies(tile, cnt_ref, seg_ref, off_ref, copies_from(par), True)

    @pl.when(tile > 0)
    def _():
        _segment_wait(tile - 1, cnt_ref, seg_ref, copies_from(1 - par))

    last = pl.num_programs(0) - 1

    @pl.when(tile == last)
    def _():
        _segment_wait(tile, cnt_ref, seg_ref, copies_from(par))
        zero_ref[...] = jnp.zeros(zero_ref.shape, F32)

        def make_zero_copy(a, b, size):
            return pltpu.make_async_copy(zero_ref.at[pl.ds(a, size)], rows_ref.at[pl.ds(b, size)], sem.at[0])

        _segment_copies(last + 1, cnt_ref, seg_ref, off_ref, make_zero_copy, True)
        _segment_copies(last + 1, cnt_ref, seg_ref, off_ref, make_zero_copy, False)

        def block_copy(b):
            return pltpu.make_async_copy(zero_ref, rows_ref.at[pl.ds(pl.multiple_of(b * MOE_BLK, MOE_BLK), MOE_BLK)],
                                         sem.at[0])

        def issue(b, carry):
            block_copy(b).start()
            return carry

        def drain(b, carry):
            block_copy(b).wait()
            return carry

        n_blocks = rows_ref.shape[0] // MOE_BLK
        lax.fori_loop(nused_ref[0], n_blocks, issue, 0)
        lax.fori_loop(nused_ref[0], n_blocks, drain, 0)


def _dispatch(h, slotT, tile_cnt, tile_seg, tile_off, n_used, n_rows):
    s, d = h.shape
    tt = MOE_TILE
    grid_spec = pltpu.PrefetchScalarGridSpec(
        num_scalar_prefetch=4,
        grid=(s // tt,),
        in_specs=[pl.BlockSpec((tt, d), lambda i, c, sg, of, nu: (i, 0)),
                  pl.BlockSpec((TOP_K, tt), lambda i, c, sg, of, nu: (0, i))],
        out_specs=pl.BlockSpec(memory_space=pl.ANY),
        scratch_shapes=[pltpu.VMEM((2, MOE_SORT_ROWS, d), F32), pltpu.VMEM((MOE_BLK, d), F32),
                        pltpu.SemaphoreType.DMA((2,))],
    )
    return pl.pallas_call(
        _dispatch_kernel,
        grid_spec=grid_spec,
        out_shape=jax.ShapeDtypeStruct((n_rows, d), F32),
        compiler_params=_cparams(("arbitrary",)),
    )(tile_cnt, tile_seg, tile_off, n_used, h, slotT)


def _ffn_kernel(be_ref, first_ref, next_ref, slot_ref, nused_ref, x_ref, wu_hbm, bu_ref, wd_hbm, bd_ref, y_ref,
                wu_f32, wd_f32, wu_bf, wd_bf, sem):
    b = pl.program_id(0)

    def weight_copies(expert, slot):
        return (pltpu.make_async_copy(wu_hbm.at[expert], wu_f32.at[slot], sem.at[0, slot]),
                pltpu.make_async_copy(wd_hbm.at[expert], wd_f32.at[slot], sem.at[1, slot]))

    @pl.when(b == 0)
    def _():
        for cp in weight_copies(be_ref[0], 0):
            cp.start()

    @pl.when(first_ref[b] == 1)
    def _():
        slot = slot_ref[b]
        for cp in weight_copies(be_ref[b], slot):
            cp.wait()

        @pl.when(next_ref[b] >= 0)
        def _():
            for cp in weight_copies(next_ref[b], 1 - slot):
                cp.start()

        wu_bf[...] = wu_f32[slot].astype(BF16)
        wd_bf[...] = wd_f32[slot].astype(BF16)

    @pl.when(b < nused_ref[0])
    def _():
        hu = jnp.dot(x_ref[...].astype(BF16), wu_bf[...], preferred_element_type=F32) + bu_ref[0]
        g = jnp.minimum(hu[:, :D_FF], SWIGLU_LIMIT)
        lin = jnp.clip(hu[:, D_FF:], -SWIGLU_LIMIT, SWIGLU_LIMIT)
        act = (lin + 1.0) * (g * _sigmoid(SWIGLU_ALPHA * g))
        y_ref[...] = jnp.dot(act.astype(BF16), wd_bf[...], preferred_element_type=F32) + bd_ref[0]

    @pl.when(b >= nused_ref[0])
    def _():
        y_ref[...] = jnp.zeros(y_ref.shape, F32)


def _ffn(x_rows, blk_expert, first, n_used, w_up, b_up, w_down, b_down):
    n_rows, d = x_rows.shape
    blk = MOE_BLK
    n_blocks = n_rows // blk
    n_exp, _, f2 = w_up.shape
    i32 = jnp.int32
    ids = jnp.arange(n_blocks, dtype=i32)
    later_start = (ids[None, :] > ids[:, None]) & (first[None, :] == 1)
    next_pos = jnp.min(jnp.where(later_start, ids[None, :], n_blocks), axis=1)
    next_expert = jnp.where(next_pos < n_blocks, blk_expert[jnp.minimum(next_pos, n_blocks - 1)], -1).astype(i32)
    slot = ((jnp.cumsum(first) - 1) & 1).astype(i32)
    idx = lambda b, be, fi, nx, sl, nu: (be[b], 0, 0)
    grid_spec = pltpu.PrefetchScalarGridSpec(
        num_scalar_prefetch=5,
        grid=(n_blocks,),
        in_specs=[pl.BlockSpec((blk, d), lambda b, be, fi, nx, sl, nu: (jnp.minimum(b, nu[0] - 1), 0)),
                  pl.BlockSpec(memory_space=pl.ANY),
                  pl.BlockSpec((1, 1, f2), idx),
                  pl.BlockSpec(memory_space=pl.ANY),
                  pl.BlockSpec((1, 1, d), idx)],
        out_specs=pl.BlockSpec((blk, d), lambda b, be, fi, nx, sl, nu: (b, 0)),
        scratch_shapes=[pltpu.VMEM((2, d, f2), F32), pltpu.VMEM((2, D_FF, d), F32),
                        pltpu.VMEM((d, f2), BF16), pltpu.VMEM((D_FF, d), BF16),
                        pltpu.SemaphoreType.DMA((2, 2))],
    )
    return pl.pallas_call(
        _ffn_kernel,
        grid_spec=grid_spec,
        out_shape=jax.ShapeDtypeStruct((n_rows, d), F32),
        compiler_params=_cparams(("arbitrary",)),
    )(blk_expert, first, next_expert, slot, n_used, x_rows, w_up, b_up.reshape(n_exp, 1, f2),
      w_down, b_down.reshape(n_exp, 1, d))


def _combine_kernel(cnt_ref, seg_ref, off_ref, x_ref, slot_ref, gate_ref, g2_ref, rows_ref, o_ref, buf_ref, sem):
    tile = pl.program_id(0)
    tt = x_ref.shape[0]
    n = buf_ref.shape[1]
    par = tile & 1

    def copies_into(p):
        def make_copy(a, b, size):
            return pltpu.make_async_copy(rows_ref.at[pl.ds(b, size)], buf_ref.at[p, pl.ds(a, size)], sem.at[p])
        return make_copy

    @pl.when(tile == 0)
    def _():
        buf_ref[...] = jnp.zeros(buf_ref.shape, F32)
        _segment_copies(tile, cnt_ref, seg_ref, off_ref, copies_into(par), True)

    @pl.when(tile + 1 < pl.num_programs(0))
    def _():
        _segment_copies(tile + 1, cnt_ref, seg_ref, off_ref, copies_into(1 - par), True)

    slots, gates = slot_ref[...], gate_ref[...]
    r_ids = lax.broadcasted_iota(jnp.int32, (tt, n), 1)
    sel = jnp.where(r_ids == slots[:, 0:1], gates[:, 0:1], 0.0)
    for k in range(1, TOP_K):
        sel = sel + jnp.where(r_ids == slots[:, k:k + 1], gates[:, k:k + 1], 0.0)
    sel_hi, sel_lo = _split_bf16(sel)
    _segment_wait(tile, cnt_ref, seg_ref, copies_into(par))
    y_hi, y_lo = _split_bf16(buf_ref[par])
    both = jnp.dot(jnp.concatenate([sel_hi, sel_lo], axis=0), y_hi, preferred_element_type=F32)
    y = both[:tt] + both[tt:] + jnp.dot(sel_hi, y_lo, preferred_element_type=F32)
    o_ref[...] = x_ref[...] + g2_ref[...] * y


def _combine(x, slots, gates, g2, y_rows, tile_cnt, tile_seg, tile_off):
    s, d = x.shape
    tt = MOE_TILE
    row = lambda i, c, sg, of: (i, 0)
    grid_spec = pltpu.PrefetchScalarGridSpec(
        num_scalar_prefetch=3,
        grid=(s // tt,),
        in_specs=[pl.BlockSpec((tt, d), row), pl.BlockSpec((tt, LANE), row), pl.BlockSpec((tt, LANE), row),
                  pl.BlockSpec((1, d), lambda i, c, sg, of: (0, 0)),
                  pl.BlockSpec(memory_space=pl.ANY)],
        out_specs=pl.BlockSpec((tt, d), row),
        scratch_shapes=[pltpu.VMEM((2, MOE_SORT_ROWS, d), F32), pltpu.SemaphoreType.DMA((2,))],
    )
    return pl.pallas_call(
        _combine_kernel,
        grid_spec=grid_spec,
        out_shape=jax.ShapeDtypeStruct((s, d), F32),
        compiler_params=_cparams(("arbitrary",)),
    )(tile_cnt, tile_seg, tile_off, x, slots, gates, g2, y_rows)


def _moe(x, h, idx_pad, gates_pad, g2, w_up, b_up, w_down, b_down, layer_idx):
    s, d = x.shape
    n = s * TOP_K
    blk = MOE_BLK
    tt = MOE_TILE
    nt = s // tt
    n_blocks = -(-(n + nt * N_EXPERTS * SUBLANE) // blk) + N_EXPERTS
    i32 = jnp.int32
    idx = idx_pad[:, :TOP_K]
    hot = (idx[:, :, None] == jnp.arange(N_EXPERTS, dtype=i32)).astype(F32)
    onehot_t = jnp.sum(hot, axis=1).reshape(nt, tt, N_EXPERTS)
    earlier = jnp.tril(jnp.ones((tt, tt), F32), -1)
    rank_in_tile = jnp.einsum('ij,tjk->tik', earlier, onehot_t, precision=HI).reshape(s, N_EXPERTS)
    tile_cnt = (jnp.sum(onehot_t, axis=1).astype(i32) + SUBLANE - 1) // SUBLANE * SUBLANE
    tile_base = jnp.cumsum(tile_cnt, axis=0) - tile_cnt
    tile_seg = jnp.cumsum(tile_cnt, axis=1) - tile_cnt
    counts = jnp.sum(tile_cnt, axis=0)
    padded = (counts + blk - 1) // blk * blk
    pad_ends = jnp.cumsum(padded)
    pad_starts = pad_ends - padded
    tile_off = pad_starts[None, :] + tile_base
    slot_by_expert = jnp.repeat(tile_seg, tt, axis=0).astype(F32) + rank_in_tile
    slot = jnp.sum(hot * slot_by_expert[:, None, :], axis=2).astype(i32)
    slot_pad = jnp.pad(slot, ((0, 0), (0, LANE - TOP_K)))
    blk_start = jnp.arange(n_blocks, dtype=i32) * blk
    blk_expert = jnp.minimum(jnp.sum((pad_ends[None, :] <= blk_start[:, None]).astype(i32), axis=1),
                             N_EXPERTS - 1).astype(i32)
    first = jnp.concatenate([jnp.ones((1,), i32), (blk_expert[1:] != blk_expert[:-1]).astype(i32)])
    n_used = (pad_ends[-1:] // blk).astype(i32)
    tile_cnt = jnp.concatenate([tile_cnt, (padded - counts)[None, :]], axis=0)
    tile_seg = jnp.concatenate([tile_seg, jnp.zeros((1, N_EXPERTS), tile_seg.dtype)], axis=0)
    tile_off = jnp.concatenate([tile_off, (pad_starts + counts)[None, :]], axis=0)
    tile_cnt, tile_seg, tile_off = [t.astype(i32).reshape(-1) for t in (tile_cnt, tile_seg, tile_off)]

    x_rows = _dispatch(h, slot.T, tile_cnt, tile_seg, tile_off, n_used, n_blocks * blk)
    y_rows = _ffn(x_rows, blk_expert + layer_idx * N_EXPERTS, first, n_used, w_up, b_up, w_down, b_down)
    return _combine(x, slot_pad, gates_pad, g2, y_rows, tile_cnt, tile_seg, tile_off)


def _block_diag(w):
    n, d, _ = w.shape
    eye = jnp.eye(n, dtype=w.dtype)
    return (eye[:, None, :, None] * w[:, :, None, :]).reshape(n * d, n * d)


def kernel(x, c, ada_w, ada_b, norm1_g, norm2_g, w_in, w_out, a_q_gain, a_k_gain, a_lam_q1, a_lam_k1, a_lam_q2, a_lam_k2, a_out_gain, b_q_gain, b_k_gain, c_w_g2, c_b_g, c_out_gain, d_conv_w, d_conv_b, d_w_a, d_b_a, d_w_x, d_b_x, d_lambda, router_w, router_b, exp_w_up, exp_b_up, exp_w_down, exp_b_down):
    bsz, s, d = x.shape
    assert bsz == 1 and d == D_MODEL and s % ROW_TILE == 0 and s % (max(A_Q_TILES, B_Q_TILES, B_KEY_BLOCKS) * ATT_TILE) == 0
    depth = ada_w.shape[0]
    mod = _modulation(c, ada_w, ada_b)
    xs = x.reshape(s, d)
    w_up_all = exp_w_up.reshape(depth * N_EXPERTS, d, 2 * D_FF)
    b_up_all = exp_b_up.reshape(depth * N_EXPERTS, 2 * D_FF)
    w_down_all = exp_w_down.reshape(depth * N_EXPERTS, D_FF, d)
    b_down_all = exp_b_down.reshape(depth * N_EXPERTS, d)
    for l in range(depth):
        sh1, sc1, g1, sh2, sc2, g2 = [mod[l, i * d:(i + 1) * d].reshape(1, d) for i in range(6)]
        w = w_in[l]
        w_cat = jnp.concatenate([w[:, :2048], w[:, 2064:2832], w[:, 2048:2064],
                                 jnp.zeros((d, W_IN_PAD - 2832), w.dtype)], axis=1).astype(BF16)
        wg2p = jnp.zeros((LANE, LANE), F32).at[:GLA_RANK, :].set(c_w_g2[l])
        (aqT, ak, avT, bqT, bk, bvT, bkm, cq, ck, cv, cr, cg, dx, dgate) = _inproj(
            xs, norm1_g[l].reshape(1, d), sc1, sh1, w_cat,
            jnp.tile(a_q_gain[l], 2 * A_HEADS).reshape(1, -1), jnp.tile(a_k_gain[l], 2 * A_HEADS).reshape(1, -1),
            jnp.tile(b_q_gain[l], B_HEADS).reshape(1, -1), jnp.tile(b_k_gain[l], B_HEADS).reshape(1, -1),
            wg2p, c_b_g[l].reshape(1, -1))
        o_aT = _diff_attention(aqT, ak, avT, a_lam_q1[l].reshape(1, -1), a_lam_k1[l].reshape(1, -1),
                               a_lam_q2[l].reshape(1, -1), a_lam_k2[l].reshape(1, -1),
                               a_out_gain[l].reshape(-1, 1), l)
        km = bkm.reshape(s // MOBA_BLOCK, B_HEADS, B_DIM).transpose(1, 0, 2)
        o_bT = _moba(bqT, bk, bvT, km)
        o_c = _gla(cq, ck, cg, cv, cr, jnp.tile(c_out_gain[l], C_HEADS).reshape(1, -1))
        o_d = _rglru(dx, dgate, d_conv_w[l], d_conv_b[l].reshape(1, -1),
                     _block_diag(d_w_a[l]).astype(BF16), d_b_a[l].reshape(1, -1),
                     _block_diag(d_w_x[l]).astype(BF16), d_b_x[l].reshape(1, -1), d_lambda[l].reshape(1, -1))
        w4 = w_out[l].astype(BF16).reshape(4, N_GROUP_W, d)
        xs, h2, idx_pad, gates_pad = _outproj(xs, o_aT, o_bT, o_c, o_d, w4, g1, norm2_g[l].reshape(1, d),
                                              sc2, sh2, router_w[l], router_b[l].reshape(1, -1))
        xs = _moe(xs, h2, idx_pad, gates_pad, g2, w_up_all, b_up_all, w_down_all, b_down_all, l)
    return xs.reshape(bsz, s, d)
```

```python
import functools
import math

import jax
import jax.numpy as jnp
from jax import lax
from jax.experimental import pallas as pl
from jax.experimental.pallas import tpu as pltpu

F32 = jnp.float32
BF16 = jnp.bfloat16
HI = lax.Precision.HIGHEST

EPS = 1e-6
NEG = -1e30
M_INIT = -1e30
MASKED = -2e30
LOG2E = 1.4426950408889634

D_MODEL = 1024
N_GROUP_W = 256
A_HEADS, A_QK, A_V = 4, 32, 64
B_HEADS, B_DIM = 4, 64
MOBA_BLOCK, MOBA_TOPK = 256, 3
C_HEADS, C_K, C_V = 4, 32, 64
GLA_RANK, GLA_TAU = 16, 16.0
GLA_SUB = 16
D_WIDTH, D_BLOCKS, CONV_W, LRU_C = 256, 4, 4, 8.0
N_EXPERTS, TOP_K, D_FF = 32, 4, 1024
SWIGLU_ALPHA, SWIGLU_LIMIT = 1.702, 7.0

LANE = 128
SUBLANE = 8
W_IN_PAD = 2944

ROW_TILE = 512
ATT_TILE = 256
A_Q_TILES = 2
B_Q_TILES = 2
B_KEY_BLOCKS = 2
SUM_ROWS = 16
MOE_BLK = 256
MOE_TILE = 256
MOE_SORT_ROWS = MOE_TILE * TOP_K + N_EXPERTS * SUBLANE
VMEM_LIMIT = 56 * 1024 * 1024


def _cparams(sem):
    return pltpu.CompilerParams(dimension_semantics=sem, vmem_limit_bytes=VMEM_LIMIT)


def _log_sigmoid(z):
    return jnp.minimum(z, 0.0) - jnp.log1p(jnp.exp(-jnp.abs(z)))


def _sigmoid(z):
    return 1.0 / (1.0 + jnp.exp(-z))


def _split_bf16(t):
    hi = t.astype(BF16)
    return hi, (t - hi.astype(F32)).astype(BF16)


def _dot_exact_rhs(t, w_bf16, parts=2):
    out, rest = None, t
    for _ in range(parts):
        piece = rest.astype(BF16)
        term = jnp.dot(piece, w_bf16, preferred_element_type=F32)
        out = term if out is None else out + term
        rest = rest - piece.astype(F32)
    return out


def _dot_exact_lhs(w_bf16, t, parts=3):
    out, rest = None, t
    for _ in range(parts):
        piece = rest.astype(BF16)
        term = jnp.dot(w_bf16, piece, preferred_element_type=F32)
        out = term if out is None else out + term
        rest = rest - piece.astype(F32)
    return out


def _group_rms(t, group):
    n = t.shape[-1]
    shift = int(math.log2(group))
    r = lax.broadcasted_iota(jnp.int32, (n, n), 0) >> shift
    c = lax.broadcasted_iota(jnp.int32, (n, n), 1) >> shift
    bd = jnp.where(r == c, 1.0, 0.0).astype(BF16)
    ss = _dot_exact_rhs(t * t, bd, parts=1)
    return t * lax.rsqrt(ss * (1.0 / group) + EPS)


def _mod_kernel(c_ref, w_ref, b_ref, o_ref):
    c = c_ref[...]
    cond = c * _sigmoid(c)
    cond8 = jnp.broadcast_to(cond, (8, cond.shape[-1]))
    o_ref[0] = jnp.dot(cond8, w_ref[0], precision=HI, preferred_element_type=F32) + b_ref[0]


def _modulation(c, ada_w, ada_b):
    depth, d, n = ada_w.shape
    tn = 1536
    out = pl.pallas_call(
        _mod_kernel,
        grid=(depth, n // tn),
        in_specs=[pl.BlockSpec((1, d), lambda l, j: (0, 0)),
                  pl.BlockSpec((1, d, tn), lambda l, j: (l, 0, j)),
                  pl.BlockSpec((1, 1, tn), lambda l, j: (l, 0, j))],
        out_specs=pl.BlockSpec((1, 8, tn), lambda l, j: (l, 0, j)),
        out_shape=jax.ShapeDtypeStruct((depth, 8, n), F32),
        compiler_params=_cparams(("arbitrary", "arbitrary")),
    )(c, ada_w, ada_b.reshape(depth, 1, n))
    return out[:, 0, :]


def _inproj_kernel(x_ref, ng_ref, sc_ref, sh_ref, w_ref, aqg_ref, akg_ref, bqg_ref, bkg_ref, wg2_ref, bg_ref,
                   aqT_ref, ak_ref, avT_ref, bqT_ref, bk_ref, bvT_ref, bkm_ref,
                   cq_ref, ck_ref, cv_ref, cr_ref, cg_ref, dx_ref, dg_ref):
    x = x_ref[...]
    tm = x.shape[0]
    h = x * lax.rsqrt(jnp.mean(x * x, axis=-1, keepdims=True) + EPS) * ng_ref[...]
    h = h * (1.0 + sc_ref[...]) + sh_ref[...]
    proj = jnp.dot(h.astype(BF16), w_ref[...], preferred_element_type=F32)

    aq = _group_rms(proj[:, 0:256], A_QK) * aqg_ref[...] * (A_QK ** -0.5 * LOG2E)
    aqT_ref[...] = aq.T.astype(BF16)
    ak = _group_rms(proj[:, 256:512], A_QK) * akg_ref[...]
    ak_ref[...] = ak.astype(BF16)
    avT_ref[...] = proj[:, 512:768].T.astype(BF16)

    bq = _group_rms(proj[:, 768:1024], B_DIM) * bqg_ref[...] * (B_DIM ** -0.5 * LOG2E)
    bqT_ref[...] = bq.T.astype(BF16)
    bk = _group_rms(proj[:, 1024:1280], B_DIM) * bkg_ref[...]
    bk_ref[...] = bk.astype(BF16)
    bvT_ref[...] = proj[:, 1280:1536].T.astype(BF16)
    bkm_ref[0] = jnp.mean(bk.reshape(tm // MOBA_BLOCK, MOBA_BLOCK, N_GROUP_W), axis=1)

    cq_ref[...] = proj[:, 1536:1664] * (C_K ** -0.5)
    ck_ref[...] = proj[:, 1664:1792]
    cv_ref[...] = proj[:, 1792:2048]
    cr_ref[...] = proj[:, 2048:2304]
    gl_hi, gl_lo = _split_bf16(proj[:, 2816:2944])
    w2_hi, w2_lo = _split_bf16(wg2_ref[...])
    z = (jnp.dot(gl_hi, w2_hi, preferred_element_type=F32) + jnp.dot(gl_lo, w2_hi, preferred_element_type=F32)
         + jnp.dot(gl_hi, w2_lo, preferred_element_type=F32)) + bg_ref[...]
    cg_ref[...] = _log_sigmoid(z) * (1.0 / GLA_TAU)

    dx_ref[...] = proj[:, 2304:2560]
    dg_ref[...] = proj[:, 2560:2816]


def _inproj(x, ng, sc, sh, w_cat, aqg, akg, bqg, bkg, wg2p, bg):
    s, d = x.shape
    tm = ROW_TILE
    nt = s // tm
    row = lambda i: (i, 0)
    const2 = lambda i: (0, 0)
    vec = pl.BlockSpec((1, d), const2)
    gvec = pl.BlockSpec((1, N_GROUP_W), const2)
    out_shape = (
        jax.ShapeDtypeStruct((N_GROUP_W, s), BF16),
        jax.ShapeDtypeStruct((s, N_GROUP_W), BF16),
        jax.ShapeDtypeStruct((N_GROUP_W, s), BF16),
        jax.ShapeDtypeStruct((N_GROUP_W, s), BF16),
        jax.ShapeDtypeStruct((s, N_GROUP_W), BF16),
        jax.ShapeDtypeStruct((N_GROUP_W, s), BF16),
        jax.ShapeDtypeStruct((nt, tm // MOBA_BLOCK, N_GROUP_W), F32),
        jax.ShapeDtypeStruct((s, 128), F32),
        jax.ShapeDtypeStruct((s, 128), F32),
        jax.ShapeDtypeStruct((s, 256), F32),
        jax.ShapeDtypeStruct((s, 256), F32),
        jax.ShapeDtypeStruct((s, 128), F32),
        jax.ShapeDtypeStruct((s, 256), F32),
        jax.ShapeDtypeStruct((s, 256), F32),
    )
    out_specs = (
        pl.BlockSpec((N_GROUP_W, tm), lambda i: (0, i)),
        pl.BlockSpec((tm, N_GROUP_W), row),
        pl.BlockSpec((N_GROUP_W, tm), lambda i: (0, i)),
        pl.BlockSpec((N_GROUP_W, tm), lambda i: (0, i)),
        pl.BlockSpec((tm, N_GROUP_W), row),
        pl.BlockSpec((N_GROUP_W, tm), lambda i: (0, i)),
        pl.BlockSpec((1, tm // MOBA_BLOCK, N_GROUP_W), lambda i: (i, 0, 0)),
        pl.BlockSpec((tm, 128), row),
        pl.BlockSpec((tm, 128), row),
        pl.BlockSpec((tm, 256), row),
        pl.BlockSpec((tm, 256), row),
        pl.BlockSpec((tm, 128), row),
        pl.BlockSpec((tm, 256), row),
        pl.BlockSpec((tm, 256), row),
    )
    return pl.pallas_call(
        _inproj_kernel,
        grid=(nt,),
        in_specs=[pl.BlockSpec((tm, d), row), vec, vec, vec,
                  pl.BlockSpec((d, W_IN_PAD), const2),
                  gvec, gvec, gvec, gvec,
                  pl.BlockSpec((LANE, LANE), const2),
                  pl.BlockSpec((1, LANE), const2)],
        out_specs=out_specs,
        out_shape=out_shape,
        compiler_params=_cparams(("arbitrary",)),
    )(x, ng, sc, sh, w_cat, aqg, akg, bqg, bkg, wg2p, bg)


def _pad_query_rows(qT_ref, qpad_ref, width):
    per = LANE // width
    n_groups = qT_ref.shape[0] // width
    t = qpad_ref.shape[2]
    rows = lax.broadcasted_iota(jnp.int32, qpad_ref.shape[1:], 0)
    for c in range(qpad_ref.shape[0]):
        g, cols = c % n_groups, slice(t * (c // n_groups), t * (c // n_groups + 1))
        blk = qT_ref[LANE * (g // per):LANE * (g // per + 1), cols]
        lo = width * (g % per)
        qpad_ref[c] = jnp.where((rows >= lo) & (rows < lo + width), blk, jnp.zeros_like(blk))


def _with_sum_rows(vT):
    return jnp.concatenate([vT, jnp.ones((SUM_ROWS, vT.shape[1]), vT.dtype)], axis=0)


def _softmax_tiles(k_slabs, qpad_ref, v_tiles, adjust, m_ref, acc_ref):
    n = len(k_slabs)
    scores = [jnp.dot(k_slabs[g], qpad_ref[g], preferred_element_type=F32) for g in range(n)]
    probs, alphas = [], []
    for g in range(n):
        sT = adjust(g, scores[g])
        m_old = m_ref[g]
        m_new = jnp.maximum(m_old, jnp.max(sT, axis=0, keepdims=True))
        alphas.append(jnp.exp2(m_old - m_new))
        probs.append(jnp.exp2(sT - m_new).astype(BF16))
        m_ref[g] = m_new
    pv = [jnp.dot(v_tiles[g], probs[g], preferred_element_type=F32) for g in range(n)]
    for g in range(n):
        acc_ref[g] = alphas[g] * acc_ref[g] + pv[g]


def _diffattn_kernel(qT_ref, k_ref, vT_ref, lq1_ref, lk1_ref, lq2_ref, lk2_ref, og_ref, o_ref,
                     m_ref, acc_ref, qpad_ref, *, lam_init):
    t, tk = ATT_TILE, A_Q_TILES * ATT_TILE
    n_maps = 2 * A_HEADS
    n_chains = A_Q_TILES * n_maps
    step = pl.program_id(0)
    lam = (jnp.exp(jnp.sum(lq1_ref[...] * lk1_ref[...], axis=-1, keepdims=True))
           - jnp.exp(jnp.sum(lq2_ref[...] * lk2_ref[...], axis=-1, keepdims=True)) + lam_init)
    m_ref[...] = jnp.full(m_ref.shape, M_INIT, F32)
    acc_ref[...] = jnp.zeros(acc_ref.shape, F32)
    _pad_query_rows(qT_ref, qpad_ref, A_QK)

    def tile(ik, adjust):
        keys = pl.ds(pl.multiple_of(ik * tk, tk), tk)
        slabs = [k_ref[keys, LANE * j:LANE * (j + 1)] for j in range(N_GROUP_W // LANE)]
        v_heads = [_with_sum_rows(vT_ref[A_V * hd:A_V * (hd + 1), keys]) for hd in range(A_HEADS)]
        _softmax_tiles([slabs[(c % n_maps) // 4] for c in range(n_chains)], qpad_ref,
                       [v_heads[(c % n_maps) // 2] for c in range(n_chains)], adjust, m_ref, acc_ref)

    def body(ik, carry):
        tile(ik, lambda c, sT: sT)
        return carry

    lax.fori_loop(0, step, body, 0)
    diff = lax.broadcasted_iota(jnp.int32, (tk, t), 0) - lax.broadcasted_iota(jnp.int32, (tk, t), 1)
    tile(step, lambda c, sT: jnp.where(diff <= (c // n_maps) * t, sT, MASKED))

    for sub in range(A_Q_TILES):
        for hd in range(A_HEADS):
            a1, a2 = acc_ref[sub * n_maps + 2 * hd], acc_ref[sub * n_maps + 2 * hd + 1]
            o = a1[:A_V] / a1[A_V:A_V + 1] - lam * (a2[:A_V] / a2[A_V:A_V + 1])
            o = o * lax.rsqrt(jnp.mean(o * o, axis=0, keepdims=True) + EPS) * og_ref[...]
            o_ref[A_V * hd:A_V * (hd + 1), t * sub:t * (sub + 1)] = o * (1.0 - lam_init)


def _diff_attention(aqT, ak, avT, lq1, lk1, lq2, lk2, out_gain_col, layer_idx):
    s = aqT.shape[1]
    t = ATT_TILE
    tq = A_Q_TILES * ATT_TILE
    n_maps = A_Q_TILES * 2 * A_HEADS
    lam_init = 0.8 - 0.6 * math.exp(-0.3 * layer_idx)
    vec = pl.BlockSpec((1, A_QK), lambda i: (0, 0))
    return pl.pallas_call(
        functools.partial(_diffattn_kernel, lam_init=lam_init),
        grid=(s // tq,),
        in_specs=[pl.BlockSpec((N_GROUP_W, tq), lambda i: (0, i)),
                  pl.BlockSpec((s, N_GROUP_W), lambda i: (0, 0)),
                  pl.BlockSpec((N_GROUP_W, s), lambda i: (0, 0)),
                  vec, vec, vec, vec,
                  pl.BlockSpec((A_V, 1), lambda i: (0, 0))],
        out_specs=pl.BlockSpec((N_GROUP_W, tq), lambda i: (0, i)),
        out_shape=jax.ShapeDtypeStruct((N_GROUP_W, s), F32),
        scratch_shapes=[pltpu.VMEM((n_maps, 1, t), F32), pltpu.VMEM((n_maps, A_V + SUM_ROWS, t), F32),
                        pltpu.VMEM((n_maps, LANE, t), BF16)],
        compiler_params=_cparams(("arbitrary",)),
    )(aqT, ak, avT, lq1, lk1, lq2, lk2, out_gain_col)


def _moba_kernel(qT_ref, k_ref, vT_ref, km_ref, o_ref, m_ref, acc_ref, bias_ref, qpad_ref):
    KEY_BLOCKS = B_KEY_BLOCKS
    t, tk = ATT_TILE, KEY_BLOCKS * ATT_TILE
    n_chains = B_Q_TILES * B_HEADS
    first_own = pl.program_id(0) * B_Q_TILES
    nb = km_ref.shape[1]
    m_ref[...] = jnp.full(m_ref.shape, M_INIT, F32)
    acc_ref[...] = jnp.zeros(acc_ref.shape, F32)
    _pad_query_rows(qT_ref, qpad_ref, B_DIM)

    blk = lax.broadcasted_iota(jnp.int32, (nb, t), 0)
    for c in range(n_chains):
        sub, hd = c // B_HEADS, c % B_HEADS
        earlier = blk < first_own + sub
        qT = qT_ref[B_DIM * hd:B_DIM * (hd + 1), t * sub:t * (sub + 1)].astype(F32)
        gate = jnp.dot(km_ref[hd], qT, precision=HI, preferred_element_type=F32)
        gate = jnp.where(earlier, gate, NEG)
        sel_bias = jnp.full((nb, t), MASKED, F32)
        for _ in range(MOBA_TOPK):
            top = jnp.max(gate, axis=0, keepdims=True)
            pick = blk == jnp.min(jnp.where(gate == top, blk, nb), axis=0, keepdims=True)
            sel_bias = jnp.where(pick, jnp.where(earlier, 0.0, MASKED), sel_bias)
            gate = jnp.where(pick, -jnp.inf, gate)
        bias_ref[c] = sel_bias

    def tile(j, adjust):
        keys = pl.ds(pl.multiple_of(j * tk, tk), tk)
        slabs = [k_ref[keys, LANE * i:LANE * (i + 1)] for i in range(N_GROUP_W // LANE)]
        v_heads = [_with_sum_rows(vT_ref[B_DIM * hd:B_DIM * (hd + 1), keys]) for hd in range(B_HEADS)]
        _softmax_tiles([slabs[(c % B_HEADS) // 2] for c in range(n_chains)], qpad_ref,
                       [v_heads[c % B_HEADS] for c in range(n_chains)], adjust, m_ref, acc_ref)

    def block_bias(c, b):
        return bias_ref[c, pl.ds(b, 1), :]

    def body(j, carry):
        tile(j, lambda c, sT: jnp.concatenate(
            [sT[t * part:t * (part + 1)] + block_bias(c, KEY_BLOCKS * j + part) for part in range(KEY_BLOCKS)],
            axis=0))
        return carry

    n_full = first_own // KEY_BLOCKS
    lax.fori_loop(0, n_full, body, 0)

    causal = (lax.broadcasted_iota(jnp.int32, (t, t), 0) <= lax.broadcasted_iota(jnp.int32, (t, t), 1))

    def last(c, sT):
        own = first_own + c // B_HEADS
        parts = []
        for part in range(KEY_BLOCKS):
            b = KEY_BLOCKS * n_full + part
            sp = sT[t * part:t * (part + 1)]
            other = sp + jnp.where(b < own, block_bias(c, b), MASKED)
            parts.append(jnp.where(b == own, jnp.where(causal, sp, MASKED), other))
        return jnp.concatenate(parts, axis=0)

    tile(n_full, last)
    for c in range(n_chains):
        sub, hd = c // B_HEADS, c % B_HEADS
        a = acc_ref[c]
        o_ref[B_DIM * hd:B_DIM * (hd + 1), t * sub:t * (sub + 1)] = a[:B_DIM] / a[B_DIM:B_DIM + 1]


def _moba(bqT, bk, bvT, km):
    assert B_KEY_BLOCKS % B_Q_TILES == 0
    s = bqT.shape[1]
    t = ATT_TILE
    tq = B_Q_TILES * ATT_TILE
    n_chains = B_Q_TILES * B_HEADS
    nb = km.shape[1]
    return pl.pallas_call(
        _moba_kernel,
        grid=(s // tq,),
        in_specs=[pl.BlockSpec((N_GROUP_W, tq), lambda i: (0, i)),
                  pl.BlockSpec((s, N_GROUP_W), lambda i: (0, 0)),
                  pl.BlockSpec((N_GROUP_W, s), lambda i: (0, 0)),
                  pl.BlockSpec((B_HEADS, nb, B_DIM), lambda i: (0, 0, 0))],
        out_specs=pl.BlockSpec((N_GROUP_W, tq), lambda i: (0, i)),
        out_shape=jax.ShapeDtypeStruct((N_GROUP_W, s), F32),
        scratch_shapes=[pltpu.VMEM((n_chains, 1, t), F32),
                        pltpu.VMEM((n_chains, B_DIM + SUM_ROWS, t), F32), pltpu.VMEM((n_chains, nb, t), F32),
                        pltpu.VMEM((n_chains, LANE, t), BF16)],
        compiler_params=_cparams(("arbitrary",)),
    )(bqT, bk, bvT, km)


def _gla_kernel(q_ref, k_ref, g_ref, v_ref, r_ref, og_ref, o_ref, stateT_ref, kbuf_ref, bbuf_ref, vbuf_ref):
    tm = q_ref.shape[0]
    c = GLA_SUB
    shift = int(math.log2(c))

    @pl.when(pl.program_id(0) == 0)
    def _():
        stateT_ref[...] = jnp.zeros(stateT_ref.shape, F32)

    q, k, v = q_ref[...], k_ref[...], v_ref[...]
    ri = lax.broadcasted_iota(jnp.int32, (tm, tm), 0)
    ci = lax.broadcasted_iota(jnp.int32, (tm, tm), 1)
    same = (ri >> shift) == (ci >> shift)
    g = g_ref[...]
    b = _dot_exact_lhs(jnp.where(same & (ri >= ci), 1.0, 0.0).astype(BF16), g)
    b_last = _dot_exact_lhs(jnp.where(same, 1.0, 0.0).astype(BF16), g)

    k_shift, v_shift = int(math.log2(C_K)), int(math.log2(C_V))
    head_sum = jnp.where((lax.broadcasted_iota(jnp.int32, (C_HEADS * C_K, C_HEADS * C_V), 0) >> k_shift)
                         == (lax.broadcasted_iota(jnp.int32, (C_HEADS * C_K, C_HEADS * C_V), 1) >> v_shift),
                         1.0, 0.0).astype(BF16)

    for buf, val in ((kbuf_ref, k), (bbuf_ref, b), (vbuf_ref, v)):
        buf[0:c, :] = jnp.zeros((c, val.shape[1]), F32)
        buf[c:c + tm, :] = val
    pos = lax.broadcasted_iota(jnp.int32, (tm, C_HEADS * C_K), 0) & (c - 1)
    o = jnp.zeros((tm, C_HEADS * C_V), F32)
    for delta in range(c):
        k_d = kbuf_ref[c - delta:c - delta + tm, :]
        b_d = bbuf_ref[c - delta:c - delta + tm, :]
        v_d = vbuf_ref[c - delta:c - delta + tm, :]
        w = q * k_d * jnp.exp(jnp.where(pos >= delta, b - b_d, NEG))
        att = jnp.dot(w.astype(BF16), head_sum, preferred_element_type=F32)
        o = o + att * v_d

    q_dec = (q * jnp.exp(b)).astype(BF16)
    k_dec = (k * jnp.exp(b_last - b)).astype(BF16)
    decay = jnp.exp(b_last)
    v_bf = v.astype(BF16)
    own_head = ((lax.broadcasted_iota(jnp.int32, (C_HEADS * C_V, C_HEADS * C_K), 0) >> v_shift)
                == (lax.broadcasted_iota(jnp.int32, (C_HEADS * C_V, C_HEADS * C_K), 1) >> k_shift))
    nt_dims = (((1,), (1,)), ((), ()))
    tn_dims = (((0,), (0,)), ((), ()))
    updates = [lax.dot_general(v_bf[c * i:c * (i + 1)], k_dec[c * i:c * (i + 1)], tn_dims,
                               preferred_element_type=F32) for i in range(tm // c)]
    stateT = stateT_ref[...]
    far = []
    for i in range(tm // c):
        far.append(lax.dot_general(q_dec[c * i:c * (i + 1)], stateT.astype(BF16), nt_dims,
                                   preferred_element_type=F32))
        stateT = stateT * decay[c * i:c * i + 1, :] + jnp.where(own_head, updates[i], 0.0)
    stateT_ref[...] = stateT
    o = o + jnp.concatenate(far, axis=0)

    r = r_ref[...]
    o_ref[...] = _group_rms(o, C_V) * og_ref[...] * (r * _sigmoid(r))


def _gla(cq, ck, cg, cv, cr, out_gain):
    s = cq.shape[0]
    tm = ROW_TILE
    row = lambda i: (i, 0)
    return pl.pallas_call(
        _gla_kernel,
        grid=(s // tm,),
        in_specs=[pl.BlockSpec((tm, 128), row), pl.BlockSpec((tm, 128), row), pl.BlockSpec((tm, 128), row),
                  pl.BlockSpec((tm, 256), row), pl.BlockSpec((tm, 256), row),
                  pl.BlockSpec((1, C_HEADS * C_V), lambda i: (0, 0))],
        out_specs=pl.BlockSpec((tm, 256), row),
        out_shape=jax.ShapeDtypeStruct((s, 256), F32),
        scratch_shapes=[pltpu.VMEM((C_HEADS * C_V, C_HEADS * C_K), F32),
                        pltpu.VMEM((tm + GLA_SUB, C_HEADS * C_K), F32),
                        pltpu.VMEM((tm + GLA_SUB, C_HEADS * C_K), F32),
                        pltpu.VMEM((tm + GLA_SUB, C_HEADS * C_V), F32)],
        compiler_params=_cparams(("arbitrary",)),
    )(cq, ck, cg, cv, cr, out_gain)


def _rglru_kernel(x_ref, gate_ref, cw_ref, cb_ref, wa_ref, ba_ref, wx_ref, bx_ref, lam_ref, o_ref,
                  xbuf_ref, h_ref):
    tm = x_ref.shape[0]
    pad = 8

    @pl.when(pl.program_id(0) == 0)
    def _():
        xbuf_ref[0:pad, :] = jnp.zeros((pad, D_WIDTH), F32)
        h_ref[...] = jnp.zeros(h_ref.shape, F32)

    xbuf_ref[pad:pad + tm, :] = x_ref[...]
    xc = cb_ref[...] + jnp.zeros((tm, D_WIDTH), F32)
    for w in range(CONV_W):
        lag = CONV_W - 1 - w
        xc = xc + cw_ref[w:w + 1, :] * xbuf_ref[pad - lag:pad - lag + tm, :]
    xbuf_ref[0:pad, :] = xbuf_ref[tm:tm + pad, :]

    xcb = xc.astype(BF16)
    r_gate = _sigmoid(jnp.dot(xcb, wa_ref[...], preferred_element_type=F32) + ba_ref[...])
    i_gate = _sigmoid(jnp.dot(xcb, wx_ref[...], preferred_element_type=F32) + bx_ref[...])
    log_a = LRU_C * r_gate * _log_sigmoid(lam_ref[...])
    a = jnp.exp(log_a)
    b = jnp.sqrt(1.0 - jnp.exp(2.0 * log_a)) * (i_gate * xc)

    rows = lax.broadcasted_iota(jnp.int32, (tm, D_WIDTH), 0)
    step = 1
    while step < tm:
        keep = rows >= step
        a_prev = jnp.where(keep, pltpu.roll(a, step, 0), 1.0)
        b_prev = jnp.where(keep, pltpu.roll(b, step, 0), 0.0)
        b = a * b_prev + b
        a = a * a_prev
        step *= 2
    h = a * h_ref[0:1, :] + b
    h_ref[...] = jnp.broadcast_to(h[tm - 1:tm, :], h_ref.shape)

    g = gate_ref[...]
    gelu = g * (0.5 * (1.0 + jnp.tanh(math.sqrt(2.0 / math.pi) * (g + 0.044715 * (g * g * g)))))
    o_ref[...] = h * gelu


def _rglru(dx, dgate, conv_w, conv_b, wa_bd, b_a, wx_bd, b_x, lam):
    s = dx.shape[0]
    tm = ROW_TILE
    row = lambda i: (i, 0)
    const2 = lambda i: (0, 0)
    vec = pl.BlockSpec((1, D_WIDTH), const2)
    mat = pl.BlockSpec((D_WIDTH, D_WIDTH), const2)
    return pl.pallas_call(
        _rglru_kernel,
        grid=(s // tm,),
        in_specs=[pl.BlockSpec((tm, D_WIDTH), row), pl.BlockSpec((tm, D_WIDTH), row),
                  pl.BlockSpec((CONV_W, D_WIDTH), const2), vec, mat, vec, mat, vec, vec],
        out_specs=pl.BlockSpec((tm, D_WIDTH), row),
        out_shape=jax.ShapeDtypeStruct((s, D_WIDTH), F32),
        scratch_shapes=[pltpu.VMEM((tm + 8, D_WIDTH), F32), pltpu.VMEM((8, D_WIDTH), F32)],
        compiler_params=_cparams(("arbitrary",)),
    )(dx, dgate, conv_w, conv_b, wa_bd, b_a, wx_bd, b_x, lam)


def _outproj_kernel(x_ref, oaT_ref, obT_ref, oc_ref, od_ref, w_ref, g1_ref, ng_ref, sc_ref, sh_ref,
                    rw_ref, rb_ref, xo_ref, h_ref, idx_ref, gate_ref):
    tm = x_ref.shape[0]
    y = jnp.dot(oaT_ref[...].T.astype(BF16), w_ref[0], preferred_element_type=F32)
    y = y + jnp.dot(obT_ref[...].T.astype(BF16), w_ref[1], preferred_element_type=F32)
    y = y + jnp.dot(oc_ref[...].astype(BF16), w_ref[2], preferred_element_type=F32)
    y = y + jnp.dot(od_ref[...].astype(BF16), w_ref[3], preferred_element_type=F32)
    x = x_ref[...] + g1_ref[...] * y
    xo_ref[...] = x
    h = x * lax.rsqrt(jnp.mean(x * x, axis=-1, keepdims=True) + EPS) * ng_ref[...]
    h = h * (1.0 + sc_ref[...]) + sh_ref[...]
    h_ref[...] = h

    h_hi, h_lo = _split_bf16(h)
    w_hi, w_lo = _split_bf16(rw_ref[...])
    logits = (jnp.dot(h_hi, w_hi, preferred_element_type=F32) + jnp.dot(h_lo, w_hi, preferred_element_type=F32)
              + jnp.dot(h_hi, w_lo, preferred_element_type=F32)) + rb_ref[...]
    e_ids = lax.broadcasted_iota(jnp.int32, (tm, N_EXPERTS), 1)
    lane = lax.broadcasted_iota(jnp.int32, (tm, LANE), 1)
    idx_out = jnp.zeros((tm, LANE), jnp.int32)
    val_out = jnp.zeros((tm, LANE), F32)
    top0 = None
    denom = jnp.zeros((tm, 1), F32)
    for k in range(TOP_K):
        top = jnp.max(logits, axis=-1, keepdims=True)
        arg = jnp.min(jnp.where(logits == top, e_ids, N_EXPERTS), axis=-1, keepdims=True)
        if k == 0:
            top0 = top
        w = jnp.exp(top - top0)
        denom = denom + w
        idx_out = jnp.where(lane == k, arg, idx_out)
        val_out = jnp.where(lane == k, w, val_out)
        logits = jnp.where(e_ids == arg, -jnp.inf, logits)
    idx_ref[...] = idx_out
    gate_ref[...] = val_out / denom


def _outproj(x, oaT, obT, oc, od, w4, g1, ng, sc, sh, rw, rb):
    s, d = x.shape
    tm = ROW_TILE
    row = lambda i: (i, 0)
    col = lambda i: (0, i)
    const2 = lambda i: (0, 0)
    vec = pl.BlockSpec((1, d), const2)
    return pl.pallas_call(
        _outproj_kernel,
        grid=(s // tm,),
        in_specs=[pl.BlockSpec((tm, d), row),
                  pl.BlockSpec((N_GROUP_W, tm), col), pl.BlockSpec((N_GROUP_W, tm), col),
                  pl.BlockSpec((tm, 256), row), pl.BlockSpec((tm, 256), row),
                  pl.BlockSpec((4, N_GROUP_W, d), lambda i: (0, 0, 0)),
                  vec, vec, vec, vec,
                  pl.BlockSpec((d, N_EXPERTS), const2), pl.BlockSpec((1, N_EXPERTS), const2)],
        out_specs=(pl.BlockSpec((tm, d), row), pl.BlockSpec((tm, d), row),
                   pl.BlockSpec((tm, LANE), row), pl.BlockSpec((tm, LANE), row)),
        out_shape=(jax.ShapeDtypeStruct((s, d), F32), jax.ShapeDtypeStruct((s, d), F32),
                   jax.ShapeDtypeStruct((s, LANE), jnp.int32), jax.ShapeDtypeStruct((s, LANE), F32)),
        compiler_params=_cparams(("arbitrary",)),
    )(x, oaT, obT, oc, od, w4, g1, ng, sc, sh, rw, rb)


def _segment_copies(tile, cnt_ref, seg_ref, off_ref, make_copy, start):
    def per_expert(e, carry):
        j = tile * N_EXPERTS + e
        cnt, a, b = cnt_ref[j], seg_ref[j], off_ref[j]
        size = MOE_TILE
        while size >= SUBLANE:
            bit = cnt & size

            @pl.when(bit != 0)
            def _():
                cp = make_copy(pl.multiple_of(a, SUBLANE), pl.multiple_of(b, SUBLANE), size)
                if start:
                    cp.start()
                else:
                    cp.wait()

            a, b = a + bit, b + bit
            size //= 2
        return carry

    lax.fori_loop(0, N_EXPERTS, per_expert, 0)


def _segment_wait(tile, cnt_ref, seg_ref, make_copy):
    j = tile * N_EXPERTS + N_EXPERTS - 1
    total = seg_ref[j] + cnt_ref[j]
    size = 1 << (MOE_SORT_ROWS.bit_length() - 1)
    while size >= SUBLANE:
        @pl.when((total & size) != 0)
        def _():
            make_copy(0, 0, size).wait()

        size //= 2


def _dispatch_kernel(cnt_ref, seg_ref, off_ref, nused_ref, h_ref, slotT_ref, rows_ref, sorted_ref, zero_ref, sem):
    tile = pl.program_id(0)
    tt = h_ref.shape[0]
    n = sorted_ref.shape[1]
    par = tile & 1
    r_ids = lax.broadcasted_iota(jnp.int32, (n, tt), 0)
    perm = jnp.where(r_ids == slotT_ref[0:1, :], 1.0, 0.0)
    for k in range(1, TOP_K):
        perm = perm + jnp.where(r_ids == slotT_ref[k:k + 1, :], 1.0, 0.0)
    perm = perm.astype(BF16)
    sorted_ref[par] = jnp.dot(perm, h_ref[...].astype(BF16), preferred_element_type=F32)

    def copies_from(p):
        def make_copy(a, b, size):
            return pltpu.make_async_copy(sorted_ref.at[p, pl.ds(a, size)], rows_ref.at[pl.ds(b, size)], sem.at[p])
        return make_copy

    _segment_copies(tile, cnt_ref, seg_ref, off_ref, copies_from(par), True)

    @pl.when(tile > 0)
    def _():
        _segment_wait(tile - 1, cnt_ref, seg_ref, copies_from(1 - par))

    last = pl.num_programs(0) - 1

    @pl.when(tile == last)
    def _():
        _segment_wait(tile, cnt_ref, seg_ref, copies_from(par))
        zero_ref[...] = jnp.zeros(zero_ref.shape, F32)

        def make_zero_copy(a, b, size):
            return pltpu.make_async_copy(zero_ref.at[pl.ds(a, size)], rows_ref.at[pl.ds(b, size)], sem.at[0])

        _segment_copies(last + 1, cnt_ref, seg_ref, off_ref, make_zero_copy, True)
        _segment_copies(last + 1, cnt_ref, seg_ref, off_ref, make_zero_copy, False)

        def block_copy(b):
            return pltpu.make_async_copy(zero_ref, rows_ref.at[pl.ds(pl.multiple_of(b * MOE_BLK, MOE_BLK), MOE_BLK)],
                                         sem.at[0])

        def issue(b, carry):
            block_copy(b).start()
            return carry

        def drain(b, carry):
            block_copy(b).wait()
            return carry

        n_blocks = rows_ref.shape[0] // MOE_BLK
        lax.fori_loop(nused_ref[0], n_blocks, issue, 0)
        lax.fori_loop(nused_ref[0], n_blocks, drain, 0)


def _dispatch(h, slotT, tile_cnt, tile_seg, tile_off, n_used, n_rows):
    s, d = h.shape
    tt = MOE_TILE
    grid_spec = pltpu.PrefetchScalarGridSpec(
        num_scalar_prefetch=4,
        grid=(s // tt,),
        in_specs=[pl.BlockSpec((tt, d), lambda i, c, sg, of, nu: (i, 0)),
                  pl.BlockSpec((TOP_K, tt), lambda i, c, sg, of, nu: (0, i))],
        out_specs=pl.BlockSpec(memory_space=pl.ANY),
        scratch_shapes=[pltpu.VMEM((2, MOE_SORT_ROWS, d), F32), pltpu.VMEM((MOE_BLK, d), F32),
                        pltpu.SemaphoreType.DMA((2,))],
    )
    return pl.pallas_call(
        _dispatch_kernel,
        grid_spec=grid_spec,
        out_shape=jax.ShapeDtypeStruct((n_rows, d), F32),
        compiler_params=_cparams(("arbitrary",)),
    )(tile_cnt, tile_seg, tile_off, n_used, h, slotT)


def _ffn_kernel(be_ref, first_ref, next_ref, slot_ref, nused_ref, x_ref, wu_hbm, bu_ref, wd_hbm, bd_ref, y_ref,
                wu_f32, wd_f32, wu_bf, wd_bf, sem):
    b = pl.program_id(0)

    def weight_copies(expert, slot):
        return (pltpu.make_async_copy(wu_hbm.at[expert], wu_f32.at[slot], sem.at[0, slot]),
                pltpu.make_async_copy(wd_hbm.at[expert], wd_f32.at[slot], sem.at[1, slot]))

    @pl.when(b == 0)
    def _():
        for cp in weight_copies(be_ref[0], 0):
            cp.start()

    @pl.when(first_ref[b] == 1)
    def _():
        slot = slot_ref[b]
        for cp in weight_copies(be_ref[b], slot):
            cp.wait()

        @pl.when(next_ref[b] >= 0)
        def _():
            for cp in weight_copies(next_ref[b], 1 - slot):
                cp.start()

        wu_bf[...] = wu_f32[slot].astype(BF16)
        wd_bf[...] = wd_f32[slot].astype(BF16)

    @pl.when(b < nused_ref[0])
    def _():
        hu = jnp.dot(x_ref[...].astype(BF16), wu_bf[...], preferred_element_type=F32) + bu_ref[0]
        g = jnp.minimum(hu[:, :D_FF], SWIGLU_LIMIT)
        lin = jnp.clip(hu[:, D_FF:], -SWIGLU_LIMIT, SWIGLU_LIMIT)
        act = (lin + 1.0) * (g * _sigmoid(SWIGLU_ALPHA * g))
        y_ref[...] = jnp.dot(act.astype(BF16), wd_bf[...], preferred_element_type=F32) + bd_ref[0]

    @pl.when(b >= nused_ref[0])
    def _():
        y_ref[...] = jnp.zeros(y_ref.shape, F32)


def _ffn(x_rows, blk_expert, first, n_used, w_up, b_up, w_down, b_down):
    n_rows, d = x_rows.shape
    blk = MOE_BLK
    n_blocks = n_rows // blk
    n_exp, _, f2 = w_up.shape
    i32 = jnp.int32
    ids = jnp.arange(n_blocks, dtype=i32)
    later_start = (ids[None, :] > ids[:, None]) & (first[None, :] == 1)
    next_pos = jnp.min(jnp.where(later_start, ids[None, :], n_blocks), axis=1)
    next_expert = jnp.where(next_pos < n_blocks, blk_expert[jnp.minimum(next_pos, n_blocks - 1)], -1).astype(i32)
    slot = ((jnp.cumsum(first) - 1) & 1).astype(i32)
    idx = lambda b, be, fi, nx, sl, nu: (be[b], 0, 0)
    grid_spec = pltpu.PrefetchScalarGridSpec(
        num_scalar_prefetch=5,
        grid=(n_blocks,),
        in_specs=[pl.BlockSpec((blk, d), lambda b, be, fi, nx, sl, nu: (jnp.minimum(b, nu[0] - 1), 0)),
                  pl.BlockSpec(memory_space=pl.ANY),
                  pl.BlockSpec((1, 1, f2), idx),
                  pl.BlockSpec(memory_space=pl.ANY),
                  pl.BlockSpec((1, 1, d), idx)],
        out_specs=pl.BlockSpec((blk, d), lambda b, be, fi, nx, sl, nu: (b, 0)),
        scratch_shapes=[pltpu.VMEM((2, d, f2), F32), pltpu.VMEM((2, D_FF, d), F32),
                        pltpu.VMEM((d, f2), BF16), pltpu.VMEM((D_FF, d), BF16),
                        pltpu.SemaphoreType.DMA((2, 2))],
    )
    return pl.pallas_call(
        _ffn_kernel,
        grid_spec=grid_spec,
        out_shape=jax.ShapeDtypeStruct((n_rows, d), F32),
        compiler_params=_cparams(("arbitrary",)),
    )(blk_expert, first, next_expert, slot, n_used, x_rows, w_up, b_up.reshape(n_exp, 1, f2),
      w_down, b_down.reshape(n_exp, 1, d))


def _combine_kernel(cnt_ref, seg_ref, off_ref, x_ref, slot_ref, gate_ref, g2_ref, rows_ref, o_ref, buf_ref, sem):
    tile = pl.program_id(0)
    tt = x_ref.shape[0]
    n = buf_ref.shape[1]
    par = tile & 1

    def copies_into(p):
        def make_copy(a, b, size):
            return pltpu.make_async_copy(rows_ref.at[pl.ds(b, size)], buf_ref.at[p, pl.ds(a, size)], sem.at[p])
        return make_copy

    @pl.when(tile == 0)
    def _():
        buf_ref[...] = jnp.zeros(buf_ref.shape, F32)
        _segment_copies(tile, cnt_ref, seg_ref, off_ref, copies_into(par), True)

    @pl.when(tile + 1 < pl.num_programs(0))
    def _():
        _segment_copies(tile + 1, cnt_ref, seg_ref, off_ref, copies_into(1 - par), True)

    slots, gates = slot_ref[...], gate_ref[...]
    r_ids = lax.broadcasted_iota(jnp.int32, (tt, n), 1)
    sel = jnp.where(r_ids == slots[:, 0:1], gates[:, 0:1], 0.0)
    for k in range(1, TOP_K):
        sel = sel + jnp.where(r_ids == slots[:, k:k + 1], gates[:, k:k + 1], 0.0)
    sel_hi, sel_lo = _split_bf16(sel)
    _segment_wait(tile, cnt_ref, seg_ref, copies_into(par))
    y_hi, y_lo = _split_bf16(buf_ref[par])
    both = jnp.dot(jnp.concatenate([sel_hi, sel_lo], axis=0), y_hi, preferred_element_type=F32)
    y = both[:tt] + both[tt:] + jnp.dot(sel_hi, y_lo, preferred_element_type=F32)
    o_ref[...] = x_ref[...] + g2_ref[...] * y


def _combine(x, slots, gates, g2, y_rows, tile_cnt, tile_seg, tile_off):
    s, d = x.shape
    tt = MOE_TILE
    row = lambda i, c, sg, of: (i, 0)
    grid_spec = pltpu.PrefetchScalarGridSpec(
        num_scalar_prefetch=3,
        grid=(s // tt,),
        in_specs=[pl.BlockSpec((tt, d), row), pl.BlockSpec((tt, LANE), row), pl.BlockSpec((tt, LANE), row),
                  pl.BlockSpec((1, d), lambda i, c, sg, of: (0, 0)),
                  pl.BlockSpec(memory_space=pl.ANY)],
        out_specs=pl.BlockSpec((tt, d), row),
        scratch_shapes=[pltpu.VMEM((2, MOE_SORT_ROWS, d), F32), pltpu.SemaphoreType.DMA((2,))],
    )
    return pl.pallas_call(
        _combine_kernel,
        grid_spec=grid_spec,
        out_shape=jax.ShapeDtypeStruct((s, d), F32),
        compiler_params=_cparams(("arbitrary",)),
    )(tile_cnt, tile_seg, tile_off, x, slots, gates, g2, y_rows)


def _moe(x, h, idx_pad, gates_pad, g2, w_up, b_up, w_down, b_down, layer_idx):
    s, d = x.shape
    n = s * TOP_K
    blk = MOE_BLK
    tt = MOE_TILE
    nt = s // tt
    n_blocks = -(-(n + nt * N_EXPERTS * SUBLANE) // blk) + N_EXPERTS
    i32 = jnp.int32
    idx = idx_pad[:, :TOP_K]
    hot = (idx[:, :, None] == jnp.arange(N_EXPERTS, dtype=i32)).astype(F32)
    onehot_t = jnp.sum(hot, axis=1).reshape(nt, tt, N_EXPERTS)
    earlier = jnp.tril(jnp.ones((tt, tt), F32), -1)
    rank_in_tile = jnp.einsum('ij,tjk->tik', earlier, onehot_t, precision=HI).reshape(s, N_EXPERTS)
    tile_cnt = (jnp.sum(onehot_t, axis=1).astype(i32) + SUBLANE - 1) // SUBLANE * SUBLANE
    tile_base = jnp.cumsum(tile_cnt, axis=0) - tile_cnt
    tile_seg = jnp.cumsum(tile_cnt, axis=1) - tile_cnt
    counts = jnp.sum(tile_cnt, axis=0)
    padded = (counts + blk - 1) // blk * blk
    pad_ends = jnp.cumsum(padded)
    pad_starts = pad_ends - padded
    tile_off = pad_starts[None, :] + tile_base
    slot_by_expert = jnp.repeat(tile_seg, tt, axis=0).astype(F32) + rank_in_tile
    slot = jnp.sum(hot * slot_by_expert[:, None, :], axis=2).astype(i32)
    slot_pad = jnp.pad(slot, ((0, 0), (0, LANE - TOP_K)))
    blk_start = jnp.arange(n_blocks, dtype=i32) * blk
    blk_expert = jnp.minimum(jnp.sum((pad_ends[None, :] <= blk_start[:, None]).astype(i32), axis=1),
                             N_EXPERTS - 1).astype(i32)
    first = jnp.concatenate([jnp.ones((1,), i32), (blk_expert[1:] != blk_expert[:-1]).astype(i32)])
    n_used = (pad_ends[-1:] // blk).astype(i32)
    tile_cnt = jnp.concatenate([tile_cnt, (padded - counts)[None, :]], axis=0)
    tile_seg = jnp.concatenate([tile_seg, jnp.zeros((1, N_EXPERTS), tile_seg.dtype)], axis=0)
    tile_off = jnp.concatenate([tile_off, (pad_starts + counts)[None, :]], axis=0)
    tile_cnt, tile_seg, tile_off = [t.astype(i32).reshape(-1) for t in (tile_cnt, tile_seg, tile_off)]

    x_rows = _dispatch(h, slot.T, tile_cnt, tile_seg, tile_off, n_used, n_blocks * blk)
    y_rows = _ffn(x_rows, blk_expert + layer_idx * N_EXPERTS, first, n_used, w_up, b_up, w_down, b_down)
    return _combine(x, slot_pad, gates_pad, g2, y_rows, tile_cnt, tile_seg, tile_off)


def _block_diag(w):
    n, d, _ = w.shape
    eye = jnp.eye(n, dtype=w.dtype)
    return (eye[:, None, :, None] * w[:, :, None, :]).reshape(n * d, n * d)


def kernel(x, c, ada_w, ada_b, norm1_g, norm2_g, w_in, w_out, a_q_gain, a_k_gain, a_lam_q1, a_lam_k1, a_lam_q2, a_lam_k2, a_out_gain, b_q_gain, b_k_gain, c_w_g2, c_b_g, c_out_gain, d_conv_w, d_conv_b, d_w_a, d_b_a, d_w_x, d_b_x, d_lambda, router_w, router_b, exp_w_up, exp_b_up, exp_w_down, exp_b_down):
    bsz, s, d = x.shape
    assert bsz == 1 and d == D_MODEL and s % ROW_TILE == 0 and s % (max(A_Q_TILES, B_Q_TILES, B_KEY_BLOCKS) * ATT_TILE) == 0
    depth = ada_w.shape[0]
    mod = _modulation(c, ada_w, ada_b)
    xs = x.reshape(s, d)
    w_up_all = exp_w_up.reshape(depth * N_EXPERTS, d, 2 * D_FF)
    b_up_all = exp_b_up.reshape(depth * N_EXPERTS, 2 * D_FF)
    w_down_all = exp_w_down.reshape(depth * N_EXPERTS, D_FF, d)
    b_down_all = exp_b_down.reshape(depth * N_EXPERTS, d)
    for l in range(depth):
        sh1, sc1, g1, sh2, sc2, g2 = [mod[l, i * d:(i + 1) * d].reshape(1, d) for i in range(6)]
        w = w_in[l]
        w_cat = jnp.concatenate([w[:, :2048], w[:, 2064:2832], w[:, 2048:2064],
                                 jnp.zeros((d, W_IN_PAD - 2832), w.dtype)], axis=1).astype(BF16)
        wg2p = jnp.zeros((LANE, LANE), F32).at[:GLA_RANK, :].set(c_w_g2[l])
        (aqT, ak, avT, bqT, bk, bvT, bkm, cq, ck, cv, cr, cg, dx, dgate) = _inproj(
            xs, norm1_g[l].reshape(1, d), sc1, sh1, w_cat,
            jnp.tile(a_q_gain[l], 2 * A_HEADS).reshape(1, -1), jnp.tile(a_k_gain[l], 2 * A_HEADS).reshape(1, -1),
            jnp.tile(b_q_gain[l], B_HEADS).reshape(1, -1), jnp.tile(b_k_gain[l], B_HEADS).reshape(1, -1),
            wg2p, c_b_g[l].reshape(1, -1))
        o_aT = _diff_attention(aqT, ak, avT, a_lam_q1[l].reshape(1, -1), a_lam_k1[l].reshape(1, -1),
                               a_lam_q2[l].reshape(1, -1), a_lam_k2[l].reshape(1, -1),
                               a_out_gain[l].reshape(-1, 1), l)
        km = bkm.reshape(s // MOBA_BLOCK, B_HEADS, B_DIM).transpose(1, 0, 2)
        o_bT = _moba(bqT, bk, bvT, km)
        o_c = _gla(cq, ck, cg, cv, cr, jnp.tile(c_out_gain[l], C_HEADS).reshape(1, -1))
        o_d = _rglru(dx, dgate, d_conv_w[l], d_conv_b[l].reshape(1, -1),
                     _block_diag(d_w_a[l]).astype(BF16), d_b_a[l].reshape(1, -1),
                     _block_diag(d_w_x[l]).astype(BF16), d_b_x[l].reshape(1, -1), d_lambda[l].reshape(1, -1))
        w4 = w_out[l].astype(BF16).reshape(4, N_GROUP_W, d)
        xs, h2, idx_pad, gates_pad = _outproj(xs, o_aT, o_bT, o_c, o_d, w4, g1, norm2_g[l].reshape(1, d),
                                              sc2, sh2, router_w[l], router_b[l].reshape(1, -1))
        xs = _moe(xs, h2, idx_pad, gates_pad, g2, w_up_all, b_up_all, w_down_all, b_down_all, l)
    return xs.reshape(bsz, s, d)
```
